```python
import math
import jax, jax.numpy as jnp
from jax import lax
import numpy as np

D_MODEL = 2048
BATCH = 16
SEQ = 256
DEPTH = 2
DEC_BATCH = 4
DEC_SEQ = 1024
PAST_LEN = 256

GRID_W = 64
MIX_W = D_MODEL
GROUP_W = MIX_W // 4

ML_H = 4
ML_DK = GROUP_W // ML_H
ML_DV = GROUP_W // ML_H
ML_CHUNK = 64
MLA_H = 4
MLA_NOPE = 128
MLA_ROPE = 64
MLA_V = GROUP_W // MLA_H
MLA_QK = MLA_NOPE + MLA_ROPE
Q_LORA = 384
KV_LORA = 256
ROPE_THETA = 10000.0
Q_BLOCK = 128
RW_H = 8
RW_N = GROUP_W // RW_H
RW_LORA_W = 64
RW_LORA_A = 64
RW_LORA_G = 128
RW_DECAY_SCALE = math.exp(-0.5)
RW_LN_EPS = 64e-5
GLA_H = 4
GLA_DK = 64
GLA_DV = GROUP_W // GLA_H
GLA_W = GLA_H * GLA_DV
GLA_GATE_RANK = 16
GLA_NORMALIZER = 16.0
GLA_CHUNK = 32
N_GROUPS = 4
EXPERTS_PER_GROUP = 4
N_EXPERTS = N_GROUPS * EXPERTS_PER_GROUP
TOP_K_IN_GROUP = 2
EXPERT_HIDDEN = 512
NORM_EPS = 1e-6

IN_SPLITS = (
    ML_H * ML_DK, ML_H * ML_DK, ML_H * ML_DV, ML_H * ML_DV, 4 * ML_H,
    Q_LORA, KV_LORA, MLA_ROPE,
    GROUP_W, GROUP_W, GROUP_W, RW_LORA_W, RW_LORA_A, RW_LORA_G,
    GLA_H * GLA_DK, GLA_H * GLA_DK, GLA_W, GLA_GATE_RANK, GLA_W,
)
IN_COLS = sum(IN_SPLITS)
F32 = jnp.float32

kernel_name = 'hybrid_flow_backbone_step'


def heads(t, n_heads):
    return t.reshape(t.shape[:-1] + (n_heads, t.shape[-1] // n_heads))


def rms_norm(x, g, eps=NORM_EPS):
    xf = x.astype(F32)
    return xf * lax.rsqrt(jnp.mean(xf * xf, axis=-1, keepdims=True) + eps) * g


def head_rms_norm(h, g, eps=NORM_EPS):
    y = h * lax.rsqrt(jnp.mean(h * h, axis=-1, keepdims=True) + eps)
    return y.reshape(h.shape[:-2] + (-1,)) * g


def head_layer_norm(h, g, eps=NORM_EPS):
    hc = h - jnp.mean(h, axis=-1, keepdims=True)
    y = hc * lax.rsqrt(jnp.mean(hc * hc, axis=-1, keepdims=True) + eps)
    return y.reshape(h.shape[:-2] + (-1,)) * g


def flip_if(t, rev, axis):
    return jnp.flip(t, axis=axis) if rev else t


def split_cols(u):
    return jnp.split(u, np.cumsum(IN_SPLITS)[:-1].tolist(), axis=-1)


def mlstm_scan(q, k, v, ig, lf, C0, n0, m0):
    B, H, T, _ = q.shape
    nc = T // ML_CHUNK

    def chunks(t):
        return jnp.moveaxis(t.reshape((B, H, nc, ML_CHUNK) + t.shape[3:]), 2, 0)

    tril = jnp.tril(jnp.ones((ML_CHUNK, ML_CHUNK), dtype=bool))

    def step(carry, inp):
        C, n, m = carry
        qc, kc, vc, ic, fc = inp
        b = jnp.cumsum(fc, axis=-1)
        D = jnp.where(tril, b[..., :, None] - b[..., None, :] + ic[..., None, :], -jnp.inf)
        m_inter = b + m[..., None]
        m_row = jnp.maximum(m_inter, jnp.max(D, axis=-1))
        w_inter = jnp.exp(m_inter - m_row)
        S = jnp.einsum('bhid,bhjd->bhij', qc, kc) * jnp.exp(D - m_row[..., None])
        num = w_inter[..., None] * jnp.einsum('bhid,bhde->bhie', qc, C) + jnp.einsum('bhij,bhje->bhie', S, vc)
        den = w_inter * jnp.einsum('bhid,bhd->bhi', qc, n) + jnp.sum(S, axis=-1)
        h = num / jnp.maximum(jnp.abs(den), jnp.exp(-m_row))[..., None]
        b_last = b[..., -1:]
        d_key = b_last - b + ic
        m_new = jnp.maximum(b_last[..., 0] + m, jnp.max(d_key, axis=-1))
        w_key = jnp.exp(d_key - m_new[..., None])
        carry_scale = jnp.exp(b_last[..., 0] + m - m_new)
        C = carry_scale[..., None, None] * C + jnp.einsum('bhj,bhjd,bhje->bhde', w_key, kc, vc)
        n = carry_scale[..., None] * n + jnp.einsum('bhj,bhjd->bhd', w_key, kc)
        return (C, n, m_new), h

    (C, n, m), h = lax.scan(step, (C0, n0, m0), tuple(chunks(t) for t in (q, k, v, ig, lf)))
    return jnp.moveaxis(h, 0, 2).reshape(B, H, T, -1), C, n, m


def mlstm_mixer(q, k, v, o, gate_pre, p, st0):
    C0, n0, m0 = st0
    B, T, _ = q.shape
    bht = lambda t: jnp.transpose(heads(t, ML_H), (0, 2, 1, 3))
    qh, kh, vh = bht(q) * ML_DK ** -0.5, bht(k), bht(v)
    g = jnp.transpose((gate_pre + p['b_ml_gates']).reshape(B, T, 2, 2, ML_H), (2, 3, 0, 4, 1))
    hs, Cs, ns, ms = [], [], [], []
    for d in range(2):
        fl = lambda t: flip_if(t, d == 1, 2)
        h, C, n, m = mlstm_scan(fl(qh), fl(kh), fl(vh), fl(g[d, 0]), fl(jax.nn.log_sigmoid(g[d, 1])),
                                C0[:, d], n0[:, d], m0[:, d])
        hs.append(fl(h)); Cs.append(C); ns.append(n); ms.append(m)
    h = jnp.transpose(hs[0] + hs[1], (0, 2, 1, 3))
    out = head_layer_norm(h, p['g_ml_norm']) * jax.nn.sigmoid(o)
    return out, (jnp.stack(Cs, 1), jnp.stack(ns, 1), jnp.stack(ms, 1))


def axial_rope_tables(n_tokens):
    rows = n_tokens // GRID_W
    row = jnp.repeat(jnp.arange(rows, dtype=F32), GRID_W)
    col = jnp.tile(jnp.arange(GRID_W, dtype=F32), rows)
    n_freq = MLA_ROPE // 4
    inv = ROPE_THETA ** (-jnp.arange(n_freq, dtype=F32) / n_freq)
    ang = jnp.stack([row[:, None] * inv, col[:, None] * inv], axis=1)
    return jnp.cos(ang), jnp.sin(ang)


def apply_axial_rope(x, cos, sin):
    xs = x.reshape(x.shape[:-1] + (2, 2, MLA_ROPE // 4))
    x1, x2 = xs[..., 0, :], xs[..., 1, :]
    c, s = cos[None, :, None], sin[None, :, None]
    return jnp.stack([x1 * c - x2 * s, x2 * c + x1 * s], axis=-2).reshape(x.shape)


def rope_part(x, cos, sin):
    return jnp.concatenate([x[..., :MLA_NOPE], apply_axial_rope(x[..., MLA_NOPE:], cos, sin)], axis=-1)


def mla_keys_values(kv_lat, p):
    c_kv, k_rope = kv_lat[..., :KV_LORA], kv_lat[..., KV_LORA:]
    kv = heads(c_kv @ p['w_mla_ukv'], MLA_H)
    k_nope, v = kv[..., :MLA_NOPE], kv[..., MLA_NOPE:]
    k_rope = jnp.broadcast_to(k_rope[:, :, None, :], k_nope.shape[:-1] + (MLA_ROPE,))
    k = rms_norm(jnp.concatenate([k_nope, k_rope], axis=-1), p['g_mla_kn'])
    return k, v


def attention(q, k, v):
    B, Tq, H, dq = q.shape
    qb = jnp.moveaxis(q.reshape(B, Tq // Q_BLOCK, Q_BLOCK, H, dq), 1, 0)
    scale = dq ** -0.5

    def blk(qi):
        s = jnp.einsum('bqhd,bkhd->bhqk', qi, k).astype(F32) * scale
        return jnp.einsum('bhqk,bkhe->bqhe', jax.nn.softmax(s, axis=-1), v)

    o = lax.map(blk, qb)
    return jnp.moveaxis(o, 0, 1).reshape(B, Tq, H, -1)


def rwkv_scan(r, w, kk, kka, kt, v, S0):
    def step(S, inp):
        r_t, w_t, kk_t, kka_t, k_t, v_t = inp
        sa = jnp.einsum('bhvk,bhk->bhv', S, kk_t)
        S = S * w_t[:, :, None, :] - sa[..., None] * kka_t[:, :, None, :] + v_t[..., None] * k_t[:, :, None, :]
        return S, jnp.einsum('bhvk,bhk->bhv', S, r_t)

    S, y = lax.scan(step, S0, tuple(jnp.moveaxis(t, 1, 0) for t in (r, w, kk, kka, kt, v)))
    return jnp.moveaxis(y, 0, 1), S


def rwkv_mixer(r, k, v, wd, ad, gd, p, S0):
    B, T, _ = r.shape
    g = jax.nn.sigmoid(gd) @ p['rw_g_up']
    kk = heads(k * p['rw_k_k'], RW_H)
    kk = kk * lax.rsqrt(jnp.sum(kk * kk, axis=-1, keepdims=True) + 1e-12)
    rh, vh = heads(r, RW_H), heads(v, RW_H)
    r_k = p['rw_r_k'].reshape(RW_H, RW_N)
    tw = jnp.tanh(wd)
    ys, Ss = [], []
    bonus = jnp.zeros_like(vh)
    for d in range(2):
        fl = lambda t: flip_if(t, d == 1, 1)
        w = jnp.exp(-RW_DECAY_SCALE * jax.nn.sigmoid(p['rw_w0'][d] + tw @ p['rw_w_up'][d]))
        a = jax.nn.sigmoid(p['rw_a0'][d] + ad @ p['rw_a_up'][d])
        kt = heads(k * (1.0 + (a - 1.0) * p['rw_k_a']), RW_H)
        y, S = rwkv_scan(fl(rh), fl(heads(w, RW_H)), fl(kk), fl(kk * heads(a, RW_H)), fl(kt), fl(vh), S0[:, d])
        ys.append(fl(y)); Ss.append(S)
        bonus = bonus + jnp.sum(rh * kt * r_k, axis=-1, keepdims=True) * vh
    y = head_layer_norm(ys[0] + ys[1], p['rw_ln'], RW_LN_EPS) + bonus.reshape(B, T, -1)
    return y * g, jnp.stack(Ss, 1)


def gla_chunked(q, k, v, la, S0):
    B, H, T, _ = q.shape
    L = GLA_CHUNK
    nc = T // L
    rs = lambda t: t.reshape(B, H, nc, L, t.shape[-1])
    q, k, v, la = rs(q), rs(k), rs(v), rs(la)
    b = jnp.cumsum(la, axis=3)
    b_last = b[:, :, :, -1:, :]
    tril = jnp.tril(jnp.ones((L, L), dtype=bool))[:, :, None]
    decay = jnp.exp(jnp.where(tril, b[:, :, :, :, None, :] - b[:, :, :, None, :, :], -jnp.inf))
    A = jnp.einsum('bhcid,bhcjd,bhcijd->bhcij', q, k, decay)
    o_intra = jnp.einsum('bhcij,bhcje->bhcie', A, v)
    q_in = q * jnp.exp(b)
    k_out = k * jnp.exp(b_last - b)
    f_chunk = jnp.exp(b_last[:, :, :, 0, :])

    def step(S, inp):
        qc, kc, vc, fc = inp
        o_inter = jnp.einsum('bhid,bhde->bhie', qc, S)
        S = fc[..., None] * S + jnp.einsum('bhjd,bhje->bhde', kc, vc)
        return S, o_inter

    S, o_inter = lax.scan(step, S0, tuple(jnp.moveaxis(t, 2, 0) for t in (q_in, k_out, v, f_chunk)))
    o = o_intra + jnp.moveaxis(o_inter, 0, 2)
    return o.reshape(B, H, T, -1), S


def gla_mixer(q, k, v, gd, g, p, S0):
    bht = lambda t: jnp.transpose(heads(t, GLA_H), (0, 2, 1, 3))
    qh, kh, vh = bht(q) * GLA_DK ** -0.5, bht(k), bht(v)
    os_, Ss = [], []
    for d in range(2):
        fl = lambda t: flip_if(t, d == 1, 2)
        la = jax.nn.log_sigmoid(gd @ p['gla_g_up'][d] + p['gla_g_b'][d]) / GLA_NORMALIZER
        o, S = gla_chunked(fl(qh), fl(kh), fl(vh), fl(bht(la)), S0[:, d])
        os_.append(fl(o)); Ss.append(S)
    o = jnp.transpose(os_[0] + os_[1], (0, 2, 1, 3))
    return head_rms_norm(o, p['gla_norm']) * jax.nn.silu(g), jnp.stack(Ss, 1)


def mixing(h, p, ctx):
    B, T, _ = h.shape
    (ml_q, ml_k, ml_v, ml_o, ml_g, mla_ql, mla_ckv, mla_kr,
     rw_r, rw_k, rw_v, rw_wd, rw_ad, rw_gd,
     gla_q, gla_k, gla_v, gla_gd, gla_g) = split_cols((h @ p['w_in']).astype(F32))
    latent = ctx is not None
    if latent:
        ctx_kv, ml_C0, ml_n0, ml_m0, rw_S0, gla_S0 = ctx
        ml_st0 = (ml_C0.astype(F32), ml_n0.astype(F32), ml_m0.astype(F32))
        rw_S0, gla_S0 = rw_S0.astype(F32), gla_S0.astype(F32)
    else:
        ml_st0 = (jnp.zeros((B, 2, ML_H, ML_DK, ML_DV), F32), jnp.zeros((B, 2, ML_H, ML_DK), F32),
                  jnp.zeros((B, 2, ML_H), F32))
        rw_S0 = jnp.zeros((B, 2, RW_H, RW_N, RW_N), F32)
        gla_S0 = jnp.zeros((B, 2, GLA_H, GLA_DK, GLA_DV), F32)

    ml_out, ml_st = mlstm_mixer(ml_q, ml_k, ml_v, ml_o, ml_g, p, ml_st0)

    own_kv = jnp.concatenate([rms_norm(mla_ckv, p['g_mla_kvlat']), mla_kr], axis=-1)
    q = rms_norm(heads(rms_norm(mla_ql, p['g_mla_qlat']) @ p['w_mla_uq'], MLA_H), p['g_mla_qn'])
    k, v = mla_keys_values(own_kv, p)
    if latent:
        cos, sin = axial_rope_tables(T)
        q, k = rope_part(q, cos, sin), rope_part(k, cos, sin)
        k_ctx, v_ctx = mla_keys_values(ctx_kv.astype(F32), p)
        k = jnp.concatenate([k, k_ctx], axis=1)
        v = jnp.concatenate([v, v_ctx], axis=1)
    mla_out = attention(q, k, v).reshape(B, T, -1)

    rw_out, rw_st = rwkv_mixer(rw_r, rw_k, rw_v, rw_wd, rw_ad, rw_gd, p, rw_S0)
    gla_out, gla_st = gla_mixer(gla_q, gla_k, gla_v, gla_gd, gla_g, p, gla_S0)

    y = jnp.concatenate([ml_out, mla_out, rw_out, gla_out], axis=-1) @ p['w_out']
    return y, (own_kv, ml_st[0], ml_st[1], ml_st[2], rw_st, gla_st)


def hier_moe(h, p):
    lg = h @ p['moe_w_rg'] + p['moe_b_rg']
    pg = jax.nn.softmax(lg, axis=-1)
    _, grp = lax.top_k(lg, 1)
    grp = grp[..., 0]
    g_oh = jax.nn.one_hot(grp, N_GROUPS, dtype=F32)
    p_grp = jnp.sum(pg * g_oh, axis=-1, keepdims=True)
    le = (h @ p['moe_w_re'] + p['moe_b_re']).reshape(h.shape[:-1] + (N_GROUPS, EXPERTS_PER_GROUP))
    le_grp = jnp.einsum('btge,btg->bte', le, g_oh)
    top_v, top_i = lax.top_k(le_grp, TOP_K_IN_GROUP)
    gates = p_grp * jax.nn.softmax(top_v, axis=-1)
    e_idx = grp[..., None] * EXPERTS_PER_GROUP + top_i
    comb = jnp.einsum('btk,btke->bte', gates, jax.nn.one_hot(e_idx, N_EXPERTS, dtype=F32))
    a = jnp.einsum('btd,edf->btef', h, p['moe_w_gate'])
    b = jnp.einsum('btd,edf->btef', h, p['moe_w_up'])
    return jnp.einsum('btef,efd->btd', jax.nn.silu(a) * b * comb[..., None], p['moe_w_down'])


def block(x, mod, p, ctx):
    sh1, sc1, gt1, sh2, sc2, gt2 = jnp.split(mod, 6, axis=-1)
    h = rms_norm(x, p['g_mix']) * (1.0 + sc1) + sh1
    y, st = mixing(h, p, ctx)
    x = x + (gt1 * y).astype(x.dtype)
    h = rms_norm(x, p['g_ffn']) * (1.0 + sc2) + sh2
    x = x + (gt2 * hier_moe(h, p)).astype(x.dtype)
    return x, st


def setup_inputs(seed: int = 0) -> dict:
    key = jax.random.key(seed)
    keys = iter(jax.random.split(key, 64))

    def nrm(shape, scale=1.0):
        return scale * jax.random.normal(next(keys), shape, F32)

    def gain(shape, base=1.0, noise=0.02):
        return base + nrm(shape, noise)

    L = DEPTH
    ml_gate_base = jnp.tile(jnp.concatenate([jnp.zeros((ML_H,), F32), 3.0 * jnp.ones((ML_H,), F32)]), 2)
    return {
        'x_prompt': nrm((BATCH, SEQ, D_MODEL)),
        'x_sample': nrm((DEC_BATCH, DEC_SEQ, D_MODEL)),
        'cache_mla': nrm((DEC_BATCH, DEPTH, PAST_LEN, KV_LORA + MLA_ROPE)),
        'state_mlstm_C': nrm((DEC_BATCH, DEPTH, 2, ML_H, ML_DK, ML_DV), 0.5),
        'state_mlstm_n': nrm((DEC_BATCH, DEPTH, 2, ML_H, ML_DK), 0.5),
        'state_mlstm_m': nrm((DEC_BATCH, DEPTH, 2, ML_H)),
        'state_rwkv': nrm((DEC_BATCH, DEPTH, 2, RW_H, RW_N, RW_N), 0.3),
        'state_gla': nrm((DEC_BATCH, DEPTH, 2, GLA_H, GLA_DK, GLA_DV), 0.3),
        'c': nrm((DEC_BATCH, D_MODEL)),
        'c_ctx': nrm((D_MODEL,)),
        'w_ada': nrm((L, D_MODEL, 6 * D_MODEL), 0.5 * D_MODEL ** -0.5),
        'b_ada': nrm((L, 6 * D_MODEL), 0.02),
        'g_mix': gain((L, D_MODEL)),
        'g_ffn': gain((L, D_MODEL)),
        'w_in': nrm((L, D_MODEL, IN_COLS), D_MODEL ** -0.5),
        'w_out': nrm((L, MIX_W, D_MODEL), MIX_W ** -0.5),
        'b_ml_gates': ml_gate_base + nrm((L, 4 * ML_H), 0.1),
        'g_ml_norm': gain((L, ML_H * ML_DV)),
        'g_mla_qlat': gain((L, Q_LORA)),
        'g_mla_kvlat': gain((L, KV_LORA)),
        'w_mla_uq': nrm((L, Q_LORA, MLA_H * MLA_QK), Q_LORA ** -0.5),
        'w_mla_ukv': nrm((L, KV_LORA, MLA_H * (MLA_NOPE + MLA_V)), KV_LORA ** -0.5),
        'g_mla_qn': gain((L, MLA_QK)),
        'g_mla_kn': gain((L, MLA_QK)),
        'rw_w0': nrm((L, 2, GROUP_W), 0.5),
        'rw_w_up': nrm((L, 2, RW_LORA_W, GROUP_W), RW_LORA_W ** -0.5),
        'rw_a0': nrm((L, 2, GROUP_W), 0.1),
        'rw_a_up': nrm((L, 2, RW_LORA_A, GROUP_W), RW_LORA_A ** -0.5),
        'rw_g_up': nrm((L, RW_LORA_G, GROUP_W), RW_LORA_G ** -0.5),
        'rw_k_k': gain((L, GROUP_W), 0.85, 0.05),
        'rw_k_a': gain((L, GROUP_W), 1.0, 0.1),
        'rw_r_k': nrm((L, GROUP_W), 0.1),
        'rw_ln': gain((L, GROUP_W)),
        'gla_g_up': nrm((L, 2, GLA_GATE_RANK, GLA_H * GLA_DK), GLA_GATE_RANK ** -0.5),
        'gla_g_b': nrm((L, 2, GLA_H * GLA_DK), 0.5),
        'gla_norm': gain((L, GLA_W)),
        'moe_w_rg': nrm((L, D_MODEL, N_GROUPS), D_MODEL ** -0.5),
        'moe_b_rg': nrm((L, N_GROUPS), 0.01),
        'moe_w_re': nrm((L, D_MODEL, N_EXPERTS), D_MODEL ** -0.5),
        'moe_b_re': nrm((L, N_EXPERTS), 0.01),
        'moe_w_gate': nrm((L, N_EXPERTS, D_MODEL, EXPERT_HIDDEN), D_MODEL ** -0.5),
        'moe_w_up': nrm((L, N_EXPERTS, D_MODEL, EXPERT_HIDDEN), D_MODEL ** -0.5),
        'moe_w_down': nrm((L, N_EXPERTS, EXPERT_HIDDEN, D_MODEL), EXPERT_HIDDEN ** -0.5),
    }


def reference(x_prompt, x_sample, cache_mla, state_mlstm_C, state_mlstm_n, state_mlstm_m, state_rwkv, state_gla,
              c, c_ctx, w_ada, b_ada, g_mix, g_ffn, w_in, w_out, b_ml_gates, g_ml_norm,
              g_mla_qlat, g_mla_kvlat, w_mla_uq, w_mla_ukv, g_mla_qn, g_mla_kn,
              rw_w0, rw_w_up, rw_a0, rw_a_up, rw_g_up, rw_k_k, rw_k_a, rw_r_k, rw_ln,
              gla_g_up, gla_g_b, gla_norm,
              moe_w_rg, moe_b_rg, moe_w_re, moe_b_re, moe_w_gate, moe_w_up, moe_w_down):
    def params_at(l):
        return {
            'w_ada': w_ada[l], 'b_ada': b_ada[l], 'g_mix': g_mix[l], 'g_ffn': g_ffn[l],
            'w_in': w_in[l], 'w_out': w_out[l], 'b_ml_gates': b_ml_gates[l], 'g_ml_norm': g_ml_norm[l],
            'g_mla_qlat': g_mla_qlat[l], 'g_mla_kvlat': g_mla_kvlat[l], 'w_mla_uq': w_mla_uq[l],
            'w_mla_ukv': w_mla_ukv[l], 'g_mla_qn': g_mla_qn[l], 'g_mla_kn': g_mla_kn[l],
            'rw_w0': rw_w0[l], 'rw_w_up': rw_w_up[l], 'rw_a0': rw_a0[l], 'rw_a_up': rw_a_up[l],
            'rw_g_up': rw_g_up[l], 'rw_k_k': rw_k_k[l], 'rw_k_a': rw_k_a[l], 'rw_r_k': rw_r_k[l], 'rw_ln': rw_ln[l],
            'gla_g_up': gla_g_up[l], 'gla_g_b': gla_g_b[l], 'gla_norm': gla_norm[l],
            'moe_w_rg': moe_w_rg[l], 'moe_b_rg': moe_b_rg[l], 'moe_w_re': moe_w_re[l], 'moe_b_re': moe_b_re[l],
            'moe_w_gate': moe_w_gate[l], 'moe_w_up': moe_w_up[l], 'moe_w_down': moe_w_down[l],
        }

    xp = x_prompt
    ctx_states = []
    for l in range(DEPTH):
        p = params_at(l)
        mod = (jax.nn.silu(c_ctx.astype(F32)) @ p['w_ada'] + p['b_ada'])[None, None, :]
        xp, st = block(xp, mod, p, None)
        ctx_states.append(st)

    xs = x_sample
    for l in range(DEPTH):
        p = params_at(l)
        mod = (jax.nn.silu(c.astype(F32)) @ p['w_ada'] + p['b_ada'])[:, None, :]
        ctx = (cache_mla[:, l], state_mlstm_C[:, l], state_mlstm_n[:, l], state_mlstm_m[:, l],
               state_rwkv[:, l], state_gla[:, l])
        xs, _ = block(xs, mod, p, ctx)

    new_cache_mla = jnp.stack([s[0] for s in ctx_states], axis=1)
    new_state_mlstm_C = jnp.stack([s[1] for s in ctx_states], axis=1)
    new_state_mlstm_n = jnp.stack([s[2] for s in ctx_states], axis=1)
    new_state_mlstm_m = jnp.stack([s[3] for s in ctx_states], axis=1)
    new_state_rwkv = jnp.stack([s[4] for s in ctx_states], axis=1)
    new_state_gla = jnp.stack([s[5] for s in ctx_states], axis=1)
    return (xp, xs, new_cache_mla, new_state_mlstm_C, new_state_mlstm_n, new_state_mlstm_m, new_state_rwkv, new_state_gla)
```

```python
import functools
import math

import numpy as np
import jax
import jax.numpy as jnp
from jax import lax
from jax.experimental import pallas as pl
from jax.experimental.pallas import tpu as pltpu

F32 = jnp.float32
BF16 = jnp.bfloat16
HI = lax.Precision.HIGHEST

D_MODEL = 2048
DEPTH = 2
GRID_W = 64
GROUP_W = 512
ML_H, ML_DK = 4, 128
MLA_H, MLA_NOPE, MLA_ROPE, MLA_V = 4, 128, 64, 128
MLA_QK = MLA_NOPE + MLA_ROPE
Q_LORA, KV_LORA = 384, 256
ROPE_THETA = 10000.0
RW_H, RW_N = 8, 64
RW_DECAY_SCALE = math.exp(-0.5)
RW_LN_EPS = 64e-5
GLA_H, GLA_DK, GLA_DV = 4, 64, 128
GLA_GATE_RANK = 16
GLA_NORMALIZER = 16.0
N_GROUPS, EXPERTS_PER_GROUP, N_EXPERTS = 4, 4, 16
EXPERT_HIDDEN = 512
NORM_EPS = 1e-6
LANES = 128
VMEM_LIMIT = 56 * 1024 * 1024

_REF_SPLITS = (
    ('ml_q', 512), ('ml_k', 512), ('ml_v', 512), ('ml_o', 512), ('ml_g', 16),
    ('mla_ql', Q_LORA), ('mla_ckv', KV_LORA), ('mla_kr', MLA_ROPE),
    ('rw_r', 512), ('rw_k', 512), ('rw_v', 512), ('rw_wd', 64), ('rw_ad', 64), ('rw_gd', 128),
    ('gla_q', 256), ('gla_k', 256), ('gla_v', 512), ('gla_gd', GLA_GATE_RANK), ('gla_g', 512),
)
_REF_OFF = {}
_o = 0
for _n, _w in _REF_SPLITS:
    _REF_OFF[_n] = (_o, _w)
    _o += _w
IN_COLS = _o

_PACKED = (
    ('ml_q', 512), ('ml_k', 512), ('ml_v', 512), ('ml_o', 512),
    ('rw_r', 512), ('rw_k', 512), ('rw_v', 512), ('gla_v', 512), ('gla_g', 512),
    ('mla_ql', 384), ('mla_ckv', 256), ('mla_kr', 64), ('mla_kr_sw', 64),
    ('gla_q', 256), ('gla_k', 256),
    ('ml_g', 128), ('rw_wd', 64), ('rw_ad', 64), ('rw_gd', 128), ('gla_gd', 128),
)
PK_OFF = {}
_o = 0
for _n, _w in _PACKED:
    PK_OFF[_n] = _o
    _o += _w
PK_COLS = _o


def _rope_swap_perm():
    idx = np.arange(MLA_ROPE)
    axis, half, f = idx // 32, (idx % 32) // 16, idx % 16
    return axis * 32 + (1 - half) * 16 + f


def _packed_column_index():
    src = np.full((PK_COLS,), -1, np.int64)
    for name, width in _PACKED:
        off = PK_OFF[name]
        if name == 'mla_kr_sw':
            s, w = _REF_OFF['mla_kr']
            src[off:off + w] = s + _rope_swap_perm()
        else:
            s, w = _REF_OFF[name]
            src[off:off + w] = s + np.arange(w)
    return src


_PK_SRC = _packed_column_index()


def _pack_w_in(w):
    valid = jnp.asarray(_PK_SRC >= 0)
    cols = jnp.take(w, jnp.asarray(np.maximum(_PK_SRC, 0)), axis=1)
    return jnp.where(valid[None, :], cols, 0.0).astype(BF16)


def _cparams(sem):
    return pltpu.CompilerParams(dimension_semantics=sem, vmem_limit_bytes=VMEM_LIMIT)


def _log_sigmoid(x):
    return jnp.minimum(x, 0.0) - jnp.log(1.0 + jnp.exp(-jnp.abs(x)))


def _sigmoid(x):
    return 1.0 / (1.0 + jnp.exp(-x))


def _dot(a, b):
    return jnp.dot(a.astype(BF16), b.astype(BF16), preferred_element_type=F32)


def _dot_nt(a, b):
    return lax.dot_general(a.astype(BF16), b.astype(BF16), (((1,), (1,)), ((), ())), preferred_element_type=F32)


def _dot_tn(a, b):
    return lax.dot_general(a.astype(BF16), b.astype(BF16), (((0,), (0,)), ((), ())), preferred_element_type=F32)


def _dot_hi(a, b):
    return jnp.dot(a, b, precision=HI, preferred_element_type=F32)


def _mod_kernel(c_ref, w_ref, b_ref, o_ref):
    c = c_ref[...]
    s = c * _sigmoid(c)
    o_ref[0] = _dot(s, w_ref[0]) + b_ref[0]


def _modulation(cc, w_ada, b_ada):
    tn = 1536
    n = 6 * D_MODEL
    return pl.pallas_call(
        _mod_kernel,
        grid=(DEPTH, n // tn),
        in_specs=[pl.BlockSpec((8, D_MODEL), lambda l, j: (0, 0)),
                  pl.BlockSpec((1, D_MODEL, tn), lambda l, j: (l, 0, j)),
                  pl.BlockSpec((1, 1, tn), lambda l, j: (l, 0, j))],
        out_specs=pl.BlockSpec((1, 8, tn), lambda l, j: (l, 0, j)),
        out_shape=jax.ShapeDtypeStruct((DEPTH, 8, n), F32),
        compiler_params=_cparams(("arbitrary", "arbitrary")),
        name="adaln_mod",
    )(cc, w_ada, b_ada.reshape(DEPTH, 1, n))


def _inproj_kernel(x_ref, mod_ref, g_ref, w_ref, o_ref, h_scr):
    @pl.when(pl.program_id(1) == 0)
    def _():
        x = x_ref[...]
        xn = x * lax.rsqrt(jnp.mean(x * x, axis=-1, keepdims=True) + NORM_EPS) * g_ref[...]
        h_scr[...] = (xn * (1.0 + mod_ref[0, 1:2, :]) + mod_ref[0, 0:1, :]).astype(BF16)

    o_ref[...] = jnp.dot(h_scr[...], w_ref[...], preferred_element_type=F32)


def _in_proj(x2, mod, g, w_packed, T):
    n_tok = x2.shape[0]
    tm = 1024 if T >= 1024 else 512
    tn = 640
    bm = mod.shape[0]
    mod_idx = (lambda i, j: (i * tm // T, 0, 0)) if bm > 1 else (lambda i, j: (0, 0, 0))
    return pl.pallas_call(
        _inproj_kernel,
        grid=(n_tok // tm, PK_COLS // tn),
        in_specs=[pl.BlockSpec((tm, D_MODEL), lambda i, j: (i, 0)),
                  pl.BlockSpec((1, 8, D_MODEL), mod_idx),
                  pl.BlockSpec((1, D_MODEL), lambda i, j: (0, 0)),
                  pl.BlockSpec((D_MODEL, tn), lambda i, j: (0, j))],
        out_specs=pl.BlockSpec((tm, tn), lambda i, j: (i, j)),
        out_shape=jax.ShapeDtypeStruct((n_tok, PK_COLS), F32),
        scratch_shapes=[pltpu.VMEM((tm, D_MODEL), BF16)],
        compiler_params=_cparams(("arbitrary", "arbitrary")),
        name="in_proj",
    )(x2, mod, g, w_packed)


ML_CHUNK = 256


def _mlstm_kernel(q_ref, k_ref, v_ref, o_ref, g_ref, bias_ref, gn_ref, c0_ref, m0_ref,
                  out_ref, c_ref, m_ref, hs_ref, *, T):
    L = ML_CHUNK
    nc = T // L
    c_ref[...] = c0_ref[...]
    m_ref[...] = m0_ref[...]
    ii = lax.broadcasted_iota(jnp.int32, (L, L), 0)
    jj = lax.broadcasted_iota(jnp.int32, (L, L), 1)
    lane = lax.broadcasted_iota(jnp.int32, (1, LANES), 1)
    is_f = jnp.logical_and(lane % 8 >= 4, lane < 16)
    ones_col = (lax.broadcasted_iota(jnp.int32, (L, LANES), 1) == 0).astype(BF16)
    scale = ML_DK ** -0.5

    for d in range(2):
        mask = (jj <= ii) if d == 0 else (jj >= ii)
        tri = mask.astype(F32)

        def chunk(ci, carry, d=d, mask=mask, tri=tri):
            c = ci if d == 0 else nc - 1 - ci
            s = pl.multiple_of(c * L, L)
            gates = g_ref[0, pl.ds(s, L), :] + bias_ref[...]
            gf = jnp.where(is_f, _log_sigmoid(gates), gates)
            cum = _dot_hi(tri, gf)
            gf_t = gf.T
            cum_t = cum.T
            for h in range(ML_H):
                ci_, cf_ = d * 8 + h, d * 8 + 4 + h
                hs = slice(h * ML_DK, (h + 1) * ML_DK)
                ig_col, ig_row = gf[:, ci_:ci_ + 1], gf_t[ci_:ci_ + 1, :]
                b_col, b_row = cum[:, cf_:cf_ + 1], cum_t[cf_:cf_ + 1, :]
                b_last = b_col[L - 1:L, :] if d == 0 else b_col[0:1, :]
                m_prev = m_ref[0, d, h][:, 0:1]
                dmat = jnp.where(mask, b_col + (ig_row - b_row), -jnp.inf)
                m_inter = b_col + m_prev
                m_row = jnp.maximum(m_inter, jnp.max(dmat, axis=-1, keepdims=True))
                w_inter = jnp.exp(m_inter - m_row)
                q = (q_ref[0, pl.ds(s, L), hs] * scale).astype(BF16)
                k = k_ref[0, pl.ds(s, L), hs]
                v_aug = jnp.concatenate([v_ref[0, pl.ds(s, L), hs].astype(BF16), ones_col], axis=1)
                smat = _dot_nt(q, k) * jnp.exp(dmat - m_row)
                c_aug = c_ref[0, d, h]
                nd = w_inter * _dot(q, c_aug) + _dot(smat, v_aug)
                num, den = nd[:, :ML_DK], nd[:, ML_DK:ML_DK + 1]
                hh = num / jnp.maximum(jnp.abs(den), jnp.exp(-m_row))
                if d == 0:
                    hs_ref[pl.ds(s, L), hs] = hh
                else:
                    hs_ref[pl.ds(s, L), hs] = hs_ref[pl.ds(s, L), hs] + hh
                dk_col = b_last - b_col + ig_col
                m_new = jnp.maximum(b_last + m_prev, jnp.max(dk_col, axis=0, keepdims=True))
                w_key = jnp.exp(dk_col - m_new)
                c_scale = jnp.exp(b_last + m_prev - m_new)
                c_ref[0, d, h] = c_scale * c_aug + _dot_tn(k * w_key, v_aug)
                m_ref[0, d, h] = jnp.broadcast_to(m_new, (1, LANES))
            return carry

        lax.fori_loop(0, nc, chunk, 0)

    def finish(ci, carry):
        s = pl.multiple_of(ci * L, L)
        for h in range(ML_H):
            hs = slice(h * ML_DK, (h + 1) * ML_DK)
            x = hs_ref[pl.ds(s, L), hs]
            xc = x - jnp.mean(x, axis=-1, keepdims=True)
            y = xc * lax.rsqrt(jnp.mean(xc * xc, axis=-1, keepdims=True) + NORM_EPS) * gn_ref[:, hs]
            out_ref[0, pl.ds(s, L), hs] = (y * _sigmoid(o_ref[0, pl.ds(s, L), hs])).astype(BF16)
        return carry

    lax.fori_loop(0, nc, finish, 0)


def _mlstm_params(p):
    bias = jnp.zeros((1, LANES), F32).at[0, :16].set(p['b_ml_gates'])
    return bias, p['g_ml_norm'].reshape(1, 512)


def _mlstm_state_in(C0, n0, m0):
    c0 = jnp.concatenate([C0, n0[..., None], jnp.zeros(C0.shape[:-1] + (ML_DK - 1,), F32)], axis=-1)
    return c0, jnp.broadcast_to(m0[..., None, None], m0.shape + (1, LANES))


def _mlstm_state_zero(B):
    return jnp.zeros((B, 2, ML_H, ML_DK, 2 * ML_DK), F32), jnp.zeros((B, 2, ML_H, 1, LANES), F32)


def _mlstm_state_out(c, m):
    return c[..., :ML_DK], c[..., ML_DK], m[..., 0, 0]


def _mlstm(u3, bias, gnorm, c0, m0):
    B, T, _ = u3.shape
    blk = lambda name: pl.BlockSpec((1, T, 512), lambda b, o=PK_OFF[name] // 512: (b, 0, o))
    st_c = pl.BlockSpec((1, 2, ML_H, ML_DK, 2 * ML_DK), lambda b: (b, 0, 0, 0, 0))
    st_m = pl.BlockSpec((1, 2, ML_H, 1, LANES), lambda b: (b, 0, 0, 0, 0))
    return pl.pallas_call(
        functools.partial(_mlstm_kernel, T=T),
        grid=(B,),
        in_specs=[blk('ml_q'), blk('ml_k'), blk('ml_v'), blk('ml_o'),
                  pl.BlockSpec((1, T, LANES), lambda b: (b, 0, PK_OFF['ml_g'] // LANES)),
                  pl.BlockSpec((1, LANES), lambda b: (0, 0)),
                  pl.BlockSpec((1, 512), lambda b: (0, 0)),
                  st_c, st_m],
        out_specs=[pl.BlockSpec((1, T, 512), lambda b: (b, 0, 0)), st_c, st_m],
        out_shape=[jax.ShapeDtypeStruct((B, T, 512), BF16),
                   jax.ShapeDtypeStruct(c0.shape, F32),
                   jax.ShapeDtypeStruct(m0.shape, F32)],
        scratch_shapes=[pltpu.VMEM((T, 512), F32)],
        compiler_params=_cparams(("arbitrary",)),
        name="mlstm",
    )(u3, u3, u3, u3, u3, bias, gnorm, c0, m0)


MLA_BLK = 256


def _rope_tables(T):
    rows = T // GRID_W
    row = np.repeat(np.arange(rows, dtype=np.float64), GRID_W)
    col = np.tile(np.arange(GRID_W, dtype=np.float64), rows)
    inv = ROPE_THETA ** (-np.arange(MLA_ROPE // 4, dtype=np.float64) / (MLA_ROPE // 4))
    ang = np.stack([row[:, None] * inv, col[:, None] * inv], axis=1)
    cos = np.stack([np.cos(ang), np.cos(ang)], axis=2).reshape(T, MLA_ROPE)
    sin = np.stack([-np.sin(ang), np.sin(ang)], axis=2).reshape(T, MLA_ROPE)
    return jnp.asarray(np.concatenate([cos, sin], axis=1), F32)


def _mla_kernel(*refs, T, n_ctx, rope):
    if n_ctx:
        (u_ref, ctx_ref, cs_ref, gql_ref, gkv_ref, gains_ref, wqn_ref, wqr_ref, wqs_ref, wkn_ref, wv_ref,
         out_ref, kv_ref, qn_s, qr_s, kn_s, kr_s, v_s) = refs
    else:
        (u_ref, cs_ref, gql_ref, gkv_ref, gains_ref, wqn_ref, wqr_ref, wqs_ref, wkn_ref, wv_ref,
         out_ref, kv_ref, qn_s, qr_s, kn_s, kr_s, v_s) = refs
    Lb = MLA_BLK
    gq_n, gq_r, gq_s = gains_ref[0:1, :], gains_ref[1:2, 0:64], gains_ref[1:2, 64:128]
    gk_n, gk_r, gk_s = gains_ref[2:3, :], gains_ref[3:4, 0:64], gains_ref[3:4, 64:128]
    sm_scale = MLA_QK ** -0.5

    def store_keys(s, kn, kr, krs, cos, sin):
        kr_ss = jnp.sum(kr * kr, axis=-1, keepdims=True)
        for h in range(MLA_H):
            kn_h = kn[:, h * 128:(h + 1) * 128]
            rk = lax.rsqrt((jnp.sum(kn_h * kn_h, axis=-1, keepdims=True) + kr_ss) / MLA_QK + NORM_EPS)
            kn_s[pl.ds(s, Lb), h * 128:(h + 1) * 128] = (kn_h * rk * gk_n).astype(BF16)
            kr_h = kr * gk_r
            if cos is not None:
                kr_h = kr_h * cos + (krs * gk_s) * sin
            kr_s[pl.ds(s, Lb), h * 64:(h + 1) * 64] = (kr_h * rk).astype(BF16)

    def prep(ci, carry):
        s = pl.multiple_of(ci * Lb, Lb)
        u = u_ref[0, pl.ds(s, Lb), :]
        ql, ckv, kr, krs = u[:, :384], u[:, 384:640], u[:, 640:704], u[:, 704:768]
        qln = ql * lax.rsqrt(jnp.mean(ql * ql, axis=-1, keepdims=True) + NORM_EPS) * gql_ref[...]
        ckvn = ckv * lax.rsqrt(jnp.mean(ckv * ckv, axis=-1, keepdims=True) + NORM_EPS) * gkv_ref[...]
        kv_ref[0, pl.ds(s, Lb), :] = jnp.concatenate([ckvn, kr], axis=1)
        cos = cs_ref[pl.ds(s, Lb), 0:64] if rope else None
        sin = cs_ref[pl.ds(s, Lb), 64:128] if rope else None
        qn, qr, qs = _dot(qln, wqn_ref[...]), _dot(qln, wqr_ref[...]), _dot(qln, wqs_ref[...])
        for h in range(MLA_H):
            qn_h, qr_h = qn[:, h * 128:(h + 1) * 128], qr[:, h * 64:(h + 1) * 64]
            ss = jnp.sum(qn_h * qn_h, axis=-1, keepdims=True) + jnp.sum(qr_h * qr_h, axis=-1, keepdims=True)
            rq = lax.rsqrt(ss / MLA_QK + NORM_EPS) * sm_scale
            qn_s[pl.ds(s, Lb), h * 128:(h + 1) * 128] = (qn_h * rq * gq_n).astype(BF16)
            qr_h = qr_h * gq_r
            if rope:
                qr_h = qr_h * cos + (qs[:, h * 64:(h + 1) * 64] * gq_s) * sin
            qr_s[pl.ds(s, Lb), h * 64:(h + 1) * 64] = (qr_h * rq).astype(BF16)
        v_s[pl.ds(s, Lb), :] = _dot(ckvn, wv_ref[...]).astype(BF16)
        store_keys(s, _dot(ckvn, wkn_ref[...]), kr, krs, cos, sin)
        return carry

    lax.fori_loop(0, T // Lb, prep, 0)

    for ci in range(n_ctx // Lb):
        cx = ctx_ref[0, ci * Lb:(ci + 1) * Lb, :]
        ckv_c, kr_c = cx[:, :KV_LORA], cx[:, KV_LORA:KV_LORA + MLA_ROPE]
        v_s[T + ci * Lb:T + (ci + 1) * Lb, :] = _dot(ckv_c, wv_ref[...]).astype(BF16)
        store_keys(T + ci * Lb, _dot(ckv_c, wkn_ref[...]), kr_c, None, None, None)

    for h in range(MLA_H):
        def attend(qi, carry, h=h):
            s = pl.multiple_of(qi * Lb, Lb)
            sc = (_dot_nt(qn_s[pl.ds(s, Lb), h * 128:(h + 1) * 128], kn_s[:, h * 128:(h + 1) * 128])
                  + _dot_nt(qr_s[pl.ds(s, Lb), h * 64:(h + 1) * 64], kr_s[:, h * 64:(h + 1) * 64]))
            p = jnp.exp(sc - jnp.max(sc, axis=-1, keepdims=True))
            o = _dot(p, v_s[:, h * 128:(h + 1) * 128]) / jnp.sum(p, axis=-1, keepdims=True)
            out_ref[0, pl.ds(s, Lb), h * 128:(h + 1) * 128] = o.astype(BF16)
            return carry

        lax.fori_loop(0, T // Lb, attend, 0)


def _mla_params(p):
    wq = p['w_mla_uq'].reshape(Q_LORA, MLA_H, MLA_QK)
    sw = _rope_swap_perm()
    wq_n = wq[:, :, :MLA_NOPE].reshape(Q_LORA, 512).astype(BF16)
    wq_r = wq[:, :, MLA_NOPE:].reshape(Q_LORA, 256).astype(BF16)
    wq_s = wq[:, :, MLA_NOPE:][:, :, sw].reshape(Q_LORA, 256).astype(BF16)
    wkv = p['w_mla_ukv'].reshape(KV_LORA, MLA_H, MLA_NOPE + MLA_V)
    wk_n = wkv[:, :, :MLA_NOPE].reshape(KV_LORA, 512).astype(BF16)
    wv = wkv[:, :, MLA_NOPE:].reshape(KV_LORA, 512).astype(BF16)
    gq, gk = p['g_mla_qn'], p['g_mla_kn']
    gains = jnp.zeros((8, LANES), F32)
    gains = gains.at[0].set(gq[:128]).at[1, :64].set(gq[128:]).at[1, 64:].set(gq[128:][sw])
    gains = gains.at[2].set(gk[:128]).at[3, :64].set(gk[128:]).at[3, 64:].set(gk[128:][sw])
    return (p['g_mla_qlat'].reshape(1, -1), p['g_mla_kvlat'].reshape(1, -1), gains, wq_n, wq_r, wq_s, wk_n, wv)


def _mla(u3, ctx_kv, mp):
    B, T, _ = u3.shape
    n_ctx = 0 if ctx_kv is None else ctx_kv.shape[1]
    rope = ctx_kv is not None
    tk = T + n_ctx
    full = lambda a: pl.BlockSpec(a.shape, lambda b, n=a.ndim: (0,) * n)
    cs = _rope_tables(T) if rope else jnp.zeros((T, LANES), F32)
    ins = [u3] + ([ctx_kv] if rope else []) + [cs] + list(mp)
    specs = [pl.BlockSpec((1, T, 768), lambda b: (b, 0, PK_OFF['mla_ql'] // 768))]
    if rope:
        specs.append(pl.BlockSpec((1, n_ctx, KV_LORA + MLA_ROPE), lambda b: (b, 0, 0)))
    specs += [full(a) for a in ins[len(specs):]]
    return pl.pallas_call(
        functools.partial(_mla_kernel, T=T, n_ctx=n_ctx, rope=rope),
        grid=(B,),
        in_specs=specs,
        out_specs=[pl.BlockSpec((1, T, 512), lambda b: (b, 0, 0)),
                   pl.BlockSpec((1, T, KV_LORA + MLA_ROPE), lambda b: (b, 0, 0))],
        out_shape=[jax.ShapeDtypeStruct((B, T, 512), BF16),
                   jax.ShapeDtypeStruct((B, T, KV_LORA + MLA_ROPE), F32)],
        scratch_shapes=[pltpu.VMEM((T, 512), BF16), pltpu.VMEM((T, 256), BF16),
                        pltpu.VMEM((tk, 512), BF16), pltpu.VMEM((tk, 256), BF16), pltpu.VMEM((tk, 512), BF16)],
        compiler_params=_cparams(("arbitrary",)),
        name="mla",
    )(*ins)


GLA_CHUNK = 64
GLA_SUB = 16


def _gla_kernel(q_ref, k_ref, v_ref, gd_ref, g_ref, gup_ref, gb_ref, gn_ref, s0_ref, out_ref, s_ref, os_ref, *, T):
    L, C = GLA_CHUNK, GLA_SUB
    nc, nsub = T // L, L // C
    s_ref[...] = s0_ref[...]
    ii = lax.broadcasted_iota(jnp.int32, (L, L), 0)
    jj = lax.broadcasted_iota(jnp.int32, (L, L), 1)
    eye = (lax.broadcasted_iota(jnp.int32, (GLA_DK, GLA_DK), 0)
           == lax.broadcasted_iota(jnp.int32, (GLA_DK, GLA_DK), 1)).astype(F32)
    row_id = lax.broadcasted_iota(jnp.int32, (L, 1), 0)
    scale = GLA_DK ** -0.5

    for d in range(2):
        causal = (jj <= ii) if d == 0 else (jj >= ii)
        tri = causal.astype(F32)

        def chunk(ci, carry, d=d, causal=causal, tri=tri):
            c = ci if d == 0 else nc - 1 - ci
            s = pl.multiple_of(c * L, L)
            la = _log_sigmoid(_dot_hi(gd_ref[0, pl.ds(s, L), :], gup_ref[d]) + gb_ref[d]) / GLA_NORMALIZER
            b = _dot_hi(tri, la)
            total = b[L - 1:L, :] if d == 0 else b[0:1, :]
            q = q_ref[0, pl.ds(s, L), :] * scale
            k = k_ref[0, pl.ds(s, L), :]
            zero_row = jnp.zeros_like(total)
            a_rows = [[] for _ in range(GLA_H)]
            for i in range(nsub):
                if d == 0:
                    ref = b[i * C - 1:i * C, :] if i > 0 else zero_row
                    seen = row_id < (i + 1) * C
                else:
                    ref = b[(i + 1) * C:(i + 1) * C + 1, :] if i < nsub - 1 else zero_row
                    seen = row_id >= i * C
                qs = q[i * C:(i + 1) * C, :] * jnp.exp(b[i * C:(i + 1) * C, :] - ref)
                kt = k * jnp.exp(jnp.where(seen, ref - b, 0.0))
                for h in range(GLA_H):
                    ks = slice(h * GLA_DK, (h + 1) * GLA_DK)
                    a = _dot_nt(qs[:, ks], kt[:, ks])
                    a_rows[h].append(jnp.where(causal[i * C:(i + 1) * C, :], a, 0.0))
            q_in = q * jnp.exp(b)
            k_out = k * jnp.exp(total - b)
            f_row = jnp.exp(total)
            for h in range(GLA_H):
                ks = slice(h * GLA_DK, (h + 1) * GLA_DK)
                vs = slice(h * GLA_DV, (h + 1) * GLA_DV)
                v = v_ref[0, pl.ds(s, L), vs]
                st = s_ref[0, d, h]
                o = _dot(jnp.concatenate(a_rows[h], axis=0), v) + _dot(q_in[:, ks], st)
                if d == 0:
                    os_ref[pl.ds(s, L), vs] = o
                else:
                    os_ref[pl.ds(s, L), vs] = os_ref[pl.ds(s, L), vs] + o
                f_col = jnp.sum(eye * f_row[:, ks], axis=1, keepdims=True)
                s_ref[0, d, h] = f_col * st + _dot_tn(k_out[:, ks], v)
            return carry

        lax.fori_loop(0, nc, chunk, 0)

    Lf = 256

    def finish(ci, carry):
        s = pl.multiple_of(ci * Lf, Lf)
        for h in range(GLA_H):
            vs = slice(h * GLA_DV, (h + 1) * GLA_DV)
            o = os_ref[pl.ds(s, Lf), vs]
            y = o * lax.rsqrt(jnp.mean(o * o, axis=-1, keepdims=True) + NORM_EPS) * gn_ref[:, vs]
            g = g_ref[0, pl.ds(s, Lf), vs]
            out_ref[0, pl.ds(s, Lf), vs] = (y * (g * _sigmoid(g))).astype(BF16)
        return carry

    lax.fori_loop(0, T // Lf, finish, 0)


def _gla_params(p):
    gup = jnp.zeros((2, LANES, GLA_H * GLA_DK), F32).at[:, :GLA_GATE_RANK, :].set(p['gla_g_up'])
    return gup, p['gla_g_b'].reshape(2, 1, -1), p['gla_norm'].reshape(1, -1)


def _gla(u3, gp, s0):
    B, T, _ = u3.shape
    blk = lambda name, w: pl.BlockSpec((1, T, w), lambda b, o=PK_OFF[name] // w: (b, 0, o))
    full = lambda a: pl.BlockSpec(a.shape, lambda b, n=a.ndim: (0,) * n)
    st = pl.BlockSpec((1, 2, GLA_H, GLA_DK, GLA_DV), lambda b: (b, 0, 0, 0, 0))
    return pl.pallas_call(
        functools.partial(_gla_kernel, T=T),
        grid=(B,),
        in_specs=[blk('gla_q', 256), blk('gla_k', 256), blk('gla_v', 512), blk('gla_gd', 128), blk('gla_g', 512),
                  full(gp[0]), full(gp[1]), full(gp[2]), st],
        out_specs=[pl.BlockSpec((1, T, 512), lambda b: (b, 0, 0)), st],
        out_shape=[jax.ShapeDtypeStruct((B, T, 512), BF16), jax.ShapeDtypeStruct(s0.shape, F32)],
        scratch_shapes=[pltpu.VMEM((T, 512), F32)],
        compiler_params=_cparams(("arbitrary",)),
        name="gla",
    )(u3, u3, u3, u3, u3, *gp, s0)


RW_CHUNK = 64


def _seg_sum(x, bd):
    hi = x.astype(BF16)
    lo = (x - hi.astype(F32)).astype(BF16)
    return jnp.dot(hi, bd, preferred_element_type=F32) + jnp.dot(lo, bd, preferred_element_type=F32)


def _rwkv_kernel(r_ref, k_ref, v_ref, wa_ref, gd_ref, wwa_ref, w0a0_ref, gup_ref, kk_ref, ka_ref, rk_ref, ln_ref,
                 bd_ref, h0_ref, out_ref, h_ref, ys_ref, bonus_ref, *, T):
    L, N = RW_CHUNK, RW_N
    nc = T // L
    h_ref[...] = h0_ref[...]
    ii = lax.broadcasted_iota(jnp.int32, (L, L), 0)
    jj = lax.broadcasted_iota(jnp.int32, (L, L), 1)
    eye = (ii == jj).astype(F32)
    lane = lax.broadcasted_iota(jnp.int32, (1, LANES), 1)
    bd = bd_ref[...]

    for d in range(2):
        strict = (jj < ii) if d == 0 else (jj > ii)
        incl = (jj <= ii) if d == 0 else (jj >= ii)
        tri = incl.astype(F32)

        def chunk(ci, carry, d=d, strict=strict, incl=incl, tri=tri):
            c = ci if d == 0 else nc - 1 - ci
            s = pl.multiple_of(c * L, L)
            r = r_ref[0, pl.ds(s, L), :]
            k = k_ref[0, pl.ds(s, L), :]
            v = v_ref[0, pl.ds(s, L), :]
            wa = wa_ref[0, pl.ds(s, L), :]
            pre = _dot_hi(jnp.where(lane < 64, jnp.tanh(wa), wa), wwa_ref[d]) + w0a0_ref[d]
            logw = -RW_DECAY_SCALE * _sigmoid(pre[:, :512])
            a = _sigmoid(pre[:, 512:])
            kkr = k * kk_ref[...]
            kk = kkr * lax.rsqrt(_seg_sum(kkr * kkr, bd) + 1e-12)
            kt = k * (1.0 + (a - 1.0) * ka_ref[...])
            bh = kk * a
            bonus = _seg_sum(r * kt * rk_ref[...], bd) * v
            if d == 0:
                bonus_ref[pl.ds(s, L), :] = bonus
            else:
                bonus_ref[pl.ds(s, L), :] = bonus_ref[pl.ds(s, L), :] + bonus
            lg = _dot_hi(tri, logw)
            lg_end = lg[L - 1:L, :] if d == 0 else lg[0:1, :]
            a_t = -kk * jnp.exp(lg - logw)
            r_t = r * jnp.exp(lg)
            e_inv = jnp.exp(-lg)
            k_t, b_t = kt * e_inv, bh * e_inv
            e_end = jnp.exp(lg_end - lg)
            k_e, b_e = kt * e_end, bh * e_end
            g_end = jnp.exp(lg_end)
            for h in range(RW_H):
                sl = slice(h * N, (h + 1) * N)
                m = _dot_nt(jnp.concatenate([a_t[:, sl], r_t[:, sl]], axis=0),
                            jnp.concatenate([b_t[:, sl], k_t[:, sl]], axis=0))
                n_ab = jnp.where(strict, m[:L, :L], 0.0)
                m_ak = jnp.where(strict, m[:L, L:], 0.0)
                m_rb = jnp.where(incl, m[L:, :L], 0.0)
                m_rk = jnp.where(incl, m[L:, L:], 0.0)
                x = eye + n_ab
                pw = n_ab
                for _ in range(5):
                    pw = _dot_hi(pw, pw)
                    x = x + _dot_hi(x, pw)
                vh = v[:, sl]
                tw = _dot_hi(x, jnp.concatenate([a_t[:, sl], _dot(m_ak, vh)], axis=1))
                qy = _dot(m_rb, tw)
                q_eff = r_t[:, sl] + qy[:, :N]
                y_loc = _dot(m_rk, vh) + qy[:, N:]
                pg = _dot_tn(b_e[:, sl], tw)
                p_mat = eye * g_end[:, sl] + pg[:, :N]
                g_mat = _dot_tn(k_e[:, sl], vh) + pg[:, N:]
                hst = h_ref[0, d, h]
                y = _dot(q_eff, hst) + y_loc
                if d == 0:
                    ys_ref[pl.ds(s, L), sl] = y
                else:
                    ys_ref[pl.ds(s, L), sl] = ys_ref[pl.ds(s, L), sl] + y
                h_ref[0, d, h] = _dot_hi(p_mat, hst) + g_mat
            return carry

        lax.fori_loop(0, nc, chunk, 0)

    Lf = 256

    def finish(ci, carry):
        s = pl.multiple_of(ci * Lf, Lf)
        y = ys_ref[pl.ds(s, Lf), :]
        yc = y - _seg_sum(y, bd) / N
        yn = yc * lax.rsqrt(_seg_sum(yc * yc, bd) / N + RW_LN_EPS) * ln_ref[...]
        g = _dot(_sigmoid(gd_ref[0, pl.ds(s, Lf), :]), gup_ref[...])
        out_ref[0, pl.ds(s, Lf), :] = ((yn + bonus_ref[pl.ds(s, Lf), :]) * g).astype(BF16)
        return carry

    lax.fori_loop(0, T // Lf, finish, 0)


def _rwkv_params(p):
    wwa = jnp.zeros((2, LANES, 1024), F32)
    wwa = wwa.at[:, :64, :512].set(p['rw_w_up']).at[:, 64:, 512:].set(p['rw_a_up'])
    w0a0 = jnp.concatenate([p['rw_w0'], p['rw_a0']], axis=-1).reshape(2, 1, 1024)
    seg = np.arange(512) // RW_N
    bd = jnp.asarray(seg[:, None] == seg[None, :], BF16)
    row = lambda n: p[n].reshape(1, -1)
    return (wwa, w0a0, p['rw_g_up'].astype(BF16), row('rw_k_k'), row('rw_k_a'), row('rw_r_k'), row('rw_ln'), bd)


def _rwkv(u3, rp, h0):
    B, T, _ = u3.shape
    blk = lambda name, w: pl.BlockSpec((1, T, w), lambda b, o=PK_OFF[name] // w: (b, 0, o))
    full = lambda a: pl.BlockSpec(a.shape, lambda b, n=a.ndim: (0,) * n)
    st = pl.BlockSpec((1, 2, RW_H, RW_N, RW_N), lambda b: (b, 0, 0, 0, 0))
    return pl.pallas_call(
        functools.partial(_rwkv_kernel, T=T),
        grid=(B,),
        in_specs=[blk('rw_r', 512), blk('rw_k', 512), blk('rw_v', 512), blk('rw_wd', 128), blk('rw_gd', 128)]
                 + [full(a) for a in rp] + [st],
        out_specs=[pl.BlockSpec((1, T, 512), lambda b: (b, 0, 0)), st],
        out_shape=[jax.ShapeDtypeStruct((B, T, 512), BF16), jax.ShapeDtypeStruct(h0.shape, F32)],
        scratch_shapes=[pltpu.VMEM((T, 512), F32), pltpu.VMEM((T, 512), F32)],
        compiler_params=_cparams(("arbitrary",)),
        name="rwkv7",
    )(u3, u3, u3, u3, u3, *rp, h0)


def _route(logits):
    lane = lax.broadcasted_iota(jnp.int32, (1, LANES), 1)
    far = jnp.int32(2 * LANES)
    neg = -jnp.inf
    gl = jnp.where(jnp.logical_and(lane >= N_EXPERTS, lane < N_EXPERTS + N_GROUPS), logits, neg)
    gmax = jnp.max(gl, axis=-1, keepdims=True)
    grp = jnp.min(jnp.where(gl == gmax, lane, far), axis=-1, keepdims=True) - N_EXPERTS
    p_grp = 1.0 / jnp.sum(jnp.exp(gl - gmax), axis=-1, keepdims=True)
    el = jnp.where(jnp.logical_and(lane < N_EXPERTS, lane // EXPERTS_PER_GROUP == grp), logits, neg)
    v1 = jnp.max(el, axis=-1, keepdims=True)
    i1 = jnp.min(jnp.where(el == v1, lane, far), axis=-1, keepdims=True)
    el2 = jnp.where(lane == i1, neg, el)
    v2 = jnp.max(el2, axis=-1, keepdims=True)
    i2 = jnp.min(jnp.where(el2 == v2, lane, far), axis=-1, keepdims=True)
    e = jnp.exp(v2 - v1)
    w1 = 1.0 / (1.0 + e)
    return jnp.where(lane == i1, p_grp * w1, jnp.where(lane == i2, p_grp * (e * w1), 0.0))


def _outproj_kernel(x_ref, m0_ref, m1_ref, m2_ref, m3_ref, w_ref, mod_ref, g_ref, wr_ref, br_ref,
                    xn_ref, h2_ref, comb_ref):
    y = jnp.dot(m0_ref[...], w_ref[0:512, :], preferred_element_type=F32)
    for i, m_ref in enumerate((m1_ref, m2_ref, m3_ref), start=1):
        y = y + jnp.dot(m_ref[...], w_ref[i * 512:(i + 1) * 512, :], preferred_element_type=F32)
    xn = x_ref[...] + mod_ref[0, 2:3, :] * y
    xn_ref[...] = xn
    h = xn * lax.rsqrt(jnp.mean(xn * xn, axis=-1, keepdims=True) + NORM_EPS) * g_ref[...]
    h = h * (1.0 + mod_ref[0, 4:5, :]) + mod_ref[0, 3:4, :]
    h2_ref[...] = h.astype(BF16)
    comb_ref[...] = _route(_dot_hi(h, wr_ref[...]) + br_ref[...])


def _out_proj(x2, mixed, w_out, mod, g, router_w, router_b, T):
    n_tok = x2.shape[0]
    tm = 512
    bm = mod.shape[0]
    mod_idx = (lambda i: (i * tm // T, 0, 0)) if bm > 1 else (lambda i: (0, 0, 0))
    row = lambda w: pl.BlockSpec((tm, w), lambda i: (i, 0))
    full = lambda a: pl.BlockSpec(a.shape, lambda i, n=a.ndim: (0,) * n)
    return pl.pallas_call(
        _outproj_kernel,
        grid=(n_tok // tm,),
        in_specs=[row(D_MODEL)] + [row(GROUP_W)] * 4 + [full(w_out), pl.BlockSpec((1, 8, D_MODEL), mod_idx),
                                                       full(g), full(router_w), full(router_b)],
        out_specs=[row(D_MODEL), row(D_MODEL), row(LANES)],
        out_shape=[jax.ShapeDtypeStruct((n_tok, D_MODEL), F32), jax.ShapeDtypeStruct((n_tok, D_MODEL), BF16),
                   jax.ShapeDtypeStruct((n_tok, LANES), F32)],
        compiler_params=_cparams(("arbitrary",)),
        name="out_proj",
    )(x2, *mixed, w_out, mod, g, router_w, router_b)


def _moe_kernel(h_ref, comb_ref, xn_ref, mod_ref, wg_ref, wu_ref, wd_ref, o_ref):
    e = pl.program_id(1)

    @pl.when(e == 0)
    def _():
        o_ref[...] = jnp.zeros_like(o_ref)

    h = h_ref[...]
    a = jnp.dot(h, wg_ref[0], preferred_element_type=F32)
    b = jnp.dot(h, wu_ref[0], preferred_element_type=F32)
    lane = lax.broadcasted_iota(jnp.int32, (1, LANES), 1)
    gate = jnp.sum(jnp.where(lane == e, comb_ref[...], 0.0), axis=-1, keepdims=True)
    act = (a * _sigmoid(a)) * b * gate
    o_ref[...] += jnp.dot(act.astype(BF16), wd_ref[0], preferred_element_type=F32)

    @pl.when(e == N_EXPERTS - 1)
    def _():
        o_ref[...] = xn_ref[...] + mod_ref[0, 5:6, :] * o_ref[...]


def _moe(h2, comb, xn, mod, wg, wu, wd, T):
    n_tok = h2.shape[0]
    tm = 512
    bm = mod.shape[0]
    mod_idx = (lambda i, e: (i * tm // T, 0, 0)) if bm > 1 else (lambda i, e: (0, 0, 0))
    row = lambda w: pl.BlockSpec((tm, w), lambda i, e: (i, 0))
    return pl.pallas_call(
        _moe_kernel,
        grid=(n_tok // tm, N_EXPERTS),
        in_specs=[row(D_MODEL), row(LANES), row(D_MODEL), pl.BlockSpec((1, 8, D_MODEL), mod_idx),
                  pl.BlockSpec((1, D_MODEL, EXPERT_HIDDEN), lambda i, e: (e, 0, 0)),
                  pl.BlockSpec((1, D_MODEL, EXPERT_HIDDEN), lambda i, e: (e, 0, 0)),
                  pl.BlockSpec((1, EXPERT_HIDDEN, D_MODEL), lambda i, e: (e, 0, 0))],
        out_specs=row(D_MODEL),
        out_shape=jax.ShapeDtypeStruct((n_tok, D_MODEL), F32),
        compiler_params=_cparams(("arbitrary", "arbitrary")),
        name="moe",
    )(h2, comb, xn, mod, wg, wu, wd)


def kernel(x_prompt, x_sample, cache_mla, state_mlstm_C, state_mlstm_n, state_mlstm_m, state_rwkv, state_gla, c, c_ctx, w_ada, b_ada, g_mix, g_ffn, w_in, w_out, b_ml_gates, g_ml_norm, g_mla_qlat, g_mla_kvlat, w_mla_uq, w_mla_ukv, g_mla_qn, g_mla_kn, rw_w0, rw_w_up, rw_a0, rw_a_up, rw_g_up, rw_k_k, rw_k_a, rw_r_k, rw_ln, gla_g_up, gla_g_b, gla_norm, moe_w_rg, moe_b_rg, moe_w_re, moe_b_re, moe_w_gate, moe_w_up, moe_w_down):
    cc = jnp.concatenate([c_ctx[None], c, jnp.zeros((3, D_MODEL), F32)], axis=0)
    mod = _modulation(cc, w_ada, b_ada).reshape(DEPTH, 8, 6, D_MODEL)
    mod = jnp.pad(mod, ((0, 0), (0, 0), (0, 2), (0, 0)))

    layers = []
    for l in range(DEPTH):
        p = {'b_ml_gates': b_ml_gates[l], 'g_ml_norm': g_ml_norm[l], 'g_mla_qlat': g_mla_qlat[l],
             'g_mla_kvlat': g_mla_kvlat[l], 'w_mla_uq': w_mla_uq[l], 'w_mla_ukv': w_mla_ukv[l],
             'g_mla_qn': g_mla_qn[l], 'g_mla_kn': g_mla_kn[l], 'rw_w0': rw_w0[l], 'rw_w_up': rw_w_up[l],
             'rw_a0': rw_a0[l], 'rw_a_up': rw_a_up[l], 'rw_g_up': rw_g_up[l], 'rw_k_k': rw_k_k[l],
             'rw_k_a': rw_k_a[l], 'rw_r_k': rw_r_k[l], 'rw_ln': rw_ln[l], 'gla_g_up': gla_g_up[l],
             'gla_g_b': gla_g_b[l], 'gla_norm': gla_norm[l]}
        router_w = jnp.zeros((D_MODEL, LANES), F32)
        router_w = router_w.at[:, :N_EXPERTS].set(moe_w_re[l]).at[:, N_EXPERTS:N_EXPERTS + N_GROUPS].set(moe_w_rg[l])
        router_b = jnp.zeros((1, LANES), F32)
        router_b = router_b.at[0, :N_EXPERTS].set(moe_b_re[l]).at[0, N_EXPERTS:N_EXPERTS + N_GROUPS].set(moe_b_rg[l])
        layers.append(dict(
            w_in=_pack_w_in(w_in[l]), g_mix=g_mix[l].reshape(1, -1), g_ffn=g_ffn[l].reshape(1, -1),
            w_out=w_out[l].astype(BF16), ml=_mlstm_params(p), mla=_mla_params(p), rw=_rwkv_params(p),
            gla=_gla_params(p), router_w=router_w, router_b=router_b,
            wg=moe_w_gate[l].astype(BF16), wu=moe_w_up[l].astype(BF16), wd=moe_w_down[l].astype(BF16)))

    def block(x2, B, T, mod_g, lp, ctx):
        u3 = _in_proj(x2, mod_g, lp['g_mix'], lp['w_in'], T).reshape(B, T, PK_COLS)
        if ctx is None:
            ctx_kv = None
            ml_c0, ml_m0 = _mlstm_state_zero(B)
            rw_h0 = jnp.zeros((B, 2, RW_H, RW_N, RW_N), F32)
            gla_s0 = jnp.zeros((B, 2, GLA_H, GLA_DK, GLA_DV), F32)
        else:
            ctx_kv, ml_C0, ml_n0, ml_m0_, rw_S0, gla_s0 = ctx
            ml_c0, ml_m0 = _mlstm_state_in(ml_C0, ml_n0, ml_m0_)
            rw_h0 = jnp.swapaxes(rw_S0, -1, -2)
        ml_out, ml_c, ml_m = _mlstm(u3, *lp['ml'], ml_c0, ml_m0)
        mla_out, own_kv = _mla(u3, ctx_kv, lp['mla'])
        rw_out, rw_h = _rwkv(u3, lp['rw'], rw_h0)
        gla_out, gla_s = _gla(u3, lp['gla'], gla_s0)
        mixed = [t.reshape(B * T, GROUP_W) for t in (ml_out, mla_out, rw_out, gla_out)]
        xn, h2, comb = _out_proj(x2, mixed, lp['w_out'], mod_g, lp['g_ffn'], lp['router_w'], lp['router_b'], T)
        x_new = _moe(h2, comb, xn, mod_g, lp['wg'], lp['wu'], lp['wd'], T)
        ml_C, ml_n, ml_mm = _mlstm_state_out(ml_c, ml_m)
        return x_new, (own_kv, ml_C, ml_n, ml_mm, jnp.swapaxes(rw_h, -1, -2), gla_s)

    Bp, Tp = x_prompt.shape[:2]
    Bs, Ts = x_sample.shape[:2]
    xp = x_prompt.reshape(Bp * Tp, D_MODEL)
    ctx_states = []
    for l in range(DEPTH):
        xp, st = block(xp, Bp, Tp, mod[l, 0:1], layers[l], None)
        ctx_states.append(st)
    xs = x_sample.reshape(Bs * Ts, D_MODEL)
    for l in range(DEPTH):
        ctx = (cache_mla[:, l], state_mlstm_C[:, l], state_mlstm_n[:, l], state_mlstm_m[:, l],
               state_rwkv[:, l], state_gla[:, l])
        xs, _ = block(xs, Bs, Ts, mod[l, 1:1 + Bs], layers[l], ctx)
    outs = [jnp.stack([s[i] for s in ctx_states], axis=1) for i in range(6)]
    return (xp.reshape(x_prompt.shape), xs.reshape(x_sample.shape), *outs)
```

```python
import functools
import math

import numpy as np
import jax
import jax.numpy as jnp
from jax import lax
from jax.experimental import pallas as pl
from jax.experimental.pallas import tpu as pltpu

F32 = jnp.float32
BF16 = jnp.bfloat16
HI = lax.Precision.HIGHEST

D_MODEL = 2048
DEPTH = 2
GRID_W = 64
GROUP_W = 512
ML_H, ML_DK = 4, 128
MLA_H, MLA_NOPE, MLA_ROPE, MLA_V = 4, 128, 64, 128
MLA_QK = MLA_NOPE + MLA_ROPE
Q_LORA, KV_LORA = 384, 256
ROPE_THETA = 10000.0
RW_H, RW_N = 8, 64
RW_DECAY_SCALE = math.exp(-0.5)
RW_LN_EPS = 64e-5
GLA_H, GLA_DK, GLA_DV = 4, 64, 128
GLA_GATE_RANK = 16
GLA_NORMALIZER = 16.0
N_GROUPS, EXPERTS_PER_GROUP, N_EXPERTS = 4, 4, 16
EXPERT_HIDDEN = 512
NORM_EPS = 1e-6
LANES = 128
VMEM_LIMIT = 56 * 1024 * 1024

_REF_SPLITS = (
    ('ml_q', 512), ('ml_k', 512), ('ml_v', 512), ('ml_o', 512), ('ml_g', 16),
    ('mla_ql', Q_LORA), ('mla_ckv', KV_LORA), ('mla_kr', MLA_ROPE),
    ('rw_r', 512), ('rw_k', 512), ('rw_v', 512), ('rw_wd', 64), ('rw_ad', 64), ('rw_gd', 128),
    ('gla_q', 256), ('gla_k', 256), ('gla_v', 512), ('gla_gd', GLA_GATE_RANK), ('gla_g', 512),
)
_REF_OFF = {}
_o = 0
for _n, _w in _REF_SPLITS:
    _REF_OFF[_n] = (_o, _w)
    _o += _w
IN_COLS = _o

_PACKED = (
    ('ml_q', 512), ('ml_k', 512), ('ml_v', 512), ('ml_o', 512),
    ('rw_r', 512), ('rw_k', 512), ('rw_v', 512), ('gla_v', 512), ('gla_g', 512),
    ('mla_ql', 384), ('mla_ckv', 256), ('mla_kr', 64), ('mla_kr_sw', 64),
    ('gla_q', 256), ('gla_k', 256),
    ('ml_g', 128), ('rw_wd', 64), ('rw_ad', 64), ('rw_gd', 128), ('gla_gd', 128),
)
PK_OFF = {}
_o = 0
for _n, _w in _PACKED:
    PK_OFF[_n] = _o
    _o += _w
PK_COLS = _o


def _rope_swap_perm():
    idx = np.arange(MLA_ROPE)
    axis, half, f = idx // 32, (idx % 32) // 16, idx % 16
    return axis * 32 + (1 - half) * 16 + f


def _packed_column_index():
    src = np.full((PK_COLS,), -1, np.int64)
    for name, width in _PACKED:
        off = PK_OFF[name]
        if name == 'mla_kr_sw':
            s, w = _REF_OFF['mla_kr']
            src[off:off + w] = s + _rope_swap_perm()
        else:
            s, w = _REF_OFF[name]
            src[off:off + w] = s + np.arange(w)
    return src


_PK_SRC = _packed_column_index()


def _pack_w_in(w):
    valid = jnp.asarray(_PK_SRC >= 0)
    cols = jnp.take(w, jnp.asarray(np.maximum(_PK_SRC, 0)), axis=1)
    return jnp.where(valid[None, :], cols, 0.0).astype(BF16)


def _cparams(sem):
    return pltpu.CompilerParams(dimension_semantics=sem, vmem_limit_bytes=VMEM_LIMIT)


def _log_sigmoid(x):
    return jnp.minimum(x, 0.0) - jnp.log(1.0 + jnp.exp(-jnp.abs(x)))


def _sigmoid(x):
    return 1.0 / (1.0 + jnp.exp(-x))


def _dot(a, b):
    return jnp.dot(a.astype(BF16), b.astype(BF16), preferred_element_type=F32)


def _dot_nt(a, b):
    return lax.dot_general(a.astype(BF16), b.astype(BF16), (((1,), (1,)), ((), ())), preferred_element_type=F32)


def _dot_tn(a, b):
    return lax.dot_general(a.astype(BF16), b.astype(BF16), (((0,), (0,)), ((), ())), preferred_element_type=F32)


def _dot_hi(a, b):
    return jnp.dot(a, b, precision=HI, preferred_element_type=F32)


def _mod_kernel(c_ref, w_ref, b_ref, o_ref):
    c = c_ref[...]
    s = c * _sigmoid(c)
    o_ref[0] = _dot(s, w_ref[0]) + b_ref[0]


def _modulation(cc, w_ada, b_ada):
    tn = 1536
    n = 6 * D_MODEL
    return pl.pallas_call(
        _mod_kernel,
        grid=(DEPTH, n // tn),
        in_specs=[pl.BlockSpec((8, D_MODEL), lambda l, j: (0, 0)),
                  pl.BlockSpec((1, D_MODEL, tn), lambda l, j: (l, 0, j)),
                  pl.BlockSpec((1, 1, tn), lambda l, j: (l, 0, j))],
        out_specs=pl.BlockSpec((1, 8, tn), lambda l, j: (l, 0, j)),
        out_shape=jax.ShapeDtypeStruct((DEPTH, 8, n), F32),
        compiler_params=_cparams(("arbitrary", "arbitrary")),
        name="adaln_mod",
    )(cc, w_ada, b_ada.reshape(DEPTH, 1, n))


def _inproj_kernel(x_ref, mod_ref, g_ref, w_ref, o_ref, h_scr):
    @pl.when(pl.program_id(1) == 0)
    def _():
        x = x_ref[...]
        xn = x * lax.rsqrt(jnp.mean(x * x, axis=-1, keepdims=True) + NORM_EPS) * g_ref[...]
        h_scr[...] = (xn * (1.0 + mod_ref[0, 1:2, :]) + mod_ref[0, 0:1, :]).astype(BF16)

    o_ref[...] = jnp.dot(h_scr[...], w_ref[...], preferred_element_type=F32)


def _in_proj(x2, mod, g, w_packed, T):
    n_tok = x2.shape[0]
    tm = 1024 if T >= 1024 else 512
    tn = 640
    bm = mod.shape[0]
    mod_idx = (lambda i, j: (i * tm // T, 0, 0)) if bm > 1 else (lambda i, j: (0, 0, 0))
    return pl.pallas_call(
        _inproj_kernel,
        grid=(n_tok // tm, PK_COLS // tn),
        in_specs=[pl.BlockSpec((tm, D_MODEL), lambda i, j: (i, 0)),
                  pl.BlockSpec((1, 8, D_MODEL), mod_idx),
                  pl.BlockSpec((1, D_MODEL), lambda i, j: (0, 0)),
                  pl.BlockSpec((D_MODEL, tn), lambda i, j: (0, j))],
        out_specs=pl.BlockSpec((tm, tn), lambda i, j: (i, j)),
        out_shape=jax.ShapeDtypeStruct((n_tok, PK_COLS), F32),
        scratch_shapes=[pltpu.VMEM((tm, D_MODEL), BF16)],
        compiler_params=_cparams(("arbitrary", "arbitrary")),
        name="in_proj",
    )(x2, mod, g, w_packed)


ML_CHUNK = 256


def _mlstm_kernel(q_ref, k_ref, v_ref, o_ref, g_ref, bias_ref, gn_ref, c0_ref, m0_ref,
                  out_ref, c_ref, m_ref, hs_ref, *, T):
    L = ML_CHUNK
    nc = T // L
    c_ref[...] = c0_ref[...]
    m_ref[...] = m0_ref[...]
    ii = lax.broadcasted_iota(jnp.int32, (L, L), 0)
    jj = lax.broadcasted_iota(jnp.int32, (L, L), 1)
    lane = lax.broadcasted_iota(jnp.int32, (1, LANES), 1)
    is_f = jnp.logical_and(lane % 8 >= 4, lane < 16)
    ones_col = (lax.broadcasted_iota(jnp.int32, (L, LANES), 1) == 0).astype(BF16)
    scale = ML_DK ** -0.5

    for d in range(2):
        mask = (jj <= ii) if d == 0 else (jj >= ii)
        tri = mask.astype(F32)

        def chunk(ci, carry, d=d, mask=mask, tri=tri):
            c = ci if d == 0 else nc - 1 - ci
            s = pl.multiple_of(c * L, L)
            gates = g_ref[0, pl.ds(s, L), :] + bias_ref[...]
            gf = jnp.where(is_f, _log_sigmoid(gates), gates)
            cum = _dot_hi(tri, gf)
            gf_t = gf.T
            cum_t = cum.T
            for h in range(ML_H):
                ci_, cf_ = d * 8 + h, d * 8 + 4 + h
                hs = slice(h * ML_DK, (h + 1) * ML_DK)
                ig_col, ig_row = gf[:, ci_:ci_ + 1], gf_t[ci_:ci_ + 1, :]
                b_col, b_row = cum[:, cf_:cf_ + 1], cum_t[cf_:cf_ + 1, :]
                b_last = b_col[L - 1:L, :] if d == 0 else b_col[0:1, :]
                m_prev = m_ref[0, d, h][:, 0:1]
                dmat = jnp.where(mask, b_col + (ig_row - b_row), -jnp.inf)
                m_inter = b_col + m_prev
                m_row = jnp.maximum(m_inter, jnp.max(dmat, axis=-1, keepdims=True))
                w_inter = jnp.exp(m_inter - m_row)
                q = (q_ref[0, pl.ds(s, L), hs] * scale).astype(BF16)
                k = k_ref[0, pl.ds(s, L), hs]
                v_aug = jnp.concatenate([v_ref[0, pl.ds(s, L), hs].astype(BF16), ones_col], axis=1)
                smat = _dot_nt(q, k) * jnp.exp(dmat - m_row)
                c_aug = c_ref[0, d, h]
                nd = w_inter * _dot(q, c_aug) + _dot(smat, v_aug)
                num, den = nd[:, :ML_DK], nd[:, ML_DK:ML_DK + 1]
                hh = num / jnp.maximum(jnp.abs(den), jnp.exp(-m_row))
                if d == 0:
                    hs_ref[pl.ds(s, L), hs] = hh
                else:
                    hs_ref[pl.ds(s, L), hs] = hs_ref[pl.ds(s, L), hs] + hh
                dk_col = b_last - b_col + ig_col
                m_new = jnp.maximum(b_last + m_prev, jnp.max(dk_col, axis=0, keepdims=True))
                w_key = jnp.exp(dk_col - m_new)
                c_scale = jnp.exp(b_last + m_prev - m_new)
                c_ref[0, d, h] = c_scale * c_aug + _dot_tn(k * w_key, v_aug)
                m_ref[0, d, h] = jnp.broadcast_to(m_new, (1, LANES))
            return carry

        lax.fori_loop(0, nc, chunk, 0)

    def finish(ci, carry):
        s = pl.multiple_of(ci * L, L)
        for h in range(ML_H):
            hs = slice(h * ML_DK, (h + 1) * ML_DK)
            x = hs_ref[pl.ds(s, L), hs]
            xc = x - jnp.mean(x, axis=-1, keepdims=True)
            y = xc * lax.rsqrt(jnp.mean(xc * xc, axis=-1, keepdims=True) + NORM_EPS) * gn_ref[:, hs]
            out_ref[0, pl.ds(s, L), hs] = (y * _sigmoid(o_ref[0, pl.ds(s, L), hs])).astype(BF16)
        return carry

    lax.fori_loop(0, nc, finish, 0)


def _mlstm_params(p):
    bias = jnp.zeros((1, LANES), F32).at[0, :16].set(p['b_ml_gates'])
    return bias, p['g_ml_norm'].reshape(1, 512)


def _mlstm_state_in(C0, n0, m0):
    c0 = jnp.concatenate([C0, n0[..., None], jnp.zeros(C0.shape[:-1] + (ML_DK - 1,), F32)], axis=-1)
    return c0, jnp.broadcast_to(m0[..., None, None], m0.shape + (1, LANES))


def _mlstm_state_zero(B):
    return jnp.zeros((B, 2, ML_H, ML_DK, 2 * ML_DK), F32), jnp.zeros((B, 2, ML_H, 1, LANES), F32)


def _mlstm_state_out(c, m):
    return c[..., :ML_DK], c[..., ML_DK], m[..., 0, 0]


def _mlstm(u3, bias, gnorm, c0, m0):
    B, T, _ = u3.shape
    blk = lambda name: pl.BlockSpec((1, T, 512), lambda b, o=PK_OFF[name] // 512: (b, 0, o))
    st_c = pl.BlockSpec((1, 2, ML_H, ML_DK, 2 * ML_DK), lambda b: (b, 0, 0, 0, 0))
    st_m = pl.BlockSpec((1, 2, ML_H, 1, LANES), lambda b: (b, 0, 0, 0, 0))
    return pl.pallas_call(
        functools.partial(_mlstm_kernel, T=T),
        grid=(B,),
        in_specs=[blk('ml_q'), blk('ml_k'), blk('ml_v'), blk('ml_o'),
                  pl.BlockSpec((1, T, LANES), lambda b: (b, 0, PK_OFF['ml_g'] // LANES)),
                  pl.BlockSpec((1, LANES), lambda b: (0, 0)),
                  pl.BlockSpec((1, 512), lambda b: (0, 0)),
                  st_c, st_m],
        out_specs=[pl.BlockSpec((1, T, 512), lambda b: (b, 0, 0)), st_c, st_m],
        out_shape=[jax.ShapeDtypeStruct((B, T, 512), BF16),
                   jax.ShapeDtypeStruct(c0.shape, F32),
                   jax.ShapeDtypeStruct(m0.shape, F32)],
        scratch_shapes=[pltpu.VMEM((T, 512), F32)],
        compiler_params=_cparams(("arbitrary",)),
        name="mlstm",
    )(u3, u3, u3, u3, u3, bias, gnorm, c0, m0)


MLA_BLK = 256


def _rope_tables(T):
    rows = T // GRID_W
    row = np.repeat(np.arange(rows, dtype=np.float64), GRID_W)
    col = np.tile(np.arange(GRID_W, dtype=np.float64), rows)
    inv = ROPE_THETA ** (-np.arange(MLA_ROPE // 4, dtype=np.float64) / (MLA_ROPE // 4))
    ang = np.stack([row[:, None] * inv, col[:, None] * inv], axis=1)
    cos = np.stack([np.cos(ang), np.cos(ang)], axis=2).reshape(T, MLA_ROPE)
    sin = np.stack([-np.sin(ang), np.sin(ang)], axis=2).reshape(T, MLA_ROPE)
    return jnp.asarray(np.concatenate([cos, sin], axis=1), F32)


def _mla_kernel(*refs, T, n_ctx, rope):
    if n_ctx:
        (u_ref, ctx_ref, cs_ref, gql_ref, gkv_ref, gains_ref, wqn_ref, wqr_ref, wqs_ref, wkn_ref, wv_ref,
         out_ref, kv_ref, qn_s, qr_s, kn_s, kr_s, v_s) = refs
    else:
        (u_ref, cs_ref, gql_ref, gkv_ref, gains_ref, wqn_ref, wqr_ref, wqs_ref, wkn_ref, wv_ref,
         out_ref, kv_ref, qn_s, qr_s, kn_s, kr_s, v_s) = refs
    Lb = MLA_BLK
    gq_n, gq_r, gq_s = gains_ref[0:1, :], gains_ref[1:2, 0:64], gains_ref[1:2, 64:128]
    gk_n, gk_r, gk_s = gains_ref[2:3, :], gains_ref[3:4, 0:64], gains_ref[3:4, 64:128]
    sm_scale = MLA_QK ** -0.5

    def store_keys(s, kn, kr, krs, cos, sin):
        kr_ss = jnp.sum(kr * kr, axis=-1, keepdims=True)
        for h in range(MLA_H):
            kn_h = kn[:, h * 128:(h + 1) * 128]
            rk = lax.rsqrt((jnp.sum(kn_h * kn_h, axis=-1, keepdims=True) + kr_ss) / MLA_QK + NORM_EPS)
            kn_s[pl.ds(s, Lb), h * 128:(h + 1) * 128] = (kn_h * rk * gk_n).astype(BF16)
            kr_h = kr * gk_r
            if cos is not None:
                kr_h = kr_h * cos + (krs * gk_s) * sin
            kr_s[pl.ds(s, Lb), h * 64:(h + 1) * 64] = (kr_h * rk).astype(BF16)

    def prep(ci, carry):
        s = pl.multiple_of(ci * Lb, Lb)
        u = u_ref[0, pl.ds(s, Lb), :]
        ql, ckv, kr, krs = u[:, :384], u[:, 384:640], u[:, 640:704], u[:, 704:768]
        qln = ql * lax.rsqrt(jnp.mean(ql * ql, axis=-1, keepdims=True) + NORM_EPS) * gql_ref[...]
        ckvn = ckv * lax.rsqrt(jnp.mean(ckv * ckv, axis=-1, keepdims=True) + NORM_EPS) * gkv_ref[...]
        kv_ref[0, pl.ds(s, Lb), :] = jnp.concatenate([ckvn, kr], axis=1)
        cos = cs_ref[pl.ds(s, Lb), 0:64] if rope else None
        sin = cs_ref[pl.ds(s, Lb), 64:128] if rope else None
        qn, qr, qs = _dot(qln, wqn_ref[...]), _dot(qln, wqr_ref[...]), _dot(qln, wqs_ref[...])
        for h in range(MLA_H):
            qn_h, qr_h = qn[:, h * 128:(h + 1) * 128], qr[:, h * 64:(h + 1) * 64]
            ss = jnp.sum(qn_h * qn_h, axis=-1, keepdims=True) + jnp.sum(qr_h * qr_h, axis=-1, keepdims=True)
            rq = lax.rsqrt(ss / MLA_QK + NORM_EPS) * sm_scale
            qn_s[pl.ds(s, Lb), h * 128:(h + 1) * 128] = (qn_h * rq * gq_n).astype(BF16)
            qr_h = qr_h * gq_r
            if rope:
                qr_h = qr_h * cos + (qs[:, h * 64:(h + 1) * 64] * gq_s) * sin
            qr_s[pl.ds(s, Lb), h * 64:(h + 1) * 64] = (qr_h * rq).astype(BF16)
        v_s[pl.ds(s, Lb), :] = _dot(ckvn, wv_ref[...]).astype(BF16)
        store_keys(s, _dot(ckvn, wkn_ref[...]), kr, krs, cos, sin)
        return carry

    lax.fori_loop(0, T // Lb, prep, 0)

    for ci in range(n_ctx // Lb):
        cx = ctx_ref[0, ci * Lb:(ci + 1) * Lb, :]
        ckv_c, kr_c = cx[:, :KV_LORA], cx[:, KV_LORA:KV_LORA + MLA_ROPE]
        v_s[T + ci * Lb:T + (ci + 1) * Lb, :] = _dot(ckv_c, wv_ref[...]).astype(BF16)
        store_keys(T + ci * Lb, _dot(ckv_c, wkn_ref[...]), kr_c, None, None, None)

    for h in range(MLA_H):
        def attend(qi, carry, h=h):
            s = pl.multiple_of(qi * Lb, Lb)
            sc = (_dot_nt(qn_s[pl.ds(s, Lb), h * 128:(h + 1) * 128], kn_s[:, h * 128:(h + 1) * 128])
                  + _dot_nt(qr_s[pl.ds(s, Lb), h * 64:(h + 1) * 64], kr_s[:, h * 64:(h + 1) * 64]))
            p = jnp.exp(sc - jnp.max(sc, axis=-1, keepdims=True))
            o = _dot(p, v_s[:, h * 128:(h + 1) * 128]) / jnp.sum(p, axis=-1, keepdims=True)
            out_ref[0, pl.ds(s, Lb), h * 128:(h + 1) * 128] = o.astype(BF16)
            return carry

        lax.fori_loop(0, T // Lb, attend, 0)


def _mla_params(p):
    wq = p['w_mla_uq'].reshape(Q_LORA, MLA_H, MLA_QK)
    sw = _rope_swap_perm()
    wq_n = wq[:, :, :MLA_NOPE].reshape(Q_LORA, 512).astype(BF16)
    wq_r = wq[:, :, MLA_NOPE:].reshape(Q_LORA, 256).astype(BF16)
    wq_s = wq[:, :, MLA_NOPE:][:, :, sw].reshape(Q_LORA, 256).astype(BF16)
    wkv = p['w_mla_ukv'].reshape(KV_LORA, MLA_H, MLA_NOPE + MLA_V)
    wk_n = wkv[:, :, :MLA_NOPE].reshape(KV_LORA, 512).astype(BF16)
    wv = wkv[:, :, MLA_NOPE:].reshape(KV_LORA, 512).astype(BF16)
    gq, gk = p['g_mla_qn'], p['g_mla_kn']
    gains = jnp.zeros((8, LANES), F32)
    gains = gains.at[0].set(gq[:128]).at[1, :64].set(gq[128:]).at[1, 64:].set(gq[128:][sw])
    gains = gains.at[2].set(gk[:128]).at[3, :64].set(gk[128:]).at[3, 64:].set(gk[128:][sw])
    return (p['g_mla_qlat'].reshape(1, -1), p['g_mla_kvlat'].reshape(1, -1), gains, wq_n, wq_r, wq_s, wk_n, wv)


def _mla(u3, ctx_kv, mp):
    B, T, _ = u3.shape
    n_ctx = 0 if ctx_kv is None else ctx_kv.shape[1]
    rope = ctx_kv is not None
    tk = T + n_ctx
    full = lambda a: pl.BlockSpec(a.shape, lambda b, n=a.ndim: (0,) * n)
    cs = _rope_tables(T) if rope else jnp.zeros((T, LANES), F32)
    ins = [u3] + ([ctx_kv] if rope else []) + [cs] + list(mp)
    specs = [pl.BlockSpec((1, T, 768), lambda b: (b, 0, PK_OFF['mla_ql'] // 768))]
    if rope:
        specs.append(pl.BlockSpec((1, n_ctx, KV_LORA + MLA_ROPE), lambda b: (b, 0, 0)))
    specs += [full(a) for a in ins[len(specs):]]
    return pl.pallas_call(
        functools.partial(_mla_kernel, T=T, n_ctx=n_ctx, rope=rope),
        grid=(B,),
        in_specs=specs,
        out_specs=[pl.BlockSpec((1, T, 512), lambda b: (b, 0, 0)),
                   pl.BlockSpec((1, T, KV_LORA + MLA_ROPE), lambda b: (b, 0, 0))],
        out_shape=[jax.ShapeDtypeStruct((B, T, 512), BF16),
                   jax.ShapeDtypeStruct((B, T, KV_LORA + MLA_ROPE), F32)],
        scratch_shapes=[pltpu.VMEM((T, 512), BF16), pltpu.VMEM((T, 256), BF16),
                        pltpu.VMEM((tk, 512), BF16), pltpu.VMEM((tk, 256), BF16), pltpu.VMEM((tk, 512), BF16)],
        compiler_params=_cparams(("arbitrary",)),
        name="mla",
    )(*ins)


GLA_CHUNK = 64
GLA_LEAF = 4


def _gla_kernel(q_ref, k_ref, v_ref, gd_ref, g_ref, gup_ref, gb_ref, gn_ref, hsel_ref, s0_ref,
                out_ref, s_ref, os_ref, *, T):
    L, C = GLA_CHUNK, GLA_LEAF
    nc = T // L
    s_ref[...] = s0_ref[...]
    ii = lax.broadcasted_iota(jnp.int32, (L, L), 0)
    jj = lax.broadcasted_iota(jnp.int32, (L, L), 1)
    eye = (lax.broadcasted_iota(jnp.int32, (GLA_DK, GLA_DK), 0)
           == lax.broadcasted_iota(jnp.int32, (GLA_DK, GLA_DK), 1)).astype(F32)
    row_id = lax.broadcasted_iota(jnp.int32, (L, 1), 0)
    scale = GLA_DK ** -0.5
    hsel = hsel_ref[...]
    levels = []
    span = C
    while span < L:
        levels.append(span)
        span *= 2

    for d in range(2):
        causal = (jj <= ii) if d == 0 else (jj >= ii)
        tri = causal.astype(F32)

        def chunk(ci, carry, d=d, tri=tri):
            c = ci if d == 0 else nc - 1 - ci
            s = pl.multiple_of(c * L, L)
            la = _log_sigmoid(_dot_hi(gd_ref[0, pl.ds(s, L), :], gup_ref[d]) + gb_ref[d]) / GLA_NORMALIZER
            b = _dot_hi(tri, la)
            total = b[L - 1:L, :] if d == 0 else b[0:1, :]
            q = q_ref[0, pl.ds(s, L), :] * scale
            k = k_ref[0, pl.ds(s, L), :]
            amat = [jnp.zeros((L, L), F32) for _ in range(GLA_H)]
            for sp in levels:
                b3 = b.reshape(L // (2 * sp), 2 * sp, GLA_H * GLA_DK)
                edge = b3[:, sp - 1:sp, :] if d == 0 else b3[:, sp:sp + 1, :]
                bref = jnp.broadcast_to(edge, b3.shape).reshape(L, GLA_H * GLA_DK)
                later = (row_id % (2 * sp) >= sp) if d == 0 else (row_id % (2 * sp) < sp)
                e = jnp.exp(jnp.where(later, b - bref, bref - b))
                qs = jnp.where(later, q * e, 0.0)
                kt = jnp.where(later, 0.0, k * e)
                same = (ii // (2 * sp)) == (jj // (2 * sp))
                for h in range(GLA_H):
                    ks = slice(h * GLA_DK, (h + 1) * GLA_DK)
                    amat[h] = amat[h] + jnp.where(same, _dot_nt(qs[:, ks], kt[:, ks]), 0.0)
            terms = []
            for dl in range(C):
                if dl == 0:
                    terms.append(q * k)
                    continue
                sh = dl if d == 0 else L - dl
                ok = (row_id % C >= dl) if d == 0 else (row_id % C < C - dl)
                kd = pltpu.roll(k, sh, axis=0)
                bd = pltpu.roll(b, sh, axis=0)
                terms.append(q * kd * jnp.exp(jnp.where(ok, b - bd, 0.0)))
            tt = jnp.concatenate(terms, axis=0)
            t_hi = tt.astype(BF16)
            t_lo = (tt - t_hi.astype(F32)).astype(BF16)
            diag = (jnp.dot(t_hi, hsel, preferred_element_type=F32)
                    + jnp.dot(t_lo, hsel, preferred_element_type=F32))
            for dl in range(C):
                pair = jnp.logical_and(jj == (ii - dl if d == 0 else ii + dl), ii // C == jj // C)
                for h in range(GLA_H):
                    amat[h] = amat[h] + jnp.where(pair, diag[dl * L:(dl + 1) * L, h:h + 1], 0.0)
            q_in = q * jnp.exp(b)
            k_out = k * jnp.exp(total - b)
            f_row = jnp.exp(total)
            for h in range(GLA_H):
                ks = slice(h * GLA_DK, (h + 1) * GLA_DK)
                vs = slice(h * GLA_DV, (h + 1) * GLA_DV)
                v = v_ref[0, pl.ds(s, L), vs]
                st = s_ref[0, d, h]
                o = _dot(amat[h], v) + _dot(q_in[:, ks], st)
                if d == 0:
                    os_ref[pl.ds(s, L), vs] = o
                else:
                    os_ref[pl.ds(s, L), vs] = os_ref[pl.ds(s, L), vs] + o
                f_col = jnp.sum(eye * f_row[:, ks], axis=1, keepdims=True)
                s_ref[0, d, h] = f_col * st + _dot_tn(k_out[:, ks], v)
            return carry

        lax.fori_loop(0, nc, chunk, 0)

    Lf = 256

    def finish(ci, carry):
        s = pl.multiple_of(ci * Lf, Lf)
        for h in range(GLA_H):
            vs = slice(h * GLA_DV, (h + 1) * GLA_DV)
            o = os_ref[pl.ds(s, Lf), vs]
            y = o * lax.rsqrt(jnp.mean(o * o, axis=-1, keepdims=True) + NORM_EPS) * gn_ref[:, vs]
            g = g_ref[0, pl.ds(s, Lf), vs]
            out_ref[0, pl.ds(s, Lf), vs] = (y * (g * _sigmoid(g))).astype(BF16)
        return carry

    lax.fori_loop(0, T // Lf, finish, 0)


def _gla_params(p):
    gup = jnp.zeros((2, LANES, GLA_H * GLA_DK), F32).at[:, :GLA_GATE_RANK, :].set(p['gla_g_up'])
    hsel = jnp.asarray(np.arange(GLA_H * GLA_DK)[:, None] // GLA_DK == np.arange(LANES)[None, :], BF16)
    return gup, p['gla_g_b'].reshape(2, 1, -1), p['gla_norm'].reshape(1, -1), hsel


def _gla(u3, gp, s0):
    B, T, _ = u3.shape
    blk = lambda name, w: pl.BlockSpec((1, T, w), lambda b, o=PK_OFF[name] // w: (b, 0, o))
    full = lambda a: pl.BlockSpec(a.shape, lambda b, n=a.ndim: (0,) * n)
    st = pl.BlockSpec((1, 2, GLA_H, GLA_DK, GLA_DV), lambda b: (b, 0, 0, 0, 0))
    return pl.pallas_call(
        functools.partial(_gla_kernel, T=T),
        grid=(B,),
        in_specs=[blk('gla_q', 256), blk('gla_k', 256), blk('gla_v', 512), blk('gla_gd', 128), blk('gla_g', 512),
                  full(gp[0]), full(gp[1]), full(gp[2]), full(gp[3]), st],
        out_specs=[pl.BlockSpec((1, T, 512), lambda b: (b, 0, 0)), st],
        out_shape=[jax.ShapeDtypeStruct((B, T, 512), BF16), jax.ShapeDtypeStruct(s0.shape, F32)],
        scratch_shapes=[pltpu.VMEM((T, 512), F32)],
        compiler_params=_cparams(("arbitrary",)),
        name="gla",
    )(u3, u3, u3, u3, u3, *gp, s0)


RW_CHUNK = 64


def _seg_sum(x, bd):
    hi = x.astype(BF16)
    lo = (x - hi.astype(F32)).astype(BF16)
    return jnp.dot(hi, bd, preferred_element_type=F32) + jnp.dot(lo, bd, preferred_element_type=F32)


def _rwkv_kernel(r_ref, k_ref, v_ref, wa_ref, gd_ref, wwa_ref, w0a0_ref, gup_ref, kk_ref, ka_ref, rk_ref, ln_ref,
                 bd_ref, h0_ref, out_ref, h_ref, ys_ref, bonus_ref, *, T):
    L, N = RW_CHUNK, RW_N
    nc = T // L
    h_ref[...] = h0_ref[...]
    ii = lax.broadcasted_iota(jnp.int32, (L, L), 0)
    jj = lax.broadcasted_iota(jnp.int32, (L, L), 1)
    eye = (ii == jj).astype(F32)
    lane = lax.broadcasted_iota(jnp.int32, (1, LANES), 1)
    bd = bd_ref[...]

    for d in range(2):
        strict = (jj < ii) if d == 0 else (jj > ii)
        incl = (jj <= ii) if d == 0 else (jj >= ii)
        tri = incl.astype(F32)

        def chunk(ci, carry, d=d, strict=strict, incl=incl, tri=tri):
            c = ci if d == 0 else nc - 1 - ci
            s = pl.multiple_of(c * L, L)
            r = r_ref[0, pl.ds(s, L), :]
            k = k_ref[0, pl.ds(s, L), :]
            v = v_ref[0, pl.ds(s, L), :]
            wa = wa_ref[0, pl.ds(s, L), :]
            pre = _dot_hi(jnp.where(lane < 64, jnp.tanh(wa), wa), wwa_ref[d]) + w0a0_ref[d]
            logw = -RW_DECAY_SCALE * _sigmoid(pre[:, :512])
            a = _sigmoid(pre[:, 512:])
            kkr = k * kk_ref[...]
            kk = kkr * lax.rsqrt(_seg_sum(kkr * kkr, bd) + 1e-12)
            kt = k * (1.0 + (a - 1.0) * ka_ref[...])
            bh = kk * a
            bonus = _seg_sum(r * kt * rk_ref[...], bd) * v
            if d == 0:
                bonus_ref[pl.ds(s, L), :] = bonus
            else:
                bonus_ref[pl.ds(s, L), :] = bonus_ref[pl.ds(s, L), :] + bonus
            lg = _dot_hi(tri, logw)
            lg_end = lg[L - 1:L, :] if d == 0 else lg[0:1, :]
            a_t = -kk * jnp.exp(lg - logw)
            r_t = r * jnp.exp(lg)
            e_inv = jnp.exp(-lg)
            k_t, b_t = kt * e_inv, bh * e_inv
            e_end = jnp.exp(lg_end - lg)
            k_e, b_e = kt * e_end, bh * e_end
            g_end = jnp.exp(lg_end)
            heads = range(RW_H)
            sls = [slice(h * N, (h + 1) * N) for h in heads]
            ms = [_dot_nt(jnp.concatenate([a_t[:, sl], r_t[:, sl]], axis=0),
                          jnp.concatenate([b_t[:, sl], k_t[:, sl]], axis=0)) for sl in sls]
            pws = [jnp.where(strict, m[:L, :L], 0.0) for m in ms]
            xs = [eye + n for n in pws]
            for _ in range(5):
                pws = [_dot(pw, pw) for pw in pws]
                xs = [x + _dot(x, pw) for x, pw in zip(xs, pws)]
            vhs = [v[:, sl] for sl in sls]
            mvs = [_dot(jnp.where(strict, m[:L, L:], 0.0), vh) for m, vh in zip(ms, vhs)]
            tws = [_dot(x, jnp.concatenate([a_t[:, sl], mv], axis=1))
                   for x, sl, mv in zip(xs, sls, mvs)]
            qys = [_dot(jnp.where(incl, m[L:, :L], 0.0), tw) for m, tw in zip(ms, tws)]
            ylocs = [_dot(jnp.where(incl, m[L:, L:], 0.0), vh) + qy[:, N:] for m, vh, qy in zip(ms, vhs, qys)]
            pgs = [_dot_tn(b_e[:, sl], tw) for sl, tw in zip(sls, tws)]
            gmats = [_dot_tn(k_e[:, sl], vh) + pg[:, N:] for sl, vh, pg in zip(sls, vhs, pgs)]
            for h in heads:
                sl = sls[h]
                hst = h_ref[0, d, h]
                y = _dot(r_t[:, sl] + qys[h][:, :N], hst) + ylocs[h]
                if d == 0:
                    ys_ref[pl.ds(s, L), sl] = y
                else:
                    ys_ref[pl.ds(s, L), sl] = ys_ref[pl.ds(s, L), sl] + y
                h_ref[0, d, h] = _dot(eye * g_end[:, sl] + pgs[h][:, :N], hst) + gmats[h]
            return carry

        lax.fori_loop(0, nc, chunk, 0)

    Lf = 256

    def finish(ci, carry):
        s = pl.multiple_of(ci * Lf, Lf)
        y = ys_ref[pl.ds(s, Lf), :]
        yc = y - _seg_sum(y, bd) / N
        yn = yc * lax.rsqrt(_seg_sum(yc * yc, bd) / N + RW_LN_EPS) * ln_ref[...]
        g = _dot(_sigmoid(gd_ref[0, pl.ds(s, Lf), :]), gup_ref[...])
        out_ref[0, pl.ds(s, Lf), :] = ((yn + bonus_ref[pl.ds(s, Lf), :]) * g).astype(BF16)
        return carry

    lax.fori_loop(0, T // Lf, finish, 0)


def _rwkv_params(p):
    wwa = jnp.zeros((2, LANES, 1024), F32)
    wwa = wwa.at[:, :64, :512].set(p['rw_w_up']).at[:, 64:, 512:].set(p['rw_a_up'])
    w0a0 = jnp.concatenate([p['rw_w0'], p['rw_a0']], axis=-1).reshape(2, 1, 1024)
    seg = np.arange(512) // RW_N
    bd = jnp.asarray(seg[:, None] == seg[None, :], BF16)
    row = lambda n: p[n].reshape(1, -1)
    return (wwa, w0a0, p['rw_g_up'].astype(BF16), row('rw_k_k'), row('rw_k_a'), row('rw_r_k'), row('rw_ln'), bd)


def _rwkv(u3, rp, h0):
    B, T, _ = u3.shape
    blk = lambda name, w: pl.BlockSpec((1, T, w), lambda b, o=PK_OFF[name] // w: (b, 0, o))
    full = lambda a: pl.BlockSpec(a.shape, lambda b, n=a.ndim: (0,) * n)
    st = pl.BlockSpec((1, 2, RW_H, RW_N, RW_N), lambda b: (b, 0, 0, 0, 0))
    return pl.pallas_call(
        functools.partial(_rwkv_kernel, T=T),
        grid=(B,),
        in_specs=[blk('rw_r', 512), blk('rw_k', 512), blk('rw_v', 512), blk('rw_wd', 128), blk('rw_gd', 128)]
                 + [full(a) for a in rp] + [st],
        out_specs=[pl.BlockSpec((1, T, 512), lambda b: (b, 0, 0)), st],
        out_shape=[jax.ShapeDtypeStruct((B, T, 512), BF16), jax.ShapeDtypeStruct(h0.shape, F32)],
        scratch_shapes=[pltpu.VMEM((T, 512), F32), pltpu.VMEM((T, 512), F32)],
        compiler_params=_cparams(("arbitrary",)),
        name="rwkv7",
    )(u3, u3, u3, u3, u3, *rp, h0)


def _route(logits):
    lane = lax.broadcasted_iota(jnp.int32, (1, LANES), 1)
    far = jnp.int32(2 * LANES)
    neg = -jnp.inf
    gl = jnp.where(jnp.logical_and(lane >= N_EXPERTS, lane < N_EXPERTS + N_GROUPS), logits, neg)
    gmax = jnp.max(gl, axis=-1, keepdims=True)
    grp = jnp.min(jnp.where(gl == gmax, lane, far), axis=-1, keepdims=True) - N_EXPERTS
    p_grp = 1.0 / jnp.sum(jnp.exp(gl - gmax), axis=-1, keepdims=True)
    el = jnp.where(jnp.logical_and(lane < N_EXPERTS, lane // EXPERTS_PER_GROUP == grp), logits, neg)
    v1 = jnp.max(el, axis=-1, keepdims=True)
    i1 = jnp.min(jnp.where(el == v1, lane, far), axis=-1, keepdims=True)
    el2 = jnp.where(lane == i1, neg, el)
    v2 = jnp.max(el2, axis=-1, keepdims=True)
    i2 = jnp.min(jnp.where(el2 == v2, lane, far), axis=-1, keepdims=True)
    e = jnp.exp(v2 - v1)
    w1 = 1.0 / (1.0 + e)
    return jnp.where(lane == i1, p_grp * w1, jnp.where(lane == i2, p_grp * (e * w1), 0.0))


def _outproj_kernel(x_ref, m0_ref, m1_ref, m2_ref, m3_ref, w_ref, mod_ref, g_ref, wr_ref, br_ref,
                    xn_ref, h2_ref, comb_ref):
    y = jnp.dot(m0_ref[...], w_ref[0:512, :], preferred_element_type=F32)
    for i, m_ref in enumerate((m1_ref, m2_ref, m3_ref), start=1):
        y = y + jnp.dot(m_ref[...], w_ref[i * 512:(i + 1) * 512, :], preferred_element_type=F32)
    xn = x_ref[...] + mod_ref[0, 2:3, :] * y
    xn_ref[...] = xn
    h = xn * lax.rsqrt(jnp.mean(xn * xn, axis=-1, keepdims=True) + NORM_EPS) * g_ref[...]
    h = h * (1.0 + mod_ref[0, 4:5, :]) + mod_ref[0, 3:4, :]
    h2_ref[...] = h.astype(BF16)
    comb_ref[...] = _route(_dot_hi(h, wr_ref[...]) + br_ref[...])


def _out_proj(x2, mixed, w_out, mod, g, router_w, router_b, T):
    n_tok = x2.shape[0]
    tm = 512
    bm = mod.shape[0]
    mod_idx = (lambda i: (i * tm // T, 0, 0)) if bm > 1 else (lambda i: (0, 0, 0))
    row = lambda w: pl.BlockSpec((tm, w), lambda i: (i, 0))
    full = lambda a: pl.BlockSpec(a.shape, lambda i, n=a.ndim: (0,) * n)
    return pl.pallas_call(
        _outproj_kernel,
        grid=(n_tok // tm,),
        in_specs=[row(D_MODEL)] + [row(GROUP_W)] * 4 + [full(w_out), pl.BlockSpec((1, 8, D_MODEL), mod_idx),
                                                       full(g), full(router_w), full(router_b)],
        out_specs=[row(D_MODEL), row(D_MODEL), row(LANES)],
        out_shape=[jax.ShapeDtypeStruct((n_tok, D_MODEL), F32), jax.ShapeDtypeStruct((n_tok, D_MODEL), BF16),
                   jax.ShapeDtypeStruct((n_tok, LANES), F32)],
        compiler_params=_cparams(("arbitrary",)),
        name="out_proj",
    )(x2, *mixed, w_out, mod, g, router_w, router_b)


def _moe_kernel(h_ref, comb_ref, xn_ref, mod_ref, wg_ref, wu_ref, wd_ref, o_ref):
    e = pl.program_id(1)

    @pl.when(e == 0)
    def _():
        o_ref[...] = jnp.zeros_like(o_ref)

    h = h_ref[...]
    a = jnp.dot(h, wg_ref[0], preferred_element_type=F32)
    b = jnp.dot(h, wu_ref[0], preferred_element_type=F32)
    lane = lax.broadcasted_iota(jnp.int32, (1, LANES), 1)
    gate = jnp.sum(jnp.where(lane == e, comb_ref[...], 0.0), axis=-1, keepdims=True)
    act = (a * _sigmoid(a)) * b * gate
    o_ref[...] += jnp.dot(act.astype(BF16), wd_ref[0], preferred_element_type=F32)

    @pl.when(e == N_EXPERTS - 1)
    def _():
        o_ref[...] = xn_ref[...] + mod_ref[0, 5:6, :] * o_ref[...]


def _moe(h2, comb, xn, mod, wg, wu, wd, T):
    n_tok = h2.shape[0]
    tm = 512
    bm = mod.shape[0]
    mod_idx = (lambda i, e: (i * tm // T, 0, 0)) if bm > 1 else (lambda i, e: (0, 0, 0))
    row = lambda w: pl.BlockSpec((tm, w), lambda i, e: (i, 0))
    return pl.pallas_call(
        _moe_kernel,
        grid=(n_tok // tm, N_EXPERTS),
        in_specs=[row(D_MODEL), row(LANES), row(D_MODEL), pl.BlockSpec((1, 8, D_MODEL), mod_idx),
                  pl.BlockSpec((1, D_MODEL, EXPERT_HIDDEN), lambda i, e: (e, 0, 0)),
                  pl.BlockSpec((1, D_MODEL, EXPERT_HIDDEN), lambda i, e: (e, 0, 0)),
                  pl.BlockSpec((1, EXPERT_HIDDEN, D_MODEL), lambda i, e: (e, 0, 0))],
        out_specs=row(D_MODEL),
        out_shape=jax.ShapeDtypeStruct((n_tok, D_MODEL), F32),
        compiler_params=_cparams(("arbitrary", "arbitrary")),
        name="moe",
    )(h2, comb, xn, mod, wg, wu, wd)


def kernel(x_prompt, x_sample, cache_mla, state_mlstm_C, state_mlstm_n, state_mlstm_m, state_rwkv, state_gla, c, c_ctx, w_ada, b_ada, g_mix, g_ffn, w_in, w_out, b_ml_gates, g_ml_norm, g_mla_qlat, g_mla_kvlat, w_mla_uq, w_mla_ukv, g_mla_qn, g_mla_kn, rw_w0, rw_w_up, rw_a0, rw_a_up, rw_g_up, rw_k_k, rw_k_a, rw_r_k, rw_ln, gla_g_up, gla_g_b, gla_norm, moe_w_rg, moe_b_rg, moe_w_re, moe_b_re, moe_w_gate, moe_w_up, moe_w_down):
    cc = jnp.concatenate([c_ctx[None], c, jnp.zeros((3, D_MODEL), F32)], axis=0)
    mod = _modulation(cc, w_ada, b_ada).reshape(DEPTH, 8, 6, D_MODEL)
    mod = jnp.pad(mod, ((0, 0), (0, 0), (0, 2), (0, 0)))

    layers = []
    for l in range(DEPTH):
        p = {'b_ml_gates': b_ml_gates[l], 'g_ml_norm': g_ml_norm[l], 'g_mla_qlat': g_mla_qlat[l],
             'g_mla_kvlat': g_mla_kvlat[l], 'w_mla_uq': w_mla_uq[l], 'w_mla_ukv': w_mla_ukv[l],
             'g_mla_qn': g_mla_qn[l], 'g_mla_kn': g_mla_kn[l], 'rw_w0': rw_w0[l], 'rw_w_up': rw_w_up[l],
             'rw_a0': rw_a0[l], 'rw_a_up': rw_a_up[l], 'rw_g_up': rw_g_up[l], 'rw_k_k': rw_k_k[l],
             'rw_k_a': rw_k_a[l], 'rw_r_k': rw_r_k[l], 'rw_ln': rw_ln[l], 'gla_g_up': gla_g_up[l],
             'gla_g_b': gla_g_b[l], 'gla_norm': gla_norm[l]}
        router_w = jnp.zeros((D_MODEL, LANES), F32)
        router_w = router_w.at[:, :N_EXPERTS].set(moe_w_re[l]).at[:, N_EXPERTS:N_EXPERTS + N_GROUPS].set(moe_w_rg[l])
        router_b = jnp.zeros((1, LANES), F32)
        router_b = router_b.at[0, :N_EXPERTS].set(moe_b_re[l]).at[0, N_EXPERTS:N_EXPERTS + N_GROUPS].set(moe_b_rg[l])
        layers.append(dict(
            w_in=_pack_w_in(w_in[l]), g_mix=g_mix[l].reshape(1, -1), g_ffn=g_ffn[l].reshape(1, -1),
            w_out=w_out[l].astype(BF16), ml=_mlstm_params(p), mla=_mla_params(p), rw=_rwkv_params(p),
            gla=_gla_params(p), router_w=router_w, router_b=router_b,
            wg=moe_w_gate[l].astype(BF16), wu=moe_w_up[l].astype(BF16), wd=moe_w_down[l].astype(BF16)))

    def block(x2, B, T, mod_g, lp, ctx):
        u3 = _in_proj(x2, mod_g, lp['g_mix'], lp['w_in'], T).reshape(B, T, PK_COLS)
        if ctx is None:
            ctx_kv = None
            ml_c0, ml_m0 = _mlstm_state_zero(B)
            rw_h0 = jnp.zeros((B, 2, RW_H, RW_N, RW_N), F32)
            gla_s0 = jnp.zeros((B, 2, GLA_H, GLA_DK, GLA_DV), F32)
        else:
            ctx_kv, ml_C0, ml_n0, ml_m0_, rw_S0, gla_s0 = ctx
            ml_c0, ml_m0 = _mlstm_state_in(ml_C0, ml_n0, ml_m0_)
            rw_h0 = jnp.swapaxes(rw_S0, -1, -2)
        ml_out, ml_c, ml_m = _mlstm(u3, *lp['ml'], ml_c0, ml_m0)
        mla_out, own_kv = _mla(u3, ctx_kv, lp['mla'])
        rw_out, rw_h = _rwkv(u3, lp['rw'], rw_h0)
        gla_out, gla_s = _gla(u3, lp['gla'], gla_s0)
        mixed = [t.reshape(B * T, GROUP_W) for t in (ml_out, mla_out, rw_out, gla_out)]
        xn, h2, comb = _out_proj(x2, mixed, lp['w_out'], mod_g, lp['g_ffn'], lp['router_w'], lp['router_b'], T)
        x_new = _moe(h2, comb, xn, mod_g, lp['wg'], lp['wu'], lp['wd'], T)
        ml_C, ml_n, ml_mm = _mlstm_state_out(ml_c, ml_m)
        return x_new, (own_kv, ml_C, ml_n, ml_mm, jnp.swapaxes(rw_h, -1, -2), gla_s)

    Bp, Tp = x_prompt.shape[:2]
    Bs, Ts = x_sample.shape[:2]
    xp = x_prompt.reshape(Bp * Tp, D_MODEL)
    ctx_states = []
    for l in range(DEPTH):
        xp, st = block(xp, Bp, Tp, mod[l, 0:1], layers[l], None)
        ctx_states.append(st)
    xs = x_sample.reshape(Bs * Ts, D_MODEL)
    for l in range(DEPTH):
        ctx = (cache_mla[:, l], state_mlstm_C[:, l], state_mlstm_n[:, l], state_mlstm_m[:, l],
               state_rwkv[:, l], state_gla[:, l])
        xs, _ = block(xs, Bs, Ts, mod[l, 1:1 + Bs], layers[l], ctx)
    outs = [jnp.stack([s[i] for s in ctx_states], axis=1) for i in range(6)]
    return (xp.reshape(x_prompt.shape), xs.reshape(x_sample.shape), *outs)
```

```python
import functools
import math

import numpy as np
import jax
import jax.numpy as jnp
from jax import lax
from jax.experimental import pallas as pl
from jax.experimental.pallas import tpu as pltpu

F32 = jnp.float32
BF16 = jnp.bfloat16
HI = lax.Precision.HIGHEST

D_MODEL = 2048
DEPTH = 2
GRID_W = 64
GROUP_W = 512
ML_H, ML_DK = 4, 128
MLA_H, MLA_NOPE, MLA_ROPE, MLA_V = 4, 128, 64, 128
MLA_QK = MLA_NOPE + MLA_ROPE
Q_LORA, KV_LORA = 384, 256
ROPE_THETA = 10000.0
RW_H, RW_N = 8, 64
RW_DECAY_SCALE = math.exp(-0.5)
RW_LN_EPS = 64e-5
GLA_H, GLA_DK, GLA_DV = 4, 64, 128
GLA_GATE_RANK = 16
GLA_NORMALIZER = 16.0
N_GROUPS, EXPERTS_PER_GROUP, N_EXPERTS = 4, 4, 16
EXPERT_HIDDEN = 512
NORM_EPS = 1e-6
LANES = 128
ROW_TILES = D_MODEL // LANES
VMEM_LIMIT = 56 * 1024 * 1024

_REF_SPLITS = (
    ('ml_q', 512), ('ml_k', 512), ('ml_v', 512), ('ml_o', 512), ('ml_g', 16),
    ('mla_ql', Q_LORA), ('mla_ckv', KV_LORA), ('mla_kr', MLA_ROPE),
    ('rw_r', 512), ('rw_k', 512), ('rw_v', 512), ('rw_wd', 64), ('rw_ad', 64), ('rw_gd', 128),
    ('gla_q', 256), ('gla_k', 256), ('gla_v', 512), ('gla_gd', GLA_GATE_RANK), ('gla_g', 512),
)
_REF_OFF = {}
_o = 0
for _n, _w in _REF_SPLITS:
    _REF_OFF[_n] = (_o, _w)
    _o += _w
IN_COLS = _o

_PACKED = (
    ('ml_q', 512), ('ml_k', 512), ('ml_v', 512), ('ml_o', 512),
    ('rw_r', 512), ('rw_k', 512), ('rw_v', 512), ('gla_v', 512), ('gla_g', 512),
    ('mla_ql', 384), ('mla_ckv', 256), ('mla_kr', 64), ('mla_kr_sw', 64),
    ('gla_q', 256), ('gla_k', 256),
    ('ml_g', 128), ('rw_wd', 64), ('rw_ad', 64), ('rw_gd', 128), ('gla_gd', 128),
)
PK_OFF = {}
_o = 0
for _n, _w in _PACKED:
    PK_OFF[_n] = _o
    _o += _w
PK_COLS = _o


def _rope_swap_perm():
    idx = np.arange(MLA_ROPE)
    axis, half, f = idx // 32, (idx % 32) // 16, idx % 16
    return axis * 32 + (1 - half) * 16 + f


def _packed_column_index():
    src = np.full((PK_COLS,), -1, np.int64)
    for name, width in _PACKED:
        off = PK_OFF[name]
        if name == 'mla_kr_sw':
            s, w = _REF_OFF['mla_kr']
            src[off:off + w] = s + _rope_swap_perm()
        else:
            s, w = _REF_OFF[name]
            src[off:off + w] = s + np.arange(w)
    return src


_PK_SRC = _packed_column_index()


def _pack_w_in(w):
    valid = jnp.asarray(_PK_SRC >= 0)
    cols = jnp.take(w, jnp.asarray(np.maximum(_PK_SRC, 0)), axis=1)
    return jnp.where(valid[None, :], cols, 0.0).astype(BF16)


def _cparams(sem):
    return pltpu.CompilerParams(dimension_semantics=sem, vmem_limit_bytes=VMEM_LIMIT)


def _log_sigmoid(x):
    return jnp.minimum(x, 0.0) - jnp.log(1.0 + jnp.exp(-jnp.abs(x)))


def _sigmoid(x):
    return 1.0 / (1.0 + jnp.exp(-x))


def _dot(a, b):
    return jnp.dot(a.astype(BF16), b.astype(BF16), preferred_element_type=F32)


def _dot_nt(a, b):
    return lax.dot_general(a.astype(BF16), b.astype(BF16), (((1,), (1,)), ((), ())), preferred_element_type=F32)


def _dot_tn(a, b):
    return lax.dot_general(a.astype(BF16), b.astype(BF16), (((0,), (0,)), ((), ())), preferred_element_type=F32)


def _dot_hi(a, b):
    return jnp.dot(a, b, precision=HI, preferred_element_type=F32)


def _mod_kernel(c_ref, w_ref, b_ref, o_ref):
    c = c_ref[...]
    s = c * _sigmoid(c)
    o_ref[0] = _dot(s, w_ref[0]) + b_ref[0]


def _modulation(cc, w_ada, b_ada):
    tn = 1536
    n = 6 * D_MODEL
    return pl.pallas_call(
        _mod_kernel,
        grid=(DEPTH, n // tn),
        in_specs=[pl.BlockSpec((8, D_MODEL), lambda l, j: (0, 0)),
                  pl.BlockSpec((1, D_MODEL, tn), lambda l, j: (l, 0, j)),
                  pl.BlockSpec((1, 1, tn), lambda l, j: (l, 0, j))],
        out_specs=pl.BlockSpec((1, 8, tn), lambda l, j: (l, 0, j)),
        out_shape=jax.ShapeDtypeStruct((DEPTH, 8, n), F32),
        compiler_params=_cparams(("arbitrary", "arbitrary")),
        name="adaln_mod",
    )(cc, w_ada, b_ada.reshape(DEPTH, 1, n))


def _inproj_kernel(x_ref, mod_ref, g_ref, w_ref, o_ref, h_scr):
    @pl.when(pl.program_id(1) == 0)
    def _():
        x = x_ref[...]
        xn = x * lax.rsqrt(jnp.mean(x * x, axis=-1, keepdims=True) + NORM_EPS) * g_ref[...]
        h_scr[...] = (xn * (1.0 + mod_ref[0, 1:2, :]) + mod_ref[0, 0:1, :]).astype(BF16)

    o_ref[...] = jnp.dot(h_scr[...], w_ref[...], preferred_element_type=F32)


def _in_proj(x2, mod, g, w_packed, T):
    n_tok = x2.shape[0]
    tm = 1024 if T >= 1024 else 512
    tn = 640
    bm = mod.shape[0]
    mod_idx = (lambda i, j: (i * tm // T, 0, 0)) if bm > 1 else (lambda i, j: (0, 0, 0))
    return pl.pallas_call(
        _inproj_kernel,
        grid=(n_tok // tm, PK_COLS // tn),
        in_specs=[pl.BlockSpec((tm, D_MODEL), lambda i, j: (i, 0)),
                  pl.BlockSpec((1, 8, D_MODEL), mod_idx),
                  pl.BlockSpec((1, D_MODEL), lambda i, j: (0, 0)),
                  pl.BlockSpec((D_MODEL, tn), lambda i, j: (0, j))],
        out_specs=pl.BlockSpec((tm, tn), lambda i, j: (i, j)),
        out_shape=jax.ShapeDtypeStruct((n_tok, PK_COLS), F32),
        scratch_shapes=[pltpu.VMEM((tm, D_MODEL), BF16)],
        compiler_params=_cparams(("arbitrary", "arbitrary")),
        name="in_proj",
    )(x2, mod, g, w_packed)


ML_CHUNK = 256


def _mlstm_kernel(q_ref, k_ref, v_ref, o_ref, g_ref, bias_ref, gn_ref, c0_ref, m0_ref,
                  out_ref, c_ref, m_ref, hs_ref, *, T):
    L = ML_CHUNK
    nc = T // L
    c_ref[...] = c0_ref[...]
    m_ref[...] = m0_ref[...]
    ii = lax.broadcasted_iota(jnp.int32, (L, L), 0)
    jj = lax.broadcasted_iota(jnp.int32, (L, L), 1)
    lane = lax.broadcasted_iota(jnp.int32, (1, LANES), 1)
    is_f = jnp.logical_and(lane % 8 >= 4, lane < 16)
    ones_col = (lax.broadcasted_iota(jnp.int32, (L, LANES), 1) == 0).astype(BF16)
    scale = ML_DK ** -0.5

    for d in range(2):
        mask = (jj <= ii) if d == 0 else (jj >= ii)
        tri = mask.astype(F32)

        def chunk(ci, carry, d=d, mask=mask, tri=tri):
            c = ci if d == 0 else nc - 1 - ci
            s = pl.multiple_of(c * L, L)
            gates = g_ref[0, pl.ds(s, L), :] + bias_ref[...]
            gf = jnp.where(is_f, _log_sigmoid(gates), gates)
            cum = _dot_hi(tri, gf)
            gf_t = gf.T
            cum_t = cum.T
            for h in range(ML_H):
                ci_, cf_ = d * 8 + h, d * 8 + 4 + h
                hs = slice(h * ML_DK, (h + 1) * ML_DK)
                ig_col, ig_row = gf[:, ci_:ci_ + 1], gf_t[ci_:ci_ + 1, :]
                b_col, b_row = cum[:, cf_:cf_ + 1], cum_t[cf_:cf_ + 1, :]
                b_last = b_col[L - 1:L, :] if d == 0 else b_col[0:1, :]
                m_prev = m_ref[0, d, h][:, 0:1]
                dmat = jnp.where(mask, b_col + (ig_row - b_row), -jnp.inf)
                m_inter = b_col + m_prev
                m_row = jnp.maximum(m_inter, jnp.max(dmat, axis=-1, keepdims=True))
                w_inter = jnp.exp(m_inter - m_row)
                q = (q_ref[0, pl.ds(s, L), hs] * scale).astype(BF16)
                k = k_ref[0, pl.ds(s, L), hs]
                v_aug = jnp.concatenate([v_ref[0, pl.ds(s, L), hs].astype(BF16), ones_col], axis=1)
                smat = _dot_nt(q, k) * jnp.exp(dmat - m_row)
                c_aug = c_ref[0, d, h]
                nd = w_inter * _dot(q, c_aug) + _dot(smat, v_aug)
                num, den = nd[:, :ML_DK], nd[:, ML_DK:ML_DK + 1]
                hh = num / jnp.maximum(jnp.abs(den), jnp.exp(-m_row))
                if d == 0:
                    hs_ref[pl.ds(s, L), hs] = hh
                else:
                    hs_ref[pl.ds(s, L), hs] = hs_ref[pl.ds(s, L), hs] + hh
                dk_col = b_last - b_col + ig_col
                m_new = jnp.maximum(b_last + m_prev, jnp.max(dk_col, axis=0, keepdims=True))
                w_key = jnp.exp(dk_col - m_new)
                c_scale = jnp.exp(b_last + m_prev - m_new)
                c_ref[0, d, h] = c_scale * c_aug + _dot_tn(k * w_key, v_aug)
                m_ref[0, d, h] = jnp.broadcast_to(m_new, (1, LANES))
            return carry

        lax.fori_loop(0, nc, chunk, 0)

    def finish(ci, carry):
        s = pl.multiple_of(ci * L, L)
        for h in range(ML_H):
            hs = slice(h * ML_DK, (h + 1) * ML_DK)
            x = hs_ref[pl.ds(s, L), hs]
            xc = x - jnp.mean(x, axis=-1, keepdims=True)
            y = xc * lax.rsqrt(jnp.mean(xc * xc, axis=-1, keepdims=True) + NORM_EPS) * gn_ref[:, hs]
            out_ref[0, pl.ds(s, L), hs] = (y * _sigmoid(o_ref[0, pl.ds(s, L), hs])).astype(BF16)
        return carry

    lax.fori_loop(0, nc, finish, 0)


def _mlstm_params(p):
    bias = jnp.zeros((1, LANES), F32).at[0, :16].set(p['b_ml_gates'])
    return bias, p['g_ml_norm'].reshape(1, 512)


def _mlstm_state_in(C0, n0, m0):
    c0 = jnp.concatenate([C0, n0[..., None], jnp.zeros(C0.shape[:-1] + (ML_DK - 1,), F32)], axis=-1)
    return c0, jnp.broadcast_to(m0[..., None, None], m0.shape + (1, LANES))


def _mlstm_state_zero(B):
    return jnp.zeros((B, 2, ML_H, ML_DK, 2 * ML_DK), F32), jnp.zeros((B, 2, ML_H, 1, LANES), F32)


def _mlstm_state_out(c, m):
    return c[..., :ML_DK], c[..., ML_DK], m[..., 0, 0]


def _mlstm(u3, bias, gnorm, c0, m0):
    B, T, _ = u3.shape
    blk = lambda name: pl.BlockSpec((1, T, 512), lambda b, o=PK_OFF[name] // 512: (b, 0, o))
    st_c = pl.BlockSpec((1, 2, ML_H, ML_DK, 2 * ML_DK), lambda b: (b, 0, 0, 0, 0))
    st_m = pl.BlockSpec((1, 2, ML_H, 1, LANES), lambda b: (b, 0, 0, 0, 0))
    return pl.pallas_call(
        functools.partial(_mlstm_kernel, T=T),
        grid=(B,),
        in_specs=[blk('ml_q'), blk('ml_k'), blk('ml_v'), blk('ml_o'),
                  pl.BlockSpec((1, T, LANES), lambda b: (b, 0, PK_OFF['ml_g'] // LANES)),
                  pl.BlockSpec((1, LANES), lambda b: (0, 0)),
                  pl.BlockSpec((1, 512), lambda b: (0, 0)),
                  st_c, st_m],
        out_specs=[pl.BlockSpec((1, T, 512), lambda b: (b, 0, 0)), st_c, st_m],
        out_shape=[jax.ShapeDtypeStruct((B, T, 512), BF16),
                   jax.ShapeDtypeStruct(c0.shape, F32),
                   jax.ShapeDtypeStruct(m0.shape, F32)],
        scratch_shapes=[pltpu.VMEM((T, 512), F32)],
        compiler_params=_cparams(("arbitrary",)),
        name="mlstm",
    )(u3, u3, u3, u3, u3, bias, gnorm, c0, m0)


MLA_BLK = 256


def _rope_tables(T):
    rows = T // GRID_W
    row = np.repeat(np.arange(rows, dtype=np.float64), GRID_W)
    col = np.tile(np.arange(GRID_W, dtype=np.float64), rows)
    inv = ROPE_THETA ** (-np.arange(MLA_ROPE // 4, dtype=np.float64) / (MLA_ROPE // 4))
    ang = np.stack([row[:, None] * inv, col[:, None] * inv], axis=1)
    cos = np.stack([np.cos(ang), np.cos(ang)], axis=2).reshape(T, MLA_ROPE)
    sin = np.stack([-np.sin(ang), np.sin(ang)], axis=2).reshape(T, MLA_ROPE)
    return jnp.asarray(np.concatenate([cos, sin], axis=1), F32)


def _mla_kernel(*refs, T, n_ctx, rope):
    if n_ctx:
        (u_ref, ctx_ref, cs_ref, gql_ref, gkv_ref, gains_ref, wqn_ref, wqr_ref, wqs_ref, wkn_ref, wv_ref,
         out_ref, kv_ref, qn_s, qr_s, kn_s, kr_s, v_s) = refs
    else:
        (u_ref, cs_ref, gql_ref, gkv_ref, gains_ref, wqn_ref, wqr_ref, wqs_ref, wkn_ref, wv_ref,
         out_ref, kv_ref, qn_s, qr_s, kn_s, kr_s, v_s) = refs
    Lb = MLA_BLK
    gq_n, gq_r, gq_s = gains_ref[0:1, :], gains_ref[1:2, 0:64], gains_ref[1:2, 64:128]
    gk_n, gk_r, gk_s = gains_ref[2:3, :], gains_ref[3:4, 0:64], gains_ref[3:4, 64:128]
    sm_scale = MLA_QK ** -0.5

    def store_keys(s, kn, kr, krs, cos, sin):
        kr_ss = jnp.sum(kr * kr, axis=-1, keepdims=True)
        for h in range(MLA_H):
            kn_h = kn[:, h * 128:(h + 1) * 128]
            rk = lax.rsqrt((jnp.sum(kn_h * kn_h, axis=-1, keepdims=True) + kr_ss) / MLA_QK + NORM_EPS)
            kn_s[pl.ds(s, Lb), h * 128:(h + 1) * 128] = (kn_h * rk * gk_n).astype(BF16)
            kr_h = kr * gk_r
            if cos is not None:
                kr_h = kr_h * cos + (krs * gk_s) * sin
            kr_s[pl.ds(s, Lb), h * 64:(h + 1) * 64] = (kr_h * rk).astype(BF16)

    def prep(ci, carry):
        s = pl.multiple_of(ci * Lb, Lb)
        u = u_ref[0, pl.ds(s, Lb), :]
        ql, ckv, kr, krs = u[:, :384], u[:, 384:640], u[:, 640:704], u[:, 704:768]
        qln = ql * lax.rsqrt(jnp.mean(ql * ql, axis=-1, keepdims=True) + NORM_EPS) * gql_ref[...]
        ckvn = ckv * lax.rsqrt(jnp.mean(ckv * ckv, axis=-1, keepdims=True) + NORM_EPS) * gkv_ref[...]
        kv_ref[0, pl.ds(s, Lb), :] = jnp.concatenate([ckvn, kr], axis=1)
        cos = cs_ref[pl.ds(s, Lb), 0:64] if rope else None
        sin = cs_ref[pl.ds(s, Lb), 64:128] if rope else None
        qn, qr, qs = _dot(qln, wqn_ref[...]), _dot(qln, wqr_ref[...]), _dot(qln, wqs_ref[...])
        for h in range(MLA_H):
            qn_h, qr_h = qn[:, h * 128:(h + 1) * 128], qr[:, h * 64:(h + 1) * 64]
            ss = jnp.sum(qn_h * qn_h, axis=-1, keepdims=True) + jnp.sum(qr_h * qr_h, axis=-1, keepdims=True)
            rq = lax.rsqrt(ss / MLA_QK + NORM_EPS) * sm_scale
            qn_s[pl.ds(s, Lb), h * 128:(h + 1) * 128] = (qn_h * rq * gq_n).astype(BF16)
            qr_h = qr_h * gq_r
            if rope:
                qr_h = qr_h * cos + (qs[:, h * 64:(h + 1) * 64] * gq_s) * sin
            qr_s[pl.ds(s, Lb), h * 64:(h + 1) * 64] = (qr_h * rq).astype(BF16)
        v_s[pl.ds(s, Lb), :] = _dot(ckvn, wv_ref[...]).astype(BF16)
        store_keys(s, _dot(ckvn, wkn_ref[...]), kr, krs, cos, sin)
        return carry

    lax.fori_loop(0, T // Lb, prep, 0)

    for ci in range(n_ctx // Lb):
        cx = ctx_ref[0, ci * Lb:(ci + 1) * Lb, :]
        ckv_c, kr_c = cx[:, :KV_LORA], cx[:, KV_LORA:KV_LORA + MLA_ROPE]
        v_s[T + ci * Lb:T + (ci + 1) * Lb, :] = _dot(ckv_c, wv_ref[...]).astype(BF16)
        store_keys(T + ci * Lb, _dot(ckv_c, wkn_ref[...]), kr_c, None, None, None)

    for h in range(MLA_H):
        def attend(qi, carry, h=h):
            s = pl.multiple_of(qi * Lb, Lb)
            sc = (_dot_nt(qn_s[pl.ds(s, Lb), h * 128:(h + 1) * 128], kn_s[:, h * 128:(h + 1) * 128])
                  + _dot_nt(qr_s[pl.ds(s, Lb), h * 64:(h + 1) * 64], kr_s[:, h * 64:(h + 1) * 64]))
            p = jnp.exp(sc - jnp.max(sc, axis=-1, keepdims=True))
            o = _dot(p, v_s[:, h * 128:(h + 1) * 128]) / jnp.sum(p, axis=-1, keepdims=True)
            out_ref[0, pl.ds(s, Lb), h * 128:(h + 1) * 128] = o.astype(BF16)
            return carry

        lax.fori_loop(0, T // Lb, attend, 0)


def _mla_params(p):
    wq = p['w_mla_uq'].reshape(Q_LORA, MLA_H, MLA_QK)
    sw = _rope_swap_perm()
    wq_n = wq[:, :, :MLA_NOPE].reshape(Q_LORA, 512).astype(BF16)
    wq_r = wq[:, :, MLA_NOPE:].reshape(Q_LORA, 256).astype(BF16)
    wq_s = wq[:, :, MLA_NOPE:][:, :, sw].reshape(Q_LORA, 256).astype(BF16)
    wkv = p['w_mla_ukv'].reshape(KV_LORA, MLA_H, MLA_NOPE + MLA_V)
    wk_n = wkv[:, :, :MLA_NOPE].reshape(KV_LORA, 512).astype(BF16)
    wv = wkv[:, :, MLA_NOPE:].reshape(KV_LORA, 512).astype(BF16)
    gq, gk = p['g_mla_qn'], p['g_mla_kn']
    gains = jnp.zeros((8, LANES), F32)
    gains = gains.at[0].set(gq[:128]).at[1, :64].set(gq[128:]).at[1, 64:].set(gq[128:][sw])
    gains = gains.at[2].set(gk[:128]).at[3, :64].set(gk[128:]).at[3, 64:].set(gk[128:][sw])
    return (p['g_mla_qlat'].reshape(1, -1), p['g_mla_kvlat'].reshape(1, -1), gains, wq_n, wq_r, wq_s, wk_n, wv)


def _mla(u3, ctx_kv, mp):
    B, T, _ = u3.shape
    n_ctx = 0 if ctx_kv is None else ctx_kv.shape[1]
    rope = ctx_kv is not None
    tk = T + n_ctx
    full = lambda a: pl.BlockSpec(a.shape, lambda b, n=a.ndim: (0,) * n)
    cs = _rope_tables(T) if rope else jnp.zeros((T, LANES), F32)
    ins = [u3] + ([ctx_kv] if rope else []) + [cs] + list(mp)
    specs = [pl.BlockSpec((1, T, 768), lambda b: (b, 0, PK_OFF['mla_ql'] // 768))]
    if rope:
        specs.append(pl.BlockSpec((1, n_ctx, KV_LORA + MLA_ROPE), lambda b: (b, 0, 0)))
    specs += [full(a) for a in ins[len(specs):]]
    return pl.pallas_call(
        functools.partial(_mla_kernel, T=T, n_ctx=n_ctx, rope=rope),
        grid=(B,),
        in_specs=specs,
        out_specs=[pl.BlockSpec((1, T, 512), lambda b: (b, 0, 0)),
                   pl.BlockSpec((1, T, KV_LORA + MLA_ROPE), lambda b: (b, 0, 0))],
        out_shape=[jax.ShapeDtypeStruct((B, T, 512), BF16),
                   jax.ShapeDtypeStruct((B, T, KV_LORA + MLA_ROPE), F32)],
        scratch_shapes=[pltpu.VMEM((T, 512), BF16), pltpu.VMEM((T, 256), BF16),
                        pltpu.VMEM((tk, 512), BF16), pltpu.VMEM((tk, 256), BF16), pltpu.VMEM((tk, 512), BF16)],
        compiler_params=_cparams(("arbitrary",)),
        name="mla",
    )(*ins)


GLA_CHUNK = 64
GLA_LEAF = 4


def _gla_kernel(q_ref, k_ref, v_ref, gd_ref, g_ref, gup_ref, gb_ref, gn_ref, hsel_ref, s0_ref,
                out_ref, s_ref, os_ref, *, T):
    L, C = GLA_CHUNK, GLA_LEAF
    nc = T // L
    s_ref[...] = s0_ref[...]
    ii = lax.broadcasted_iota(jnp.int32, (L, L), 0)
    jj = lax.broadcasted_iota(jnp.int32, (L, L), 1)
    eye = (lax.broadcasted_iota(jnp.int32, (GLA_DK, GLA_DK), 0)
           == lax.broadcasted_iota(jnp.int32, (GLA_DK, GLA_DK), 1)).astype(F32)
    row_id = lax.broadcasted_iota(jnp.int32, (L, 1), 0)
    scale = GLA_DK ** -0.5
    hsel = hsel_ref[...]
    levels = []
    span = C
    while span < L:
        levels.append(span)
        span *= 2

    for d in range(2):
        causal = (jj <= ii) if d == 0 else (jj >= ii)
        tri = causal.astype(F32)

        def chunk(ci, carry, d=d, tri=tri):
            c = ci if d == 0 else nc - 1 - ci
            s = pl.multiple_of(c * L, L)
            la = _log_sigmoid(_dot_hi(gd_ref[0, pl.ds(s, L), :], gup_ref[d]) + gb_ref[d]) / GLA_NORMALIZER
            b = _dot_hi(tri, la)
            total = b[L - 1:L, :] if d == 0 else b[0:1, :]
            q = q_ref[0, pl.ds(s, L), :] * scale
            k = k_ref[0, pl.ds(s, L), :]
            amat = [jnp.zeros((L, L), F32) for _ in range(GLA_H)]
            for sp in levels:
                b3 = b.reshape(L // (2 * sp), 2 * sp, GLA_H * GLA_DK)
                edge = b3[:, sp - 1:sp, :] if d == 0 else b3[:, sp:sp + 1, :]
                bref = jnp.broadcast_to(edge, b3.shape).reshape(L, GLA_H * GLA_DK)
                later = (row_id % (2 * sp) >= sp) if d == 0 else (row_id % (2 * sp) < sp)
                e = jnp.exp(jnp.where(later, b - bref, bref - b))
                qs = jnp.where(later, q * e, 0.0)
                kt = jnp.where(later, 0.0, k * e)
                same = (ii // (2 * sp)) == (jj // (2 * sp))
                for h in range(GLA_H):
                    ks = slice(h * GLA_DK, (h + 1) * GLA_DK)
                    amat[h] = amat[h] + jnp.where(same, _dot_nt(qs[:, ks], kt[:, ks]), 0.0)
            terms = []
            for dl in range(C):
                if dl == 0:
                    terms.append(q * k)
                    continue
                sh = dl if d == 0 else L - dl
                ok = (row_id % C >= dl) if d == 0 else (row_id % C < C - dl)
                kd = pltpu.roll(k, sh, axis=0)
                bd = pltpu.roll(b, sh, axis=0)
                terms.append(q * kd * jnp.exp(jnp.where(ok, b - bd, 0.0)))
            tt = jnp.concatenate(terms, axis=0)
            t_hi = tt.astype(BF16)
            t_lo = (tt - t_hi.astype(F32)).astype(BF16)
            diag = (jnp.dot(t_hi, hsel, preferred_element_type=F32)
                    + jnp.dot(t_lo, hsel, preferred_element_type=F32))
            for dl in range(C):
                pair = jnp.logical_and(jj == (ii - dl if d == 0 else ii + dl), ii // C == jj // C)
                for h in range(GLA_H):
                    amat[h] = amat[h] + jnp.where(pair, diag[dl * L:(dl + 1) * L, h:h + 1], 0.0)
            q_in = q * jnp.exp(b)
            k_out = k * jnp.exp(total - b)
            f_row = jnp.exp(total)
            for h in range(GLA_H):
                ks = slice(h * GLA_DK, (h + 1) * GLA_DK)
                vs = slice(h * GLA_DV, (h + 1) * GLA_DV)
                v = v_ref[0, pl.ds(s, L), vs]
                st = s_ref[0, d, h]
                o = _dot(amat[h], v) + _dot(q_in[:, ks], st)
                if d == 0:
                    os_ref[pl.ds(s, L), vs] = o
                else:
                    os_ref[pl.ds(s, L), vs] = os_ref[pl.ds(s, L), vs] + o
                f_col = jnp.sum(eye * f_row[:, ks], axis=1, keepdims=True)
                s_ref[0, d, h] = f_col * st + _dot_tn(k_out[:, ks], v)
            return carry

        lax.fori_loop(0, nc, chunk, 0)

    Lf = 256

    def finish(ci, carry):
        s = pl.multiple_of(ci * Lf, Lf)
        for h in range(GLA_H):
            vs = slice(h * GLA_DV, (h + 1) * GLA_DV)
            o = os_ref[pl.ds(s, Lf), vs]
            y = o * lax.rsqrt(jnp.mean(o * o, axis=-1, keepdims=True) + NORM_EPS) * gn_ref[:, vs]
            g = g_ref[0, pl.ds(s, Lf), vs]
            out_ref[0, pl.ds(s, Lf), vs] = (y * (g * _sigmoid(g))).astype(BF16)
        return carry

    lax.fori_loop(0, T // Lf, finish, 0)


def _gla_params(p):
    gup = jnp.zeros((2, LANES, GLA_H * GLA_DK), F32).at[:, :GLA_GATE_RANK, :].set(p['gla_g_up'])
    hsel = jnp.asarray(np.arange(GLA_H * GLA_DK)[:, None] // GLA_DK == np.arange(LANES)[None, :], BF16)
    return gup, p['gla_g_b'].reshape(2, 1, -1), p['gla_norm'].reshape(1, -1), hsel


def _gla(u3, gp, s0):
    B, T, _ = u3.shape
    blk = lambda name, w: pl.BlockSpec((1, T, w), lambda b, o=PK_OFF[name] // w: (b, 0, o))
    full = lambda a: pl.BlockSpec(a.shape, lambda b, n=a.ndim: (0,) * n)
    st = pl.BlockSpec((1, 2, GLA_H, GLA_DK, GLA_DV), lambda b: (b, 0, 0, 0, 0))
    return pl.pallas_call(
        functools.partial(_gla_kernel, T=T),
        grid=(B,),
        in_specs=[blk('gla_q', 256), blk('gla_k', 256), blk('gla_v', 512), blk('gla_gd', 128), blk('gla_g', 512),
                  full(gp[0]), full(gp[1]), full(gp[2]), full(gp[3]), st],
        out_specs=[pl.BlockSpec((1, T, 512), lambda b: (b, 0, 0)), st],
        out_shape=[jax.ShapeDtypeStruct((B, T, 512), BF16), jax.ShapeDtypeStruct(s0.shape, F32)],
        scratch_shapes=[pltpu.VMEM((T, 512), F32)],
        compiler_params=_cparams(("arbitrary",)),
        name="gla",
    )(u3, u3, u3, u3, u3, *gp, s0)


RW_CHUNK = 64


def _seg_sum(x, bd):
    hi = x.astype(BF16)
    lo = (x - hi.astype(F32)).astype(BF16)
    return jnp.dot(hi, bd, preferred_element_type=F32) + jnp.dot(lo, bd, preferred_element_type=F32)


def _rwkv_kernel(r_ref, k_ref, v_ref, wa_ref, gd_ref, wwa_ref, w0a0_ref, gup_ref, kk_ref, ka_ref, rk_ref, ln_ref,
                 bd_ref, h0_ref, out_ref, h_ref, ys_ref, bonus_ref, *, T):
    L, N = RW_CHUNK, RW_N
    nc = T // L
    h_ref[...] = h0_ref[...]
    ii = lax.broadcasted_iota(jnp.int32, (L, L), 0)
    jj = lax.broadcasted_iota(jnp.int32, (L, L), 1)
    eye = (ii == jj).astype(F32)
    lane = lax.broadcasted_iota(jnp.int32, (1, LANES), 1)
    bd = bd_ref[...]

    for d in range(2):
        strict = (jj < ii) if d == 0 else (jj > ii)
        incl = (jj <= ii) if d == 0 else (jj >= ii)
        tri = incl.astype(F32)

        def chunk(ci, carry, d=d, strict=strict, incl=incl, tri=tri):
            c = ci if d == 0 else nc - 1 - ci
            s = pl.multiple_of(c * L, L)
            r = r_ref[0, pl.ds(s, L), :]
            k = k_ref[0, pl.ds(s, L), :]
            v = v_ref[0, pl.ds(s, L), :]
            wa = wa_ref[0, pl.ds(s, L), :]
            pre = _dot_hi(jnp.where(lane < 64, jnp.tanh(wa), wa), wwa_ref[d]) + w0a0_ref[d]
            logw = -RW_DECAY_SCALE * _sigmoid(pre[:, :512])
            a = _sigmoid(pre[:, 512:])
            kkr = k * kk_ref[...]
            kk = kkr * lax.rsqrt(_seg_sum(kkr * kkr, bd) + 1e-12)
            kt = k * (1.0 + (a - 1.0) * ka_ref[...])
            bh = kk * a
            bonus = _seg_sum(r * kt * rk_ref[...], bd) * v
            if d == 0:
                bonus_ref[pl.ds(s, L), :] = bonus
            else:
                bonus_ref[pl.ds(s, L), :] = bonus_ref[pl.ds(s, L), :] + bonus
            lg = _dot_hi(tri, logw)
            lg_end = lg[L - 1:L, :] if d == 0 else lg[0:1, :]
            a_t = -kk * jnp.exp(lg - logw)
            r_t = r * jnp.exp(lg)
            e_inv = jnp.exp(-lg)
            k_t, b_t = kt * e_inv, bh * e_inv
            e_end = jnp.exp(lg_end - lg)
            k_e, b_e = kt * e_end, bh * e_end
            g_end = jnp.exp(lg_end)
            heads = range(RW_H)
            sls = [slice(h * N, (h + 1) * N) for h in heads]
            ms = [_dot_nt(jnp.concatenate([a_t[:, sl], r_t[:, sl]], axis=0),
                          jnp.concatenate([b_t[:, sl], k_t[:, sl]], axis=0)) for sl in sls]
            pws = [jnp.where(strict, m[:L, :L], 0.0) for m in ms]
            xs = [eye + n for n in pws]
            for _ in range(5):
                pws = [_dot(pw, pw) for pw in pws]
                xs = [x + _dot(x, pw) for x, pw in zip(xs, pws)]
            vhs = [v[:, sl] for sl in sls]
            mvs = [_dot(jnp.where(strict, m[:L, L:], 0.0), vh) for m, vh in zip(ms, vhs)]
            tws = [_dot(x, jnp.concatenate([a_t[:, sl], mv], axis=1))
                   for x, sl, mv in zip(xs, sls, mvs)]
            qys = [_dot(jnp.where(incl, m[L:, :L], 0.0), tw) for m, tw in zip(ms, tws)]
            ylocs = [_dot(jnp.where(incl, m[L:, L:], 0.0), vh) + qy[:, N:] for m, vh, qy in zip(ms, vhs, qys)]
            pgs = [_dot_tn(b_e[:, sl], tw) for sl, tw in zip(sls, tws)]
            gmats = [_dot_tn(k_e[:, sl], vh) + pg[:, N:] for sl, vh, pg in zip(sls, vhs, pgs)]
            for h in heads:
                sl = sls[h]
                hst = h_ref[0, d, h]
                y = _dot(r_t[:, sl] + qys[h][:, :N], hst) + ylocs[h]
                if d == 0:
                    ys_ref[pl.ds(s, L), sl] = y
                else:
                    ys_ref[pl.ds(s, L), sl] = ys_ref[pl.ds(s, L), sl] + y
                h_ref[0, d, h] = _dot(eye * g_end[:, sl] + pgs[h][:, :N], hst) + gmats[h]
            return carry

        lax.fori_loop(0, nc, chunk, 0)

    Lf = 256

    def finish(ci, carry):
        s = pl.multiple_of(ci * Lf, Lf)
        y = ys_ref[pl.ds(s, Lf), :]
        yc = y - _seg_sum(y, bd) / N
        yn = yc * lax.rsqrt(_seg_sum(yc * yc, bd) / N + RW_LN_EPS) * ln_ref[...]
        g = _dot(_sigmoid(gd_ref[0, pl.ds(s, Lf), :]), gup_ref[...])
        out_ref[0, pl.ds(s, Lf), :] = ((yn + bonus_ref[pl.ds(s, Lf), :]) * g).astype(BF16)
        return carry

    lax.fori_loop(0, T // Lf, finish, 0)


def _rwkv_params(p):
    wwa = jnp.zeros((2, LANES, 1024), F32)
    wwa = wwa.at[:, :64, :512].set(p['rw_w_up']).at[:, 64:, 512:].set(p['rw_a_up'])
    w0a0 = jnp.concatenate([p['rw_w0'], p['rw_a0']], axis=-1).reshape(2, 1, 1024)
    seg = np.arange(512) // RW_N
    bd = jnp.asarray(seg[:, None] == seg[None, :], BF16)
    row = lambda n: p[n].reshape(1, -1)
    return (wwa, w0a0, p['rw_g_up'].astype(BF16), row('rw_k_k'), row('rw_k_a'), row('rw_r_k'), row('rw_ln'), bd)


def _rwkv(u3, rp, h0):
    B, T, _ = u3.shape
    blk = lambda name, w: pl.BlockSpec((1, T, w), lambda b, o=PK_OFF[name] // w: (b, 0, o))
    full = lambda a: pl.BlockSpec(a.shape, lambda b, n=a.ndim: (0,) * n)
    st = pl.BlockSpec((1, 2, RW_H, RW_N, RW_N), lambda b: (b, 0, 0, 0, 0))
    return pl.pallas_call(
        functools.partial(_rwkv_kernel, T=T),
        grid=(B,),
        in_specs=[blk('rw_r', 512), blk('rw_k', 512), blk('rw_v', 512), blk('rw_wd', 128), blk('rw_gd', 128)]
                 + [full(a) for a in rp] + [st],
        out_specs=[pl.BlockSpec((1, T, 512), lambda b: (b, 0, 0)), st],
        out_shape=[jax.ShapeDtypeStruct((B, T, 512), BF16), jax.ShapeDtypeStruct(h0.shape, F32)],
        scratch_shapes=[pltpu.VMEM((T, 512), F32), pltpu.VMEM((T, 512), F32)],
        compiler_params=_cparams(("arbitrary",)),
        name="rwkv7",
    )(u3, u3, u3, u3, u3, *rp, h0)


def _route(logits):
    lane = lax.broadcasted_iota(jnp.int32, (1, LANES), 1)
    far = jnp.int32(2 * LANES)
    neg = -jnp.inf
    gl = jnp.where(jnp.logical_and(lane >= N_EXPERTS, lane < N_EXPERTS + N_GROUPS), logits, neg)
    gmax = jnp.max(gl, axis=-1, keepdims=True)
    grp = jnp.min(jnp.where(gl == gmax, lane, far), axis=-1, keepdims=True) - N_EXPERTS
    p_grp = 1.0 / jnp.sum(jnp.exp(gl - gmax), axis=-1, keepdims=True)
    el = jnp.where(jnp.logical_and(lane < N_EXPERTS, lane // EXPERTS_PER_GROUP == grp), logits, neg)
    v1 = jnp.max(el, axis=-1, keepdims=True)
    i1 = jnp.min(jnp.where(el == v1, lane, far), axis=-1, keepdims=True)
    el2 = jnp.where(lane == i1, neg, el)
    v2 = jnp.max(el2, axis=-1, keepdims=True)
    i2 = jnp.min(jnp.where(el2 == v2, lane, far), axis=-1, keepdims=True)
    e = jnp.exp(v2 - v1)
    w1 = 1.0 / (1.0 + e)
    comb = jnp.where(lane == i1, p_grp * w1, jnp.where(lane == i2, p_grp * (e * w1), 0.0))
    sel = jnp.where(lane == i1, 1.0, jnp.where(lane == i2, 2.0, 0.0))
    return comb, sel


def _outproj_kernel(x_ref, m0_ref, m1_ref, m2_ref, m3_ref, w_ref, mod_ref, g_ref, wr_ref, br_ref,
                    xn_ref, h3_ref, comb_ref, sel_ref):
    y = jnp.dot(m0_ref[...], w_ref[0:512, :], preferred_element_type=F32)
    for i, m_ref in enumerate((m1_ref, m2_ref, m3_ref), start=1):
        y = y + jnp.dot(m_ref[...], w_ref[i * 512:(i + 1) * 512, :], preferred_element_type=F32)
    xn = x_ref[...] + mod_ref[0, 2:3, :] * y
    xn_ref[...] = xn
    h = xn * lax.rsqrt(jnp.mean(xn * xn, axis=-1, keepdims=True) + NORM_EPS) * g_ref[...]
    h = h * (1.0 + mod_ref[0, 4:5, :]) + mod_ref[0, 3:4, :]
    for c in range(ROW_TILES):
        h3_ref[:, c, :] = h[:, c * LANES:(c + 1) * LANES]
    comb_ref[...], sel_ref[...] = _route(_dot_hi(h, wr_ref[...]) + br_ref[...])


def _out_proj(x2, mixed, w_out, mod, g, router_w, router_b, T):
    n_tok = x2.shape[0]
    tm = 512
    bm = mod.shape[0]
    mod_idx = (lambda i: (i * tm // T, 0, 0)) if bm > 1 else (lambda i: (0, 0, 0))
    row = lambda w: pl.BlockSpec((tm, w), lambda i: (i, 0))
    full = lambda a: pl.BlockSpec(a.shape, lambda i, n=a.ndim: (0,) * n)
    return pl.pallas_call(
        _outproj_kernel,
        grid=(n_tok // tm,),
        in_specs=[row(D_MODEL)] + [row(GROUP_W)] * 4 + [full(w_out), pl.BlockSpec((1, 8, D_MODEL), mod_idx),
                                                       full(g), full(router_w), full(router_b)],
        out_specs=[row(D_MODEL), pl.BlockSpec((tm, ROW_TILES, LANES), lambda i: (i, 0, 0)), row(LANES), row(LANES)],
        out_shape=[jax.ShapeDtypeStruct((n_tok, D_MODEL), F32),
                   jax.ShapeDtypeStruct((n_tok, ROW_TILES, LANES), F32),
                   jax.ShapeDtypeStruct((n_tok, LANES), F32), jax.ShapeDtypeStruct((n_tok, LANES), F32)],
        compiler_params=_cparams(("arbitrary",)),
        name="out_proj",
    )(x2, *mixed, w_out, mod, g, router_w, router_b)


MOE_TM = 256
MOE_WINDOW = 32


def _moe_rows(n_tok):
    return 2 * n_tok + N_EXPERTS * MOE_TM


def _plan_kernel(sel_ref, pos_ref, tmap_ref):
    n_tok = sel_ref.shape[0]
    blk = 512
    lane = lax.broadcasted_iota(jnp.int32, (1, LANES), 1)
    earlier = (lax.broadcasted_iota(jnp.int32, (blk, blk), 1)
               < lax.broadcasted_iota(jnp.int32, (blk, blk), 0)).astype(BF16)
    before = (lax.broadcasted_iota(jnp.int32, (LANES, LANES), 0)
              < lax.broadcasted_iota(jnp.int32, (LANES, LANES), 1)).astype(BF16)

    def count(i, acc):
        s = pl.multiple_of(i * blk, blk)
        return acc + jnp.sum((sel_ref[pl.ds(s, blk), :] > 0.0).astype(F32), axis=0, keepdims=True)

    counts = lax.fori_loop(0, n_tok // blk, count, jnp.zeros((1, LANES), F32))
    tiles = jnp.floor((counts + (MOE_TM - 1)) * (1.0 / MOE_TM))
    tile_start = _dot(jnp.broadcast_to(tiles, (8, LANES)), before)[0:1, :]
    tile_end = tile_start + tiles
    base = tile_start * MOE_TM
    n_tiles = jnp.sum(tiles, axis=-1, keepdims=True)
    j = lax.broadcasted_iota(jnp.int32, (tmap_ref.shape[0], 1), 0).astype(F32)
    done = jnp.logical_and(tile_end <= j, lane < N_EXPERTS)
    expert = jnp.minimum(jnp.sum(done.astype(F32), axis=-1, keepdims=True), N_EXPERTS - 1.0)
    valid = (j < n_tiles).astype(F32)
    tmap_ref[...] = jnp.where(lane == 0, expert, jnp.where(lane == 1, valid, 0.0)).astype(jnp.int32)

    def place(i, seen):
        s = pl.multiple_of(i * blk, blk)
        sel = sel_ref[pl.ds(s, blk), :]
        one = (sel > 0.0).astype(F32)
        row = base + seen + _dot(earlier, one)
        p1 = jnp.sum(jnp.where(sel == 1.0, row, 0.0), axis=-1, keepdims=True)
        p2 = jnp.sum(jnp.where(sel == 2.0, row, 0.0), axis=-1, keepdims=True)
        pos_ref[pl.ds(s, blk), :] = jnp.where(lane == 0, p1, jnp.where(lane == 1, p2, 0.0)).astype(jnp.int32)
        return seen + jnp.sum(one, axis=0, keepdims=True)

    lax.fori_loop(0, n_tok // blk, place, jnp.zeros((1, LANES), F32))


def _moe_plan(sel):
    n_tok = sel.shape[0]
    n_tiles = _moe_rows(n_tok) // MOE_TM
    pos, tmap = pl.pallas_call(
        _plan_kernel,
        out_shape=[jax.ShapeDtypeStruct((n_tok, LANES), jnp.int32), jax.ShapeDtypeStruct((64, LANES), jnp.int32)],
        compiler_params=pltpu.CompilerParams(vmem_limit_bytes=VMEM_LIMIT),
        name="moe_plan",
    )(sel)
    return jnp.transpose(pos[:, :2]), tmap[:n_tiles, 0], tmap[:n_tiles, 1]


def _row_copy_kernel(pos_ref, src_ref, init_ref, dst_ref, sem, *, n_tok, scatter):
    del init_ref

    def copy(t, k):
        if scatter:
            return pltpu.make_async_copy(src_ref.at[t], dst_ref.at[pos_ref[k, t]], sem)
        return pltpu.make_async_copy(src_ref.at[pos_ref[k, t]], dst_ref.at[t, k], sem)

    def start(t):
        copy(t, 0).start()
        copy(t, 1).start()

    def wait(t):
        copy(t, 0).wait()
        copy(t, 1).wait()

    def prologue(t, c):
        start(t)
        return c

    def steady(t, c):
        wait(t - MOE_WINDOW)
        start(t)
        return c

    def epilogue(t, c):
        wait(t)
        return c

    lax.fori_loop(0, MOE_WINDOW, prologue, 0)
    lax.fori_loop(MOE_WINDOW, n_tok, steady, 0)
    lax.fori_loop(n_tok - MOE_WINDOW, n_tok, epilogue, 0)


def _row_copy(pos, src, init, scatter):
    n_tok = pos.shape[1]
    any_spec = pl.BlockSpec(memory_space=pl.ANY)
    return pl.pallas_call(
        functools.partial(_row_copy_kernel, n_tok=n_tok, scatter=scatter),
        grid_spec=pltpu.PrefetchScalarGridSpec(
            num_scalar_prefetch=1, grid=(1,), in_specs=[any_spec, any_spec], out_specs=any_spec,
            scratch_shapes=[pltpu.SemaphoreType.DMA(())]),
        out_shape=jax.ShapeDtypeStruct(init.shape, init.dtype),
        input_output_aliases={2: 0},
        compiler_params=pltpu.CompilerParams(dimension_semantics=("arbitrary",)),
        name="moe_dispatch" if scatter else "moe_collect",
    )(pos, src, init)


def _experts_kernel(te_ref, tv_ref, xs_ref, wg_ref, wu_ref, wd_ref, ys_ref, wg_s, wu_s, wd_s):
    i = pl.program_id(0)
    fresh = jnp.logical_or(i == 0, te_ref[i] != te_ref[jnp.maximum(i - 1, 0)])

    @pl.when(jnp.logical_and(fresh, tv_ref[i] == 1))
    def _():
        wg_s[...] = wg_ref[0].astype(BF16)
        wu_s[...] = wu_ref[0].astype(BF16)
        wd_s[...] = wd_ref[0].astype(BF16)

    @pl.when(tv_ref[i] == 1)
    def _():
        x = jnp.concatenate([xs_ref[:, c, :] for c in range(ROW_TILES)], axis=1).astype(BF16)
        a = jnp.dot(x, wg_s[...], preferred_element_type=F32)
        b = jnp.dot(x, wu_s[...], preferred_element_type=F32)
        y = jnp.dot(((a * _sigmoid(a)) * b).astype(BF16), wd_s[...], preferred_element_type=F32)
        for c in range(ROW_TILES):
            ys_ref[:, c, :] = y[:, c * LANES:(c + 1) * LANES]

    @pl.when(tv_ref[i] == 0)
    def _():
        ys_ref[...] = jnp.zeros_like(ys_ref)


def _experts(tile_expert, tile_valid, xs, wg, wu, wd):
    n_rows = xs.shape[0]
    rows = pl.BlockSpec((MOE_TM, ROW_TILES, LANES), lambda i, te, tv: (i, 0, 0))
    return pl.pallas_call(
        _experts_kernel,
        grid_spec=pltpu.PrefetchScalarGridSpec(
            num_scalar_prefetch=2, grid=(n_rows // MOE_TM,),
            in_specs=[rows,
                      pl.BlockSpec((1, D_MODEL, EXPERT_HIDDEN), lambda i, te, tv: (te[i], 0, 0)),
                      pl.BlockSpec((1, D_MODEL, EXPERT_HIDDEN), lambda i, te, tv: (te[i], 0, 0)),
                      pl.BlockSpec((1, EXPERT_HIDDEN, D_MODEL), lambda i, te, tv: (te[i], 0, 0))],
            out_specs=rows,
            scratch_shapes=[pltpu.VMEM((D_MODEL, EXPERT_HIDDEN), BF16), pltpu.VMEM((D_MODEL, EXPERT_HIDDEN), BF16),
                            pltpu.VMEM((EXPERT_HIDDEN, D_MODEL), BF16)]),
        out_shape=jax.ShapeDtypeStruct(xs.shape, F32),
        compiler_params=_cparams(("arbitrary",)),
        name="moe_experts",
    )(tile_expert, tile_valid, xs, wg, wu, wd)


def _combine_kernel(y_ref, comb_ref, sel_ref, xn_ref, mod_ref, o_ref):
    comb, sel = comb_ref[...], sel_ref[...]
    w1 = jnp.sum(jnp.where(sel == 1.0, comb, 0.0), axis=-1, keepdims=True)
    w2 = jnp.sum(jnp.where(sel == 2.0, comb, 0.0), axis=-1, keepdims=True)
    for c in range(ROW_TILES):
        cs = slice(c * LANES, (c + 1) * LANES)
        moe = w1 * y_ref[:, 0, c, :] + w2 * y_ref[:, 1, c, :]
        o_ref[:, cs] = xn_ref[:, cs] + mod_ref[0, 5:6, cs] * moe


def _combine(y2, comb, sel, xn, mod, T):
    n_tok = xn.shape[0]
    tm = 512
    bm = mod.shape[0]
    mod_idx = (lambda i: (i * tm // T, 0, 0)) if bm > 1 else (lambda i: (0, 0, 0))
    row = lambda w: pl.BlockSpec((tm, w), lambda i: (i, 0))
    return pl.pallas_call(
        _combine_kernel,
        grid=(n_tok // tm,),
        in_specs=[pl.BlockSpec((tm, 2, ROW_TILES, LANES), lambda i: (i, 0, 0, 0)), row(LANES), row(LANES),
                  row(D_MODEL), pl.BlockSpec((1, 8, D_MODEL), mod_idx)],
        out_specs=row(D_MODEL),
        out_shape=jax.ShapeDtypeStruct((n_tok, D_MODEL), F32),
        compiler_params=_cparams(("arbitrary",)),
        name="moe_combine",
    )(y2, comb, sel, xn, mod)


def _moe(h3, comb, sel, xn, mod, wg, wu, wd, T):
    n_tok = h3.shape[0]
    pos, tile_expert, tile_valid = _moe_plan(sel)
    xs = _row_copy(pos, h3, jnp.zeros((_moe_rows(n_tok), ROW_TILES, LANES), F32), scatter=True)
    ys = _experts(tile_expert, tile_valid, xs, wg, wu, wd)
    y2 = _row_copy(pos, ys, jnp.zeros((n_tok, 2, ROW_TILES, LANES), F32), scatter=False)
    return _combine(y2, comb, sel, xn, mod, T)


def kernel(x_prompt, x_sample, cache_mla, state_mlstm_C, state_mlstm_n, state_mlstm_m, state_rwkv, state_gla, c, c_ctx, w_ada, b_ada, g_mix, g_ffn, w_in, w_out, b_ml_gates, g_ml_norm, g_mla_qlat, g_mla_kvlat, w_mla_uq, w_mla_ukv, g_mla_qn, g_mla_kn, rw_w0, rw_w_up, rw_a0, rw_a_up, rw_g_up, rw_k_k, rw_k_a, rw_r_k, rw_ln, gla_g_up, gla_g_b, gla_norm, moe_w_rg, moe_b_rg, moe_w_re, moe_b_re, moe_w_gate, moe_w_up, moe_w_down):
    cc = jnp.concatenate([c_ctx[None], c, jnp.zeros((3, D_MODEL), F32)], axis=0)
    mod = _modulation(cc, w_ada, b_ada).reshape(DEPTH, 8, 6, D_MODEL)
    mod = jnp.pad(mod, ((0, 0), (0, 0), (0, 2), (0, 0)))

    layers = []
    for l in range(DEPTH):
        p = {'b_ml_gates': b_ml_gates[l], 'g_ml_norm': g_ml_norm[l], 'g_mla_qlat': g_mla_qlat[l],
             'g_mla_kvlat': g_mla_kvlat[l], 'w_mla_uq': w_mla_uq[l], 'w_mla_ukv': w_mla_ukv[l],
             'g_mla_qn': g_mla_qn[l], 'g_mla_kn': g_mla_kn[l], 'rw_w0': rw_w0[l], 'rw_w_up': rw_w_up[l],
             'rw_a0': rw_a0[l], 'rw_a_up': rw_a_up[l], 'rw_g_up': rw_g_up[l], 'rw_k_k': rw_k_k[l],
             'rw_k_a': rw_k_a[l], 'rw_r_k': rw_r_k[l], 'rw_ln': rw_ln[l], 'gla_g_up': gla_g_up[l],
             'gla_g_b': gla_g_b[l], 'gla_norm': gla_norm[l]}
        router_w = jnp.zeros((D_MODEL, LANES), F32)
        router_w = router_w.at[:, :N_EXPERTS].set(moe_w_re[l]).at[:, N_EXPERTS:N_EXPERTS + N_GROUPS].set(moe_w_rg[l])
        router_b = jnp.zeros((1, LANES), F32)
        router_b = router_b.at[0, :N_EXPERTS].set(moe_b_re[l]).at[0, N_EXPERTS:N_EXPERTS + N_GROUPS].set(moe_b_rg[l])
        layers.append(dict(
            w_in=_pack_w_in(w_in[l]), g_mix=g_mix[l].reshape(1, -1), g_ffn=g_ffn[l].reshape(1, -1),
            w_out=w_out[l].astype(BF16), ml=_mlstm_params(p), mla=_mla_params(p), rw=_rwkv_params(p),
            gla=_gla_params(p), router_w=router_w, router_b=router_b,
            wg=moe_w_gate[l], wu=moe_w_up[l], wd=moe_w_down[l]))

    def block(x2, B, T, mod_g, lp, ctx):
        u3 = _in_proj(x2, mod_g, lp['g_mix'], lp['w_in'], T).reshape(B, T, PK_COLS)
        if ctx is None:
            ctx_kv = None
            ml_c0, ml_m0 = _mlstm_state_zero(B)
            rw_h0 = jnp.zeros((B, 2, RW_H, RW_N, RW_N), F32)
            gla_s0 = jnp.zeros((B, 2, GLA_H, GLA_DK, GLA_DV), F32)
        else:
            ctx_kv, ml_C0, ml_n0, ml_m0_, rw_S0, gla_s0 = ctx
            ml_c0, ml_m0 = _mlstm_state_in(ml_C0, ml_n0, ml_m0_)
            rw_h0 = jnp.swapaxes(rw_S0, -1, -2)
        ml_out, ml_c, ml_m = _mlstm(u3, *lp['ml'], ml_c0, ml_m0)
        mla_out, own_kv = _mla(u3, ctx_kv, lp['mla'])
        rw_out, rw_h = _rwkv(u3, lp['rw'], rw_h0)
        gla_out, gla_s = _gla(u3, lp['gla'], gla_s0)
        mixed = [t.reshape(B * T, GROUP_W) for t in (ml_out, mla_out, rw_out, gla_out)]
        xn, h3, comb, sel = _out_proj(x2, mixed, lp['w_out'], mod_g, lp['g_ffn'], lp['router_w'], lp['router_b'], T)
        x_new = _moe(h3, comb, sel, xn, mod_g, lp['wg'], lp['wu'], lp['wd'], T)
        ml_C, ml_n, ml_mm = _mlstm_state_out(ml_c, ml_m)
        return x_new, (own_kv, ml_C, ml_n, ml_mm, jnp.swapaxes(rw_h, -1, -2), gla_s)

    Bp, Tp = x_prompt.shape[:2]
    Bs, Ts = x_sample.shape[:2]
    xp = x_prompt.reshape(Bp * Tp, D_MODEL)
    ctx_states = []
    for l in range(DEPTH):
        xp, st = block(xp, Bp, Tp, mod[l, 0:1], layers[l], None)
        ctx_states.append(st)
    xs = x_sample.reshape(Bs * Ts, D_MODEL)
    for l in range(DEPTH):
        ctx = (cache_mla[:, l], state_mlstm_C[:, l], state_mlstm_n[:, l], state_mlstm_m[:, l],
               state_rwkv[:, l], state_gla[:, l])
        xs, _ = block(xs, Bs, Ts, mod[l, 1:1 + Bs], layers[l], ctx)
    outs = [jnp.stack([s[i] for s in ctx_states], axis=1) for i in range(6)]
    return (xp.reshape(x_prompt.shape), xs.reshape(x_sample.shape), *outs)
```

```python
import functools
import math

import numpy as np
import jax
import jax.numpy as jnp
from jax import lax
from jax.experimental import pallas as pl
from jax.experimental.pallas import tpu as pltpu

F32 = jnp.float32
BF16 = jnp.bfloat16
HI = lax.Precision.HIGHEST

D_MODEL = 2048
DEPTH = 2
GRID_W = 64
GROUP_W = 512
ML_H, ML_DK = 4, 128
MLA_H, MLA_NOPE, MLA_ROPE, MLA_V = 4, 128, 64, 128
MLA_QK = MLA_NOPE + MLA_ROPE
Q_LORA, KV_LORA = 384, 256
ROPE_THETA = 10000.0
RW_H, RW_N = 8, 64
RW_DECAY_SCALE = math.exp(-0.5)
RW_LN_EPS = 64e-5
GLA_H, GLA_DK, GLA_DV = 4, 64, 128
GLA_GATE_RANK = 16
GLA_NORMALIZER = 16.0
N_GROUPS, EXPERTS_PER_GROUP, N_EXPERTS = 4, 4, 16
EXPERT_HIDDEN = 512
NORM_EPS = 1e-6
LANES = 128
ROW_TILES = D_MODEL // LANES
VMEM_LIMIT = 56 * 1024 * 1024

_REF_SPLITS = (
    ('ml_q', 512), ('ml_k', 512), ('ml_v', 512), ('ml_o', 512), ('ml_g', 16),
    ('mla_ql', Q_LORA), ('mla_ckv', KV_LORA), ('mla_kr', MLA_ROPE),
    ('rw_r', 512), ('rw_k', 512), ('rw_v', 512), ('rw_wd', 64), ('rw_ad', 64), ('rw_gd', 128),
    ('gla_q', 256), ('gla_k', 256), ('gla_v', 512), ('gla_gd', GLA_GATE_RANK), ('gla_g', 512),
)
_REF_OFF = {}
_o = 0
for _n, _w in _REF_SPLITS:
    _REF_OFF[_n] = (_o, _w)
    _o += _w
IN_COLS = _o

_PACKED = (
    ('ml_q', 512), ('ml_k', 512), ('ml_v', 512), ('ml_o', 512),
    ('rw_r', 512), ('rw_k', 512), ('rw_v', 512), ('gla_v', 512), ('gla_g', 512),
    ('mla_ql', 384), ('mla_ckv', 256), ('mla_kr', 64), ('mla_kr_sw', 64),
    ('gla_q', 256), ('gla_k', 256),
    ('ml_g', 128), ('rw_wd', 64), ('rw_ad', 64), ('rw_gd', 128), ('gla_gd', 128),
)
PK_OFF = {}
_o = 0
for _n, _w in _PACKED:
    PK_OFF[_n] = _o
    _o += _w
PK_COLS = _o


def _rope_swap_perm():
    idx = np.arange(MLA_ROPE)
    axis, half, f = idx // 32, (idx % 32) // 16, idx % 16
    return axis * 32 + (1 - half) * 16 + f


def _packed_column_index():
    src = np.full((PK_COLS,), -1, np.int64)
    for name, width in _PACKED:
        off = PK_OFF[name]
        if name == 'mla_kr_sw':
            s, w = _REF_OFF['mla_kr']
            src[off:off + w] = s + _rope_swap_perm()
        else:
            s, w = _REF_OFF[name]
            src[off:off + w] = s + np.arange(w)
    return src


_PK_SRC = _packed_column_index()


def _column_runs(src):
    runs, i = [], 0
    while i < len(src):
        j = i + 1
        while j < len(src) and ((src[i] < 0 and src[j] < 0) or (src[i] >= 0 and src[j] == src[i] + (j - i))):
            j += 1
        runs.append((int(src[i]), j - i))
        i = j
    return runs


_PK_RUNS = _column_runs(_PK_SRC)


def _pack_w_in(w):
    parts = [w[:, s:s + n] if s >= 0 else jnp.zeros((w.shape[0], n), w.dtype) for s, n in _PK_RUNS]
    return jnp.concatenate(parts, axis=1).astype(BF16)


def _cparams(sem):
    return pltpu.CompilerParams(dimension_semantics=sem, vmem_limit_bytes=VMEM_LIMIT)


def _log_sigmoid(x):
    return jnp.minimum(x, 0.0) - jnp.log(1.0 + jnp.exp(-jnp.abs(x)))


def _sigmoid(x):
    return 1.0 / (1.0 + jnp.exp(-x))


def _dot(a, b):
    return jnp.dot(a.astype(BF16), b.astype(BF16), preferred_element_type=F32)


def _dot_nt(a, b):
    return lax.dot_general(a.astype(BF16), b.astype(BF16), (((1,), (1,)), ((), ())), preferred_element_type=F32)


def _dot_tn(a, b):
    return lax.dot_general(a.astype(BF16), b.astype(BF16), (((0,), (0,)), ((), ())), preferred_element_type=F32)


def _dot_hi(a, b):
    return jnp.dot(a, b, precision=HI, preferred_element_type=F32)


def _mod_kernel(c_ref, w_ref, b_ref, o_ref):
    c = c_ref[...]
    s = c * _sigmoid(c)
    o_ref[0] = _dot(s, w_ref[0]) + b_ref[0]


def _modulation(cc, w_ada, b_ada):
    tn = 1536
    n = 6 * D_MODEL
    return pl.pallas_call(
        _mod_kernel,
        grid=(DEPTH, n // tn),
        in_specs=[pl.BlockSpec((8, D_MODEL), lambda l, j: (0, 0)),
                  pl.BlockSpec((1, D_MODEL, tn), lambda l, j: (l, 0, j)),
                  pl.BlockSpec((1, 1, tn), lambda l, j: (l, 0, j))],
        out_specs=pl.BlockSpec((1, 8, tn), lambda l, j: (l, 0, j)),
        out_shape=jax.ShapeDtypeStruct((DEPTH, 8, n), F32),
        compiler_params=_cparams(("arbitrary", "arbitrary")),
        name="adaln_mod",
    )(cc, w_ada, b_ada.reshape(DEPTH, 1, n))


def _inproj_kernel(x_ref, mod_ref, g_ref, w_ref, o_ref, h_scr):
    @pl.when(pl.program_id(1) == 0)
    def _():
        x = x_ref[...]
        xn = x * lax.rsqrt(jnp.mean(x * x, axis=-1, keepdims=True) + NORM_EPS) * g_ref[...]
        h_scr[...] = (xn * (1.0 + mod_ref[0, 1:2, :]) + mod_ref[0, 0:1, :]).astype(BF16)

    o_ref[...] = jnp.dot(h_scr[...], w_ref[...], preferred_element_type=F32)


def _in_proj(x2, mod, g, w_packed, T):
    n_tok = x2.shape[0]
    tm = 1024
    tn = 640
    bm = mod.shape[0]
    mod_idx = (lambda i, j: (i * tm // T, 0, 0)) if bm > 1 else (lambda i, j: (0, 0, 0))
    return pl.pallas_call(
        _inproj_kernel,
        grid=(n_tok // tm, PK_COLS // tn),
        in_specs=[pl.BlockSpec((tm, D_MODEL), lambda i, j: (i, 0)),
                  pl.BlockSpec((1, 8, D_MODEL), mod_idx),
                  pl.BlockSpec((1, D_MODEL), lambda i, j: (0, 0)),
                  pl.BlockSpec((D_MODEL, tn), lambda i, j: (0, j))],
        out_specs=pl.BlockSpec((tm, tn), lambda i, j: (i, j)),
        out_shape=jax.ShapeDtypeStruct((n_tok, PK_COLS), F32),
        scratch_shapes=[pltpu.VMEM((tm, D_MODEL), BF16)],
        compiler_params=_cparams(("arbitrary", "arbitrary")),
        name="in_proj",
    )(x2, mod, g, w_packed)


ML_CHUNK = 256


def _mlstm_kernel(q_ref, k_ref, v_ref, o_ref, g_ref, bias_ref, gn_ref, c0_ref, m0_ref,
                  out_ref, c_ref, m_ref, hs_ref, *, T):
    L = ML_CHUNK
    nc = T // L
    c_ref[...] = c0_ref[...]
    m_ref[...] = m0_ref[...]
    ii = lax.broadcasted_iota(jnp.int32, (L, L), 0)
    jj = lax.broadcasted_iota(jnp.int32, (L, L), 1)
    lane = lax.broadcasted_iota(jnp.int32, (1, LANES), 1)
    is_f = jnp.logical_and(lane % 8 >= 4, lane < 16)
    ones_col = (lax.broadcasted_iota(jnp.int32, (L, LANES), 1) == 0).astype(BF16)
    scale = ML_DK ** -0.5

    for d in range(2):
        mask = (jj <= ii) if d == 0 else (jj >= ii)
        tri = mask.astype(F32)

        def chunk(ci, carry, d=d, mask=mask, tri=tri):
            c = ci if d == 0 else nc - 1 - ci
            s = pl.multiple_of(c * L, L)
            gates = g_ref[0, pl.ds(s, L), :] + bias_ref[...]
            gf = jnp.where(is_f, _log_sigmoid(gates), gates)
            cum = _dot_hi(tri, gf)
            gf_t = gf.T
            cum_t = cum.T
            for h in range(ML_H):
                ci_, cf_ = d * 8 + h, d * 8 + 4 + h
                hs = slice(h * ML_DK, (h + 1) * ML_DK)
                ig_col, ig_row = gf[:, ci_:ci_ + 1], gf_t[ci_:ci_ + 1, :]
                b_col, b_row = cum[:, cf_:cf_ + 1], cum_t[cf_:cf_ + 1, :]
                b_last = b_col[L - 1:L, :] if d == 0 else b_col[0:1, :]
                m_prev = m_ref[0, d, h][:, 0:1]
                dmat = jnp.where(mask, b_col + (ig_row - b_row), -jnp.inf)
                m_inter = b_col + m_prev
                m_row = jnp.maximum(m_inter, jnp.max(dmat, axis=-1, keepdims=True))
                w_inter = jnp.exp(m_inter - m_row)
                q = (q_ref[0, pl.ds(s, L), hs] * scale).astype(BF16)
                k = k_ref[0, pl.ds(s, L), hs]
                v_aug = jnp.concatenate([v_ref[0, pl.ds(s, L), hs].astype(BF16), ones_col], axis=1)
                smat = _dot_nt(q, k) * jnp.exp(dmat - m_row)
                c_aug = c_ref[0, d, h]
                nd = w_inter * _dot(q, c_aug) + _dot(smat, v_aug)
                num, den = nd[:, :ML_DK], nd[:, ML_DK:ML_DK + 1]
                hh = num / jnp.maximum(jnp.abs(den), jnp.exp(-m_row))
                if d == 0:
                    hs_ref[pl.ds(s, L), hs] = hh
                else:
                    hs_ref[pl.ds(s, L), hs] = hs_ref[pl.ds(s, L), hs] + hh
                dk_col = b_last - b_col + ig_col
                m_new = jnp.maximum(b_last + m_prev, jnp.max(dk_col, axis=0, keepdims=True))
                w_key = jnp.exp(dk_col - m_new)
                c_scale = jnp.exp(b_last + m_prev - m_new)
                c_ref[0, d, h] = c_scale * c_aug + _dot_tn(k * w_key, v_aug)
                m_ref[0, d, h] = jnp.broadcast_to(m_new, (1, LANES))
            return carry

        lax.fori_loop(0, nc, chunk, 0)

    def finish(ci, carry):
        s = pl.multiple_of(ci * L, L)
        for h in range(ML_H):
            hs = slice(h * ML_DK, (h + 1) * ML_DK)
            x = hs_ref[pl.ds(s, L), hs]
            xc = x - jnp.mean(x, axis=-1, keepdims=True)
            y = xc * lax.rsqrt(jnp.mean(xc * xc, axis=-1, keepdims=True) + NORM_EPS) * gn_ref[:, hs]
            out_ref[0, pl.ds(s, L), hs] = (y * _sigmoid(o_ref[0, pl.ds(s, L), hs])).astype(BF16)
        return carry

    lax.fori_loop(0, nc, finish, 0)


def _mlstm_params(p):
    bias = jnp.zeros((1, LANES), F32).at[0, :16].set(p['b_ml_gates'])
    return bias, p['g_ml_norm'].reshape(1, 512)


def _mlstm_state_in(C0, n0, m0):
    c0 = jnp.concatenate([C0, n0[..., None], jnp.zeros(C0.shape[:-1] + (ML_DK - 1,), F32)], axis=-1)
    return c0, jnp.broadcast_to(m0[..., None, None], m0.shape + (1, LANES))


def _mlstm_state_zero(B):
    return jnp.zeros((B, 2, ML_H, ML_DK, 2 * ML_DK), F32), jnp.zeros((B, 2, ML_H, 1, LANES), F32)


def _mlstm_state_out(c, m):
    return c[..., :ML_DK], c[..., ML_DK], m[..., 0, 0]


def _mlstm(u3, bias, gnorm, c0, m0):
    B, T, _ = u3.shape
    blk = lambda name: pl.BlockSpec((1, T, 512), lambda b, o=PK_OFF[name] // 512: (b, 0, o))
    st_c = pl.BlockSpec((1, 2, ML_H, ML_DK, 2 * ML_DK), lambda b: (b, 0, 0, 0, 0))
    st_m = pl.BlockSpec((1, 2, ML_H, 1, LANES), lambda b: (b, 0, 0, 0, 0))
    return pl.pallas_call(
        functools.partial(_mlstm_kernel, T=T),
        grid=(B,),
        in_specs=[blk('ml_q'), blk('ml_k'), blk('ml_v'), blk('ml_o'),
                  pl.BlockSpec((1, T, LANES), lambda b: (b, 0, PK_OFF['ml_g'] // LANES)),
                  pl.BlockSpec((1, LANES), lambda b: (0, 0)),
                  pl.BlockSpec((1, 512), lambda b: (0, 0)),
                  st_c, st_m],
        out_specs=[pl.BlockSpec((1, T, 512), lambda b: (b, 0, 0)), st_c, st_m],
        out_shape=[jax.ShapeDtypeStruct((B, T, 512), BF16),
                   jax.ShapeDtypeStruct(c0.shape, F32),
                   jax.ShapeDtypeStruct(m0.shape, F32)],
        scratch_shapes=[pltpu.VMEM((T, 512), F32)],
        compiler_params=_cparams(("arbitrary",)),
        name="mlstm",
    )(u3, u3, u3, u3, u3, bias, gnorm, c0, m0)


MLA_BLK = 256


def _rope_tables(T):
    rows = T // GRID_W
    row = np.repeat(np.arange(rows, dtype=np.float64), GRID_W)
    col = np.tile(np.arange(GRID_W, dtype=np.float64), rows)
    inv = ROPE_THETA ** (-np.arange(MLA_ROPE // 4, dtype=np.float64) / (MLA_ROPE // 4))
    ang = np.stack([row[:, None] * inv, col[:, None] * inv], axis=1)
    cos = np.stack([np.cos(ang), np.cos(ang)], axis=2).reshape(T, MLA_ROPE)
    sin = np.stack([-np.sin(ang), np.sin(ang)], axis=2).reshape(T, MLA_ROPE)
    return jnp.asarray(np.concatenate([cos, sin], axis=1), F32)


def _mla_kernel(*refs, T, n_ctx, rope):
    if n_ctx:
        (u_ref, ctx_ref, cs_ref, gql_ref, gkv_ref, gains_ref, wqn_ref, wqr_ref, wqs_ref, wkn_ref, wv_ref,
         out_ref, kv_ref, qn_s, qr_s, kn_s, kr_s, v_s) = refs
    else:
        (u_ref, cs_ref, gql_ref, gkv_ref, gains_ref, wqn_ref, wqr_ref, wqs_ref, wkn_ref, wv_ref,
         out_ref, kv_ref, qn_s, qr_s, kn_s, kr_s, v_s) = refs
    Lb = MLA_BLK
    gq_n, gq_r, gq_s = gains_ref[0:1, :], gains_ref[1:2, 0:64], gains_ref[1:2, 64:128]
    gk_n, gk_r, gk_s = gains_ref[2:3, :], gains_ref[3:4, 0:64], gains_ref[3:4, 64:128]
    sm_scale = MLA_QK ** -0.5

    def store_keys(s, kn, kr, krs, cos, sin):
        kr_ss = jnp.sum(kr * kr, axis=-1, keepdims=True)
        for h in range(MLA_H):
            kn_h = kn[:, h * 128:(h + 1) * 128]
            rk = lax.rsqrt((jnp.sum(kn_h * kn_h, axis=-1, keepdims=True) + kr_ss) / MLA_QK + NORM_EPS)
            kn_s[pl.ds(s, Lb), h * 128:(h + 1) * 128] = (kn_h * rk * gk_n).astype(BF16)
            kr_h = kr * gk_r
            if cos is not None:
                kr_h = kr_h * cos + (krs * gk_s) * sin
            kr_s[pl.ds(s, Lb), h * 64:(h + 1) * 64] = (kr_h * rk).astype(BF16)

    def prep(ci, carry):
        s = pl.multiple_of(ci * Lb, Lb)
        u = u_ref[0, pl.ds(s, Lb), :]
        ql, ckv, kr, krs = u[:, :384], u[:, 384:640], u[:, 640:704], u[:, 704:768]
        qln = ql * lax.rsqrt(jnp.mean(ql * ql, axis=-1, keepdims=True) + NORM_EPS) * gql_ref[...]
        ckvn = ckv * lax.rsqrt(jnp.mean(ckv * ckv, axis=-1, keepdims=True) + NORM_EPS) * gkv_ref[...]
        kv_ref[0, pl.ds(s, Lb), :] = jnp.concatenate([ckvn, kr], axis=1)
        cos = cs_ref[pl.ds(s, Lb), 0:64] if rope else None
        sin = cs_ref[pl.ds(s, Lb), 64:128] if rope else None
        qn, qr, qs = _dot(qln, wqn_ref[...]), _dot(qln, wqr_ref[...]), _dot(qln, wqs_ref[...])
        for h in range(MLA_H):
            qn_h, qr_h = qn[:, h * 128:(h + 1) * 128], qr[:, h * 64:(h + 1) * 64]
            ss = jnp.sum(qn_h * qn_h, axis=-1, keepdims=True) + jnp.sum(qr_h * qr_h, axis=-1, keepdims=True)
            rq = lax.rsqrt(ss / MLA_QK + NORM_EPS) * sm_scale
            qn_s[pl.ds(s, Lb), h * 128:(h + 1) * 128] = (qn_h * rq * gq_n).astype(BF16)
            qr_h = qr_h * gq_r
            if rope:
                qr_h = qr_h * cos + (qs[:, h * 64:(h + 1) * 64] * gq_s) * sin
            qr_s[pl.ds(s, Lb), h * 64:(h + 1) * 64] = (qr_h * rq).astype(BF16)
        v_s[pl.ds(s, Lb), :] = _dot(ckvn, wv_ref[...]).astype(BF16)
        store_keys(s, _dot(ckvn, wkn_ref[...]), kr, krs, cos, sin)
        return carry

    lax.fori_loop(0, T // Lb, prep, 0)

    for ci in range(n_ctx // Lb):
        cx = ctx_ref[0, ci * Lb:(ci + 1) * Lb, :]
        ckv_c, kr_c = cx[:, :KV_LORA], cx[:, KV_LORA:KV_LORA + MLA_ROPE]
        v_s[T + ci * Lb:T + (ci + 1) * Lb, :] = _dot(ckv_c, wv_ref[...]).astype(BF16)
        store_keys(T + ci * Lb, _dot(ckv_c, wkn_ref[...]), kr_c, None, None, None)

    for h in range(MLA_H):
        def attend(qi, carry, h=h):
            s = pl.multiple_of(qi * Lb, Lb)
            sc = (_dot_nt(qn_s[pl.ds(s, Lb), h * 128:(h + 1) * 128], kn_s[:, h * 128:(h + 1) * 128])
                  + _dot_nt(qr_s[pl.ds(s, Lb), h * 64:(h + 1) * 64], kr_s[:, h * 64:(h + 1) * 64]))
            p = jnp.exp(sc - jnp.max(sc, axis=-1, keepdims=True))
            o = _dot(p, v_s[:, h * 128:(h + 1) * 128]) / jnp.sum(p, axis=-1, keepdims=True)
            out_ref[0, pl.ds(s, Lb), h * 128:(h + 1) * 128] = o.astype(BF16)
            return carry

        lax.fori_loop(0, T // Lb, attend, 0)


def _mla_params(p):
    wq = p['w_mla_uq'].reshape(Q_LORA, MLA_H, MLA_QK)
    sw = _rope_swap_perm()
    wq_n = wq[:, :, :MLA_NOPE].reshape(Q_LORA, 512).astype(BF16)
    wq_r = wq[:, :, MLA_NOPE:].reshape(Q_LORA, 256).astype(BF16)
    wq_s = wq[:, :, MLA_NOPE:][:, :, sw].reshape(Q_LORA, 256).astype(BF16)
    wkv = p['w_mla_ukv'].reshape(KV_LORA, MLA_H, MLA_NOPE + MLA_V)
    wk_n = wkv[:, :, :MLA_NOPE].reshape(KV_LORA, 512).astype(BF16)
    wv = wkv[:, :, MLA_NOPE:].reshape(KV_LORA, 512).astype(BF16)
    gq, gk = p['g_mla_qn'], p['g_mla_kn']
    gains = jnp.zeros((8, LANES), F32)
    gains = gains.at[0].set(gq[:128]).at[1, :64].set(gq[128:]).at[1, 64:].set(gq[128:][sw])
    gains = gains.at[2].set(gk[:128]).at[3, :64].set(gk[128:]).at[3, 64:].set(gk[128:][sw])
    return (p['g_mla_qlat'].reshape(1, -1), p['g_mla_kvlat'].reshape(1, -1), gains, wq_n, wq_r, wq_s, wk_n, wv)


def _mla(u3, ctx_kv, mp):
    B, T, _ = u3.shape
    n_ctx = 0 if ctx_kv is None else ctx_kv.shape[1]
    rope = ctx_kv is not None
    tk = T + n_ctx
    full = lambda a: pl.BlockSpec(a.shape, lambda b, n=a.ndim: (0,) * n)
    cs = _rope_tables(T) if rope else jnp.zeros((T, LANES), F32)
    ins = [u3] + ([ctx_kv] if rope else []) + [cs] + list(mp)
    specs = [pl.BlockSpec((1, T, 768), lambda b: (b, 0, PK_OFF['mla_ql'] // 768))]
    if rope:
        specs.append(pl.BlockSpec((1, n_ctx, KV_LORA + MLA_ROPE), lambda b: (b, 0, 0)))
    specs += [full(a) for a in ins[len(specs):]]
    return pl.pallas_call(
        functools.partial(_mla_kernel, T=T, n_ctx=n_ctx, rope=rope),
        grid=(B,),
        in_specs=specs,
        out_specs=[pl.BlockSpec((1, T, 512), lambda b: (b, 0, 0)),
                   pl.BlockSpec((1, T, KV_LORA + MLA_ROPE), lambda b: (b, 0, 0))],
        out_shape=[jax.ShapeDtypeStruct((B, T, 512), BF16),
                   jax.ShapeDtypeStruct((B, T, KV_LORA + MLA_ROPE), F32)],
        scratch_shapes=[pltpu.VMEM((T, 512), BF16), pltpu.VMEM((T, 256), BF16),
                        pltpu.VMEM((tk, 512), BF16), pltpu.VMEM((tk, 256), BF16), pltpu.VMEM((tk, 512), BF16)],
        compiler_params=_cparams(("arbitrary",)),
        name="mla",
    )(*ins)


GLA_CHUNK = 64
GLA_LEAF = 4


def _gla_kernel(q_ref, k_ref, v_ref, gd_ref, g_ref, gup_ref, gb_ref, gn_ref, hsel_ref, s0_ref,
                out_ref, s_ref, os_ref, *, T):
    L, C = GLA_CHUNK, GLA_LEAF
    nc = T // L
    s_ref[...] = s0_ref[...]
    ii = lax.broadcasted_iota(jnp.int32, (L, L), 0)
    jj = lax.broadcasted_iota(jnp.int32, (L, L), 1)
    eye = (lax.broadcasted_iota(jnp.int32, (GLA_DK, GLA_DK), 0)
           == lax.broadcasted_iota(jnp.int32, (GLA_DK, GLA_DK), 1)).astype(F32)
    row_id = lax.broadcasted_iota(jnp.int32, (L, 1), 0)
    scale = GLA_DK ** -0.5
    hsel = hsel_ref[...]
    levels = []
    span = C
    while span < L:
        levels.append(span)
        span *= 2

    for d in range(2):
        causal = (jj <= ii) if d == 0 else (jj >= ii)
        tri = causal.astype(F32)

        def chunk(ci, carry, d=d, tri=tri):
            c = ci if d == 0 else nc - 1 - ci
            s = pl.multiple_of(c * L, L)
            la = _log_sigmoid(_dot_hi(gd_ref[0, pl.ds(s, L), :], gup_ref[d]) + gb_ref[d]) / GLA_NORMALIZER
            b = _dot_hi(tri, la)
            total = b[L - 1:L, :] if d == 0 else b[0:1, :]
            q = q_ref[0, pl.ds(s, L), :] * scale
            k = k_ref[0, pl.ds(s, L), :]
            amat = [jnp.zeros((L, L), F32) for _ in range(GLA_H)]
            for sp in levels:
                b3 = b.reshape(L // (2 * sp), 2 * sp, GLA_H * GLA_DK)
                edge = b3[:, sp - 1:sp, :] if d == 0 else b3[:, sp:sp + 1, :]
                bref = jnp.broadcast_to(edge, b3.shape).reshape(L, GLA_H * GLA_DK)
                later = (row_id % (2 * sp) >= sp) if d == 0 else (row_id % (2 * sp) < sp)
                e = jnp.exp(jnp.where(later, b - bref, bref - b))
                qs = jnp.where(later, q * e, 0.0)
                kt = jnp.where(later, 0.0, k * e)
                same = (ii // (2 * sp)) == (jj // (2 * sp))
                for h in range(GLA_H):
                    ks = slice(h * GLA_DK, (h + 1) * GLA_DK)
                    amat[h] = amat[h] + jnp.where(same, _dot_nt(qs[:, ks], kt[:, ks]), 0.0)
            terms = []
            for dl in range(C):
                if dl == 0:
                    terms.append(q * k)
                    continue
                sh = dl if d == 0 else L - dl
                ok = (row_id % C >= dl) if d == 0 else (row_id % C < C - dl)
                kd = pltpu.roll(k, sh, axis=0)
                bd = pltpu.roll(b, sh, axis=0)
                terms.append(q * kd * jnp.exp(jnp.where(ok, b - bd, 0.0)))
            tt = jnp.concatenate(terms, axis=0)
            t_hi = tt.astype(BF16)
            t_lo = (tt - t_hi.astype(F32)).astype(BF16)
            diag = (jnp.dot(t_hi, hsel, preferred_element_type=F32)
                    + jnp.dot(t_lo, hsel, preferred_element_type=F32))
            for dl in range(C):
                pair = jnp.logical_and(jj == (ii - dl if d == 0 else ii + dl), ii // C == jj // C)
                for h in range(GLA_H):
                    amat[h] = amat[h] + jnp.where(pair, diag[dl * L:(dl + 1) * L, h:h + 1], 0.0)
            q_in = q * jnp.exp(b)
            k_out = k * jnp.exp(total - b)
            f_row = jnp.exp(total)
            for h in range(GLA_H):
                ks = slice(h * GLA_DK, (h + 1) * GLA_DK)
                vs = slice(h * GLA_DV, (h + 1) * GLA_DV)
                v = v_ref[0, pl.ds(s, L), vs]
                st = s_ref[0, d, h]
                o = _dot(amat[h], v) + _dot(q_in[:, ks], st)
                if d == 0:
                    os_ref[pl.ds(s, L), vs] = o
                else:
                    os_ref[pl.ds(s, L), vs] = os_ref[pl.ds(s, L), vs] + o
                f_col = jnp.sum(eye * f_row[:, ks], axis=1, keepdims=True)
                s_ref[0, d, h] = f_col * st + _dot_tn(k_out[:, ks], v)
            return carry

        lax.fori_loop(0, nc, chunk, 0)

    Lf = 256

    def finish(ci, carry):
        s = pl.multiple_of(ci * Lf, Lf)
        for h in range(GLA_H):
            vs = slice(h * GLA_DV, (h + 1) * GLA_DV)
            o = os_ref[pl.ds(s, Lf), vs]
            y = o * lax.rsqrt(jnp.mean(o * o, axis=-1, keepdims=True) + NORM_EPS) * gn_ref[:, vs]
            g = g_ref[0, pl.ds(s, Lf), vs]
            out_ref[0, pl.ds(s, Lf), vs] = (y * (g * _sigmoid(g))).astype(BF16)
        return carry

    lax.fori_loop(0, T // Lf, finish, 0)


def _gla_params(p):
    gup = jnp.zeros((2, LANES, GLA_H * GLA_DK), F32).at[:, :GLA_GATE_RANK, :].set(p['gla_g_up'])
    hsel = jnp.asarray(np.arange(GLA_H * GLA_DK)[:, None] // GLA_DK == np.arange(LANES)[None, :], BF16)
    return gup, p['gla_g_b'].reshape(2, 1, -1), p['gla_norm'].reshape(1, -1), hsel


def _gla(u3, gp, s0):
    B, T, _ = u3.shape
    blk = lambda name, w: pl.BlockSpec((1, T, w), lambda b, o=PK_OFF[name] // w: (b, 0, o))
    full = lambda a: pl.BlockSpec(a.shape, lambda b, n=a.ndim: (0,) * n)
    st = pl.BlockSpec((1, 2, GLA_H, GLA_DK, GLA_DV), lambda b: (b, 0, 0, 0, 0))
    return pl.pallas_call(
        functools.partial(_gla_kernel, T=T),
        grid=(B,),
        in_specs=[blk('gla_q', 256), blk('gla_k', 256), blk('gla_v', 512), blk('gla_gd', 128), blk('gla_g', 512),
                  full(gp[0]), full(gp[1]), full(gp[2]), full(gp[3]), st],
        out_specs=[pl.BlockSpec((1, T, 512), lambda b: (b, 0, 0)), st],
        out_shape=[jax.ShapeDtypeStruct((B, T, 512), BF16), jax.ShapeDtypeStruct(s0.shape, F32)],
        scratch_shapes=[pltpu.VMEM((T, 512), F32)],
        compiler_params=_cparams(("arbitrary",)),
        name="gla",
    )(u3, u3, u3, u3, u3, *gp, s0)


RW_CHUNK = 64


def _seg_sum(x, bd):
    hi = x.astype(BF16)
    lo = (x - hi.astype(F32)).astype(BF16)
    return jnp.dot(hi, bd, preferred_element_type=F32) + jnp.dot(lo, bd, preferred_element_type=F32)


def _rwkv_kernel(r_ref, k_ref, v_ref, wa_ref, gd_ref, wwa_ref, w0a0_ref, gup_ref, kk_ref, ka_ref, rk_ref, ln_ref,
                 bd_ref, h0_ref, out_ref, h_ref, ys_ref, bonus_ref, *, T):
    L, N = RW_CHUNK, RW_N
    nc = T // L
    h_ref[...] = h0_ref[...]
    ii = lax.broadcasted_iota(jnp.int32, (L, L), 0)
    jj = lax.broadcasted_iota(jnp.int32, (L, L), 1)
    eye = (ii == jj).astype(F32)
    lane = lax.broadcasted_iota(jnp.int32, (1, LANES), 1)
    bd = bd_ref[...]

    for d in range(2):
        strict = (jj < ii) if d == 0 else (jj > ii)
        incl = (jj <= ii) if d == 0 else (jj >= ii)
        tri = incl.astype(F32)

        def chunk(ci, carry, d=d, strict=strict, incl=incl, tri=tri):
            c = ci if d == 0 else nc - 1 - ci
            s = pl.multiple_of(c * L, L)
            r = r_ref[0, pl.ds(s, L), :]
            k = k_ref[0, pl.ds(s, L), :]
            v = v_ref[0, pl.ds(s, L), :]
            wa = wa_ref[0, pl.ds(s, L), :]
            pre = _dot_hi(jnp.where(lane < 64, jnp.tanh(wa), wa), wwa_ref[d]) + w0a0_ref[d]
            logw = -RW_DECAY_SCALE * _sigmoid(pre[:, :512])
            a = _sigmoid(pre[:, 512:])
            kkr = k * kk_ref[...]
            kk = kkr * lax.rsqrt(_seg_sum(kkr * kkr, bd) + 1e-12)
            kt = k * (1.0 + (a - 1.0) * ka_ref[...])
            bh = kk * a
            bonus = _seg_sum(r * kt * rk_ref[...], bd) * v
            if d == 0:
                bonus_ref[pl.ds(s, L), :] = bonus
            else:
                bonus_ref[pl.ds(s, L), :] = bonus_ref[pl.ds(s, L), :] + bonus
            lg = _dot_hi(tri, logw)
            lg_end = lg[L - 1:L, :] if d == 0 else lg[0:1, :]
            a_t = -kk * jnp.exp(lg - logw)
            r_t = r * jnp.exp(lg)
            e_inv = jnp.exp(-lg)
            k_t, b_t = kt * e_inv, bh * e_inv
            e_end = jnp.exp(lg_end - lg)
            k_e, b_e = kt * e_end, bh * e_end
            g_end = jnp.exp(lg_end)
            heads = range(RW_H)
            sls = [slice(h * N, (h + 1) * N) for h in heads]
            ms = [_dot_nt(jnp.concatenate([a_t[:, sl], r_t[:, sl]], axis=0),
                          jnp.concatenate([b_t[:, sl], k_t[:, sl]], axis=0)) for sl in sls]
            pws = [jnp.where(strict, m[:L, :L], 0.0) for m in ms]
            xs = [eye + n for n in pws]
            for _ in range(5):
                pws = [_dot(pw, pw) for pw in pws]
                xs = [x + _dot(x, pw) for x, pw in zip(xs, pws)]
            vhs = [v[:, sl] for sl in sls]
            mvs = [_dot(jnp.where(strict, m[:L, L:], 0.0), vh) for m, vh in zip(ms, vhs)]
            tws = [_dot(x, jnp.concatenate([a_t[:, sl], mv], axis=1))
                   for x, sl, mv in zip(xs, sls, mvs)]
            qys = [_dot(jnp.where(incl, m[L:, :L], 0.0), tw) for m, tw in zip(ms, tws)]
            ylocs = [_dot(jnp.where(incl, m[L:, L:], 0.0), vh) + qy[:, N:] for m, vh, qy in zip(ms, vhs, qys)]
            pgs = [_dot_tn(b_e[:, sl], tw) for sl, tw in zip(sls, tws)]
            gmats = [_dot_tn(k_e[:, sl], vh) + pg[:, N:] for sl, vh, pg in zip(sls, vhs, pgs)]
            for h in heads:
                sl = sls[h]
                hst = h_ref[0, d, h]
                y = _dot(r_t[:, sl] + qys[h][:, :N], hst) + ylocs[h]
                if d == 0:
                    ys_ref[pl.ds(s, L), sl] = y
                else:
                    ys_ref[pl.ds(s, L), sl] = ys_ref[pl.ds(s, L), sl] + y
                h_ref[0, d, h] = _dot(eye * g_end[:, sl] + pgs[h][:, :N], hst) + gmats[h]
            return carry

        lax.fori_loop(0, nc, chunk, 0)

    Lf = 256

    def finish(ci, carry):
        s = pl.multiple_of(ci * Lf, Lf)
        y = ys_ref[pl.ds(s, Lf), :]
        yc = y - _seg_sum(y, bd) / N
        yn = yc * lax.rsqrt(_seg_sum(yc * yc, bd) / N + RW_LN_EPS) * ln_ref[...]
        g = _dot(_sigmoid(gd_ref[0, pl.ds(s, Lf), :]), gup_ref[...])
        out_ref[0, pl.ds(s, Lf), :] = ((yn + bonus_ref[pl.ds(s, Lf), :]) * g).astype(BF16)
        return carry

    lax.fori_loop(0, T // Lf, finish, 0)


def _rwkv_params(p):
    wwa = jnp.zeros((2, LANES, 1024), F32)
    wwa = wwa.at[:, :64, :512].set(p['rw_w_up']).at[:, 64:, 512:].set(p['rw_a_up'])
    w0a0 = jnp.concatenate([p['rw_w0'], p['rw_a0']], axis=-1).reshape(2, 1, 1024)
    seg = np.arange(512) // RW_N
    bd = jnp.asarray(seg[:, None] == seg[None, :], BF16)
    row = lambda n: p[n].reshape(1, -1)
    return (wwa, w0a0, p['rw_g_up'].astype(BF16), row('rw_k_k'), row('rw_k_a'), row('rw_r_k'), row('rw_ln'), bd)


def _rwkv(u3, rp, h0):
    B, T, _ = u3.shape
    blk = lambda name, w: pl.BlockSpec((1, T, w), lambda b, o=PK_OFF[name] // w: (b, 0, o))
    full = lambda a: pl.BlockSpec(a.shape, lambda b, n=a.ndim: (0,) * n)
    st = pl.BlockSpec((1, 2, RW_H, RW_N, RW_N), lambda b: (b, 0, 0, 0, 0))
    return pl.pallas_call(
        functools.partial(_rwkv_kernel, T=T),
        grid=(B,),
        in_specs=[blk('rw_r', 512), blk('rw_k', 512), blk('rw_v', 512), blk('rw_wd', 128), blk('rw_gd', 128)]
                 + [full(a) for a in rp] + [st],
        out_specs=[pl.BlockSpec((1, T, 512), lambda b: (b, 0, 0)), st],
        out_shape=[jax.ShapeDtypeStruct((B, T, 512), BF16), jax.ShapeDtypeStruct(h0.shape, F32)],
        scratch_shapes=[pltpu.VMEM((T, 512), F32), pltpu.VMEM((T, 512), F32)],
        compiler_params=_cparams(("arbitrary",)),
        name="rwkv7",
    )(u3, u3, u3, u3, u3, *rp, h0)


def _route(logits):
    lane = lax.broadcasted_iota(jnp.int32, (1, LANES), 1)
    far = jnp.int32(2 * LANES)
    neg = -jnp.inf
    gl = jnp.where(jnp.logical_and(lane >= N_EXPERTS, lane < N_EXPERTS + N_GROUPS), logits, neg)
    gmax = jnp.max(gl, axis=-1, keepdims=True)
    grp = jnp.min(jnp.where(gl == gmax, lane, far), axis=-1, keepdims=True) - N_EXPERTS
    p_grp = 1.0 / jnp.sum(jnp.exp(gl - gmax), axis=-1, keepdims=True)
    el = jnp.where(jnp.logical_and(lane < N_EXPERTS, lane // EXPERTS_PER_GROUP == grp), logits, neg)
    v1 = jnp.max(el, axis=-1, keepdims=True)
    i1 = jnp.min(jnp.where(el == v1, lane, far), axis=-1, keepdims=True)
    el2 = jnp.where(lane == i1, neg, el)
    v2 = jnp.max(el2, axis=-1, keepdims=True)
    i2 = jnp.min(jnp.where(el2 == v2, lane, far), axis=-1, keepdims=True)
    e = jnp.exp(v2 - v1)
    w1 = 1.0 / (1.0 + e)
    comb = jnp.where(lane == i1, p_grp * w1, jnp.where(lane == i2, p_grp * (e * w1), 0.0))
    sel = jnp.where(lane == i1, 1.0, jnp.where(lane == i2, 2.0, 0.0))
    return comb, sel


def _outproj_kernel(x_ref, m0_ref, m1_ref, m2_ref, m3_ref, w_ref, mod_ref, g_ref, wr_ref, br_ref,
                    xn_ref, h3_ref, comb_ref, sel_ref):
    y = jnp.dot(m0_ref[...], w_ref[0:512, :], preferred_element_type=F32)
    for i, m_ref in enumerate((m1_ref, m2_ref, m3_ref), start=1):
        y = y + jnp.dot(m_ref[...], w_ref[i * 512:(i + 1) * 512, :], preferred_element_type=F32)
    xn = x_ref[...] + mod_ref[0, 2:3, :] * y
    xn_ref[...] = xn
    h = xn * lax.rsqrt(jnp.mean(xn * xn, axis=-1, keepdims=True) + NORM_EPS) * g_ref[...]
    h = h * (1.0 + mod_ref[0, 4:5, :]) + mod_ref[0, 3:4, :]
    tm = h.shape[0]
    for c in range(ROW_TILES):
        h3_ref[pl.ds(c, tm, stride=ROW_TILES), :] = h[:, c * LANES:(c + 1) * LANES]
    comb_ref[...], sel_ref[...] = _route(_dot_hi(h, wr_ref[...]) + br_ref[...])


def _out_proj(x2, mixed, w_out, mod, g, router_w, router_b, T):
    n_tok = x2.shape[0]
    tm = 512
    bm = mod.shape[0]
    mod_idx = (lambda i: (i * tm // T, 0, 0)) if bm > 1 else (lambda i: (0, 0, 0))
    row = lambda w: pl.BlockSpec((tm, w), lambda i: (i, 0))
    full = lambda a: pl.BlockSpec(a.shape, lambda i, n=a.ndim: (0,) * n)
    return pl.pallas_call(
        _outproj_kernel,
        grid=(n_tok // tm,),
        in_specs=[row(D_MODEL)] + [row(GROUP_W)] * 4 + [full(w_out), pl.BlockSpec((1, 8, D_MODEL), mod_idx),
                                                       full(g), full(router_w), full(router_b)],
        out_specs=[row(D_MODEL), pl.BlockSpec((tm * ROW_TILES, LANES), lambda i: (i, 0)), row(LANES), row(LANES)],
        out_shape=[jax.ShapeDtypeStruct((n_tok, D_MODEL), F32),
                   jax.ShapeDtypeStruct((n_tok * ROW_TILES, LANES), F32),
                   jax.ShapeDtypeStruct((n_tok, LANES), F32), jax.ShapeDtypeStruct((n_tok, LANES), F32)],
        compiler_params=_cparams(("arbitrary",)),
        name="out_proj",
    )(x2, *mixed, w_out, mod, g, router_w, router_b)


MOE_TM = 256
MOE_ROWS = 256


def _moe_rows(n_tok):
    return 2 * n_tok + N_EXPERTS * MOE_TM


def _plan_kernel(sel_ref, pos_ref, tmap_ref):
    n_tok = sel_ref.shape[0]
    blk = 512
    lane = lax.broadcasted_iota(jnp.int32, (1, LANES), 1)
    earlier = (lax.broadcasted_iota(jnp.int32, (blk, blk), 1)
               < lax.broadcasted_iota(jnp.int32, (blk, blk), 0)).astype(BF16)
    before = (lax.broadcasted_iota(jnp.int32, (LANES, LANES), 0)
              < lax.broadcasted_iota(jnp.int32, (LANES, LANES), 1)).astype(BF16)

    def count(i, acc):
        s = pl.multiple_of(i * blk, blk)
        return acc + jnp.sum((sel_ref[pl.ds(s, blk), :] > 0.0).astype(F32), axis=0, keepdims=True)

    counts = lax.fori_loop(0, n_tok // blk, count, jnp.zeros((1, LANES), F32))
    tiles = jnp.floor((counts + (MOE_TM - 1)) * (1.0 / MOE_TM))
    tile_start = _dot(jnp.broadcast_to(tiles, (8, LANES)), before)[0:1, :]
    tile_end = tile_start + tiles
    base = tile_start * MOE_TM
    n_tiles = jnp.sum(tiles, axis=-1, keepdims=True)
    j = lax.broadcasted_iota(jnp.int32, (tmap_ref.shape[0], 1), 0).astype(F32)
    done = jnp.logical_and(tile_end <= j, lane < N_EXPERTS)
    expert = jnp.minimum(jnp.sum(done.astype(F32), axis=-1, keepdims=True), N_EXPERTS - 1.0)
    valid = (j < n_tiles).astype(F32)
    tmap_ref[...] = jnp.where(lane == 0, expert, jnp.where(lane == 1, valid, 0.0)).astype(jnp.int32)

    def place(i, seen):
        s = pl.multiple_of(i * blk, blk)
        sel = sel_ref[pl.ds(s, blk), :]
        one = (sel > 0.0).astype(F32)
        row = base + seen + _dot(earlier, one)
        p1 = jnp.sum(jnp.where(sel == 1.0, row, 0.0), axis=-1, keepdims=True)
        p2 = jnp.sum(jnp.where(sel == 2.0, row, 0.0), axis=-1, keepdims=True)
        pos_ref[pl.ds(s, blk), :] = jnp.where(lane == 0, p1, jnp.where(lane == 1, p2, 0.0)).astype(jnp.int32)
        return seen + jnp.sum(one, axis=0, keepdims=True)

    lax.fori_loop(0, n_tok // blk, place, jnp.zeros((1, LANES), F32))


def _moe_plan(sel):
    n_tok = sel.shape[0]
    n_tiles = _moe_rows(n_tok) // MOE_TM
    pos, tmap = pl.pallas_call(
        _plan_kernel,
        out_shape=[jax.ShapeDtypeStruct((n_tok, LANES), jnp.int32), jax.ShapeDtypeStruct((64, LANES), jnp.int32)],
        compiler_params=pltpu.CompilerParams(vmem_limit_bytes=VMEM_LIMIT),
        name="moe_plan",
    )(sel)
    return jnp.transpose(pos[:, :2]), tmap[:n_tiles, 0], tmap[:n_tiles, 1]


def _dispatch_kernel(pos_ref, src_ref, init_ref, dst_ref, sem):
    del init_ref
    base = pl.program_id(0) * MOE_ROWS

    def copy(j, k):
        return pltpu.make_async_copy(src_ref.at[j], dst_ref.at[pos_ref[k, base + j]], sem)

    def start(j, c):
        copy(j, 0).start()
        copy(j, 1).start()
        return c

    def wait(j, c):
        copy(j, 0).wait()
        copy(j, 1).wait()
        return c

    lax.fori_loop(0, MOE_ROWS, start, 0)
    lax.fori_loop(0, MOE_ROWS, wait, 0)


def _collect_kernel(pos_ref, src_ref, dst_ref, sem):
    base = pl.program_id(0) * MOE_ROWS

    def copy(j, k):
        return pltpu.make_async_copy(src_ref.at[pos_ref[k, base + j]], dst_ref.at[k, j], sem)

    def start(j, c):
        copy(j, 0).start()
        copy(j, 1).start()
        return c

    def wait(j, c):
        copy(j, 0).wait()
        copy(j, 1).wait()
        return c

    lax.fori_loop(0, MOE_ROWS, start, 0)
    lax.fori_loop(0, MOE_ROWS, wait, 0)


def _dispatch(pos, h3, init):
    n_tok = h3.shape[0]
    any_spec = pl.BlockSpec(memory_space=pl.ANY)
    return pl.pallas_call(
        _dispatch_kernel,
        grid_spec=pltpu.PrefetchScalarGridSpec(
            num_scalar_prefetch=1, grid=(n_tok // MOE_ROWS,),
            in_specs=[pl.BlockSpec((MOE_ROWS, ROW_TILES, LANES), lambda i, p: (i, 0, 0)), any_spec],
            out_specs=any_spec, scratch_shapes=[pltpu.SemaphoreType.DMA(())]),
        out_shape=jax.ShapeDtypeStruct(init.shape, init.dtype),
        input_output_aliases={2: 0},
        compiler_params=pltpu.CompilerParams(dimension_semantics=("arbitrary",)),
        name="moe_dispatch",
    )(pos, h3, init)


def _collect(pos, ys):
    n_tok = pos.shape[1]
    return pl.pallas_call(
        _collect_kernel,
        grid_spec=pltpu.PrefetchScalarGridSpec(
            num_scalar_prefetch=1, grid=(n_tok // MOE_ROWS,),
            in_specs=[pl.BlockSpec(memory_space=pl.ANY)],
            out_specs=pl.BlockSpec((2, MOE_ROWS, ROW_TILES, LANES), lambda i, p: (0, i, 0, 0)),
            scratch_shapes=[pltpu.SemaphoreType.DMA(())]),
        out_shape=jax.ShapeDtypeStruct((2, n_tok, ROW_TILES, LANES), F32),
        compiler_params=pltpu.CompilerParams(dimension_semantics=("arbitrary",)),
        name="moe_collect",
    )(pos, ys)


def _experts_kernel(te_ref, tv_ref, xs_ref, wg_ref, wu_ref, wd_ref, ys_ref, wg_s, wu_s, wd_s):
    i = pl.program_id(0)
    fresh = jnp.logical_or(i == 0, te_ref[i] != te_ref[jnp.maximum(i - 1, 0)])

    @pl.when(jnp.logical_and(fresh, tv_ref[i] == 1))
    def _():
        wg_s[...] = wg_ref[0, 0].astype(BF16)
        wu_s[...] = wu_ref[0, 0].astype(BF16)
        wd_s[...] = wd_ref[0, 0].astype(BF16)

    @pl.when(tv_ref[i] == 1)
    def _():
        x = jnp.concatenate([xs_ref[pl.ds(c, MOE_TM, stride=ROW_TILES), :] for c in range(ROW_TILES)],
                            axis=1).astype(BF16)
        a = jnp.dot(x, wg_s[...], preferred_element_type=F32)
        b = jnp.dot(x, wu_s[...], preferred_element_type=F32)
        y = jnp.dot(((a * _sigmoid(a)) * b).astype(BF16), wd_s[...], preferred_element_type=F32)
        for c in range(ROW_TILES):
            ys_ref[pl.ds(c, MOE_TM, stride=ROW_TILES), :] = y[:, c * LANES:(c + 1) * LANES]

    @pl.when(tv_ref[i] == 0)
    def _():
        ys_ref[...] = jnp.zeros_like(ys_ref)


def _experts(tile_expert, tile_valid, xs, wg, wu, wd, layer):
    n_rows = xs.shape[0] // ROW_TILES
    rows = pl.BlockSpec((MOE_TM * ROW_TILES, LANES), lambda i, te, tv: (i, 0))
    return pl.pallas_call(
        _experts_kernel,
        grid_spec=pltpu.PrefetchScalarGridSpec(
            num_scalar_prefetch=2, grid=(n_rows // MOE_TM,),
            in_specs=[rows,
                      pl.BlockSpec((1, 1, D_MODEL, EXPERT_HIDDEN), lambda i, te, tv: (layer, te[i], 0, 0)),
                      pl.BlockSpec((1, 1, D_MODEL, EXPERT_HIDDEN), lambda i, te, tv: (layer, te[i], 0, 0)),
                      pl.BlockSpec((1, 1, EXPERT_HIDDEN, D_MODEL), lambda i, te, tv: (layer, te[i], 0, 0))],
            out_specs=rows,
            scratch_shapes=[pltpu.VMEM((D_MODEL, EXPERT_HIDDEN), BF16), pltpu.VMEM((D_MODEL, EXPERT_HIDDEN), BF16),
                            pltpu.VMEM((EXPERT_HIDDEN, D_MODEL), BF16)]),
        out_shape=jax.ShapeDtypeStruct(xs.shape, F32),
        compiler_params=_cparams(("arbitrary",)),
        name="moe_experts",
    )(tile_expert, tile_valid, xs, wg, wu, wd)


def _combine_kernel(y_ref, comb_ref, sel_ref, xn_ref, mod_ref, o_ref):
    comb, sel = comb_ref[...], sel_ref[...]
    w1 = jnp.sum(jnp.where(sel == 1.0, comb, 0.0), axis=-1, keepdims=True)
    w2 = jnp.sum(jnp.where(sel == 2.0, comb, 0.0), axis=-1, keepdims=True)
    tm = xn_ref.shape[0]
    for c in range(ROW_TILES):
        cs = slice(c * LANES, (c + 1) * LANES)
        rows = pl.ds(c, tm, stride=ROW_TILES)
        moe = w1 * y_ref[0, rows, :] + w2 * y_ref[1, rows, :]
        o_ref[:, cs] = xn_ref[:, cs] + mod_ref[0, 5:6, cs] * moe


def _combine(y2, comb, sel, xn, mod, T):
    n_tok = xn.shape[0]
    tm = 512
    bm = mod.shape[0]
    mod_idx = (lambda i: (i * tm // T, 0, 0)) if bm > 1 else (lambda i: (0, 0, 0))
    row = lambda w: pl.BlockSpec((tm, w), lambda i: (i, 0))
    return pl.pallas_call(
        _combine_kernel,
        grid=(n_tok // tm,),
        in_specs=[pl.BlockSpec((2, tm * ROW_TILES, LANES), lambda i: (0, i, 0)), row(LANES), row(LANES),
                  row(D_MODEL), pl.BlockSpec((1, 8, D_MODEL), mod_idx)],
        out_specs=row(D_MODEL),
        out_shape=jax.ShapeDtypeStruct((n_tok, D_MODEL), F32),
        compiler_params=_cparams(("arbitrary",)),
        name="moe_combine",
    )(y2, comb, sel, xn, mod)


def _moe(h3, comb, sel, xn, mod, wg, wu, wd, layer, T):
    n_tok = xn.shape[0]
    n_rows = _moe_rows(n_tok)
    pos, tile_expert, tile_valid = _moe_plan(sel)
    xs = _dispatch(pos, h3.reshape(n_tok, ROW_TILES, LANES), jnp.zeros((n_rows, ROW_TILES, LANES), F32))
    ys = _experts(tile_expert, tile_valid, xs.reshape(n_rows * ROW_TILES, LANES), wg, wu, wd, layer)
    y2 = _collect(pos, ys.reshape(n_rows, ROW_TILES, LANES))
    return _combine(y2.reshape(2, n_tok * ROW_TILES, LANES), comb, sel, xn, mod, T)


def kernel(x_prompt, x_sample, cache_mla, state_mlstm_C, state_mlstm_n, state_mlstm_m, state_rwkv, state_gla, c, c_ctx, w_ada, b_ada, g_mix, g_ffn, w_in, w_out, b_ml_gates, g_ml_norm, g_mla_qlat, g_mla_kvlat, w_mla_uq, w_mla_ukv, g_mla_qn, g_mla_kn, rw_w0, rw_w_up, rw_a0, rw_a_up, rw_g_up, rw_k_k, rw_k_a, rw_r_k, rw_ln, gla_g_up, gla_g_b, gla_norm, moe_w_rg, moe_b_rg, moe_w_re, moe_b_re, moe_w_gate, moe_w_up, moe_w_down):
    cc = jnp.concatenate([c_ctx[None], c, jnp.zeros((3, D_MODEL), F32)], axis=0)
    mod = _modulation(cc, w_ada, b_ada).reshape(DEPTH, 8, 6, D_MODEL)
    mod = jnp.pad(mod, ((0, 0), (0, 0), (0, 2), (0, 0)))

    layers = []
    for l in range(DEPTH):
        p = {'b_ml_gates': b_ml_gates[l], 'g_ml_norm': g_ml_norm[l], 'g_mla_qlat': g_mla_qlat[l],
             'g_mla_kvlat': g_mla_kvlat[l], 'w_mla_uq': w_mla_uq[l], 'w_mla_ukv': w_mla_ukv[l],
             'g_mla_qn': g_mla_qn[l], 'g_mla_kn': g_mla_kn[l], 'rw_w0': rw_w0[l], 'rw_w_up': rw_w_up[l],
             'rw_a0': rw_a0[l], 'rw_a_up': rw_a_up[l], 'rw_g_up': rw_g_up[l], 'rw_k_k': rw_k_k[l],
             'rw_k_a': rw_k_a[l], 'rw_r_k': rw_r_k[l], 'rw_ln': rw_ln[l], 'gla_g_up': gla_g_up[l],
             'gla_g_b': gla_g_b[l], 'gla_norm': gla_norm[l]}
        router_w = jnp.zeros((D_MODEL, LANES), F32)
        router_w = router_w.at[:, :N_EXPERTS].set(moe_w_re[l]).at[:, N_EXPERTS:N_EXPERTS + N_GROUPS].set(moe_w_rg[l])
        router_b = jnp.zeros((1, LANES), F32)
        router_b = router_b.at[0, :N_EXPERTS].set(moe_b_re[l]).at[0, N_EXPERTS:N_EXPERTS + N_GROUPS].set(moe_b_rg[l])
        layers.append(dict(
            w_in=_pack_w_in(w_in[l]), g_mix=g_mix[l].reshape(1, -1), g_ffn=g_ffn[l].reshape(1, -1),
            w_out=w_out[l].astype(BF16), ml=_mlstm_params(p), mla=_mla_params(p), rw=_rwkv_params(p),
            gla=_gla_params(p), router_w=router_w, router_b=router_b,
            layer=l))

    def block(x2, B, T, mod_g, lp, ctx):
        u3 = _in_proj(x2, mod_g, lp['g_mix'], lp['w_in'], T).reshape(B, T, PK_COLS)
        if ctx is None:
            ctx_kv = None
            ml_c0, ml_m0 = _mlstm_state_zero(B)
            rw_h0 = jnp.zeros((B, 2, RW_H, RW_N, RW_N), F32)
            gla_s0 = jnp.zeros((B, 2, GLA_H, GLA_DK, GLA_DV), F32)
        else:
            ctx_kv, ml_C0, ml_n0, ml_m0_, rw_S0, gla_s0 = ctx
            ml_c0, ml_m0 = _mlstm_state_in(ml_C0, ml_n0, ml_m0_)
            rw_h0 = jnp.swapaxes(rw_S0, -1, -2)
        ml_out, ml_c, ml_m = _mlstm(u3, *lp['ml'], ml_c0, ml_m0)
        mla_out, own_kv = _mla(u3, ctx_kv, lp['mla'])
        rw_out, rw_h = _rwkv(u3, lp['rw'], rw_h0)
        gla_out, gla_s = _gla(u3, lp['gla'], gla_s0)
        mixed = [t.reshape(B * T, GROUP_W) for t in (ml_out, mla_out, rw_out, gla_out)]
        xn, h3, comb, sel = _out_proj(x2, mixed, lp['w_out'], mod_g, lp['g_ffn'], lp['router_w'], lp['router_b'], T)
        x_new = _moe(h3, comb, sel, xn, mod_g, moe_w_gate, moe_w_up, moe_w_down, lp['layer'], T)
        ml_C, ml_n, ml_mm = _mlstm_state_out(ml_c, ml_m)
        return x_new, (own_kv, ml_C, ml_n, ml_mm, jnp.swapaxes(rw_h, -1, -2), gla_s)

    Bp, Tp = x_prompt.shape[:2]
    Bs, Ts = x_sample.shape[:2]
    xp = x_prompt.reshape(Bp * Tp, D_MODEL)
    ctx_states = []
    for l in range(DEPTH):
        xp, st = block(xp, Bp, Tp, mod[l, 0:1], layers[l], None)
        ctx_states.append(st)
    xs = x_sample.reshape(Bs * Ts, D_MODEL)
    for l in range(DEPTH):
        ctx = (cache_mla[:, l], state_mlstm_C[:, l], state_mlstm_n[:, l], state_mlstm_m[:, l],
               state_rwkv[:, l], state_gla[:, l])
        xs, _ = block(xs, Bs, Ts, mod[l, 1:1 + Bs], layers[l], ctx)
    outs = [jnp.stack([s[i] for s in ctx_states], axis=1) for i in range(6)]
    return (xp.reshape(x_prompt.shape), xs.reshape(x_sample.shape), *outs)
```

```python
import functools
import math

import numpy as np
import jax
import jax.numpy as jnp
from jax import lax
from jax.experimental import pallas as pl
from jax.experimental.pallas import tpu as pltpu

F32 = jnp.float32
BF16 = jnp.bfloat16

D_MODEL = 2048
DEPTH = 2
GRID_W = 64
GROUP_W = 512
ML_H, ML_DK = 4, 128
MLA_H, MLA_NOPE, MLA_ROPE, MLA_V = 4, 128, 64, 128
MLA_QK = MLA_NOPE + MLA_ROPE
Q_LORA, KV_LORA = 384, 256
ROPE_THETA = 10000.0
RW_H, RW_N = 8, 64
RW_DECAY_SCALE = math.exp(-0.5)
RW_LN_EPS = 64e-5
GLA_H, GLA_DK, GLA_DV = 4, 64, 128
GLA_GATE_RANK = 16
GLA_NORMALIZER = 16.0
N_GROUPS, EXPERTS_PER_GROUP, N_EXPERTS = 4, 4, 16
EXPERT_HIDDEN = 512
NORM_EPS = 1e-6
LANES = 128
ROW_TILES = D_MODEL // LANES
VMEM_LIMIT = 56 * 1024 * 1024

_REF_SPLITS = (
    ('ml_q', 512), ('ml_k', 512), ('ml_v', 512), ('ml_o', 512), ('ml_g', 16),
    ('mla_ql', Q_LORA), ('mla_ckv', KV_LORA), ('mla_kr', MLA_ROPE),
    ('rw_r', 512), ('rw_k', 512), ('rw_v', 512), ('rw_wd', 64), ('rw_ad', 64), ('rw_gd', 128),
    ('gla_q', 256), ('gla_k', 256), ('gla_v', 512), ('gla_gd', GLA_GATE_RANK), ('gla_g', 512),
)
_REF_OFF = {}
_o = 0
for _n, _w in _REF_SPLITS:
    _REF_OFF[_n] = (_o, _w)
    _o += _w
IN_COLS = _o

_PACKED = (
    ('ml_q', 512), ('ml_k', 512), ('ml_v', 512), ('ml_o', 512),
    ('rw_r', 512), ('rw_k', 512), ('rw_v', 512), ('gla_v', 512), ('gla_g', 512),
    ('mla_ql', 384), ('mla_ckv', 256), ('mla_kr', 64), ('mla_kr_sw', 64),
    ('gla_q', 256), ('gla_k', 256),
    ('ml_g', 128), ('rw_wd', 64), ('rw_ad', 64), ('rw_gd', 128), ('gla_gd', 128),
)
PK_OFF = {}
_o = 0
for _n, _w in _PACKED:
    PK_OFF[_n] = _o
    _o += _w
PK_COLS = _o


def _rope_swap_perm():
    idx = np.arange(MLA_ROPE)
    axis, half, f = idx // 32, (idx % 32) // 16, idx % 16
    return axis * 32 + (1 - half) * 16 + f


def _packed_column_index():
    src = np.full((PK_COLS,), -1, np.int64)
    for name, width in _PACKED:
        off = PK_OFF[name]
        if name == 'mla_kr_sw':
            s, w = _REF_OFF['mla_kr']
            src[off:off + w] = s + _rope_swap_perm()
        else:
            s, w = _REF_OFF[name]
            src[off:off + w] = s + np.arange(w)
    return src


_PK_SRC = _packed_column_index()


def _column_runs(src):
    runs, i = [], 0
    while i < len(src):
        j = i + 1
        while j < len(src) and ((src[i] < 0 and src[j] < 0) or (src[i] >= 0 and src[j] == src[i] + (j - i))):
            j += 1
        runs.append((int(src[i]), j - i))
        i = j
    return runs


_PK_RUNS = _column_runs(_PK_SRC)


def _pack_w_in(w):
    parts = [w[:, s:s + n] if s >= 0 else jnp.zeros((w.shape[0], n), w.dtype) for s, n in _PK_RUNS]
    return jnp.concatenate(parts, axis=1).astype(BF16)


def _cparams(sem):
    return pltpu.CompilerParams(dimension_semantics=sem, vmem_limit_bytes=VMEM_LIMIT)


def _log_sigmoid(x):
    return jnp.minimum(x, 0.0) - jnp.log(1.0 + jnp.exp(-jnp.abs(x)))


def _sigmoid(x):
    return 1.0 / (1.0 + jnp.exp(-x))


def _dot(a, b):
    return jnp.dot(a.astype(BF16), b.astype(BF16), preferred_element_type=F32)


def _dot_nt(a, b):
    return lax.dot_general(a.astype(BF16), b.astype(BF16), (((1,), (1,)), ((), ())), preferred_element_type=F32)


def _dot_tn(a, b):
    return lax.dot_general(a.astype(BF16), b.astype(BF16), (((0,), (0,)), ((), ())), preferred_element_type=F32)


def _split2(x):
    hi = x.astype(BF16)
    return hi, (x - hi.astype(F32)).astype(BF16)


def _split_weight(w):
    hi, lo = _split2(w)
    return jnp.stack([hi, lo])


def _running_sum(tri, x):
    hi = x.astype(BF16)
    r = x - hi.astype(F32)
    mid = r.astype(BF16)
    lo = (r - mid.astype(F32)).astype(BF16)
    n = x.shape[1]
    s = jnp.dot(tri, jnp.concatenate([hi, mid, lo], axis=1), preferred_element_type=F32)
    return s[:, :n] + s[:, n:2 * n] + s[:, 2 * n:]


def _dot_split(a, w_ref):
    a_hi, a_lo = _split2(a)
    w_hi = w_ref[0]
    return (jnp.dot(a_hi, w_hi, preferred_element_type=F32) + jnp.dot(a_lo, w_hi, preferred_element_type=F32)
            + jnp.dot(a_hi, w_ref[1], preferred_element_type=F32))


def _mod_kernel(c_ref, w_ref, b_ref, o_ref):
    c = c_ref[...]
    s = c * _sigmoid(c)
    o_ref[0] = _dot(s, w_ref[0]) + b_ref[0]


def _modulation(cc, w_ada, b_ada):
    tn = 1536
    n = 6 * D_MODEL
    return pl.pallas_call(
        _mod_kernel,
        grid=(DEPTH, n // tn),
        in_specs=[pl.BlockSpec((8, D_MODEL), lambda l, j: (0, 0)),
                  pl.BlockSpec((1, D_MODEL, tn), lambda l, j: (l, 0, j)),
                  pl.BlockSpec((1, 1, tn), lambda l, j: (l, 0, j))],
        out_specs=pl.BlockSpec((1, 8, tn), lambda l, j: (l, 0, j)),
        out_shape=jax.ShapeDtypeStruct((DEPTH, 8, n), F32),
        compiler_params=_cparams(("arbitrary", "arbitrary")),
        name="adaln_mod",
    )(cc, w_ada, b_ada.reshape(DEPTH, 1, n))


def _inproj_kernel(x_ref, mod_ref, g_ref, w_ref, o_ref, h_scr):
    @pl.when(pl.program_id(1) == 0)
    def _():
        x = x_ref[...]
        xn = x * lax.rsqrt(jnp.mean(x * x, axis=-1, keepdims=True) + NORM_EPS) * g_ref[...]
        h_scr[...] = (xn * (1.0 + mod_ref[0, 1:2, :]) + mod_ref[0, 0:1, :]).astype(BF16)

    o_ref[...] = jnp.dot(h_scr[...], w_ref[...], preferred_element_type=F32)


def _in_proj(x2, mod, g, w_packed, T):
    n_tok = x2.shape[0]
    tm = 1024
    tn = 640
    bm = mod.shape[0]
    mod_idx = (lambda i, j: (i * tm // T, 0, 0)) if bm > 1 else (lambda i, j: (0, 0, 0))
    return pl.pallas_call(
        _inproj_kernel,
        grid=(n_tok // tm, PK_COLS // tn),
        in_specs=[pl.BlockSpec((tm, D_MODEL), lambda i, j: (i, 0)),
                  pl.BlockSpec((1, 8, D_MODEL), mod_idx),
                  pl.BlockSpec((1, D_MODEL), lambda i, j: (0, 0)),
                  pl.BlockSpec((D_MODEL, tn), lambda i, j: (0, j))],
        out_specs=pl.BlockSpec((tm, tn), lambda i, j: (i, j)),
        out_shape=jax.ShapeDtypeStruct((n_tok, PK_COLS), F32),
        scratch_shapes=[pltpu.VMEM((tm, D_MODEL), BF16)],
        compiler_params=_cparams(("arbitrary", "arbitrary")),
        name="in_proj",
    )(x2, mod, g, w_packed)


ML_CHUNK = 256


def _mlstm_kernel(q_ref, k_ref, v_ref, o_ref, g_ref, bias_ref, gn_ref, c0_ref, m0_ref,
                  out_ref, c_ref, m_ref, hs_ref, *, T):
    L = ML_CHUNK
    nc = T // L
    c_ref[...] = c0_ref[...]
    m_ref[...] = m0_ref[...]
    ii = lax.broadcasted_iota(jnp.int32, (L, L), 0)
    jj = lax.broadcasted_iota(jnp.int32, (L, L), 1)
    lane = lax.broadcasted_iota(jnp.int32, (1, LANES), 1)
    is_f = jnp.logical_and(lane % 8 >= 4, lane < 16)
    ones_col = (lax.broadcasted_iota(jnp.int32, (L, LANES), 1) == 0).astype(BF16)
    scale = ML_DK ** -0.5

    for d in range(2):
        mask = (jj <= ii) if d == 0 else (jj >= ii)
        tri = mask.astype(BF16)

        def chunk(ci, carry, d=d, mask=mask, tri=tri):
            c = ci if d == 0 else nc - 1 - ci
            s = pl.multiple_of(c * L, L)
            gates = g_ref[0, pl.ds(s, L), :] + bias_ref[...]
            gf = jnp.where(is_f, _log_sigmoid(gates), gates)
            cum = _running_sum(tri, gf)
            gf_t = gf.T
            cum_t = cum.T
            for h in range(ML_H):
                ci_, cf_ = d * 8 + h, d * 8 + 4 + h
                hs = slice(h * ML_DK, (h + 1) * ML_DK)
                ig_col, ig_row = gf[:, ci_:ci_ + 1], gf_t[ci_:ci_ + 1, :]
                b_col, b_row = cum[:, cf_:cf_ + 1], cum_t[cf_:cf_ + 1, :]
                b_last = b_col[L - 1:L, :] if d == 0 else b_col[0:1, :]
                m_prev = m_ref[0, d, h][:, 0:1]
                dmat = jnp.where(mask, b_col + (ig_row - b_row), -jnp.inf)
                m_inter = b_col + m_prev
                m_row = jnp.maximum(m_inter, jnp.max(dmat, axis=-1, keepdims=True))
                w_inter = jnp.exp(m_inter - m_row)
                q = (q_ref[0, pl.ds(s, L), hs] * scale).astype(BF16)
                k = k_ref[0, pl.ds(s, L), hs]
                v_aug = jnp.concatenate([v_ref[0, pl.ds(s, L), hs].astype(BF16), ones_col], axis=1)
                smat = _dot_nt(q, k) * jnp.exp(dmat - m_row)
                c_aug = c_ref[0, d, h]
                nd = w_inter * _dot(q, c_aug) + _dot(smat, v_aug)
                num, den = nd[:, :ML_DK], nd[:, ML_DK:ML_DK + 1]
                hh = num / jnp.maximum(jnp.abs(den), jnp.exp(-m_row))
                if d == 0:
                    hs_ref[pl.ds(s, L), hs] = hh
                else:
                    hs_ref[pl.ds(s, L), hs] = hs_ref[pl.ds(s, L), hs] + hh
                dk_col = b_last - b_col + ig_col
                m_new = jnp.maximum(b_last + m_prev, jnp.max(dk_col, axis=0, keepdims=True))
                w_key = jnp.exp(dk_col - m_new)
                c_scale = jnp.exp(b_last + m_prev - m_new)
                c_ref[0, d, h] = c_scale * c_aug + _dot_tn(k * w_key, v_aug)
                m_ref[0, d, h] = jnp.broadcast_to(m_new, (1, LANES))
            return carry

        lax.fori_loop(0, nc, chunk, 0)

    def finish(ci, carry):
        s = pl.multiple_of(ci * L, L)
        for h in range(ML_H):
            hs = slice(h * ML_DK, (h + 1) * ML_DK)
            x = hs_ref[pl.ds(s, L), hs]
            xc = x - jnp.mean(x, axis=-1, keepdims=True)
            y = xc * lax.rsqrt(jnp.mean(xc * xc, axis=-1, keepdims=True) + NORM_EPS) * gn_ref[:, hs]
            out_ref[0, pl.ds(s, L), hs] = (y * _sigmoid(o_ref[0, pl.ds(s, L), hs])).astype(BF16)
        return carry

    lax.fori_loop(0, nc, finish, 0)


def _mlstm_params(p):
    bias = jnp.zeros((1, LANES), F32).at[0, :16].set(p['b_ml_gates'])
    return bias, p['g_ml_norm'].reshape(1, 512)


def _mlstm_state_in(C0, n0, m0):
    c0 = jnp.concatenate([C0, n0[..., None], jnp.zeros(C0.shape[:-1] + (ML_DK - 1,), F32)], axis=-1)
    return c0, jnp.broadcast_to(m0[..., None, None], m0.shape + (1, LANES))


def _mlstm_state_zero(B):
    return jnp.zeros((B, 2, ML_H, ML_DK, 2 * ML_DK), F32), jnp.zeros((B, 2, ML_H, 1, LANES), F32)


def _mlstm_state_out(c, m):
    return c[..., :ML_DK], c[..., ML_DK], m[..., 0, 0]


def _mlstm(u3, bias, gnorm, c0, m0):
    B, T, _ = u3.shape
    blk = lambda name: pl.BlockSpec((1, T, 512), lambda b, o=PK_OFF[name] // 512: (b, 0, o))
    st_c = pl.BlockSpec((1, 2, ML_H, ML_DK, 2 * ML_DK), lambda b: (b, 0, 0, 0, 0))
    st_m = pl.BlockSpec((1, 2, ML_H, 1, LANES), lambda b: (b, 0, 0, 0, 0))
    return pl.pallas_call(
        functools.partial(_mlstm_kernel, T=T),
        grid=(B,),
        in_specs=[blk('ml_q'), blk('ml_k'), blk('ml_v'), blk('ml_o'),
                  pl.BlockSpec((1, T, LANES), lambda b: (b, 0, PK_OFF['ml_g'] // LANES)),
                  pl.BlockSpec((1, LANES), lambda b: (0, 0)),
                  pl.BlockSpec((1, 512), lambda b: (0, 0)),
                  st_c, st_m],
        out_specs=[pl.BlockSpec((1, T, 512), lambda b: (b, 0, 0)), st_c, st_m],
        out_shape=[jax.ShapeDtypeStruct((B, T, 512), BF16),
                   jax.ShapeDtypeStruct(c0.shape, F32),
                   jax.ShapeDtypeStruct(m0.shape, F32)],
        scratch_shapes=[pltpu.VMEM((T, 512), F32)],
        compiler_params=_cparams(("arbitrary",)),
        name="mlstm",
    )(u3, u3, u3, u3, u3, bias, gnorm, c0, m0)


MLA_BLK = 256


def _rope_tables(T):
    rows = T // GRID_W
    row = np.repeat(np.arange(rows, dtype=np.float64), GRID_W)
    col = np.tile(np.arange(GRID_W, dtype=np.float64), rows)
    inv = ROPE_THETA ** (-np.arange(MLA_ROPE // 4, dtype=np.float64) / (MLA_ROPE // 4))
    ang = np.stack([row[:, None] * inv, col[:, None] * inv], axis=1)
    cos = np.stack([np.cos(ang), np.cos(ang)], axis=2).reshape(T, MLA_ROPE)
    sin = np.stack([-np.sin(ang), np.sin(ang)], axis=2).reshape(T, MLA_ROPE)
    return jnp.asarray(np.concatenate([cos, sin], axis=1), F32)


def _mla_kernel(*refs, T, n_ctx, rope):
    if n_ctx:
        (u_ref, ctx_ref, cs_ref, gql_ref, gkv_ref, gains_ref, wqn_ref, wqr_ref, wqs_ref, wkn_ref, wv_ref,
         out_ref, kv_ref, qn_s, qr_s, kn_s, kr_s, v_s) = refs
    else:
        (u_ref, cs_ref, gql_ref, gkv_ref, gains_ref, wqn_ref, wqr_ref, wqs_ref, wkn_ref, wv_ref,
         out_ref, kv_ref, qn_s, qr_s, kn_s, kr_s, v_s) = refs
    Lb = MLA_BLK
    gq_n, gq_r, gq_s = gains_ref[0:1, :], gains_ref[1:2, 0:64], gains_ref[1:2, 64:128]
    gk_n, gk_r, gk_s = gains_ref[2:3, :], gains_ref[3:4, 0:64], gains_ref[3:4, 64:128]
    sm_scale = MLA_QK ** -0.5

    def store_keys(s, kn, kr, krs, cos, sin):
        kr_ss = jnp.sum(kr * kr, axis=-1, keepdims=True)
        for h in range(MLA_H):
            kn_h = kn[:, h * 128:(h + 1) * 128]
            rk = lax.rsqrt((jnp.sum(kn_h * kn_h, axis=-1, keepdims=True) + kr_ss) / MLA_QK + NORM_EPS)
            kn_s[pl.ds(s, Lb), h * 128:(h + 1) * 128] = (kn_h * rk * gk_n).astype(BF16)
            kr_h = kr * gk_r
            if cos is not None:
                kr_h = kr_h * cos + (krs * gk_s) * sin
            kr_s[pl.ds(s, Lb), h * 64:(h + 1) * 64] = (kr_h * rk).astype(BF16)

    def prep(ci, carry):
        s = pl.multiple_of(ci * Lb, Lb)
        u = u_ref[0, pl.ds(s, Lb), :]
        ql, ckv, kr, krs = u[:, :384], u[:, 384:640], u[:, 640:704], u[:, 704:768]
        qln = ql * lax.rsqrt(jnp.mean(ql * ql, axis=-1, keepdims=True) + NORM_EPS) * gql_ref[...]
        ckvn = ckv * lax.rsqrt(jnp.mean(ckv * ckv, axis=-1, keepdims=True) + NORM_EPS) * gkv_ref[...]
        kv_ref[0, pl.ds(s, Lb), :] = jnp.concatenate([ckvn, kr], axis=1)
        cos = cs_ref[pl.ds(s, Lb), 0:64] if rope else None
        sin = cs_ref[pl.ds(s, Lb), 64:128] if rope else None
        qn, qr, qs = _dot(qln, wqn_ref[...]), _dot(qln, wqr_ref[...]), _dot(qln, wqs_ref[...])
        for h in range(MLA_H):
            qn_h, qr_h = qn[:, h * 128:(h + 1) * 128], qr[:, h * 64:(h + 1) * 64]
            ss = jnp.sum(qn_h * qn_h, axis=-1, keepdims=True) + jnp.sum(qr_h * qr_h, axis=-1, keepdims=True)
            rq = lax.rsqrt(ss / MLA_QK + NORM_EPS) * sm_scale
            qn_s[pl.ds(s, Lb), h * 128:(h + 1) * 128] = (qn_h * rq * gq_n).astype(BF16)
            qr_h = qr_h * gq_r
            if rope:
                qr_h = qr_h * cos + (qs[:, h * 64:(h + 1) * 64] * gq_s) * sin
            qr_s[pl.ds(s, Lb), h * 64:(h + 1) * 64] = (qr_h * rq).astype(BF16)
        v_s[pl.ds(s, Lb), :] = _dot(ckvn, wv_ref[...]).astype(BF16)
        store_keys(s, _dot(ckvn, wkn_ref[...]), kr, krs, cos, sin)
        return carry

    lax.fori_loop(0, T // Lb, prep, 0)

    for ci in range(n_ctx // Lb):
        cx = ctx_ref[0, ci * Lb:(ci + 1) * Lb, :]
        ckv_c, kr_c = cx[:, :KV_LORA], cx[:, KV_LORA:KV_LORA + MLA_ROPE]
        v_s[T + ci * Lb:T + (ci + 1) * Lb, :] = _dot(ckv_c, wv_ref[...]).astype(BF16)
        store_keys(T + ci * Lb, _dot(ckv_c, wkn_ref[...]), kr_c, None, None, None)

    for h in range(MLA_H):
        def attend(qi, carry, h=h):
            s = pl.multiple_of(qi * Lb, Lb)
            sc = (_dot_nt(qn_s[pl.ds(s, Lb), h * 128:(h + 1) * 128], kn_s[:, h * 128:(h + 1) * 128])
                  + _dot_nt(qr_s[pl.ds(s, Lb), h * 64:(h + 1) * 64], kr_s[:, h * 64:(h + 1) * 64]))
            p = jnp.exp(sc - jnp.max(sc, axis=-1, keepdims=True))
            o = _dot(p, v_s[:, h * 128:(h + 1) * 128]) / jnp.sum(p, axis=-1, keepdims=True)
            out_ref[0, pl.ds(s, Lb), h * 128:(h + 1) * 128] = o.astype(BF16)
            return carry

        lax.fori_loop(0, T // Lb, attend, 0)


def _mla_params(p):
    wq = p['w_mla_uq'].reshape(Q_LORA, MLA_H, MLA_QK)
    sw = _rope_swap_perm()
    wq_n = wq[:, :, :MLA_NOPE].reshape(Q_LORA, 512).astype(BF16)
    wq_r = wq[:, :, MLA_NOPE:].reshape(Q_LORA, 256).astype(BF16)
    wq_s = wq[:, :, MLA_NOPE:][:, :, sw].reshape(Q_LORA, 256).astype(BF16)
    wkv = p['w_mla_ukv'].reshape(KV_LORA, MLA_H, MLA_NOPE + MLA_V)
    wk_n = wkv[:, :, :MLA_NOPE].reshape(KV_LORA, 512).astype(BF16)
    wv = wkv[:, :, MLA_NOPE:].reshape(KV_LORA, 512).astype(BF16)
    gq, gk = p['g_mla_qn'], p['g_mla_kn']
    gains = jnp.zeros((8, LANES), F32)
    gains = gains.at[0].set(gq[:128]).at[1, :64].set(gq[128:]).at[1, 64:].set(gq[128:][sw])
    gains = gains.at[2].set(gk[:128]).at[3, :64].set(gk[128:]).at[3, 64:].set(gk[128:][sw])
    return (p['g_mla_qlat'].reshape(1, -1), p['g_mla_kvlat'].reshape(1, -1), gains, wq_n, wq_r, wq_s, wk_n, wv)


def _mla(u3, ctx_kv, mp):
    B, T, _ = u3.shape
    n_ctx = 0 if ctx_kv is None else ctx_kv.shape[1]
    rope = ctx_kv is not None
    tk = T + n_ctx
    full = lambda a: pl.BlockSpec(a.shape, lambda b, n=a.ndim: (0,) * n)
    cs = _rope_tables(T) if rope else jnp.zeros((T, LANES), F32)
    ins = [u3] + ([ctx_kv] if rope else []) + [cs] + list(mp)
    specs = [pl.BlockSpec((1, T, 768), lambda b: (b, 0, PK_OFF['mla_ql'] // 768))]
    if rope:
        specs.append(pl.BlockSpec((1, n_ctx, KV_LORA + MLA_ROPE), lambda b: (b, 0, 0)))
    specs += [full(a) for a in ins[len(specs):]]
    return pl.pallas_call(
        functools.partial(_mla_kernel, T=T, n_ctx=n_ctx, rope=rope),
        grid=(B,),
        in_specs=specs,
        out_specs=[pl.BlockSpec((1, T, 512), lambda b: (b, 0, 0)),
                   pl.BlockSpec((1, T, KV_LORA + MLA_ROPE), lambda b: (b, 0, 0))],
        out_shape=[jax.ShapeDtypeStruct((B, T, 512), BF16),
                   jax.ShapeDtypeStruct((B, T, KV_LORA + MLA_ROPE), F32)],
        scratch_shapes=[pltpu.VMEM((T, 512), BF16), pltpu.VMEM((T, 256), BF16),
                        pltpu.VMEM((tk, 512), BF16), pltpu.VMEM((tk, 256), BF16), pltpu.VMEM((tk, 512), BF16)],
        compiler_params=_cparams(("arbitrary",)),
        name="mla",
    )(*ins)


GLA_CHUNK = 64
GLA_LEAF = 4


def _gla_kernel(q_ref, k_ref, v_ref, gd_ref, g_ref, gup_ref, gb_ref, gn_ref, hsel_ref, s0_ref,
                out_ref, s_ref, os_ref, la_ref, *, T):
    L, C = GLA_CHUNK, GLA_LEAF
    nc = T // L
    Lf = 256
    s_ref[...] = s0_ref[...]

    def gates(ci, carry):
        s = pl.multiple_of(ci * Lf, Lf)
        x = _dot_split(gd_ref[0, pl.ds(s, Lf), :], gup_ref) + gb_ref[...]
        la_ref[pl.ds(s, Lf), :] = _log_sigmoid(x) / GLA_NORMALIZER
        return carry

    lax.fori_loop(0, T // Lf, gates, 0)
    ii = lax.broadcasted_iota(jnp.int32, (L, L), 0)
    jj = lax.broadcasted_iota(jnp.int32, (L, L), 1)
    eye = (lax.broadcasted_iota(jnp.int32, (GLA_DK, GLA_DK), 0)
           == lax.broadcasted_iota(jnp.int32, (GLA_DK, GLA_DK), 1)).astype(F32)
    row_id = lax.broadcasted_iota(jnp.int32, (L, 1), 0)
    scale = GLA_DK ** -0.5
    hsel = hsel_ref[...]
    levels = []
    span = C
    while span < L:
        levels.append(span)
        span *= 2

    for d in range(2):
        causal = (jj <= ii) if d == 0 else (jj >= ii)
        tri = causal.astype(BF16)

        def chunk(ci, carry, d=d, tri=tri):
            c = ci if d == 0 else nc - 1 - ci
            s = pl.multiple_of(c * L, L)
            la = la_ref[pl.ds(s, L), d * GLA_H * GLA_DK:(d + 1) * GLA_H * GLA_DK]
            b = _running_sum(tri, la)
            total = b[L - 1:L, :] if d == 0 else b[0:1, :]
            q = q_ref[0, pl.ds(s, L), :] * scale
            k = k_ref[0, pl.ds(s, L), :]
            amat = [jnp.zeros((L, L), F32) for _ in range(GLA_H)]
            for sp in levels:
                b3 = b.reshape(L // (2 * sp), 2 * sp, GLA_H * GLA_DK)
                edge = b3[:, sp - 1:sp, :] if d == 0 else b3[:, sp:sp + 1, :]
                bref = jnp.broadcast_to(edge, b3.shape).reshape(L, GLA_H * GLA_DK)
                later = (row_id % (2 * sp) >= sp) if d == 0 else (row_id % (2 * sp) < sp)
                e = jnp.exp(jnp.where(later, b - bref, bref - b))
                qs = jnp.where(later, q * e, 0.0)
                kt = jnp.where(later, 0.0, k * e)
                same = (ii // (2 * sp)) == (jj // (2 * sp))
                for h in range(GLA_H):
                    ks = slice(h * GLA_DK, (h + 1) * GLA_DK)
                    amat[h] = amat[h] + jnp.where(same, _dot_nt(qs[:, ks], kt[:, ks]), 0.0)
            terms = []
            for dl in range(C):
                if dl == 0:
                    terms.append(q * k)
                    continue
                sh = dl if d == 0 else L - dl
                ok = (row_id % C >= dl) if d == 0 else (row_id % C < C - dl)
                kd = pltpu.roll(k, sh, axis=0)
                bd = pltpu.roll(b, sh, axis=0)
                terms.append(q * kd * jnp.exp(jnp.where(ok, b - bd, 0.0)))
            tt = jnp.concatenate(terms, axis=0)
            t_hi = tt.astype(BF16)
            t_lo = (tt - t_hi.astype(F32)).astype(BF16)
            diag = (jnp.dot(t_hi, hsel, preferred_element_type=F32)
                    + jnp.dot(t_lo, hsel, preferred_element_type=F32))
            for dl in range(C):
                pair = jnp.logical_and(jj == (ii - dl if d == 0 else ii + dl), ii // C == jj // C)
                for h in range(GLA_H):
                    amat[h] = amat[h] + jnp.where(pair, diag[dl * L:(dl + 1) * L, h:h + 1], 0.0)
            q_in = q * jnp.exp(b)
            k_out = k * jnp.exp(total - b)
            f_row = jnp.exp(total)
            for h in range(GLA_H):
                ks = slice(h * GLA_DK, (h + 1) * GLA_DK)
                vs = slice(h * GLA_DV, (h + 1) * GLA_DV)
                v = v_ref[0, pl.ds(s, L), vs]
                st = s_ref[0, d, h]
                o = _dot(amat[h], v) + _dot(q_in[:, ks], st)
                if d == 0:
                    os_ref[pl.ds(s, L), vs] = o
                else:
                    os_ref[pl.ds(s, L), vs] = os_ref[pl.ds(s, L), vs] + o
                f_col = jnp.sum(eye * f_row[:, ks], axis=1, keepdims=True)
                s_ref[0, d, h] = f_col * st + _dot_tn(k_out[:, ks], v)
            return carry

        lax.fori_loop(0, nc, chunk, 0)

    Lf = 256

    def finish(ci, carry):
        s = pl.multiple_of(ci * Lf, Lf)
        for h in range(GLA_H):
            vs = slice(h * GLA_DV, (h + 1) * GLA_DV)
            o = os_ref[pl.ds(s, Lf), vs]
            y = o * lax.rsqrt(jnp.mean(o * o, axis=-1, keepdims=True) + NORM_EPS) * gn_ref[:, vs]
            g = g_ref[0, pl.ds(s, Lf), vs]
            out_ref[0, pl.ds(s, Lf), vs] = (y * (g * _sigmoid(g))).astype(BF16)
        return carry

    lax.fori_loop(0, T // Lf, finish, 0)


def _gla_params(p):
    gup = jnp.zeros((LANES, 2 * GLA_H * GLA_DK), F32)
    gup = gup.at[:GLA_GATE_RANK, :].set(jnp.concatenate([p['gla_g_up'][0], p['gla_g_up'][1]], axis=1))
    hsel = jnp.asarray(np.arange(GLA_H * GLA_DK)[:, None] // GLA_DK == np.arange(LANES)[None, :], BF16)
    return _split_weight(gup), p['gla_g_b'].reshape(1, -1), p['gla_norm'].reshape(1, -1), hsel


def _gla(u3, gp, s0):
    B, T, _ = u3.shape
    blk = lambda name, w: pl.BlockSpec((1, T, w), lambda b, o=PK_OFF[name] // w: (b, 0, o))
    full = lambda a: pl.BlockSpec(a.shape, lambda b, n=a.ndim: (0,) * n)
    st = pl.BlockSpec((1, 2, GLA_H, GLA_DK, GLA_DV), lambda b: (b, 0, 0, 0, 0))
    return pl.pallas_call(
        functools.partial(_gla_kernel, T=T),
        grid=(B,),
        in_specs=[blk('gla_q', 256), blk('gla_k', 256), blk('gla_v', 512), blk('gla_gd', 128), blk('gla_g', 512),
                  full(gp[0]), full(gp[1]), full(gp[2]), full(gp[3]), st],
        out_specs=[pl.BlockSpec((1, T, 512), lambda b: (b, 0, 0)), st],
        out_shape=[jax.ShapeDtypeStruct((B, T, 512), BF16), jax.ShapeDtypeStruct(s0.shape, F32)],
        scratch_shapes=[pltpu.VMEM((T, 512), F32), pltpu.VMEM((T, 2 * GLA_H * GLA_DK), F32)],
        compiler_params=_cparams(("arbitrary",)),
        name="gla",
    )(u3, u3, u3, u3, u3, *gp, s0)


RW_CHUNK = 64
RW_UNROLL = 2


def _seg_sum(x, bd):
    hi = x.astype(BF16)
    lo = (x - hi.astype(F32)).astype(BF16)
    return jnp.dot(hi, bd, preferred_element_type=F32) + jnp.dot(lo, bd, preferred_element_type=F32)


def _rwkv_kernel(r_ref, k_ref, v_ref, wa_ref, gd_ref, wwa_ref, w0a0_ref, gup_ref, kk_ref, ka_ref, rk_ref, ln_ref,
                 bd_ref, h0_ref, out_ref, h_ref, ys_ref, pre_ref, *, T):
    L, N = RW_CHUNK, RW_N
    nc = T // L
    h_ref[...] = h0_ref[...]
    ii = lax.broadcasted_iota(jnp.int32, (L, L), 0)
    jj = lax.broadcasted_iota(jnp.int32, (L, L), 1)
    eye = (ii == jj).astype(F32)
    lane = lax.broadcasted_iota(jnp.int32, (1, LANES), 1)
    bd = bd_ref[...]
    Lf = 256

    def gates(ci, carry):
        s = pl.multiple_of(ci * Lf, Lf)
        wa = wa_ref[0, pl.ds(s, Lf), :]
        pre_ref[pl.ds(s, Lf), :] = _dot_split(jnp.where(lane < 64, jnp.tanh(wa), wa), wwa_ref) + w0a0_ref[...]
        return carry

    lax.fori_loop(0, T // Lf, gates, 0)

    masks = [((jj < ii), (jj <= ii)), ((jj > ii), (jj >= ii))]
    U = RW_UNROLL

    def chunk(ci, carry):
        chains = []
        for d in range(2):
            strict, incl = masks[d]
            tri = incl.astype(BF16)
            for u in range(U):
                c = ci * U + u if d == 0 else nc - 1 - (ci * U + u)
                s = pl.multiple_of(c * L, L)
                r = r_ref[0, pl.ds(s, L), :]
                k = k_ref[0, pl.ds(s, L), :]
                v = v_ref[0, pl.ds(s, L), :]
                pre = pre_ref[pl.ds(s, L), d * 1024:(d + 1) * 1024]
                logw = -RW_DECAY_SCALE * _sigmoid(pre[:, :512])
                a = _sigmoid(pre[:, 512:])
                kkr = k * kk_ref[...]
                kk = kkr * lax.rsqrt(_seg_sum(kkr * kkr, bd) + 1e-12)
                kt = k * (1.0 + (a - 1.0) * ka_ref[...])
                bh = kk * a
                lg = _running_sum(tri, logw)
                lg_end = lg[L - 1:L, :] if d == 0 else lg[0:1, :]
                a_t = -kk * jnp.exp(lg - logw)
                r_t = r * jnp.exp(lg)
                e_inv = jnp.exp(-lg)
                k_t, b_t = kt * e_inv, bh * e_inv
                e_end = jnp.exp(lg_end - lg)
                k_e, b_e = kt * e_end, bh * e_end
                g_end = jnp.exp(lg_end)
                for h in range(RW_H):
                    sl = slice(h * N, (h + 1) * N)
                    chains.append(dict(d=d, u=u, h=h, s=s, strict=strict, incl=incl, a=a_t[:, sl], r=r_t[:, sl],
                                       b=b_t[:, sl], k=k_t[:, sl], ke=k_e[:, sl], be=b_e[:, sl], g=g_end[:, sl],
                                       v=v[:, sl]))
        ms = [_dot_nt(jnp.concatenate([c['a'], c['r']], axis=0), jnp.concatenate([c['b'], c['k']], axis=0))
              for c in chains]
        pws = [jnp.where(c['strict'], m[:L, :L], 0.0) for c, m in zip(chains, ms)]
        xs = [eye + n for n in pws]
        for _ in range(5):
            pws = [_dot(pw, pw) for pw in pws]
            xs = [x + _dot(x, pw) for x, pw in zip(xs, pws)]
        mvs = [_dot(jnp.where(c['strict'], m[:L, L:], 0.0), c['v']) for c, m in zip(chains, ms)]
        tws = [_dot(x, jnp.concatenate([c['a'], mv], axis=1)) for x, c, mv in zip(xs, chains, mvs)]
        qys = [_dot(jnp.where(c['incl'], m[L:, :L], 0.0), tw) for c, m, tw in zip(chains, ms, tws)]
        ylocs = [_dot(jnp.where(c['incl'], m[L:, L:], 0.0), c['v']) + qy[:, N:] for c, m, qy in zip(chains, ms, qys)]
        pgs = [_dot_tn(c['be'], tw) for c, tw in zip(chains, tws)]
        gmats = [_dot_tn(c['ke'], c['v']) + pg[:, N:] for c, pg in zip(chains, pgs)]
        for d in range(2):
            for h in range(RW_H):
                hst = h_ref[0, d, h]
                for u in range(U):
                    i = (d * U + u) * RW_H + h
                    c = chains[i]
                    y = _dot(c['r'] + qys[i][:, :N], hst) + ylocs[i]
                    ys_ref[pl.ds(c['s'], L), d * 512 + h * N:d * 512 + (h + 1) * N] = y
                    hst = _dot(eye * c['g'] + pgs[i][:, :N], hst) + gmats[i]
                h_ref[0, d, h] = hst
        return carry

    lax.fori_loop(0, nc // U, chunk, 0)

    def finish(ci, carry):
        s = pl.multiple_of(ci * Lf, Lf)
        r = r_ref[0, pl.ds(s, Lf), :]
        k = k_ref[0, pl.ds(s, Lf), :]
        rk = r * k * rk_ref[...]
        bonus = jnp.zeros((Lf, 512), F32)
        for d in range(2):
            a = _sigmoid(pre_ref[pl.ds(s, Lf), d * 1024 + 512:(d + 1) * 1024])
            bonus = bonus + _seg_sum(rk * (1.0 + (a - 1.0) * ka_ref[...]), bd)
        y = ys_ref[pl.ds(s, Lf), 0:512] + ys_ref[pl.ds(s, Lf), 512:1024]
        yc = y - _seg_sum(y, bd) / N
        yn = yc * lax.rsqrt(_seg_sum(yc * yc, bd) / N + RW_LN_EPS) * ln_ref[...]
        g = _dot(_sigmoid(gd_ref[0, pl.ds(s, Lf), :]), gup_ref[...])
        out_ref[0, pl.ds(s, Lf), :] = ((yn + bonus * v_ref[0, pl.ds(s, Lf), :]) * g).astype(BF16)
        return carry

    lax.fori_loop(0, T // Lf, finish, 0)


def _rwkv_params(p):
    wwa = jnp.zeros((LANES, 2048), F32)
    for d in range(2):
        wwa = wwa.at[:64, d * 1024:d * 1024 + 512].set(p['rw_w_up'][d])
        wwa = wwa.at[64:, d * 1024 + 512:(d + 1) * 1024].set(p['rw_a_up'][d])
    w0a0 = jnp.concatenate([p['rw_w0'][0], p['rw_a0'][0], p['rw_w0'][1], p['rw_a0'][1]]).reshape(1, 2048)
    seg = np.arange(512) // RW_N
    bd = jnp.asarray(seg[:, None] == seg[None, :], BF16)
    row = lambda n: p[n].reshape(1, -1)
    return (_split_weight(wwa), w0a0, p['rw_g_up'].astype(BF16), row('rw_k_k'), row('rw_k_a'), row('rw_r_k'),
            row('rw_ln'), bd)


def _rwkv(u3, rp, h0):
    B, T, _ = u3.shape
    blk = lambda name, w: pl.BlockSpec((1, T, w), lambda b, o=PK_OFF[name] // w: (b, 0, o))
    full = lambda a: pl.BlockSpec(a.shape, lambda b, n=a.ndim: (0,) * n)
    st = pl.BlockSpec((1, 2, RW_H, RW_N, RW_N), lambda b: (b, 0, 0, 0, 0))
    return pl.pallas_call(
        functools.partial(_rwkv_kernel, T=T),
        grid=(B,),
        in_specs=[blk('rw_r', 512), blk('rw_k', 512), blk('rw_v', 512), blk('rw_wd', 128), blk('rw_gd', 128)]
                 + [full(a) for a in rp] + [st],
        out_specs=[pl.BlockSpec((1, T, 512), lambda b: (b, 0, 0)), st],
        out_shape=[jax.ShapeDtypeStruct((B, T, 512), BF16), jax.ShapeDtypeStruct(h0.shape, F32)],
        scratch_shapes=[pltpu.VMEM((T, 1024), F32), pltpu.VMEM((T, 2048), F32)],
        compiler_params=_cparams(("arbitrary",)),
        name="rwkv7",
    )(u3, u3, u3, u3, u3, *rp, h0)


def _route(logits):
    lane = lax.broadcasted_iota(jnp.int32, (1, LANES), 1)
    far = jnp.int32(2 * LANES)
    neg = -jnp.inf
    gl = jnp.where(jnp.logical_and(lane >= N_EXPERTS, lane < N_EXPERTS + N_GROUPS), logits, neg)
    gmax = jnp.max(gl, axis=-1, keepdims=True)
    grp = jnp.min(jnp.where(gl == gmax, lane, far), axis=-1, keepdims=True) - N_EXPERTS
    p_grp = 1.0 / jnp.sum(jnp.exp(gl - gmax), axis=-1, keepdims=True)
    el = jnp.where(jnp.logical_and(lane < N_EXPERTS, lane // EXPERTS_PER_GROUP == grp), logits, neg)
    v1 = jnp.max(el, axis=-1, keepdims=True)
    i1 = jnp.min(jnp.where(el == v1, lane, far), axis=-1, keepdims=True)
    el2 = jnp.where(lane == i1, neg, el)
    v2 = jnp.max(el2, axis=-1, keepdims=True)
    i2 = jnp.min(jnp.where(el2 == v2, lane, far), axis=-1, keepdims=True)
    e = jnp.exp(v2 - v1)
    w1 = 1.0 / (1.0 + e)
    comb = jnp.where(lane == i1, p_grp * w1, jnp.where(lane == i2, p_grp * (e * w1), 0.0))
    sel = jnp.where(lane == i1, 1.0, jnp.where(lane == i2, 2.0, 0.0))
    return comb, sel


def _outproj_kernel(x_ref, m0_ref, m1_ref, m2_ref, m3_ref, w_ref, mod_ref, g_ref, wr_ref, br_ref,
                    xn_ref, h3_ref, comb_ref, sel_ref):
    y = jnp.dot(m0_ref[...], w_ref[0:512, :], preferred_element_type=F32)
    for i, m_ref in enumerate((m1_ref, m2_ref, m3_ref), start=1):
        y = y + jnp.dot(m_ref[...], w_ref[i * 512:(i + 1) * 512, :], preferred_element_type=F32)
    xn = x_ref[...] + mod_ref[0, 2:3, :] * y
    xn_ref[...] = xn
    h = xn * lax.rsqrt(jnp.mean(xn * xn, axis=-1, keepdims=True) + NORM_EPS) * g_ref[...]
    h = h * (1.0 + mod_ref[0, 4:5, :]) + mod_ref[0, 3:4, :]
    tm = h.shape[0]
    for c in range(ROW_TILES):
        h3_ref[pl.ds(c, tm, stride=ROW_TILES), :] = h[:, c * LANES:(c + 1) * LANES]
    comb_ref[...], sel_ref[...] = _route(_dot_split(h, wr_ref) + br_ref[...])


def _out_proj(x2, mixed, w_out, mod, g, router_w, router_b, T):
    n_tok = x2.shape[0]
    tm = 512
    bm = mod.shape[0]
    mod_idx = (lambda i: (i * tm // T, 0, 0)) if bm > 1 else (lambda i: (0, 0, 0))
    row = lambda w: pl.BlockSpec((tm, w), lambda i: (i, 0))
    full = lambda a: pl.BlockSpec(a.shape, lambda i, n=a.ndim: (0,) * n)
    return pl.pallas_call(
        _outproj_kernel,
        grid=(n_tok // tm,),
        in_specs=[row(D_MODEL)] + [row(GROUP_W)] * 4 + [full(w_out), pl.BlockSpec((1, 8, D_MODEL), mod_idx),
                                                       full(g), full(router_w), full(router_b)],
        out_specs=[row(D_MODEL), pl.BlockSpec((tm * ROW_TILES, LANES), lambda i: (i, 0)), row(LANES), row(LANES)],
        out_shape=[jax.ShapeDtypeStruct((n_tok, D_MODEL), F32),
                   jax.ShapeDtypeStruct((n_tok * ROW_TILES, LANES), F32),
                   jax.ShapeDtypeStruct((n_tok, LANES), F32), jax.ShapeDtypeStruct((n_tok, LANES), F32)],
        compiler_params=_cparams(("arbitrary",)),
        name="out_proj",
    )(x2, *mixed, w_out, mod, g, router_w, router_b)


MOE_TM = 256
MOE_ROWS = 256


def _moe_rows(n_tok):
    return 2 * n_tok + N_EXPERTS * MOE_TM


def _plan_kernel(sel_ref, pos_ref, tmap_ref):
    n_tok = sel_ref.shape[0]
    blk = 512
    lane = lax.broadcasted_iota(jnp.int32, (1, LANES), 1)
    earlier = (lax.broadcasted_iota(jnp.int32, (blk, blk), 1)
               < lax.broadcasted_iota(jnp.int32, (blk, blk), 0)).astype(BF16)
    before = (lax.broadcasted_iota(jnp.int32, (LANES, LANES), 0)
              < lax.broadcasted_iota(jnp.int32, (LANES, LANES), 1)).astype(BF16)

    def count(i, acc):
        s = pl.multiple_of(i * blk, blk)
        return acc + jnp.sum((sel_ref[pl.ds(s, blk), :] > 0.0).astype(F32), axis=0, keepdims=True)

    counts = lax.fori_loop(0, n_tok // blk, count, jnp.zeros((1, LANES), F32))
    tiles = jnp.floor((counts + (MOE_TM - 1)) * (1.0 / MOE_TM))
    tile_start = _dot(jnp.broadcast_to(tiles, (8, LANES)), before)[0:1, :]
    tile_end = tile_start + tiles
    base = tile_start * MOE_TM
    n_tiles = jnp.sum(tiles, axis=-1, keepdims=True)
    j = lax.broadcasted_iota(jnp.int32, (tmap_ref.shape[0], 1), 0).astype(F32)
    done = jnp.logical_and(tile_end <= j, lane < N_EXPERTS)
    expert = jnp.minimum(jnp.sum(done.astype(F32), axis=-1, keepdims=True), N_EXPERTS - 1.0)
    valid = (j < n_tiles).astype(F32)
    tmap_ref[...] = jnp.where(lane == 0, expert, jnp.where(lane == 1, valid, 0.0)).astype(jnp.int32)

    def place(i, seen):
        s = pl.multiple_of(i * blk, blk)
        sel = sel_ref[pl.ds(s, blk), :]
        one = (sel > 0.0).astype(F32)
        row = base + seen + _dot(earlier, one)
        p1 = jnp.sum(jnp.where(sel == 1.0, row, 0.0), axis=-1, keepdims=True)
        p2 = jnp.sum(jnp.where(sel == 2.0, row, 0.0), axis=-1, keepdims=True)
        pos_ref[pl.ds(s, blk), :] = jnp.where(lane == 0, p1, jnp.where(lane == 1, p2, 0.0)).astype(jnp.int32)
        return seen + jnp.sum(one, axis=0, keepdims=True)

    lax.fori_loop(0, n_tok // blk, place, jnp.zeros((1, LANES), F32))


def _moe_plan(sel):
    n_tok = sel.shape[0]
    n_tiles = _moe_rows(n_tok) // MOE_TM
    pos, tmap = pl.pallas_call(
        _plan_kernel,
        out_shape=[jax.ShapeDtypeStruct((n_tok, LANES), jnp.int32), jax.ShapeDtypeStruct((64, LANES), jnp.int32)],
        compiler_params=pltpu.CompilerParams(vmem_limit_bytes=VMEM_LIMIT),
        name="moe_plan",
    )(sel)
    return jnp.transpose(pos[:, :2]), tmap[:n_tiles, 0], tmap[:n_tiles, 1]


def _dispatch_kernel(pos_ref, src_ref, init_ref, dst_ref, sem):
    del init_ref
    base = pl.program_id(0) * MOE_ROWS

    def copy(j, k):
        return pltpu.make_async_copy(src_ref.at[j], dst_ref.at[pos_ref[k, base + j]], sem)

    def start(j, c):
        copy(j, 0).start()
        copy(j, 1).start()
        return c

    def wait(j, c):
        copy(j, 0).wait()
        copy(j, 1).wait()
        return c

    lax.fori_loop(0, MOE_ROWS, start, 0)
    lax.fori_loop(0, MOE_ROWS, wait, 0)


def _collect_kernel(pos_ref, src_ref, dst_ref, sem):
    base = pl.program_id(0) * MOE_ROWS

    def copy(j, k):
        return pltpu.make_async_copy(src_ref.at[pos_ref[k, base + j]], dst_ref.at[k, j], sem)

    def start(j, c):
        copy(j, 0).start()
        copy(j, 1).start()
        return c

    def wait(j, c):
        copy(j, 0).wait()
        copy(j, 1).wait()
        return c

    lax.fori_loop(0, MOE_ROWS, start, 0)
    lax.fori_loop(0, MOE_ROWS, wait, 0)


def _dispatch(pos, h3, init):
    n_tok = h3.shape[0]
    any_spec = pl.BlockSpec(memory_space=pl.ANY)
    return pl.pallas_call(
        _dispatch_kernel,
        grid_spec=pltpu.PrefetchScalarGridSpec(
            num_scalar_prefetch=1, grid=(n_tok // MOE_ROWS,),
            in_specs=[pl.BlockSpec((MOE_ROWS, ROW_TILES, LANES), lambda i, p: (i, 0, 0)), any_spec],
            out_specs=any_spec, scratch_shapes=[pltpu.SemaphoreType.DMA(())]),
        out_shape=jax.ShapeDtypeStruct(init.shape, init.dtype),
        input_output_aliases={2: 0},
        compiler_params=pltpu.CompilerParams(dimension_semantics=("arbitrary",)),
        name="moe_dispatch",
    )(pos, h3, init)


def _collect(pos, ys):
    n_tok = pos.shape[1]
    return pl.pallas_call(
        _collect_kernel,
        grid_spec=pltpu.PrefetchScalarGridSpec(
            num_scalar_prefetch=1, grid=(n_tok // MOE_ROWS,),
            in_specs=[pl.BlockSpec(memory_space=pl.ANY)],
            out_specs=pl.BlockSpec((2, MOE_ROWS, ROW_TILES, LANES), lambda i, p: (0, i, 0, 0)),
            scratch_shapes=[pltpu.SemaphoreType.DMA(())]),
        out_shape=jax.ShapeDtypeStruct((2, n_tok, ROW_TILES, LANES), F32),
        compiler_params=pltpu.CompilerParams(dimension_semantics=("arbitrary",)),
        name="moe_collect",
    )(pos, ys)


def _experts_kernel(te_ref, tv_ref, xs_ref, wg_ref, wu_ref, wd_ref, ys_ref, wg_s, wu_s, wd_s):
    i = pl.program_id(0)
    fresh = jnp.logical_or(i == 0, te_ref[i] != te_ref[jnp.maximum(i - 1, 0)])

    @pl.when(jnp.logical_and(fresh, tv_ref[i] == 1))
    def _():
        wg_s[...] = wg_ref[0, 0].astype(BF16)
        wu_s[...] = wu_ref[0, 0].astype(BF16)
        wd_s[...] = wd_ref[0, 0].astype(BF16)

    @pl.when(tv_ref[i] == 1)
    def _():
        x = jnp.concatenate([xs_ref[pl.ds(c, MOE_TM, stride=ROW_TILES), :] for c in range(ROW_TILES)],
                            axis=1).astype(BF16)
        a = jnp.dot(x, wg_s[...], preferred_element_type=F32)
        b = jnp.dot(x, wu_s[...], preferred_element_type=F32)
        y = jnp.dot(((a * _sigmoid(a)) * b).astype(BF16), wd_s[...], preferred_element_type=F32)
        for c in range(ROW_TILES):
            ys_ref[pl.ds(c, MOE_TM, stride=ROW_TILES), :] = y[:, c * LANES:(c + 1) * LANES]

    @pl.when(tv_ref[i] == 0)
    def _():
        ys_ref[...] = jnp.zeros_like(ys_ref)


def _experts(tile_expert, tile_valid, xs, wg, wu, wd, layer):
    n_rows = xs.shape[0] // ROW_TILES
    rows = pl.BlockSpec((MOE_TM * ROW_TILES, LANES), lambda i, te, tv: (i, 0))
    return pl.pallas_call(
        _experts_kernel,
        grid_spec=pltpu.PrefetchScalarGridSpec(
            num_scalar_prefetch=2, grid=(n_rows // MOE_TM,),
            in_specs=[rows,
                      pl.BlockSpec((1, 1, D_MODEL, EXPERT_HIDDEN), lambda i, te, tv: (layer, te[i], 0, 0)),
                      pl.BlockSpec((1, 1, D_MODEL, EXPERT_HIDDEN), lambda i, te, tv: (layer, te[i], 0, 0)),
                      pl.BlockSpec((1, 1, EXPERT_HIDDEN, D_MODEL), lambda i, te, tv: (layer, te[i], 0, 0))],
            out_specs=rows,
            scratch_shapes=[pltpu.VMEM((D_MODEL, EXPERT_HIDDEN), BF16), pltpu.VMEM((D_MODEL, EXPERT_HIDDEN), BF16),
                            pltpu.VMEM((EXPERT_HIDDEN, D_MODEL), BF16)]),
        out_shape=jax.ShapeDtypeStruct(xs.shape, F32),
        compiler_params=_cparams(("arbitrary",)),
        name="moe_experts",
    )(tile_expert, tile_valid, xs, wg, wu, wd)


def _combine_kernel(y_ref, comb_ref, sel_ref, xn_ref, mod_ref, o_ref):
    comb, sel = comb_ref[...], sel_ref[...]
    w1 = jnp.sum(jnp.where(sel == 1.0, comb, 0.0), axis=-1, keepdims=True)
    w2 = jnp.sum(jnp.where(sel == 2.0, comb, 0.0), axis=-1, keepdims=True)
    tm = xn_ref.shape[0]
    for c in range(ROW_TILES):
        cs = slice(c * LANES, (c + 1) * LANES)
        rows = pl.ds(c, tm, stride=ROW_TILES)
        moe = w1 * y_ref[0, rows, :] + w2 * y_ref[1, rows, :]
        o_ref[:, cs] = xn_ref[:, cs] + mod_ref[0, 5:6, cs] * moe


def _combine(y2, comb, sel, xn, mod, T):
    n_tok = xn.shape[0]
    tm = 512
    bm = mod.shape[0]
    mod_idx = (lambda i: (i * tm // T, 0, 0)) if bm > 1 else (lambda i: (0, 0, 0))
    row = lambda w: pl.BlockSpec((tm, w), lambda i: (i, 0))
    return pl.pallas_call(
        _combine_kernel,
        grid=(n_tok // tm,),
        in_specs=[pl.BlockSpec((2, tm * ROW_TILES, LANES), lambda i: (0, i, 0)), row(LANES), row(LANES),
                  row(D_MODEL), pl.BlockSpec((1, 8, D_MODEL), mod_idx)],
        out_specs=row(D_MODEL),
        out_shape=jax.ShapeDtypeStruct((n_tok, D_MODEL), F32),
        compiler_params=_cparams(("arbitrary",)),
        name="moe_combine",
    )(y2, comb, sel, xn, mod)


def _moe(h3, comb, sel, xn, mod, wg, wu, wd, layer, T):
    n_tok = xn.shape[0]
    n_rows = _moe_rows(n_tok)
    pos, tile_expert, tile_valid = _moe_plan(sel)
    xs = _dispatch(pos, h3.reshape(n_tok, ROW_TILES, LANES), jnp.zeros((n_rows, ROW_TILES, LANES), F32))
    ys = _experts(tile_expert, tile_valid, xs.reshape(n_rows * ROW_TILES, LANES), wg, wu, wd, layer)
    y2 = _collect(pos, ys.reshape(n_rows, ROW_TILES, LANES))
    return _combine(y2.reshape(2, n_tok * ROW_TILES, LANES), comb, sel, xn, mod, T)


def kernel(x_prompt, x_sample, cache_mla, state_mlstm_C, state_mlstm_n, state_mlstm_m, state_rwkv, state_gla, c, c_ctx, w_ada, b_ada, g_mix, g_ffn, w_in, w_out, b_ml_gates, g_ml_norm, g_mla_qlat, g_mla_kvlat, w_mla_uq, w_mla_ukv, g_mla_qn, g_mla_kn, rw_w0, rw_w_up, rw_a0, rw_a_up, rw_g_up, rw_k_k, rw_k_a, rw_r_k, rw_ln, gla_g_up, gla_g_b, gla_norm, moe_w_rg, moe_b_rg, moe_w_re, moe_b_re, moe_w_gate, moe_w_up, moe_w_down):
    cc = jnp.concatenate([c_ctx[None], c, jnp.zeros((3, D_MODEL), F32)], axis=0)
    mod = _modulation(cc, w_ada, b_ada).reshape(DEPTH, 8, 6, D_MODEL)
    mod = jnp.pad(mod, ((0, 0), (0, 0), (0, 2), (0, 0)))

    layers = []
    for l in range(DEPTH):
        p = {'b_ml_gates': b_ml_gates[l], 'g_ml_norm': g_ml_norm[l], 'g_mla_qlat': g_mla_qlat[l],
             'g_mla_kvlat': g_mla_kvlat[l], 'w_mla_uq': w_mla_uq[l], 'w_mla_ukv': w_mla_ukv[l],
             'g_mla_qn': g_mla_qn[l], 'g_mla_kn': g_mla_kn[l], 'rw_w0': rw_w0[l], 'rw_w_up': rw_w_up[l],
             'rw_a0': rw_a0[l], 'rw_a_up': rw_a_up[l], 'rw_g_up': rw_g_up[l], 'rw_k_k': rw_k_k[l],
             'rw_k_a': rw_k_a[l], 'rw_r_k': rw_r_k[l], 'rw_ln': rw_ln[l], 'gla_g_up': gla_g_up[l],
             'gla_g_b': gla_g_b[l], 'gla_norm': gla_norm[l]}
        router_w = jnp.zeros((D_MODEL, LANES), F32)
        router_w = router_w.at[:, :N_EXPERTS].set(moe_w_re[l]).at[:, N_EXPERTS:N_EXPERTS + N_GROUPS].set(moe_w_rg[l])
        router_b = jnp.zeros((1, LANES), F32)
        router_b = router_b.at[0, :N_EXPERTS].set(moe_b_re[l]).at[0, N_EXPERTS:N_EXPERTS + N_GROUPS].set(moe_b_rg[l])
        layers.append(dict(
            w_in=_pack_w_in(w_in[l]), g_mix=g_mix[l].reshape(1, -1), g_ffn=g_ffn[l].reshape(1, -1),
            w_out=w_out[l].astype(BF16), ml=_mlstm_params(p), mla=_mla_params(p), rw=_rwkv_params(p),
            gla=_gla_params(p), router_w=_split_weight(router_w), router_b=router_b,
            layer=l))

    def block(x2, B, T, mod_g, lp, ctx):
        u3 = _in_proj(x2, mod_g, lp['g_mix'], lp['w_in'], T).reshape(B, T, PK_COLS)
        if ctx is None:
            ctx_kv = None
            ml_c0, ml_m0 = _mlstm_state_zero(B)
            rw_h0 = jnp.zeros((B, 2, RW_H, RW_N, RW_N), F32)
            gla_s0 = jnp.zeros((B, 2, GLA_H, GLA_DK, GLA_DV), F32)
        else:
            ctx_kv, ml_C0, ml_n0, ml_m0_, rw_S0, gla_s0 = ctx
            ml_c0, ml_m0 = _mlstm_state_in(ml_C0, ml_n0, ml_m0_)
            rw_h0 = jnp.swapaxes(rw_S0, -1, -2)
        ml_out, ml_c, ml_m = _mlstm(u3, *lp['ml'], ml_c0, ml_m0)
        mla_out, own_kv = _mla(u3, ctx_kv, lp['mla'])
        rw_out, rw_h = _rwkv(u3, lp['rw'], rw_h0)
        gla_out, gla_s = _gla(u3, lp['gla'], gla_s0)
        mixed = [t.reshape(B * T, GROUP_W) for t in (ml_out, mla_out, rw_out, gla_out)]
        xn, h3, comb, sel = _out_proj(x2, mixed, lp['w_out'], mod_g, lp['g_ffn'], lp['router_w'], lp['router_b'], T)
        x_new = _moe(h3, comb, sel, xn, mod_g, moe_w_gate, moe_w_up, moe_w_down, lp['layer'], T)
        ml_C, ml_n, ml_mm = _mlstm_state_out(ml_c, ml_m)
        return x_new, (own_kv, ml_C, ml_n, ml_mm, jnp.swapaxes(rw_h, -1, -2), gla_s)

    Bp, Tp = x_prompt.shape[:2]
    Bs, Ts = x_sample.shape[:2]
    xp = x_prompt.reshape(Bp * Tp, D_MODEL)
    ctx_states = []
    for l in range(DEPTH):
        xp, st = block(xp, Bp, Tp, mod[l, 0:1], layers[l], None)
        ctx_states.append(st)
    xs = x_sample.reshape(Bs * Ts, D_MODEL)
    for l in range(DEPTH):
        ctx = (cache_mla[:, l], state_mlstm_C[:, l], state_mlstm_n[:, l], state_mlstm_m[:, l],
               state_rwkv[:, l], state_gla[:, l])
        xs, _ = block(xs, Bs, Ts, mod[l, 1:1 + Bs], layers[l], ctx)
    outs = [jnp.stack([s[i] for s in ctx_states], axis=1) for i in range(6)]
    return (xp.reshape(x_prompt.shape), xs.reshape(x_sample.shape), *outs)
```

```python
import functools
import math

import numpy as np
import jax
import jax.numpy as jnp
from jax import lax
from jax.experimental import pallas as pl
from jax.experimental.pallas import tpu as pltpu

F32 = jnp.float32
BF16 = jnp.bfloat16

D_MODEL = 2048
DEPTH = 2
GRID_W = 64
GROUP_W = 512
ML_H, ML_DK = 4, 128
MLA_H, MLA_NOPE, MLA_ROPE, MLA_V = 4, 128, 64, 128
MLA_QK = MLA_NOPE + MLA_ROPE
Q_LORA, KV_LORA = 384, 256
ROPE_THETA = 10000.0
RW_H, RW_N = 8, 64
RW_DECAY_SCALE = math.exp(-0.5)
RW_LN_EPS = 64e-5
GLA_H, GLA_DK, GLA_DV = 4, 64, 128
GLA_GATE_RANK = 16
GLA_NORMALIZER = 16.0
N_GROUPS, EXPERTS_PER_GROUP, N_EXPERTS = 4, 4, 16
EXPERT_HIDDEN = 512
NORM_EPS = 1e-6
LANES = 128
ROW_TILES = D_MODEL // LANES
VMEM_LIMIT = 56 * 1024 * 1024

_REF_SPLITS = (
    ('ml_q', 512), ('ml_k', 512), ('ml_v', 512), ('ml_o', 512), ('ml_g', 16),
    ('mla_ql', Q_LORA), ('mla_ckv', KV_LORA), ('mla_kr', MLA_ROPE),
    ('rw_r', 512), ('rw_k', 512), ('rw_v', 512), ('rw_wd', 64), ('rw_ad', 64), ('rw_gd', 128),
    ('gla_q', 256), ('gla_k', 256), ('gla_v', 512), ('gla_gd', GLA_GATE_RANK), ('gla_g', 512),
)
_REF_OFF = {}
_o = 0
for _n, _w in _REF_SPLITS:
    _REF_OFF[_n] = (_o, _w)
    _o += _w
IN_COLS = _o

_PACKED = (
    ('ml_q', 512), ('ml_k', 512), ('ml_v', 512), ('ml_o', 512),
    ('rw_r', 512), ('rw_k', 512), ('rw_v', 512), ('gla_v', 512), ('gla_g', 512),
    ('mla_ql', 384), ('mla_ckv', 256), ('mla_kr', 64), ('mla_kr_sw', 64),
    ('gla_q', 256), ('gla_k', 256),
    ('ml_g', 128), ('rw_wd', 64), ('rw_ad', 64), ('rw_gd', 128), ('gla_gd', 128),
)
PK_OFF = {}
_o = 0
for _n, _w in _PACKED:
    PK_OFF[_n] = _o
    _o += _w
PK_COLS = _o


def _rope_swap_perm():
    idx = np.arange(MLA_ROPE)
    axis, half, f = idx // 32, (idx % 32) // 16, idx % 16
    return axis * 32 + (1 - half) * 16 + f


def _packed_column_index():
    src = np.full((PK_COLS,), -1, np.int64)
    for name, width in _PACKED:
        off = PK_OFF[name]
        if name == 'mla_kr_sw':
            s, w = _REF_OFF['mla_kr']
            src[off:off + w] = s + _rope_swap_perm()
        else:
            s, w = _REF_OFF[name]
            src[off:off + w] = s + np.arange(w)
    return src


_PK_SRC = _packed_column_index()


def _column_runs(src):
    runs, i = [], 0
    while i < len(src):
        j = i + 1
        while j < len(src) and ((src[i] < 0 and src[j] < 0) or (src[i] >= 0 and src[j] == src[i] + (j - i))):
            j += 1
        runs.append((int(src[i]), j - i))
        i = j
    return runs


_PK_RUNS = _column_runs(_PK_SRC)


def _pack_w_in(w):
    parts = [w[:, s:s + n] if s >= 0 else jnp.zeros((w.shape[0], n), w.dtype) for s, n in _PK_RUNS]
    return jnp.concatenate(parts, axis=1).astype(BF16)


def _cparams(sem):
    return pltpu.CompilerParams(dimension_semantics=sem, vmem_limit_bytes=VMEM_LIMIT)


def _log_sigmoid(x):
    return jnp.minimum(x, 0.0) - jnp.log(1.0 + jnp.exp(-jnp.abs(x)))


def _sigmoid(x):
    return 1.0 / (1.0 + jnp.exp(-x))


def _dot(a, b):
    return jnp.dot(a.astype(BF16), b.astype(BF16), preferred_element_type=F32)


def _dot_nt(a, b):
    return lax.dot_general(a.astype(BF16), b.astype(BF16), (((1,), (1,)), ((), ())), preferred_element_type=F32)


def _dot_tn(a, b):
    return lax.dot_general(a.astype(BF16), b.astype(BF16), (((0,), (0,)), ((), ())), preferred_element_type=F32)


def _split2(x):
    hi = x.astype(BF16)
    return hi, (x - hi.astype(F32)).astype(BF16)


def _split_weight(w):
    hi, lo = _split2(w)
    return jnp.stack([hi, lo])


def _running_sum(tri, x):
    hi = x.astype(BF16)
    r = x - hi.astype(F32)
    mid = r.astype(BF16)
    lo = (r - mid.astype(F32)).astype(BF16)
    n = x.shape[1]
    s = jnp.dot(tri, jnp.concatenate([hi, mid, lo], axis=1), preferred_element_type=F32)
    return s[:, :n] + s[:, n:2 * n] + s[:, 2 * n:]


def _dot_split(a, w_ref):
    a_hi, a_lo = _split2(a)
    w_hi = w_ref[0]
    return (jnp.dot(a_hi, w_hi, preferred_element_type=F32) + jnp.dot(a_lo, w_hi, preferred_element_type=F32)
            + jnp.dot(a_hi, w_ref[1], preferred_element_type=F32))


def _mod_kernel(c_ref, w_ref, b_ref, o_ref):
    c = c_ref[...]
    s = c * _sigmoid(c)
    o_ref[0] = _dot(s, w_ref[0]) + b_ref[0]


def _modulation(cc, w_ada, b_ada):
    tn = 1536
    n = 6 * D_MODEL
    return pl.pallas_call(
        _mod_kernel,
        grid=(DEPTH, n // tn),
        in_specs=[pl.BlockSpec((8, D_MODEL), lambda l, j: (0, 0)),
                  pl.BlockSpec((1, D_MODEL, tn), lambda l, j: (l, 0, j)),
                  pl.BlockSpec((1, 1, tn), lambda l, j: (l, 0, j))],
        out_specs=pl.BlockSpec((1, 8, tn), lambda l, j: (l, 0, j)),
        out_shape=jax.ShapeDtypeStruct((DEPTH, 8, n), F32),
        compiler_params=_cparams(("arbitrary", "arbitrary")),
        name="adaln_mod",
    )(cc, w_ada, b_ada.reshape(DEPTH, 1, n))


def _inproj_kernel(x_ref, mod_ref, g_ref, w_ref, o_ref, h_scr):
    @pl.when(pl.program_id(1) == 0)
    def _():
        x = x_ref[...]
        xn = x * lax.rsqrt(jnp.mean(x * x, axis=-1, keepdims=True) + NORM_EPS) * g_ref[...]
        h_scr[...] = (xn * (1.0 + mod_ref[0, 1:2, :]) + mod_ref[0, 0:1, :]).astype(BF16)

    o_ref[...] = jnp.dot(h_scr[...], w_ref[...], preferred_element_type=F32)


def _in_proj(x2, mod, g, w_packed, T):
    n_tok = x2.shape[0]
    tm = 1024
    tn = 1280
    bm = mod.shape[0]
    mod_idx = (lambda i, j: (i * tm // T, 0, 0)) if bm > 1 else (lambda i, j: (0, 0, 0))
    return pl.pallas_call(
        _inproj_kernel,
        grid=(n_tok // tm, PK_COLS // tn),
        in_specs=[pl.BlockSpec((tm, D_MODEL), lambda i, j: (i, 0)),
                  pl.BlockSpec((1, 8, D_MODEL), mod_idx),
                  pl.BlockSpec((1, D_MODEL), lambda i, j: (0, 0)),
                  pl.BlockSpec((D_MODEL, tn), lambda i, j: (0, j))],
        out_specs=pl.BlockSpec((tm, tn), lambda i, j: (i, j)),
        out_shape=jax.ShapeDtypeStruct((n_tok, PK_COLS), F32),
        scratch_shapes=[pltpu.VMEM((tm, D_MODEL), BF16)],
        compiler_params=_cparams(("arbitrary", "arbitrary")),
        name="in_proj",
    )(x2, mod, g, w_packed)


ML_CHUNK = 256


def _mlstm_kernel(q_ref, k_ref, v_ref, o_ref, g_ref, bias_ref, gn_ref, c0_ref, m0_ref,
                  out_ref, c_ref, m_ref, hs_ref, *, T):
    L = ML_CHUNK
    nc = T // L
    c_ref[...] = c0_ref[...]
    m_ref[...] = m0_ref[...]
    ii = lax.broadcasted_iota(jnp.int32, (L, L), 0)
    jj = lax.broadcasted_iota(jnp.int32, (L, L), 1)
    lane = lax.broadcasted_iota(jnp.int32, (1, LANES), 1)
    is_f = jnp.logical_and(lane % 8 >= 4, lane < 16)
    ones_col = (lax.broadcasted_iota(jnp.int32, (L, LANES), 1) == 0).astype(BF16)
    scale = ML_DK ** -0.5

    for d in range(2):
        mask = (jj <= ii) if d == 0 else (jj >= ii)
        tri = mask.astype(BF16)

        def chunk(ci, carry, d=d, mask=mask, tri=tri):
            c = ci if d == 0 else nc - 1 - ci
            s = pl.multiple_of(c * L, L)
            gates = g_ref[0, pl.ds(s, L), :] + bias_ref[...]
            gf = jnp.where(is_f, _log_sigmoid(gates), gates)
            cum = _running_sum(tri, gf)
            gf_t = gf.T
            cum_t = cum.T
            for h in range(ML_H):
                ci_, cf_ = d * 8 + h, d * 8 + 4 + h
                hs = slice(h * ML_DK, (h + 1) * ML_DK)
                ig_col, ig_row = gf[:, ci_:ci_ + 1], gf_t[ci_:ci_ + 1, :]
                b_col, b_row = cum[:, cf_:cf_ + 1], cum_t[cf_:cf_ + 1, :]
                b_last = b_col[L - 1:L, :] if d == 0 else b_col[0:1, :]
                m_prev = m_ref[0, d, h][:, 0:1]
                dmat = jnp.where(mask, b_col + (ig_row - b_row), -jnp.inf)
                m_inter = b_col + m_prev
                m_row = jnp.maximum(m_inter, jnp.max(dmat, axis=-1, keepdims=True))
                w_inter = jnp.exp(m_inter - m_row)
                q = (q_ref[0, pl.ds(s, L), hs] * scale).astype(BF16)
                k = k_ref[0, pl.ds(s, L), hs]
                v_aug = jnp.concatenate([v_ref[0, pl.ds(s, L), hs].astype(BF16), ones_col], axis=1)
                smat = _dot_nt(q, k) * jnp.exp(dmat - m_row)
                c_aug = c_ref[0, d, h]
                nd = w_inter * _dot(q, c_aug) + _dot(smat, v_aug)
                num, den = nd[:, :ML_DK], nd[:, ML_DK:ML_DK + 1]
                hh = num / jnp.maximum(jnp.abs(den), jnp.exp(-m_row))
                if d == 0:
                    hs_ref[pl.ds(s, L), hs] = hh
                else:
                    hs_ref[pl.ds(s, L), hs] = hs_ref[pl.ds(s, L), hs] + hh
                dk_col = b_last - b_col + ig_col
                m_new = jnp.maximum(b_last + m_prev, jnp.max(dk_col, axis=0, keepdims=True))
                w_key = jnp.exp(dk_col - m_new)
                c_scale = jnp.exp(b_last + m_prev - m_new)
                c_ref[0, d, h] = c_scale * c_aug + _dot_tn(k * w_key, v_aug)
                m_ref[0, d, h] = jnp.broadcast_to(m_new, (1, LANES))
            return carry

        lax.fori_loop(0, nc, chunk, 0)

    def finish(ci, carry):
        s = pl.multiple_of(ci * L, L)
        for h in range(ML_H):
            hs = slice(h * ML_DK, (h + 1) * ML_DK)
            x = hs_ref[pl.ds(s, L), hs]
            xc = x - jnp.mean(x, axis=-1, keepdims=True)
            y = xc * lax.rsqrt(jnp.mean(xc * xc, axis=-1, keepdims=True) + NORM_EPS) * gn_ref[:, hs]
            out_ref[0, pl.ds(s, L), hs] = (y * _sigmoid(o_ref[0, pl.ds(s, L), hs])).astype(BF16)
        return carry

    lax.fori_loop(0, nc, finish, 0)


def _mlstm_params(p):
    bias = jnp.zeros((1, LANES), F32).at[0, :16].set(p['b_ml_gates'])
    return bias, p['g_ml_norm'].reshape(1, 512)


def _mlstm_state_in(C0, n0, m0):
    c0 = jnp.concatenate([C0, n0[..., None], jnp.zeros(C0.shape[:-1] + (ML_DK - 1,), F32)], axis=-1)
    return c0, jnp.broadcast_to(m0[..., None, None], m0.shape + (1, LANES))


def _mlstm_state_zero(B):
    return jnp.zeros((B, 2, ML_H, ML_DK, 2 * ML_DK), F32), jnp.zeros((B, 2, ML_H, 1, LANES), F32)


def _mlstm_state_out(c, m):
    return c[..., :ML_DK], c[..., ML_DK], m[..., 0, 0]


def _mlstm(u3, bias, gnorm, c0, m0):
    B, T, _ = u3.shape
    blk = lambda name: pl.BlockSpec((1, T, 512), lambda b, o=PK_OFF[name] // 512: (b, 0, o))
    st_c = pl.BlockSpec((1, 2, ML_H, ML_DK, 2 * ML_DK), lambda b: (b, 0, 0, 0, 0))
    st_m = pl.BlockSpec((1, 2, ML_H, 1, LANES), lambda b: (b, 0, 0, 0, 0))
    return pl.pallas_call(
        functools.partial(_mlstm_kernel, T=T),
        grid=(B,),
        in_specs=[blk('ml_q'), blk('ml_k'), blk('ml_v'), blk('ml_o'),
                  pl.BlockSpec((1, T, LANES), lambda b: (b, 0, PK_OFF['ml_g'] // LANES)),
                  pl.BlockSpec((1, LANES), lambda b: (0, 0)),
                  pl.BlockSpec((1, 512), lambda b: (0, 0)),
                  st_c, st_m],
        out_specs=[pl.BlockSpec((1, T, 512), lambda b: (b, 0, 0)), st_c, st_m],
        out_shape=[jax.ShapeDtypeStruct((B, T, 512), BF16),
                   jax.ShapeDtypeStruct(c0.shape, F32),
                   jax.ShapeDtypeStruct(m0.shape, F32)],
        scratch_shapes=[pltpu.VMEM((T, 512), F32)],
        compiler_params=_cparams(("arbitrary",)),
        name="mlstm",
    )(u3, u3, u3, u3, u3, bias, gnorm, c0, m0)


MLA_BLK = 256


def _rope_tables(T):
    rows = T // GRID_W
    row = np.repeat(np.arange(rows, dtype=np.float64), GRID_W)
    col = np.tile(np.arange(GRID_W, dtype=np.float64), rows)
    inv = ROPE_THETA ** (-np.arange(MLA_ROPE // 4, dtype=np.float64) / (MLA_ROPE // 4))
    ang = np.stack([row[:, None] * inv, col[:, None] * inv], axis=1)
    cos = np.stack([np.cos(ang), np.cos(ang)], axis=2).reshape(T, MLA_ROPE)
    sin = np.stack([-np.sin(ang), np.sin(ang)], axis=2).reshape(T, MLA_ROPE)
    return jnp.asarray(np.concatenate([cos, sin], axis=1), F32)


def _mla_kernel(*refs, T, n_ctx, rope):
    if n_ctx:
        (u_ref, ctx_ref, cs_ref, gql_ref, gkv_ref, gains_ref, wqn_ref, wqr_ref, wqs_ref, wkn_ref, wv_ref,
         out_ref, kv_ref, qn_s, qr_s, kn_s, kr_s, v_s) = refs
    else:
        (u_ref, cs_ref, gql_ref, gkv_ref, gains_ref, wqn_ref, wqr_ref, wqs_ref, wkn_ref, wv_ref,
         out_ref, kv_ref, qn_s, qr_s, kn_s, kr_s, v_s) = refs
    Lb = MLA_BLK
    gq_n, gq_r, gq_s = gains_ref[0:1, :], gains_ref[1:2, 0:64], gains_ref[1:2, 64:128]
    gk_n, gk_r, gk_s = gains_ref[2:3, :], gains_ref[3:4, 0:64], gains_ref[3:4, 64:128]
    sm_scale = MLA_QK ** -0.5

    def store_keys(s, kn, kr, krs, cos, sin):
        kr_ss = jnp.sum(kr * kr, axis=-1, keepdims=True)
        for h in range(MLA_H):
            kn_h = kn[:, h * 128:(h + 1) * 128]
            rk = lax.rsqrt((jnp.sum(kn_h * kn_h, axis=-1, keepdims=True) + kr_ss) / MLA_QK + NORM_EPS)
            kn_s[pl.ds(s, Lb), h * 128:(h + 1) * 128] = (kn_h * rk * gk_n).astype(BF16)
            kr_h = kr * gk_r
            if cos is not None:
                kr_h = kr_h * cos + (krs * gk_s) * sin
            kr_s[pl.ds(s, Lb), h * 64:(h + 1) * 64] = (kr_h * rk).astype(BF16)

    def prep(ci, carry):
        s = pl.multiple_of(ci * Lb, Lb)
        u = u_ref[0, pl.ds(s, Lb), :]
        ql, ckv, kr, krs = u[:, :384], u[:, 384:640], u[:, 640:704], u[:, 704:768]
        qln = ql * lax.rsqrt(jnp.mean(ql * ql, axis=-1, keepdims=True) + NORM_EPS) * gql_ref[...]
        ckvn = ckv * lax.rsqrt(jnp.mean(ckv * ckv, axis=-1, keepdims=True) + NORM_EPS) * gkv_ref[...]
        kv_ref[0, pl.ds(s, Lb), :] = jnp.concatenate([ckvn, kr], axis=1)
        cos = cs_ref[pl.ds(s, Lb), 0:64] if rope else None
        sin = cs_ref[pl.ds(s, Lb), 64:128] if rope else None
        qn, qr, qs = _dot(qln, wqn_ref[...]), _dot(qln, wqr_ref[...]), _dot(qln, wqs_ref[...])
        for h in range(MLA_H):
            qn_h, qr_h = qn[:, h * 128:(h + 1) * 128], qr[:, h * 64:(h + 1) * 64]
            ss = jnp.sum(qn_h * qn_h, axis=-1, keepdims=True) + jnp.sum(qr_h * qr_h, axis=-1, keepdims=True)
            rq = lax.rsqrt(ss / MLA_QK + NORM_EPS) * sm_scale
            qn_s[pl.ds(s, Lb), h * 128:(h + 1) * 128] = (qn_h * rq * gq_n).astype(BF16)
            qr_h = qr_h * gq_r
            if rope:
                qr_h = qr_h * cos + (qs[:, h * 64:(h + 1) * 64] * gq_s) * sin
            qr_s[pl.ds(s, Lb), h * 64:(h + 1) * 64] = (qr_h * rq).astype(BF16)
        v_s[pl.ds(s, Lb), :] = _dot(ckvn, wv_ref[...]).astype(BF16)
        store_keys(s, _dot(ckvn, wkn_ref[...]), kr, krs, cos, sin)
        return carry

    lax.fori_loop(0, T // Lb, prep, 0)

    for ci in range(n_ctx // Lb):
        cx = ctx_ref[0, ci * Lb:(ci + 1) * Lb, :]
        ckv_c, kr_c = cx[:, :KV_LORA], cx[:, KV_LORA:KV_LORA + MLA_ROPE]
        v_s[T + ci * Lb:T + (ci + 1) * Lb, :] = _dot(ckv_c, wv_ref[...]).astype(BF16)
        store_keys(T + ci * Lb, _dot(ckv_c, wkn_ref[...]), kr_c, None, None, None)

    for h in range(MLA_H):
        def attend(qi, carry, h=h):
            s = pl.multiple_of(qi * Lb, Lb)
            sc = (_dot_nt(qn_s[pl.ds(s, Lb), h * 128:(h + 1) * 128], kn_s[:, h * 128:(h + 1) * 128])
                  + _dot_nt(qr_s[pl.ds(s, Lb), h * 64:(h + 1) * 64], kr_s[:, h * 64:(h + 1) * 64]))
            p = jnp.exp(sc - jnp.max(sc, axis=-1, keepdims=True))
            o = _dot(p, v_s[:, h * 128:(h + 1) * 128]) / jnp.sum(p, axis=-1, keepdims=True)
            out_ref[0, pl.ds(s, Lb), h * 128:(h + 1) * 128] = o.astype(BF16)
            return carry

        lax.fori_loop(0, T // Lb, attend, 0)


def _mla_params(p):
    wq = p['w_mla_uq'].reshape(Q_LORA, MLA_H, MLA_QK)
    sw = _rope_swap_perm()
    wq_n = wq[:, :, :MLA_NOPE].reshape(Q_LORA, 512).astype(BF16)
    wq_r = wq[:, :, MLA_NOPE:].reshape(Q_LORA, 256).astype(BF16)
    wq_s = wq[:, :, MLA_NOPE:][:, :, sw].reshape(Q_LORA, 256).astype(BF16)
    wkv = p['w_mla_ukv'].reshape(KV_LORA, MLA_H, MLA_NOPE + MLA_V)
    wk_n = wkv[:, :, :MLA_NOPE].reshape(KV_LORA, 512).astype(BF16)
    wv = wkv[:, :, MLA_NOPE:].reshape(KV_LORA, 512).astype(BF16)
    gq, gk = p['g_mla_qn'], p['g_mla_kn']
    gains = jnp.zeros((8, LANES), F32)
    gains = gains.at[0].set(gq[:128]).at[1, :64].set(gq[128:]).at[1, 64:].set(gq[128:][sw])
    gains = gains.at[2].set(gk[:128]).at[3, :64].set(gk[128:]).at[3, 64:].set(gk[128:][sw])
    return (p['g_mla_qlat'].reshape(1, -1), p['g_mla_kvlat'].reshape(1, -1), gains, wq_n, wq_r, wq_s, wk_n, wv)


def _mla(u3, ctx_kv, mp):
    B, T, _ = u3.shape
    n_ctx = 0 if ctx_kv is None else ctx_kv.shape[1]
    rope = ctx_kv is not None
    tk = T + n_ctx
    full = lambda a: pl.BlockSpec(a.shape, lambda b, n=a.ndim: (0,) * n)
    cs = _rope_tables(T) if rope else jnp.zeros((T, LANES), F32)
    ins = [u3] + ([ctx_kv] if rope else []) + [cs] + list(mp)
    specs = [pl.BlockSpec((1, T, 768), lambda b: (b, 0, PK_OFF['mla_ql'] // 768))]
    if rope:
        specs.append(pl.BlockSpec((1, n_ctx, KV_LORA + MLA_ROPE), lambda b: (b, 0, 0)))
    specs += [full(a) for a in ins[len(specs):]]
    return pl.pallas_call(
        functools.partial(_mla_kernel, T=T, n_ctx=n_ctx, rope=rope),
        grid=(B,),
        in_specs=specs,
        out_specs=[pl.BlockSpec((1, T, 512), lambda b: (b, 0, 0)),
                   pl.BlockSpec((1, T, KV_LORA + MLA_ROPE), lambda b: (b, 0, 0))],
        out_shape=[jax.ShapeDtypeStruct((B, T, 512), BF16),
                   jax.ShapeDtypeStruct((B, T, KV_LORA + MLA_ROPE), F32)],
        scratch_shapes=[pltpu.VMEM((T, 512), BF16), pltpu.VMEM((T, 256), BF16),
                        pltpu.VMEM((tk, 512), BF16), pltpu.VMEM((tk, 256), BF16), pltpu.VMEM((tk, 512), BF16)],
        compiler_params=_cparams(("arbitrary",)),
        name="mla",
    )(*ins)


GLA_CHUNK = 64
GLA_LEAF = 4


def _gla_kernel(q_ref, k_ref, v_ref, gd_ref, g_ref, gup_ref, gb_ref, gn_ref, hsel_ref, s0_ref,
                out_ref, s_ref, os_ref, la_ref, *, T):
    L, C = GLA_CHUNK, GLA_LEAF
    nc = T // L
    Lf = 256
    s_ref[...] = s0_ref[...]

    def gates(ci, carry):
        s = pl.multiple_of(ci * Lf, Lf)
        x = _dot_split(gd_ref[0, pl.ds(s, Lf), :], gup_ref) + gb_ref[...]
        la_ref[pl.ds(s, Lf), :] = _log_sigmoid(x) / GLA_NORMALIZER
        return carry

    lax.fori_loop(0, T // Lf, gates, 0)
    ii = lax.broadcasted_iota(jnp.int32, (L, L), 0)
    jj = lax.broadcasted_iota(jnp.int32, (L, L), 1)
    eye = (lax.broadcasted_iota(jnp.int32, (GLA_DK, GLA_DK), 0)
           == lax.broadcasted_iota(jnp.int32, (GLA_DK, GLA_DK), 1)).astype(F32)
    row_id = lax.broadcasted_iota(jnp.int32, (L, 1), 0)
    scale = GLA_DK ** -0.5
    hsel = hsel_ref[...]
    levels = []
    span = C
    while span < L:
        levels.append(span)
        span *= 2

    for d in range(2):
        causal = (jj <= ii) if d == 0 else (jj >= ii)
        tri = causal.astype(BF16)

        def chunk(ci, carry, d=d, tri=tri):
            c = ci if d == 0 else nc - 1 - ci
            s = pl.multiple_of(c * L, L)
            la = la_ref[pl.ds(s, L), d * GLA_H * GLA_DK:(d + 1) * GLA_H * GLA_DK]
            b = _running_sum(tri, la)
            total = b[L - 1:L, :] if d == 0 else b[0:1, :]
            q = q_ref[0, pl.ds(s, L), :] * scale
            k = k_ref[0, pl.ds(s, L), :]
            amat = [jnp.zeros((L, L), F32) for _ in range(GLA_H)]
            for sp in levels:
                b3 = b.reshape(L // (2 * sp), 2 * sp, GLA_H * GLA_DK)
                edge = b3[:, sp - 1:sp, :] if d == 0 else b3[:, sp:sp + 1, :]
                bref = jnp.broadcast_to(edge, b3.shape).reshape(L, GLA_H * GLA_DK)
                later = (row_id % (2 * sp) >= sp) if d == 0 else (row_id % (2 * sp) < sp)
                e = jnp.exp(jnp.where(later, b - bref, bref - b))
                qs = jnp.where(later, q * e, 0.0)
                kt = jnp.where(later, 0.0, k * e)
                same = (ii // (2 * sp)) == (jj // (2 * sp))
                for h in range(GLA_H):
                    ks = slice(h * GLA_DK, (h + 1) * GLA_DK)
                    amat[h] = amat[h] + jnp.where(same, _dot_nt(qs[:, ks], kt[:, ks]), 0.0)
            terms = []
            for dl in range(C):
                if dl == 0:
                    terms.append(q * k)
                    continue
                sh = dl if d == 0 else L - dl
                ok = (row_id % C >= dl) if d == 0 else (row_id % C < C - dl)
                kd = pltpu.roll(k, sh, axis=0)
                bd = pltpu.roll(b, sh, axis=0)
                terms.append(q * kd * jnp.exp(jnp.where(ok, b - bd, 0.0)))
            tt = jnp.concatenate(terms, axis=0)
            t_hi = tt.astype(BF16)
            t_lo = (tt - t_hi.astype(F32)).astype(BF16)
            diag = (jnp.dot(t_hi, hsel, preferred_element_type=F32)
                    + jnp.dot(t_lo, hsel, preferred_element_type=F32))
            for dl in range(C):
                pair = jnp.logical_and(jj == (ii - dl if d == 0 else ii + dl), ii // C == jj // C)
                for h in range(GLA_H):
                    amat[h] = amat[h] + jnp.where(pair, diag[dl * L:(dl + 1) * L, h:h + 1], 0.0)
            q_in = q * jnp.exp(b)
            k_out = k * jnp.exp(total - b)
            f_row = jnp.exp(total)
            for h in range(GLA_H):
                ks = slice(h * GLA_DK, (h + 1) * GLA_DK)
                vs = slice(h * GLA_DV, (h + 1) * GLA_DV)
                v = v_ref[0, pl.ds(s, L), vs]
                st = s_ref[0, d, h]
                o = _dot(amat[h], v) + _dot(q_in[:, ks], st)
                if d == 0:
                    os_ref[pl.ds(s, L), vs] = o
                else:
                    os_ref[pl.ds(s, L), vs] = os_ref[pl.ds(s, L), vs] + o
                f_col = jnp.sum(eye * f_row[:, ks], axis=1, keepdims=True)
                s_ref[0, d, h] = f_col * st + _dot_tn(k_out[:, ks], v)
            return carry

        lax.fori_loop(0, nc, chunk, 0)

    Lf = 256

    def finish(ci, carry):
        s = pl.multiple_of(ci * Lf, Lf)
        for h in range(GLA_H):
            vs = slice(h * GLA_DV, (h + 1) * GLA_DV)
            o = os_ref[pl.ds(s, Lf), vs]
            y = o * lax.rsqrt(jnp.mean(o * o, axis=-1, keepdims=True) + NORM_EPS) * gn_ref[:, vs]
            g = g_ref[0, pl.ds(s, Lf), vs]
            out_ref[0, pl.ds(s, Lf), vs] = (y * (g * _sigmoid(g))).astype(BF16)
        return carry

    lax.fori_loop(0, T // Lf, finish, 0)


def _gla_params(p):
    gup = jnp.zeros((LANES, 2 * GLA_H * GLA_DK), F32)
    gup = gup.at[:GLA_GATE_RANK, :].set(jnp.concatenate([p['gla_g_up'][0], p['gla_g_up'][1]], axis=1))
    hsel = jnp.asarray(np.arange(GLA_H * GLA_DK)[:, None] // GLA_DK == np.arange(LANES)[None, :], BF16)
    return _split_weight(gup), p['gla_g_b'].reshape(1, -1), p['gla_norm'].reshape(1, -1), hsel


def _gla(u3, gp, s0):
    B, T, _ = u3.shape
    blk = lambda name, w: pl.BlockSpec((1, T, w), lambda b, o=PK_OFF[name] // w: (b, 0, o))
    full = lambda a: pl.BlockSpec(a.shape, lambda b, n=a.ndim: (0,) * n)
    st = pl.BlockSpec((1, 2, GLA_H, GLA_DK, GLA_DV), lambda b: (b, 0, 0, 0, 0))
    return pl.pallas_call(
        functools.partial(_gla_kernel, T=T),
        grid=(B,),
        in_specs=[blk('gla_q', 256), blk('gla_k', 256), blk('gla_v', 512), blk('gla_gd', 128), blk('gla_g', 512),
                  full(gp[0]), full(gp[1]), full(gp[2]), full(gp[3]), st],
        out_specs=[pl.BlockSpec((1, T, 512), lambda b: (b, 0, 0)), st],
        out_shape=[jax.ShapeDtypeStruct((B, T, 512), BF16), jax.ShapeDtypeStruct(s0.shape, F32)],
        scratch_shapes=[pltpu.VMEM((T, 512), F32), pltpu.VMEM((T, 2 * GLA_H * GLA_DK), F32)],
        compiler_params=_cparams(("arbitrary",)),
        name="gla",
    )(u3, u3, u3, u3, u3, *gp, s0)


RW_CHUNK = 64
RW_UNROLL = 2


def _seg_sum(x, bd):
    hi = x.astype(BF16)
    lo = (x - hi.astype(F32)).astype(BF16)
    return jnp.dot(hi, bd, preferred_element_type=F32) + jnp.dot(lo, bd, preferred_element_type=F32)


def _rwkv_kernel(r_ref, k_ref, v_ref, wa_ref, gd_ref, wwa_ref, w0a0_ref, gup_ref, kk_ref, ka_ref, rk_ref, ln_ref,
                 bd_ref, h0_ref, out_ref, h_ref, ys_ref, pre_ref, *, T):
    L, N = RW_CHUNK, RW_N
    nc = T // L
    h_ref[...] = h0_ref[...]
    ii = lax.broadcasted_iota(jnp.int32, (L, L), 0)
    jj = lax.broadcasted_iota(jnp.int32, (L, L), 1)
    eye = (ii == jj).astype(F32)
    lane = lax.broadcasted_iota(jnp.int32, (1, LANES), 1)
    bd = bd_ref[...]
    Lf = 256

    def gates(ci, carry):
        s = pl.multiple_of(ci * Lf, Lf)
        wa = wa_ref[0, pl.ds(s, Lf), :]
        pre_ref[pl.ds(s, Lf), :] = _dot_split(jnp.where(lane < 64, jnp.tanh(wa), wa), wwa_ref) + w0a0_ref[...]
        return carry

    lax.fori_loop(0, T // Lf, gates, 0)

    masks = [((jj < ii), (jj <= ii)), ((jj > ii), (jj >= ii))]
    U = RW_UNROLL

    def chunk(ci, carry):
        chains = []
        for d in range(2):
            strict, incl = masks[d]
            tri = incl.astype(BF16)
            for u in range(U):
                c = ci * U + u if d == 0 else nc - 1 - (ci * U + u)
                s = pl.multiple_of(c * L, L)
                r = r_ref[0, pl.ds(s, L), :]
                k = k_ref[0, pl.ds(s, L), :]
                v = v_ref[0, pl.ds(s, L), :]
                pre = pre_ref[pl.ds(s, L), d * 1024:(d + 1) * 1024]
                logw = -RW_DECAY_SCALE * _sigmoid(pre[:, :512])
                a = _sigmoid(pre[:, 512:])
                kkr = k * kk_ref[...]
                kk = kkr * lax.rsqrt(_seg_sum(kkr * kkr, bd) + 1e-12)
                kt = k * (1.0 + (a - 1.0) * ka_ref[...])
                bh = kk * a
                lg = _running_sum(tri, logw)
                lg_end = lg[L - 1:L, :] if d == 0 else lg[0:1, :]
                a_t = -kk * jnp.exp(lg - logw)
                r_t = r * jnp.exp(lg)
                e_inv = jnp.exp(-lg)
                k_t, b_t = kt * e_inv, bh * e_inv
                e_end = jnp.exp(lg_end - lg)
                k_e, b_e = kt * e_end, bh * e_end
                g_end = jnp.exp(lg_end)
                for h in range(RW_H):
                    sl = slice(h * N, (h + 1) * N)
                    chains.append(dict(d=d, u=u, h=h, s=s, strict=strict, incl=incl, a=a_t[:, sl], r=r_t[:, sl],
                                       b=b_t[:, sl], k=k_t[:, sl], ke=k_e[:, sl], be=b_e[:, sl], g=g_end[:, sl],
                                       v=v[:, sl]))
        ms = [_dot_nt(jnp.concatenate([c['a'], c['r']], axis=0), jnp.concatenate([c['b'], c['k']], axis=0))
              for c in chains]
        pws = [jnp.where(c['strict'], m[:L, :L], 0.0) for c, m in zip(chains, ms)]
        xs = [eye + n for n in pws]
        for _ in range(5):
            pws = [_dot(pw, pw) for pw in pws]
            xs = [x + _dot(x, pw) for x, pw in zip(xs, pws)]
        mvs = [_dot(jnp.where(c['strict'], m[:L, L:], 0.0), c['v']) for c, m in zip(chains, ms)]
        tws = [_dot(x, jnp.concatenate([c['a'], mv], axis=1)) for x, c, mv in zip(xs, chains, mvs)]
        qys = [_dot(jnp.where(c['incl'], m[L:, :L], 0.0), tw) for c, m, tw in zip(chains, ms, tws)]
        ylocs = [_dot(jnp.where(c['incl'], m[L:, L:], 0.0), c['v']) + qy[:, N:] for c, m, qy in zip(chains, ms, qys)]
        pgs = [_dot_tn(c['be'], tw) for c, tw in zip(chains, tws)]
        gmats = [_dot_tn(c['ke'], c['v']) + pg[:, N:] for c, pg in zip(chains, pgs)]
        for d in range(2):
            for h in range(RW_H):
                hst = h_ref[0, d, h]
                for u in range(U):
                    i = (d * U + u) * RW_H + h
                    c = chains[i]
                    y = _dot(c['r'] + qys[i][:, :N], hst) + ylocs[i]
                    ys_ref[pl.ds(c['s'], L), d * 512 + h * N:d * 512 + (h + 1) * N] = y
                    hst = _dot(eye * c['g'] + pgs[i][:, :N], hst) + gmats[i]
                h_ref[0, d, h] = hst
        return carry

    lax.fori_loop(0, nc // U, chunk, 0)

    def finish(ci, carry):
        s = pl.multiple_of(ci * Lf, Lf)
        r = r_ref[0, pl.ds(s, Lf), :]
        k = k_ref[0, pl.ds(s, Lf), :]
        rk = r * k * rk_ref[...]
        bonus = jnp.zeros((Lf, 512), F32)
        for d in range(2):
            a = _sigmoid(pre_ref[pl.ds(s, Lf), d * 1024 + 512:(d + 1) * 1024])
            bonus = bonus + _seg_sum(rk * (1.0 + (a - 1.0) * ka_ref[...]), bd)
        y = ys_ref[pl.ds(s, Lf), 0:512] + ys_ref[pl.ds(s, Lf), 512:1024]
        yc = y - _seg_sum(y, bd) / N
        yn = yc * lax.rsqrt(_seg_sum(yc * yc, bd) / N + RW_LN_EPS) * ln_ref[...]
        g = _dot(_sigmoid(gd_ref[0, pl.ds(s, Lf), :]), gup_ref[...])
        out_ref[0, pl.ds(s, Lf), :] = ((yn + bonus * v_ref[0, pl.ds(s, Lf), :]) * g).astype(BF16)
        return carry

    lax.fori_loop(0, T // Lf, finish, 0)


def _rwkv_params(p):
    wwa = jnp.zeros((LANES, 2048), F32)
    for d in range(2):
        wwa = wwa.at[:64, d * 1024:d * 1024 + 512].set(p['rw_w_up'][d])
        wwa = wwa.at[64:, d * 1024 + 512:(d + 1) * 1024].set(p['rw_a_up'][d])
    w0a0 = jnp.concatenate([p['rw_w0'][0], p['rw_a0'][0], p['rw_w0'][1], p['rw_a0'][1]]).reshape(1, 2048)
    seg = np.arange(512) // RW_N
    bd = jnp.asarray(seg[:, None] == seg[None, :], BF16)
    row = lambda n: p[n].reshape(1, -1)
    return (_split_weight(wwa), w0a0, p['rw_g_up'].astype(BF16), row('rw_k_k'), row('rw_k_a'), row('rw_r_k'),
            row('rw_ln'), bd)


def _rwkv(u3, rp, h0):
    B, T, _ = u3.shape
    blk = lambda name, w: pl.BlockSpec((1, T, w), lambda b, o=PK_OFF[name] // w: (b, 0, o))
    full = lambda a: pl.BlockSpec(a.shape, lambda b, n=a.ndim: (0,) * n)
    st = pl.BlockSpec((1, 2, RW_H, RW_N, RW_N), lambda b: (b, 0, 0, 0, 0))
    return pl.pallas_call(
        functools.partial(_rwkv_kernel, T=T),
        grid=(B,),
        in_specs=[blk('rw_r', 512), blk('rw_k', 512), blk('rw_v', 512), blk('rw_wd', 128), blk('rw_gd', 128)]
                 + [full(a) for a in rp] + [st],
        out_specs=[pl.BlockSpec((1, T, 512), lambda b: (b, 0, 0)), st],
        out_shape=[jax.ShapeDtypeStruct((B, T, 512), BF16), jax.ShapeDtypeStruct(h0.shape, F32)],
        scratch_shapes=[pltpu.VMEM((T, 1024), F32), pltpu.VMEM((T, 2048), F32)],
        compiler_params=_cparams(("arbitrary",)),
        name="rwkv7",
    )(u3, u3, u3, u3, u3, *rp, h0)


def _route(logits):
    lane = lax.broadcasted_iota(jnp.int32, (1, LANES), 1)
    far = jnp.int32(2 * LANES)
    neg = -jnp.inf
    gl = jnp.where(jnp.logical_and(lane >= N_EXPERTS, lane < N_EXPERTS + N_GROUPS), logits, neg)
    gmax = jnp.max(gl, axis=-1, keepdims=True)
    grp = jnp.min(jnp.where(gl == gmax, lane, far), axis=-1, keepdims=True) - N_EXPERTS
    p_grp = 1.0 / jnp.sum(jnp.exp(gl - gmax), axis=-1, keepdims=True)
    el = jnp.where(jnp.logical_and(lane < N_EXPERTS, lane // EXPERTS_PER_GROUP == grp), logits, neg)
    v1 = jnp.max(el, axis=-1, keepdims=True)
    i1 = jnp.min(jnp.where(el == v1, lane, far), axis=-1, keepdims=True)
    el2 = jnp.where(lane == i1, neg, el)
    v2 = jnp.max(el2, axis=-1, keepdims=True)
    i2 = jnp.min(jnp.where(el2 == v2, lane, far), axis=-1, keepdims=True)
    e = jnp.exp(v2 - v1)
    w1 = 1.0 / (1.0 + e)
    comb = jnp.where(lane == i1, p_grp * w1, jnp.where(lane == i2, p_grp * (e * w1), 0.0))
    sel = jnp.where(lane == i1, 1.0, jnp.where(lane == i2, 2.0, 0.0))
    return comb, sel


def _outproj_kernel(x_ref, m0_ref, m1_ref, m2_ref, m3_ref, w_ref, mod_ref, g_ref, wr_ref, br_ref,
                    xn_ref, h3_ref, comb_ref, sel_ref):
    y = jnp.dot(m0_ref[...], w_ref[0:512, :], preferred_element_type=F32)
    for i, m_ref in enumerate((m1_ref, m2_ref, m3_ref), start=1):
        y = y + jnp.dot(m_ref[...], w_ref[i * 512:(i + 1) * 512, :], preferred_element_type=F32)
    xn = x_ref[...] + mod_ref[0, 2:3, :] * y
    xn_ref[...] = xn
    h = xn * lax.rsqrt(jnp.mean(xn * xn, axis=-1, keepdims=True) + NORM_EPS) * g_ref[...]
    h = h * (1.0 + mod_ref[0, 4:5, :]) + mod_ref[0, 3:4, :]
    tm = h.shape[0]
    for c in range(ROW_TILES):
        h3_ref[pl.ds(c, tm, stride=ROW_TILES), :] = h[:, c * LANES:(c + 1) * LANES]
    comb_ref[...], sel_ref[...] = _route(_dot_split(h, wr_ref) + br_ref[...])


def _out_proj(x2, mixed, w_out, mod, g, router_w, router_b, T):
    n_tok = x2.shape[0]
    tm = 512
    bm = mod.shape[0]
    mod_idx = (lambda i: (i * tm // T, 0, 0)) if bm > 1 else (lambda i: (0, 0, 0))
    row = lambda w: pl.BlockSpec((tm, w), lambda i: (i, 0))
    full = lambda a: pl.BlockSpec(a.shape, lambda i, n=a.ndim: (0,) * n)
    return pl.pallas_call(
        _outproj_kernel,
        grid=(n_tok // tm,),
        in_specs=[row(D_MODEL)] + [row(GROUP_W)] * 4 + [full(w_out), pl.BlockSpec((1, 8, D_MODEL), mod_idx),
                                                       full(g), full(router_w), full(router_b)],
        out_specs=[row(D_MODEL), pl.BlockSpec((tm * ROW_TILES, LANES), lambda i: (i, 0)), row(LANES), row(LANES)],
        out_shape=[jax.ShapeDtypeStruct((n_tok, D_MODEL), F32),
                   jax.ShapeDtypeStruct((n_tok * ROW_TILES, LANES), F32),
                   jax.ShapeDtypeStruct((n_tok, LANES), F32), jax.ShapeDtypeStruct((n_tok, LANES), F32)],
        compiler_params=_cparams(("arbitrary",)),
        name="out_proj",
    )(x2, *mixed, w_out, mod, g, router_w, router_b)


MOE_TM = 256
MOE_ROWS = 256


def _moe_rows(n_tok):
    return 2 * n_tok + N_EXPERTS * MOE_TM


def _plan_kernel(sel_ref, pos_ref, tmap_ref):
    n_tok = sel_ref.shape[0]
    blk = 512
    lane = lax.broadcasted_iota(jnp.int32, (1, LANES), 1)
    earlier = (lax.broadcasted_iota(jnp.int32, (blk, blk), 1)
               < lax.broadcasted_iota(jnp.int32, (blk, blk), 0)).astype(BF16)
    before = (lax.broadcasted_iota(jnp.int32, (LANES, LANES), 0)
              < lax.broadcasted_iota(jnp.int32, (LANES, LANES), 1)).astype(BF16)

    def count(i, acc):
        s = pl.multiple_of(i * blk, blk)
        return acc + jnp.sum((sel_ref[pl.ds(s, blk), :] > 0.0).astype(F32), axis=0, keepdims=True)

    counts = lax.fori_loop(0, n_tok // blk, count, jnp.zeros((1, LANES), F32))
    tiles = jnp.floor((counts + (MOE_TM - 1)) * (1.0 / MOE_TM))
    tile_start = _dot(jnp.broadcast_to(tiles, (8, LANES)), before)[0:1, :]
    tile_end = tile_start + tiles
    base = tile_start * MOE_TM
    n_tiles = jnp.sum(tiles, axis=-1, keepdims=True)
    j = lax.broadcasted_iota(jnp.int32, (tmap_ref.shape[0], 1), 0).astype(F32)
    done = jnp.logical_and(tile_end <= j, lane < N_EXPERTS)
    expert = jnp.minimum(jnp.sum(done.astype(F32), axis=-1, keepdims=True), N_EXPERTS - 1.0)
    valid = (j < n_tiles).astype(F32)
    tmap_ref[...] = jnp.where(lane == 0, expert, jnp.where(lane == 1, valid, 0.0)).astype(jnp.int32)

    def place(i, seen):
        s = pl.multiple_of(i * blk, blk)
        sel = sel_ref[pl.ds(s, blk), :]
        one = (sel > 0.0).astype(F32)
        row = base + seen + _dot(earlier, one)
        p1 = jnp.sum(jnp.where(sel == 1.0, row, 0.0), axis=-1, keepdims=True)
        p2 = jnp.sum(jnp.where(sel == 2.0, row, 0.0), axis=-1, keepdims=True)
        pos_ref[pl.ds(s, blk), :] = jnp.where(lane == 0, p1, jnp.where(lane == 1, p2, 0.0)).astype(jnp.int32)
        return seen + jnp.sum(one, axis=0, keepdims=True)

    lax.fori_loop(0, n_tok // blk, place, jnp.zeros((1, LANES), F32))


def _moe_plan(sel):
    n_tok = sel.shape[0]
    n_tiles = _moe_rows(n_tok) // MOE_TM
    pos, tmap = pl.pallas_call(
        _plan_kernel,
        out_shape=[jax.ShapeDtypeStruct((n_tok, LANES), jnp.int32),
                   jax.ShapeDtypeStruct((-(-n_tiles // 8) * 8, LANES), jnp.int32)],
        compiler_params=pltpu.CompilerParams(vmem_limit_bytes=VMEM_LIMIT),
        name="moe_plan",
    )(sel)
    return jnp.transpose(pos[:, :2]), tmap[:n_tiles, 0], tmap[:n_tiles, 1]


def _dispatch_kernel(pos_ref, src_ref, init_ref, dst_ref, sem):
    del init_ref
    base = pl.program_id(0) * MOE_ROWS

    def copy(j, k):
        return pltpu.make_async_copy(src_ref.at[j], dst_ref.at[pos_ref[k, base + j]], sem)

    def start(j, c):
        copy(j, 0).start()
        copy(j, 1).start()
        return c

    def wait(j, c):
        copy(j, 0).wait()
        copy(j, 1).wait()
        return c

    lax.fori_loop(0, MOE_ROWS, start, 0)
    lax.fori_loop(0, MOE_ROWS, wait, 0)


def _dispatch(pos, h3, init):
    n_tok = h3.shape[0]
    any_spec = pl.BlockSpec(memory_space=pl.ANY)
    return pl.pallas_call(
        _dispatch_kernel,
        grid_spec=pltpu.PrefetchScalarGridSpec(
            num_scalar_prefetch=1, grid=(n_tok // MOE_ROWS,),
            in_specs=[pl.BlockSpec((MOE_ROWS, ROW_TILES, LANES), lambda i, p: (i, 0, 0)), any_spec],
            out_specs=any_spec, scratch_shapes=[pltpu.SemaphoreType.DMA(())]),
        out_shape=jax.ShapeDtypeStruct(init.shape, init.dtype),
        input_output_aliases={2: 0},
        compiler_params=pltpu.CompilerParams(dimension_semantics=("arbitrary",)),
        name="moe_dispatch",
    )(pos, h3, init)


def _experts_kernel(te_ref, tv_ref, xs_ref, wg_ref, wu_ref, wd_ref, ys_ref, wg_s, wu_s, wd_s):
    i = pl.program_id(0)
    fresh = jnp.logical_or(i == 0, te_ref[i] != te_ref[jnp.maximum(i - 1, 0)])

    @pl.when(jnp.logical_and(fresh, tv_ref[i] == 1))
    def _():
        wg_s[...] = wg_ref[0, 0].astype(BF16)
        wu_s[...] = wu_ref[0, 0].astype(BF16)
        wd_s[...] = wd_ref[0, 0].astype(BF16)

    @pl.when(tv_ref[i] == 1)
    def _():
        x = jnp.concatenate([xs_ref[pl.ds(c, MOE_TM, stride=ROW_TILES), :] for c in range(ROW_TILES)],
                            axis=1).astype(BF16)
        a = jnp.dot(x, wg_s[...], preferred_element_type=F32)
        b = jnp.dot(x, wu_s[...], preferred_element_type=F32)
        y = jnp.dot(((a * _sigmoid(a)) * b).astype(BF16), wd_s[...], preferred_element_type=F32)
        for c in range(ROW_TILES):
            ys_ref[pl.ds(c, MOE_TM, stride=ROW_TILES), :] = y[:, c * LANES:(c + 1) * LANES]

    @pl.when(tv_ref[i] == 0)
    def _():
        ys_ref[...] = jnp.zeros_like(ys_ref)


def _experts(tile_expert, tile_valid, xs, wg, wu, wd, layer):
    n_rows = xs.shape[0] // ROW_TILES
    rows = pl.BlockSpec((MOE_TM * ROW_TILES, LANES), lambda i, te, tv: (i, 0))
    return pl.pallas_call(
        _experts_kernel,
        grid_spec=pltpu.PrefetchScalarGridSpec(
            num_scalar_prefetch=2, grid=(n_rows // MOE_TM,),
            in_specs=[rows,
                      pl.BlockSpec((1, 1, D_MODEL, EXPERT_HIDDEN), lambda i, te, tv: (layer, te[i], 0, 0)),
                      pl.BlockSpec((1, 1, D_MODEL, EXPERT_HIDDEN), lambda i, te, tv: (layer, te[i], 0, 0)),
                      pl.BlockSpec((1, 1, EXPERT_HIDDEN, D_MODEL), lambda i, te, tv: (layer, te[i], 0, 0))],
            out_specs=rows,
            scratch_shapes=[pltpu.VMEM((D_MODEL, EXPERT_HIDDEN), BF16), pltpu.VMEM((D_MODEL, EXPERT_HIDDEN), BF16),
                            pltpu.VMEM((EXPERT_HIDDEN, D_MODEL), BF16)]),
        out_shape=jax.ShapeDtypeStruct(xs.shape, F32),
        compiler_params=_cparams(("arbitrary",)),
        name="moe_experts",
    )(tile_expert, tile_valid, xs, wg, wu, wd)


def _combine_kernel(pos_ref, ys_ref, comb_ref, sel_ref, xn_ref, mod_ref, o_ref, y_s, sem):
    base = pl.program_id(0) * MOE_ROWS

    def copy(j, k):
        dst = pl.multiple_of((k * MOE_ROWS + j) * ROW_TILES, ROW_TILES)
        return pltpu.make_async_copy(ys_ref.at[pos_ref[k, base + j]], y_s.at[pl.ds(dst, ROW_TILES), :], sem)

    def start(j, c):
        copy(j, 0).start()
        copy(j, 1).start()
        return c

    def wait(j, c):
        copy(j, 0).wait()
        copy(j, 1).wait()
        return c

    lax.fori_loop(0, MOE_ROWS, start, 0)
    comb, sel = comb_ref[...], sel_ref[...]
    w1 = jnp.sum(jnp.where(sel == 1.0, comb, 0.0), axis=-1, keepdims=True)
    w2 = jnp.sum(jnp.where(sel == 2.0, comb, 0.0), axis=-1, keepdims=True)
    lax.fori_loop(0, MOE_ROWS, wait, 0)
    for c in range(ROW_TILES):
        cs = slice(c * LANES, (c + 1) * LANES)
        y1 = y_s[pl.ds(c, MOE_ROWS, stride=ROW_TILES), :]
        y2 = y_s[pl.ds(MOE_ROWS * ROW_TILES + c, MOE_ROWS, stride=ROW_TILES), :]
        o_ref[:, cs] = xn_ref[:, cs] + mod_ref[0, 5:6, cs] * (w1 * y1 + w2 * y2)


def _combine(pos, ys, comb, sel, xn, mod, T):
    n_tok = xn.shape[0]
    tm = MOE_ROWS
    bm = mod.shape[0]
    mod_idx = (lambda i, p: (i * tm // T, 0, 0)) if bm > 1 else (lambda i, p: (0, 0, 0))
    row = lambda w: pl.BlockSpec((tm, w), lambda i, p: (i, 0))
    return pl.pallas_call(
        _combine_kernel,
        grid_spec=pltpu.PrefetchScalarGridSpec(
            num_scalar_prefetch=1, grid=(n_tok // tm,),
            in_specs=[pl.BlockSpec(memory_space=pl.ANY), row(LANES), row(LANES), row(D_MODEL),
                      pl.BlockSpec((1, 8, D_MODEL), mod_idx)],
            out_specs=row(D_MODEL),
            scratch_shapes=[pltpu.VMEM((2 * tm * ROW_TILES, LANES), F32), pltpu.SemaphoreType.DMA(())]),
        out_shape=jax.ShapeDtypeStruct((n_tok, D_MODEL), F32),
        compiler_params=_cparams(("arbitrary",)),
        name="moe_combine",
    )(pos, ys, comb, sel, xn, mod)


def _moe(groups, wg, wu, wd, layer):
    sizes = [g[3].shape[0] for g in groups]
    n_rows = _moe_rows(sum(sizes))
    pos, tile_expert, tile_valid = _moe_plan(jnp.concatenate([g[2] for g in groups], axis=0))
    xs = jnp.zeros((n_rows, ROW_TILES, LANES), F32)
    start = 0
    for (h3, _, _, _, _, _), n in zip(groups, sizes):
        xs = _dispatch(pos[:, start:start + n], h3.reshape(n, ROW_TILES, LANES), xs)
        start += n
    ys = _experts(tile_expert, tile_valid, xs.reshape(n_rows * ROW_TILES, LANES), wg, wu, wd, layer)
    ys = ys.reshape(n_rows, ROW_TILES, LANES)
    outs, start = [], 0
    for (_, comb, sel, xn, mod, T), n in zip(groups, sizes):
        outs.append(_combine(pos[:, start:start + n], ys, comb, sel, xn, mod, T))
        start += n
    return outs


def kernel(x_prompt, x_sample, cache_mla, state_mlstm_C, state_mlstm_n, state_mlstm_m, state_rwkv, state_gla, c, c_ctx, w_ada, b_ada, g_mix, g_ffn, w_in, w_out, b_ml_gates, g_ml_norm, g_mla_qlat, g_mla_kvlat, w_mla_uq, w_mla_ukv, g_mla_qn, g_mla_kn, rw_w0, rw_w_up, rw_a0, rw_a_up, rw_g_up, rw_k_k, rw_k_a, rw_r_k, rw_ln, gla_g_up, gla_g_b, gla_norm, moe_w_rg, moe_b_rg, moe_w_re, moe_b_re, moe_w_gate, moe_w_up, moe_w_down):
    cc = jnp.concatenate([c_ctx[None], c, jnp.zeros((3, D_MODEL), F32)], axis=0)
    mod = _modulation(cc, w_ada, b_ada).reshape(DEPTH, 8, 6, D_MODEL)
    mod = jnp.pad(mod, ((0, 0), (0, 0), (0, 2), (0, 0)))

    layers = []
    for l in range(DEPTH):
        p = {'b_ml_gates': b_ml_gates[l], 'g_ml_norm': g_ml_norm[l], 'g_mla_qlat': g_mla_qlat[l],
             'g_mla_kvlat': g_mla_kvlat[l], 'w_mla_uq': w_mla_uq[l], 'w_mla_ukv': w_mla_ukv[l],
             'g_mla_qn': g_mla_qn[l], 'g_mla_kn': g_mla_kn[l], 'rw_w0': rw_w0[l], 'rw_w_up': rw_w_up[l],
             'rw_a0': rw_a0[l], 'rw_a_up': rw_a_up[l], 'rw_g_up': rw_g_up[l], 'rw_k_k': rw_k_k[l],
             'rw_k_a': rw_k_a[l], 'rw_r_k': rw_r_k[l], 'rw_ln': rw_ln[l], 'gla_g_up': gla_g_up[l],
             'gla_g_b': gla_g_b[l], 'gla_norm': gla_norm[l]}
        router_w = jnp.zeros((D_MODEL, LANES), F32)
        router_w = router_w.at[:, :N_EXPERTS].set(moe_w_re[l]).at[:, N_EXPERTS:N_EXPERTS + N_GROUPS].set(moe_w_rg[l])
        router_b = jnp.zeros((1, LANES), F32)
        router_b = router_b.at[0, :N_EXPERTS].set(moe_b_re[l]).at[0, N_EXPERTS:N_EXPERTS + N_GROUPS].set(moe_b_rg[l])
        layers.append(dict(
            w_in=_pack_w_in(w_in[l]), g_mix=g_mix[l].reshape(1, -1), g_ffn=g_ffn[l].reshape(1, -1),
            w_out=w_out[l].astype(BF16), ml=_mlstm_params(p), mla=_mla_params(p), rw=_rwkv_params(p),
            gla=_gla_params(p), router_w=_split_weight(router_w), router_b=router_b,
            layer=l))

    def mix(x2, B, T, mod_g, lp, ctx):
        u3 = _in_proj(x2, mod_g, lp['g_mix'], lp['w_in'], T).reshape(B, T, PK_COLS)
        if ctx is None:
            ctx_kv = None
            ml_c0, ml_m0 = _mlstm_state_zero(B)
            rw_h0 = jnp.zeros((B, 2, RW_H, RW_N, RW_N), F32)
            gla_s0 = jnp.zeros((B, 2, GLA_H, GLA_DK, GLA_DV), F32)
        else:
            ctx_kv, ml_C0, ml_n0, ml_m0_, rw_S0, gla_s0 = ctx
            ml_c0, ml_m0 = _mlstm_state_in(ml_C0, ml_n0, ml_m0_)
            rw_h0 = jnp.swapaxes(rw_S0, -1, -2)
        ml_out, ml_c, ml_m = _mlstm(u3, *lp['ml'], ml_c0, ml_m0)
        mla_out, own_kv = _mla(u3, ctx_kv, lp['mla'])
        rw_out, rw_h = _rwkv(u3, lp['rw'], rw_h0)
        gla_out, gla_s = _gla(u3, lp['gla'], gla_s0)
        mixed = [t.reshape(B * T, GROUP_W) for t in (ml_out, mla_out, rw_out, gla_out)]
        xn, h3, comb, sel = _out_proj(x2, mixed, lp['w_out'], mod_g, lp['g_ffn'], lp['router_w'], lp['router_b'], T)
        ml_C, ml_n, ml_mm = _mlstm_state_out(ml_c, ml_m)
        return (h3, comb, sel, xn, mod_g, T), (own_kv, ml_C, ml_n, ml_mm, jnp.swapaxes(rw_h, -1, -2), gla_s)

    Bp, Tp = x_prompt.shape[:2]
    Bs, Ts = x_sample.shape[:2]
    xp = x_prompt.reshape(Bp * Tp, D_MODEL)
    xs = x_sample.reshape(Bs * Ts, D_MODEL)
    ctx_states = []
    for l in range(DEPTH):
        ctx = (cache_mla[:, l], state_mlstm_C[:, l], state_mlstm_n[:, l], state_mlstm_m[:, l],
               state_rwkv[:, l], state_gla[:, l])
        moe_p, st = mix(xp, Bp, Tp, mod[l, 0:1], layers[l], None)
        moe_s, _ = mix(xs, Bs, Ts, mod[l, 1:1 + Bs], layers[l], ctx)
        ctx_states.append(st)
        xp, xs = _moe([moe_p, moe_s], moe_w_gate, moe_w_up, moe_w_down, l)
    outs = [jnp.stack([s[i] for s in ctx_states], axis=1) for i in range(6)]
    return (xp.reshape(x_prompt.shape), xs.reshape(x_sample.shape), *outs)
```

```python
import functools
import math

import numpy as np
import jax
import jax.numpy as jnp
from jax import lax
from jax.experimental import pallas as pl
from jax.experimental.pallas import tpu as pltpu

F32 = jnp.float32
BF16 = jnp.bfloat16

D_MODEL = 2048
DEPTH = 2
GRID_W = 64
GROUP_W = 512
ML_H, ML_DK = 4, 128
MLA_H, MLA_NOPE, MLA_ROPE, MLA_V = 4, 128, 64, 128
MLA_QK = MLA_NOPE + MLA_ROPE
Q_LORA, KV_LORA = 384, 256
ROPE_THETA = 10000.0
RW_H, RW_N = 8, 64
RW_DECAY_SCALE = math.exp(-0.5)
RW_LN_EPS = 64e-5
GLA_H, GLA_DK, GLA_DV = 4, 64, 128
GLA_GATE_RANK = 16
GLA_NORMALIZER = 16.0
N_GROUPS, EXPERTS_PER_GROUP, N_EXPERTS = 4, 4, 16
EXPERT_HIDDEN = 512
NORM_EPS = 1e-6
LANES = 128
ROW_TILES = D_MODEL // LANES
VMEM_LIMIT = 56 * 1024 * 1024

_REF_SPLITS = (
    ('ml_q', 512), ('ml_k', 512), ('ml_v', 512), ('ml_o', 512), ('ml_g', 16),
    ('mla_ql', Q_LORA), ('mla_ckv', KV_LORA), ('mla_kr', MLA_ROPE),
    ('rw_r', 512), ('rw_k', 512), ('rw_v', 512), ('rw_wd', 64), ('rw_ad', 64), ('rw_gd', 128),
    ('gla_q', 256), ('gla_k', 256), ('gla_v', 512), ('gla_gd', GLA_GATE_RANK), ('gla_g', 512),
)
_REF_OFF = {}
_o = 0
for _n, _w in _REF_SPLITS:
    _REF_OFF[_n] = (_o, _w)
    _o += _w
IN_COLS = _o

_PACKED = (
    ('ml_q', 512), ('ml_k', 512), ('ml_v', 512), ('ml_o', 512),
    ('rw_r', 512), ('rw_k', 512), ('rw_v', 512), ('gla_v', 512), ('gla_g', 512),
    ('mla_ql', 384), ('mla_ckv', 256), ('mla_kr', 64), ('mla_kr_sw', 64),
    ('gla_q', 256), ('gla_k', 256),
    ('ml_g', 128), ('rw_wd', 64), ('rw_ad', 64), ('rw_gd', 128), ('gla_gd', 128),
)
PK_OFF = {}
_o = 0
for _n, _w in _PACKED:
    PK_OFF[_n] = _o
    _o += _w
PK_COLS = _o


def _rope_swap_perm():
    idx = np.arange(MLA_ROPE)
    axis, half, f = idx // 32, (idx % 32) // 16, idx % 16
    return axis * 32 + (1 - half) * 16 + f


def _packed_column_index():
    src = np.full((PK_COLS,), -1, np.int64)
    for name, width in _PACKED:
        off = PK_OFF[name]
        if name == 'mla_kr_sw':
            s, w = _REF_OFF['mla_kr']
            src[off:off + w] = s + _rope_swap_perm()
        else:
            s, w = _REF_OFF[name]
            src[off:off + w] = s + np.arange(w)
    return src


_PK_SRC = _packed_column_index()


def _column_runs(src):
    runs, i = [], 0
    while i < len(src):
        j = i + 1
        while j < len(src) and ((src[i] < 0 and src[j] < 0) or (src[i] >= 0 and src[j] == src[i] + (j - i))):
            j += 1
        runs.append((int(src[i]), j - i))
        i = j
    return runs


_PK_RUNS = _column_runs(_PK_SRC)


def _pack_w_in(w):
    parts = [w[:, s:s + n] if s >= 0 else jnp.zeros((w.shape[0], n), w.dtype) for s, n in _PK_RUNS]
    return jnp.concatenate(parts, axis=1).astype(BF16)


def _cparams(sem):
    return pltpu.CompilerParams(dimension_semantics=sem, vmem_limit_bytes=VMEM_LIMIT)


def _log_sigmoid(x):
    return jnp.minimum(x, 0.0) - jnp.log(1.0 + jnp.exp(-jnp.abs(x)))


def _sigmoid(x):
    return 1.0 / (1.0 + jnp.exp(-x))


def _dot(a, b):
    return jnp.dot(a.astype(BF16), b.astype(BF16), preferred_element_type=F32)


def _dot_nt(a, b):
    return lax.dot_general(a.astype(BF16), b.astype(BF16), (((1,), (1,)), ((), ())), preferred_element_type=F32)


def _dot_tn(a, b):
    return lax.dot_general(a.astype(BF16), b.astype(BF16), (((0,), (0,)), ((), ())), preferred_element_type=F32)


def _split2(x):
    hi = x.astype(BF16)
    return hi, (x - hi.astype(F32)).astype(BF16)


def _split_weight(w):
    hi, lo = _split2(w)
    return jnp.stack([hi, lo])


def _running_sum(tri, x):
    hi = x.astype(BF16)
    r = x - hi.astype(F32)
    mid = r.astype(BF16)
    lo = (r - mid.astype(F32)).astype(BF16)
    n = x.shape[1]
    s = jnp.dot(tri, jnp.concatenate([hi, mid, lo], axis=1), preferred_element_type=F32)
    return s[:, :n] + s[:, n:2 * n] + s[:, 2 * n:]


def _dot_split(a, w_ref):
    a_hi, a_lo = _split2(a)
    w_hi = w_ref[0]
    return (jnp.dot(a_hi, w_hi, preferred_element_type=F32) + jnp.dot(a_lo, w_hi, preferred_element_type=F32)
            + jnp.dot(a_hi, w_ref[1], preferred_element_type=F32))


def _mod_kernel(c_ref, w_ref, b_ref, o_ref):
    c = c_ref[...]
    s = c * _sigmoid(c)
    o_ref[0] = _dot(s, w_ref[0]) + b_ref[0]


def _modulation(cc, w_ada, b_ada):
    tn = 1536
    n = 6 * D_MODEL
    return pl.pallas_call(
        _mod_kernel,
        grid=(DEPTH, n // tn),
        in_specs=[pl.BlockSpec((8, D_MODEL), lambda l, j: (0, 0)),
                  pl.BlockSpec((1, D_MODEL, tn), lambda l, j: (l, 0, j)),
                  pl.BlockSpec((1, 1, tn), lambda l, j: (l, 0, j))],
        out_specs=pl.BlockSpec((1, 8, tn), lambda l, j: (l, 0, j)),
        out_shape=jax.ShapeDtypeStruct((DEPTH, 8, n), F32),
        compiler_params=_cparams(("arbitrary", "arbitrary")),
        name="adaln_mod",
    )(cc, w_ada, b_ada.reshape(DEPTH, 1, n))


def _inproj_kernel(x_ref, mod_ref, g_ref, w_ref, o_ref, h_scr):
    @pl.when(pl.program_id(1) == 0)
    def _():
        x = x_ref[...]
        xn = x * lax.rsqrt(jnp.mean(x * x, axis=-1, keepdims=True) + NORM_EPS) * g_ref[...]
        h_scr[...] = (xn * (1.0 + mod_ref[0, 1:2, :]) + mod_ref[0, 0:1, :]).astype(BF16)

    o_ref[...] = jnp.dot(h_scr[...], w_ref[...], preferred_element_type=F32)


def _in_proj(x2, mod, g, w_packed, T):
    n_tok = x2.shape[0]
    tm = 1024
    tn = 1280
    bm = mod.shape[0]
    mod_idx = (lambda i, j: (i * tm // T, 0, 0)) if bm > 1 else (lambda i, j: (0, 0, 0))
    return pl.pallas_call(
        _inproj_kernel,
        grid=(n_tok // tm, PK_COLS // tn),
        in_specs=[pl.BlockSpec((tm, D_MODEL), lambda i, j: (i, 0)),
                  pl.BlockSpec((1, 8, D_MODEL), mod_idx),
                  pl.BlockSpec((1, D_MODEL), lambda i, j: (0, 0)),
                  pl.BlockSpec((D_MODEL, tn), lambda i, j: (0, j))],
        out_specs=pl.BlockSpec((tm, tn), lambda i, j: (i, j)),
        out_shape=jax.ShapeDtypeStruct((n_tok, PK_COLS), F32),
        scratch_shapes=[pltpu.VMEM((tm, D_MODEL), BF16)],
        compiler_params=_cparams(("arbitrary", "arbitrary")),
        name="in_proj",
    )(x2, mod, g, w_packed)


ML_CHUNK = 256


def _mlstm_kernel(q_ref, k_ref, v_ref, o_ref, g_ref, bias_ref, gn_ref, c0_ref, m0_ref,
                  out_ref, c_ref, m_ref, hs_ref, *, T):
    L = ML_CHUNK
    nc = T // L
    c_ref[...] = c0_ref[...]
    m_ref[...] = m0_ref[...]
    ii = lax.broadcasted_iota(jnp.int32, (L, L), 0)
    jj = lax.broadcasted_iota(jnp.int32, (L, L), 1)
    lane = lax.broadcasted_iota(jnp.int32, (1, LANES), 1)
    is_f = jnp.logical_and(lane % 8 >= 4, lane < 16)
    ones_col = (lax.broadcasted_iota(jnp.int32, (L, LANES), 1) == 0).astype(BF16)
    scale = ML_DK ** -0.5

    def chunk(ci, carry):
        ch = []
        for d in range(2):
            mask = (jj <= ii) if d == 0 else (jj >= ii)
            c = ci if d == 0 else nc - 1 - ci
            s = pl.multiple_of(c * L, L)
            gates = g_ref[0, pl.ds(s, L), :] + bias_ref[...]
            gf = jnp.where(is_f, _log_sigmoid(gates), gates)
            cum = _running_sum(mask.astype(BF16), gf)
            gf_t = gf.T
            cum_t = cum.T
            for h in range(ML_H):
                ci_, cf_ = d * 8 + h, d * 8 + 4 + h
                hs = slice(h * ML_DK, (h + 1) * ML_DK)
                ig_col, ig_row = gf[:, ci_:ci_ + 1], gf_t[ci_:ci_ + 1, :]
                b_col, b_row = cum[:, cf_:cf_ + 1], cum_t[cf_:cf_ + 1, :]
                b_last = b_col[L - 1:L, :] if d == 0 else b_col[0:1, :]
                m_prev = m_ref[0, d, h][:, 0:1]
                dmat = jnp.where(mask, b_col + (ig_row - b_row), -jnp.inf)
                m_inter = b_col + m_prev
                m_row = jnp.maximum(m_inter, jnp.max(dmat, axis=-1, keepdims=True))
                dk_col = b_last - b_col + ig_col
                m_new = jnp.maximum(b_last + m_prev, jnp.max(dk_col, axis=0, keepdims=True))
                ch.append(dict(
                    d=d, h=h, s=s, hs=hs, dmat=dmat, m_row=m_row, w_inter=jnp.exp(m_inter - m_row), m_new=m_new,
                    w_key=jnp.exp(dk_col - m_new), c_scale=jnp.exp(b_last + m_prev - m_new),
                    q=(q_ref[0, pl.ds(s, L), hs] * scale).astype(BF16), k=k_ref[0, pl.ds(s, L), hs],
                    v_aug=jnp.concatenate([v_ref[0, pl.ds(s, L), hs].astype(BF16), ones_col], axis=1),
                    c_aug=c_ref[0, d, h]))
        qk = [_dot_nt(c['q'], c['k']) for c in ch]
        qc = [_dot(c['q'], c['c_aug']) for c in ch]
        sv = [_dot(s_ * jnp.exp(c['dmat'] - c['m_row']), c['v_aug']) for s_, c in zip(qk, ch)]
        kv = [_dot_tn(c['k'] * c['w_key'], c['v_aug']) for c in ch]
        for c, qc_, sv_, kv_ in zip(ch, qc, sv, kv):
            nd = c['w_inter'] * qc_ + sv_
            num, den = nd[:, :ML_DK], nd[:, ML_DK:ML_DK + 1]
            hh = num / jnp.maximum(jnp.abs(den), jnp.exp(-c['m_row']))
            hs_ref[pl.ds(c['s'], L), c['d'] * 512 + c['h'] * ML_DK:c['d'] * 512 + (c['h'] + 1) * ML_DK] = hh
            c_ref[0, c['d'], c['h']] = c['c_scale'] * c['c_aug'] + kv_
            m_ref[0, c['d'], c['h']] = jnp.broadcast_to(c['m_new'], (1, LANES))
        return carry

    lax.fori_loop(0, nc, chunk, 0)

    def finish(ci, carry):
        s = pl.multiple_of(ci * L, L)
        for h in range(ML_H):
            hs = slice(h * ML_DK, (h + 1) * ML_DK)
            x = hs_ref[pl.ds(s, L), hs] + hs_ref[pl.ds(s, L), 512 + h * ML_DK:512 + (h + 1) * ML_DK]
            xc = x - jnp.mean(x, axis=-1, keepdims=True)
            y = xc * lax.rsqrt(jnp.mean(xc * xc, axis=-1, keepdims=True) + NORM_EPS) * gn_ref[:, hs]
            out_ref[0, pl.ds(s, L), hs] = (y * _sigmoid(o_ref[0, pl.ds(s, L), hs])).astype(BF16)
        return carry

    lax.fori_loop(0, nc, finish, 0)


def _mlstm_params(p):
    bias = jnp.zeros((1, LANES), F32).at[0, :16].set(p['b_ml_gates'])
    return bias, p['g_ml_norm'].reshape(1, 512)


def _mlstm_state_in(C0, n0, m0):
    c0 = jnp.concatenate([C0, n0[..., None], jnp.zeros(C0.shape[:-1] + (ML_DK - 1,), F32)], axis=-1)
    return c0, jnp.broadcast_to(m0[..., None, None], m0.shape + (1, LANES))


def _mlstm_state_zero(B):
    return jnp.zeros((B, 2, ML_H, ML_DK, 2 * ML_DK), F32), jnp.zeros((B, 2, ML_H, 1, LANES), F32)


def _mlstm_state_out(c, m):
    return c[..., :ML_DK], c[..., ML_DK], m[..., 0, 0]


def _mlstm(u3, bias, gnorm, c0, m0):
    B, T, _ = u3.shape
    blk = lambda name: pl.BlockSpec((1, T, 512), lambda b, o=PK_OFF[name] // 512: (b, 0, o))
    st_c = pl.BlockSpec((1, 2, ML_H, ML_DK, 2 * ML_DK), lambda b: (b, 0, 0, 0, 0))
    st_m = pl.BlockSpec((1, 2, ML_H, 1, LANES), lambda b: (b, 0, 0, 0, 0))
    return pl.pallas_call(
        functools.partial(_mlstm_kernel, T=T),
        grid=(B,),
        in_specs=[blk('ml_q'), blk('ml_k'), blk('ml_v'), blk('ml_o'),
                  pl.BlockSpec((1, T, LANES), lambda b: (b, 0, PK_OFF['ml_g'] // LANES)),
                  pl.BlockSpec((1, LANES), lambda b: (0, 0)),
                  pl.BlockSpec((1, 512), lambda b: (0, 0)),
                  st_c, st_m],
        out_specs=[pl.BlockSpec((1, T, 512), lambda b: (b, 0, 0)), st_c, st_m],
        out_shape=[jax.ShapeDtypeStruct((B, T, 512), BF16),
                   jax.ShapeDtypeStruct(c0.shape, F32),
                   jax.ShapeDtypeStruct(m0.shape, F32)],
        scratch_shapes=[pltpu.VMEM((T, 1024), F32)],
        compiler_params=_cparams(("arbitrary",)),
        name="mlstm",
    )(u3, u3, u3, u3, u3, bias, gnorm, c0, m0)


MLA_BLK = 256


def _rope_tables(T):
    rows = T // GRID_W
    row = np.repeat(np.arange(rows, dtype=np.float64), GRID_W)
    col = np.tile(np.arange(GRID_W, dtype=np.float64), rows)
    inv = ROPE_THETA ** (-np.arange(MLA_ROPE // 4, dtype=np.float64) / (MLA_ROPE // 4))
    ang = np.stack([row[:, None] * inv, col[:, None] * inv], axis=1)
    cos = np.stack([np.cos(ang), np.cos(ang)], axis=2).reshape(T, MLA_ROPE)
    sin = np.stack([-np.sin(ang), np.sin(ang)], axis=2).reshape(T, MLA_ROPE)
    return jnp.asarray(np.concatenate([cos, sin], axis=1), F32)


def _mla_kernel(*refs, T, n_ctx, rope):
    if n_ctx:
        (u_ref, ctx_ref, cs_ref, gql_ref, gkv_ref, gains_ref, wqn_ref, wqr_ref, wqs_ref, wkn_ref, wv_ref,
         out_ref, kv_ref, qn_s, qr_s, kn_s, kr_s, v_s) = refs
    else:
        (u_ref, cs_ref, gql_ref, gkv_ref, gains_ref, wqn_ref, wqr_ref, wqs_ref, wkn_ref, wv_ref,
         out_ref, kv_ref, qn_s, qr_s, kn_s, kr_s, v_s) = refs
    Lb = MLA_BLK
    gq_n, gq_r, gq_s = gains_ref[0:1, :], gains_ref[1:2, 0:64], gains_ref[1:2, 64:128]
    gk_n, gk_r, gk_s = gains_ref[2:3, :], gains_ref[3:4, 0:64], gains_ref[3:4, 64:128]
    sm_scale = MLA_QK ** -0.5

    def store_keys(s, kn, kr, krs, cos, sin):
        kr_ss = jnp.sum(kr * kr, axis=-1, keepdims=True)
        for h in range(MLA_H):
            kn_h = kn[:, h * 128:(h + 1) * 128]
            rk = lax.rsqrt((jnp.sum(kn_h * kn_h, axis=-1, keepdims=True) + kr_ss) / MLA_QK + NORM_EPS)
            kn_s[pl.ds(s, Lb), h * 128:(h + 1) * 128] = (kn_h * rk * gk_n).astype(BF16)
            kr_h = kr * gk_r
            if cos is not None:
                kr_h = kr_h * cos + (krs * gk_s) * sin
            kr_s[pl.ds(s, Lb), h * 64:(h + 1) * 64] = (kr_h * rk).astype(BF16)

    def prep(ci, carry):
        s = pl.multiple_of(ci * Lb, Lb)
        u = u_ref[0, pl.ds(s, Lb), :]
        ql, ckv, kr, krs = u[:, :384], u[:, 384:640], u[:, 640:704], u[:, 704:768]
        qln = ql * lax.rsqrt(jnp.mean(ql * ql, axis=-1, keepdims=True) + NORM_EPS) * gql_ref[...]
        ckvn = ckv * lax.rsqrt(jnp.mean(ckv * ckv, axis=-1, keepdims=True) + NORM_EPS) * gkv_ref[...]
        kv_ref[0, pl.ds(s, Lb), :] = jnp.concatenate([ckvn, kr], axis=1)
        cos = cs_ref[pl.ds(s, Lb), 0:64] if rope else None
        sin = cs_ref[pl.ds(s, Lb), 64:128] if rope else None
        qn, qr, qs = _dot(qln, wqn_ref[...]), _dot(qln, wqr_ref[...]), _dot(qln, wqs_ref[...])
        for h in range(MLA_H):
            qn_h, qr_h = qn[:, h * 128:(h + 1) * 128], qr[:, h * 64:(h + 1) * 64]
            ss = jnp.sum(qn_h * qn_h, axis=-1, keepdims=True) + jnp.sum(qr_h * qr_h, axis=-1, keepdims=True)
            rq = lax.rsqrt(ss / MLA_QK + NORM_EPS) * sm_scale
            qn_s[pl.ds(s, Lb), h * 128:(h + 1) * 128] = (qn_h * rq * gq_n).astype(BF16)
            qr_h = qr_h * gq_r
            if rope:
                qr_h = qr_h * cos + (qs[:, h * 64:(h + 1) * 64] * gq_s) * sin
            qr_s[pl.ds(s, Lb), h * 64:(h + 1) * 64] = (qr_h * rq).astype(BF16)
        v_s[pl.ds(s, Lb), :] = _dot(ckvn, wv_ref[...]).astype(BF16)
        store_keys(s, _dot(ckvn, wkn_ref[...]), kr, krs, cos, sin)
        return carry

    lax.fori_loop(0, T // Lb, prep, 0)

    for ci in range(n_ctx // Lb):
        cx = ctx_ref[0, ci * Lb:(ci + 1) * Lb, :]
        ckv_c, kr_c = cx[:, :KV_LORA], cx[:, KV_LORA:KV_LORA + MLA_ROPE]
        v_s[T + ci * Lb:T + (ci + 1) * Lb, :] = _dot(ckv_c, wv_ref[...]).astype(BF16)
        store_keys(T + ci * Lb, _dot(ckv_c, wkn_ref[...]), kr_c, None, None, None)

    for h in range(MLA_H):
        def attend(qi, carry, h=h):
            s = pl.multiple_of(qi * Lb, Lb)
            sc = (_dot_nt(qn_s[pl.ds(s, Lb), h * 128:(h + 1) * 128], kn_s[:, h * 128:(h + 1) * 128])
                  + _dot_nt(qr_s[pl.ds(s, Lb), h * 64:(h + 1) * 64], kr_s[:, h * 64:(h + 1) * 64]))
            p = jnp.exp(sc - jnp.max(sc, axis=-1, keepdims=True))
            o = _dot(p, v_s[:, h * 128:(h + 1) * 128]) / jnp.sum(p, axis=-1, keepdims=True)
            out_ref[0, pl.ds(s, Lb), h * 128:(h + 1) * 128] = o.astype(BF16)
            return carry

        lax.fori_loop(0, T // Lb, attend, 0)


def _mla_params(p):
    wq = p['w_mla_uq'].reshape(Q_LORA, MLA_H, MLA_QK)
    sw = _rope_swap_perm()
    wq_n = wq[:, :, :MLA_NOPE].reshape(Q_LORA, 512).astype(BF16)
    wq_r = wq[:, :, MLA_NOPE:].reshape(Q_LORA, 256).astype(BF16)
    wq_s = wq[:, :, MLA_NOPE:][:, :, sw].reshape(Q_LORA, 256).astype(BF16)
    wkv = p['w_mla_ukv'].reshape(KV_LORA, MLA_H, MLA_NOPE + MLA_V)
    wk_n = wkv[:, :, :MLA_NOPE].reshape(KV_LORA, 512).astype(BF16)
    wv = wkv[:, :, MLA_NOPE:].reshape(KV_LORA, 512).astype(BF16)
    gq, gk = p['g_mla_qn'], p['g_mla_kn']
    gains = jnp.zeros((8, LANES), F32)
    gains = gains.at[0].set(gq[:128]).at[1, :64].set(gq[128:]).at[1, 64:].set(gq[128:][sw])
    gains = gains.at[2].set(gk[:128]).at[3, :64].set(gk[128:]).at[3, 64:].set(gk[128:][sw])
    return (p['g_mla_qlat'].reshape(1, -1), p['g_mla_kvlat'].reshape(1, -1), gains, wq_n, wq_r, wq_s, wk_n, wv)


def _mla(u3, ctx_kv, mp):
    B, T, _ = u3.shape
    n_ctx = 0 if ctx_kv is None else ctx_kv.shape[1]
    rope = ctx_kv is not None
    tk = T + n_ctx
    full = lambda a: pl.BlockSpec(a.shape, lambda b, n=a.ndim: (0,) * n)
    cs = _rope_tables(T) if rope else jnp.zeros((T, LANES), F32)
    ins = [u3] + ([ctx_kv] if rope else []) + [cs] + list(mp)
    specs = [pl.BlockSpec((1, T, 768), lambda b: (b, 0, PK_OFF['mla_ql'] // 768))]
    if rope:
        specs.append(pl.BlockSpec((1, n_ctx, KV_LORA + MLA_ROPE), lambda b: (b, 0, 0)))
    specs += [full(a) for a in ins[len(specs):]]
    return pl.pallas_call(
        functools.partial(_mla_kernel, T=T, n_ctx=n_ctx, rope=rope),
        grid=(B,),
        in_specs=specs,
        out_specs=[pl.BlockSpec((1, T, 512), lambda b: (b, 0, 0)),
                   pl.BlockSpec((1, T, KV_LORA + MLA_ROPE), lambda b: (b, 0, 0))],
        out_shape=[jax.ShapeDtypeStruct((B, T, 512), BF16),
                   jax.ShapeDtypeStruct((B, T, KV_LORA + MLA_ROPE), F32)],
        scratch_shapes=[pltpu.VMEM((T, 512), BF16), pltpu.VMEM((T, 256), BF16),
                        pltpu.VMEM((tk, 512), BF16), pltpu.VMEM((tk, 256), BF16), pltpu.VMEM((tk, 512), BF16)],
        compiler_params=_cparams(("arbitrary",)),
        name="mla",
    )(*ins)


GLA_CHUNK = 64
GLA_LEAF = 4
GLA_UNROLL = 2


def _gla_kernel(q_ref, k_ref, v_ref, gd_ref, g_ref, gup_ref, gb_ref, gn_ref, hsel_ref, s0_ref,
                out_ref, s_ref, os_ref, la_ref, *, T):
    L, C = GLA_CHUNK, GLA_LEAF
    nc = T // L
    Lf = 256
    s_ref[...] = s0_ref[...]

    def gates(ci, carry):
        s = pl.multiple_of(ci * Lf, Lf)
        x = _dot_split(gd_ref[0, pl.ds(s, Lf), :], gup_ref) + gb_ref[...]
        la_ref[pl.ds(s, Lf), :] = _log_sigmoid(x) / GLA_NORMALIZER
        return carry

    lax.fori_loop(0, T // Lf, gates, 0)
    ii = lax.broadcasted_iota(jnp.int32, (L, L), 0)
    jj = lax.broadcasted_iota(jnp.int32, (L, L), 1)
    eye = (lax.broadcasted_iota(jnp.int32, (GLA_DK, GLA_DK), 0)
           == lax.broadcasted_iota(jnp.int32, (GLA_DK, GLA_DK), 1)).astype(F32)
    row_id = lax.broadcasted_iota(jnp.int32, (L, 1), 0)
    scale = GLA_DK ** -0.5
    hsel = hsel_ref[...]
    levels = []
    span = C
    while span < L:
        levels.append(span)
        span *= 2

    U = GLA_UNROLL

    def chunk(ci, carry):
        parts = []
        for d in range(2):
            causal = (jj <= ii) if d == 0 else (jj >= ii)
            tri = causal.astype(BF16)
            for u in range(U):
                c = ci * U + u if d == 0 else nc - 1 - (ci * U + u)
                s = pl.multiple_of(c * L, L)
                la = la_ref[pl.ds(s, L), d * GLA_H * GLA_DK:(d + 1) * GLA_H * GLA_DK]
                b = _running_sum(tri, la)
                total = b[L - 1:L, :] if d == 0 else b[0:1, :]
                q = q_ref[0, pl.ds(s, L), :] * scale
                k = k_ref[0, pl.ds(s, L), :]
                lv = []
                for sp in levels:
                    b3 = b.reshape(L // (2 * sp), 2 * sp, GLA_H * GLA_DK)
                    edge = b3[:, sp - 1:sp, :] if d == 0 else b3[:, sp:sp + 1, :]
                    bref = jnp.broadcast_to(edge, b3.shape).reshape(L, GLA_H * GLA_DK)
                    later = (row_id % (2 * sp) >= sp) if d == 0 else (row_id % (2 * sp) < sp)
                    e = jnp.exp(jnp.where(later, b - bref, bref - b))
                    lv.append((jnp.where(later, q * e, 0.0), jnp.where(later, 0.0, k * e),
                               (ii // (2 * sp)) == (jj // (2 * sp))))
                terms = []
                for dl in range(C):
                    if dl == 0:
                        terms.append(q * k)
                        continue
                    sh = dl if d == 0 else L - dl
                    ok = (row_id % C >= dl) if d == 0 else (row_id % C < C - dl)
                    kd = pltpu.roll(k, sh, axis=0)
                    bd = pltpu.roll(b, sh, axis=0)
                    terms.append(q * kd * jnp.exp(jnp.where(ok, b - bd, 0.0)))
                tt = jnp.concatenate(terms, axis=0)
                t_hi = tt.astype(BF16)
                t_lo = (tt - t_hi.astype(F32)).astype(BF16)
                diag = (jnp.dot(t_hi, hsel, preferred_element_type=F32)
                        + jnp.dot(t_lo, hsel, preferred_element_type=F32))
                parts.append(dict(d=d, s=s, lv=lv, diag=diag, q_in=q * jnp.exp(b), k_out=k * jnp.exp(total - b),
                                  f_row=jnp.exp(total)))
        chains = [(p, h) for p in parts for h in range(GLA_H)]
        amats = []
        for p, h in chains:
            ks = slice(h * GLA_DK, (h + 1) * GLA_DK)
            a = jnp.zeros((L, L), F32)
            for qs, kt, same in p['lv']:
                a = a + jnp.where(same, _dot_nt(qs[:, ks], kt[:, ks]), 0.0)
            for dl in range(C):
                pair = jnp.logical_and(jj == (ii - dl if p['d'] == 0 else ii + dl), ii // C == jj // C)
                a = a + jnp.where(pair, p['diag'][dl * L:(dl + 1) * L, h:h + 1], 0.0)
            amats.append(a)
        vals = [v_ref[0, pl.ds(p['s'], L), h * GLA_DV:(h + 1) * GLA_DV] for p, h in chains]
        intra = [_dot(a, v) for a, v in zip(amats, vals)]
        upd = [_dot_tn(p['k_out'][:, h * GLA_DK:(h + 1) * GLA_DK], v) for (p, h), v in zip(chains, vals)]
        for d in range(2):
            for h in range(GLA_H):
                ks = slice(h * GLA_DK, (h + 1) * GLA_DK)
                st = s_ref[0, d, h]
                for u in range(U):
                    i = (d * U + u) * GLA_H + h
                    p = parts[d * U + u]
                    o = intra[i] + _dot(p['q_in'][:, ks], st)
                    os_ref[pl.ds(p['s'], L), d * 512 + h * GLA_DV:d * 512 + (h + 1) * GLA_DV] = o
                    f_col = jnp.sum(eye * p['f_row'][:, ks], axis=1, keepdims=True)
                    st = f_col * st + upd[i]
                s_ref[0, d, h] = st
        return carry

    lax.fori_loop(0, nc // U, chunk, 0)

    def finish(ci, carry):
        s = pl.multiple_of(ci * Lf, Lf)
        for h in range(GLA_H):
            vs = slice(h * GLA_DV, (h + 1) * GLA_DV)
            o = os_ref[pl.ds(s, Lf), vs] + os_ref[pl.ds(s, Lf), 512 + h * GLA_DV:512 + (h + 1) * GLA_DV]
            y = o * lax.rsqrt(jnp.mean(o * o, axis=-1, keepdims=True) + NORM_EPS) * gn_ref[:, vs]
            g = g_ref[0, pl.ds(s, Lf), vs]
            out_ref[0, pl.ds(s, Lf), vs] = (y * (g * _sigmoid(g))).astype(BF16)
        return carry

    lax.fori_loop(0, T // Lf, finish, 0)


def _gla_params(p):
    gup = jnp.zeros((LANES, 2 * GLA_H * GLA_DK), F32)
    gup = gup.at[:GLA_GATE_RANK, :].set(jnp.concatenate([p['gla_g_up'][0], p['gla_g_up'][1]], axis=1))
    hsel = jnp.asarray(np.arange(GLA_H * GLA_DK)[:, None] // GLA_DK == np.arange(LANES)[None, :], BF16)
    return _split_weight(gup), p['gla_g_b'].reshape(1, -1), p['gla_norm'].reshape(1, -1), hsel


def _gla(u3, gp, s0):
    B, T, _ = u3.shape
    blk = lambda name, w: pl.BlockSpec((1, T, w), lambda b, o=PK_OFF[name] // w: (b, 0, o))
    full = lambda a: pl.BlockSpec(a.shape, lambda b, n=a.ndim: (0,) * n)
    st = pl.BlockSpec((1, 2, GLA_H, GLA_DK, GLA_DV), lambda b: (b, 0, 0, 0, 0))
    return pl.pallas_call(
        functools.partial(_gla_kernel, T=T),
        grid=(B,),
        in_specs=[blk('gla_q', 256), blk('gla_k', 256), blk('gla_v', 512), blk('gla_gd', 128), blk('gla_g', 512),
                  full(gp[0]), full(gp[1]), full(gp[2]), full(gp[3]), st],
        out_specs=[pl.BlockSpec((1, T, 512), lambda b: (b, 0, 0)), st],
        out_shape=[jax.ShapeDtypeStruct((B, T, 512), BF16), jax.ShapeDtypeStruct(s0.shape, F32)],
        scratch_shapes=[pltpu.VMEM((T, 1024), F32), pltpu.VMEM((T, 2 * GLA_H * GLA_DK), F32)],
        compiler_params=_cparams(("arbitrary",)),
        name="gla",
    )(u3, u3, u3, u3, u3, *gp, s0)


RW_CHUNK = 64
RW_UNROLL = 2


def _seg_sum(x, bd):
    hi = x.astype(BF16)
    lo = (x - hi.astype(F32)).astype(BF16)
    return jnp.dot(hi, bd, preferred_element_type=F32) + jnp.dot(lo, bd, preferred_element_type=F32)


def _rwkv_kernel(r_ref, k_ref, v_ref, wa_ref, gd_ref, wwa_ref, w0a0_ref, gup_ref, kk_ref, ka_ref, rk_ref, ln_ref,
                 bd_ref, h0_ref, out_ref, h_ref, ys_ref, pre_ref, *, T):
    L, N = RW_CHUNK, RW_N
    nc = T // L
    h_ref[...] = h0_ref[...]
    ii = lax.broadcasted_iota(jnp.int32, (L, L), 0)
    jj = lax.broadcasted_iota(jnp.int32, (L, L), 1)
    eye = (ii == jj).astype(F32)
    lane = lax.broadcasted_iota(jnp.int32, (1, LANES), 1)
    bd = bd_ref[...]
    Lf = 256

    def gates(ci, carry):
        s = pl.multiple_of(ci * Lf, Lf)
        wa = wa_ref[0, pl.ds(s, Lf), :]
        pre_ref[pl.ds(s, Lf), :] = _dot_split(jnp.where(lane < 64, jnp.tanh(wa), wa), wwa_ref) + w0a0_ref[...]
        return carry

    lax.fori_loop(0, T // Lf, gates, 0)

    masks = [((jj < ii), (jj <= ii)), ((jj > ii), (jj >= ii))]
    U = RW_UNROLL

    def chunk(ci, carry):
        chains = []
        for d in range(2):
            strict, incl = masks[d]
            tri = incl.astype(BF16)
            for u in range(U):
                c = ci * U + u if d == 0 else nc - 1 - (ci * U + u)
                s = pl.multiple_of(c * L, L)
                r = r_ref[0, pl.ds(s, L), :]
                k = k_ref[0, pl.ds(s, L), :]
                v = v_ref[0, pl.ds(s, L), :]
                pre = pre_ref[pl.ds(s, L), d * 1024:(d + 1) * 1024]
                logw = -RW_DECAY_SCALE * _sigmoid(pre[:, :512])
                a = _sigmoid(pre[:, 512:])
                kkr = k * kk_ref[...]
                kk = kkr * lax.rsqrt(_seg_sum(kkr * kkr, bd) + 1e-12)
                kt = k * (1.0 + (a - 1.0) * ka_ref[...])
                bh = kk * a
                lg = _running_sum(tri, logw)
                lg_end = lg[L - 1:L, :] if d == 0 else lg[0:1, :]
                a_t = -kk * jnp.exp(lg - logw)
                r_t = r * jnp.exp(lg)
                e_inv = jnp.exp(-lg)
                k_t, b_t = kt * e_inv, bh * e_inv
                e_end = jnp.exp(lg_end - lg)
                k_e, b_e = kt * e_end, bh * e_end
                g_end = jnp.exp(lg_end)
                for h in range(RW_H):
                    sl = slice(h * N, (h + 1) * N)
                    chains.append(dict(d=d, u=u, h=h, s=s, strict=strict, incl=incl, a=a_t[:, sl], r=r_t[:, sl],
                                       b=b_t[:, sl], k=k_t[:, sl], ke=k_e[:, sl], be=b_e[:, sl], g=g_end[:, sl],
                                       v=v[:, sl]))
        ms = [_dot_nt(jnp.concatenate([c['a'], c['r']], axis=0), jnp.concatenate([c['b'], c['k']], axis=0))
              for c in chains]
        pws = [jnp.where(c['strict'], m[:L, :L], 0.0) for c, m in zip(chains, ms)]
        xs = [eye + n for n in pws]
        for _ in range(5):
            pws = [_dot(pw, pw) for pw in pws]
            xs = [x + _dot(x, pw) for x, pw in zip(xs, pws)]
        mvs = [_dot(jnp.where(c['strict'], m[:L, L:], 0.0), c['v']) for c, m in zip(chains, ms)]
        tws = [_dot(x, jnp.concatenate([c['a'], mv], axis=1)) for x, c, mv in zip(xs, chains, mvs)]
        qys = [_dot(jnp.where(c['incl'], m[L:, :L], 0.0), tw) for c, m, tw in zip(chains, ms, tws)]
        ylocs = [_dot(jnp.where(c['incl'], m[L:, L:], 0.0), c['v']) + qy[:, N:] for c, m, qy in zip(chains, ms, qys)]
        pgs = [_dot_tn(c['be'], tw) for c, tw in zip(chains, tws)]
        gmats = [_dot_tn(c['ke'], c['v']) + pg[:, N:] for c, pg in zip(chains, pgs)]
        for d in range(2):
            for h in range(RW_H):
                hst = h_ref[0, d, h]
                for u in range(U):
                    i = (d * U + u) * RW_H + h
                    c = chains[i]
                    y = _dot(c['r'] + qys[i][:, :N], hst) + ylocs[i]
                    ys_ref[pl.ds(c['s'], L), d * 512 + h * N:d * 512 + (h + 1) * N] = y
                    hst = _dot(eye * c['g'] + pgs[i][:, :N], hst) + gmats[i]
                h_ref[0, d, h] = hst
        return carry

    lax.fori_loop(0, nc // U, chunk, 0)

    def finish(ci, carry):
        s = pl.multiple_of(ci * Lf, Lf)
        r = r_ref[0, pl.ds(s, Lf), :]
        k = k_ref[0, pl.ds(s, Lf), :]
        rk = r * k * rk_ref[...]
        bonus = jnp.zeros((Lf, 512), F32)
        for d in range(2):
            a = _sigmoid(pre_ref[pl.ds(s, Lf), d * 1024 + 512:(d + 1) * 1024])
            bonus = bonus + _seg_sum(rk * (1.0 + (a - 1.0) * ka_ref[...]), bd)
        y = ys_ref[pl.ds(s, Lf), 0:512] + ys_ref[pl.ds(s, Lf), 512:1024]
        yc = y - _seg_sum(y, bd) / N
        yn = yc * lax.rsqrt(_seg_sum(yc * yc, bd) / N + RW_LN_EPS) * ln_ref[...]
        g = _dot(_sigmoid(gd_ref[0, pl.ds(s, Lf), :]), gup_ref[...])
        out_ref[0, pl.ds(s, Lf), :] = ((yn + bonus * v_ref[0, pl.ds(s, Lf), :]) * g).astype(BF16)
        return carry

    lax.fori_loop(0, T // Lf, finish, 0)


def _rwkv_params(p):
    wwa = jnp.zeros((LANES, 2048), F32)
    for d in range(2):
        wwa = wwa.at[:64, d * 1024:d * 1024 + 512].set(p['rw_w_up'][d])
        wwa = wwa.at[64:, d * 1024 + 512:(d + 1) * 1024].set(p['rw_a_up'][d])
    w0a0 = jnp.concatenate([p['rw_w0'][0], p['rw_a0'][0], p['rw_w0'][1], p['rw_a0'][1]]).reshape(1, 2048)
    seg = np.arange(512) // RW_N
    bd = jnp.asarray(seg[:, None] == seg[None, :], BF16)
    row = lambda n: p[n].reshape(1, -1)
    return (_split_weight(wwa), w0a0, p['rw_g_up'].astype(BF16), row('rw_k_k'), row('rw_k_a'), row('rw_r_k'),
            row('rw_ln'), bd)


def _rwkv(u3, rp, h0):
    B, T, _ = u3.shape
    blk = lambda name, w: pl.BlockSpec((1, T, w), lambda b, o=PK_OFF[name] // w: (b, 0, o))
    full = lambda a: pl.BlockSpec(a.shape, lambda b, n=a.ndim: (0,) * n)
    st = pl.BlockSpec((1, 2, RW_H, RW_N, RW_N), lambda b: (b, 0, 0, 0, 0))
    return pl.pallas_call(
        functools.partial(_rwkv_kernel, T=T),
        grid=(B,),
        in_specs=[blk('rw_r', 512), blk('rw_k', 512), blk('rw_v', 512), blk('rw_wd', 128), blk('rw_gd', 128)]
                 + [full(a) for a in rp] + [st],
        out_specs=[pl.BlockSpec((1, T, 512), lambda b: (b, 0, 0)), st],
        out_shape=[jax.ShapeDtypeStruct((B, T, 512), BF16), jax.ShapeDtypeStruct(h0.shape, F32)],
        scratch_shapes=[pltpu.VMEM((T, 1024), F32), pltpu.VMEM((T, 2048), F32)],
        compiler_params=_cparams(("arbitrary",)),
        name="rwkv7",
    )(u3, u3, u3, u3, u3, *rp, h0)


def _route(logits):
    lane = lax.broadcasted_iota(jnp.int32, (1, LANES), 1)
    far = jnp.int32(2 * LANES)
    neg = -jnp.inf
    gl = jnp.where(jnp.logical_and(lane >= N_EXPERTS, lane < N_EXPERTS + N_GROUPS), logits, neg)
    gmax = jnp.max(gl, axis=-1, keepdims=True)
    grp = jnp.min(jnp.where(gl == gmax, lane, far), axis=-1, keepdims=True) - N_EXPERTS
    p_grp = 1.0 / jnp.sum(jnp.exp(gl - gmax), axis=-1, keepdims=True)
    el = jnp.where(jnp.logical_and(lane < N_EXPERTS, lane // EXPERTS_PER_GROUP == grp), logits, neg)
    v1 = jnp.max(el, axis=-1, keepdims=True)
    i1 = jnp.min(jnp.where(el == v1, lane, far), axis=-1, keepdims=True)
    el2 = jnp.where(lane == i1, neg, el)
    v2 = jnp.max(el2, axis=-1, keepdims=True)
    i2 = jnp.min(jnp.where(el2 == v2, lane, far), axis=-1, keepdims=True)
    e = jnp.exp(v2 - v1)
    w1 = 1.0 / (1.0 + e)
    comb = jnp.where(lane == i1, p_grp * w1, jnp.where(lane == i2, p_grp * (e * w1), 0.0))
    sel = jnp.where(lane == i1, 1.0, jnp.where(lane == i2, 2.0, 0.0))
    return comb, sel


def _outproj_kernel(x_ref, m0_ref, m1_ref, m2_ref, m3_ref, w_ref, mod_ref, g_ref, wr_ref, br_ref,
                    xn_ref, h3_ref, comb_ref, sel_ref):
    y = jnp.dot(m0_ref[...], w_ref[0:512, :], preferred_element_type=F32)
    for i, m_ref in enumerate((m1_ref, m2_ref, m3_ref), start=1):
        y = y + jnp.dot(m_ref[...], w_ref[i * 512:(i + 1) * 512, :], preferred_element_type=F32)
    xn = x_ref[...] + mod_ref[0, 2:3, :] * y
    xn_ref[...] = xn
    h = xn * lax.rsqrt(jnp.mean(xn * xn, axis=-1, keepdims=True) + NORM_EPS) * g_ref[...]
    h = h * (1.0 + mod_ref[0, 4:5, :]) + mod_ref[0, 3:4, :]
    tm = h.shape[0]
    for c in range(ROW_TILES):
        h3_ref[pl.ds(c, tm, stride=ROW_TILES), :] = h[:, c * LANES:(c + 1) * LANES]
    comb_ref[...], sel_ref[...] = _route(_dot_split(h, wr_ref) + br_ref[...])


def _out_proj(x2, mixed, w_out, mod, g, router_w, router_b, T):
    n_tok = x2.shape[0]
    tm = 512
    bm = mod.shape[0]
    mod_idx = (lambda i: (i * tm // T, 0, 0)) if bm > 1 else (lambda i: (0, 0, 0))
    row = lambda w: pl.BlockSpec((tm, w), lambda i: (i, 0))
    full = lambda a: pl.BlockSpec(a.shape, lambda i, n=a.ndim: (0,) * n)
    return pl.pallas_call(
        _outproj_kernel,
        grid=(n_tok // tm,),
        in_specs=[row(D_MODEL)] + [row(GROUP_W)] * 4 + [full(w_out), pl.BlockSpec((1, 8, D_MODEL), mod_idx),
                                                       full(g), full(router_w), full(router_b)],
        out_specs=[row(D_MODEL), pl.BlockSpec((tm * ROW_TILES, LANES), lambda i: (i, 0)), row(LANES), row(LANES)],
        out_shape=[jax.ShapeDtypeStruct((n_tok, D_MODEL), F32),
                   jax.ShapeDtypeStruct((n_tok * ROW_TILES, LANES), F32),
                   jax.ShapeDtypeStruct((n_tok, LANES), F32), jax.ShapeDtypeStruct((n_tok, LANES), F32)],
        compiler_params=_cparams(("arbitrary",)),
        name="out_proj",
    )(x2, *mixed, w_out, mod, g, router_w, router_b)


MOE_TM = 256
MOE_ROWS = 256


def _moe_rows(n_tok):
    return 2 * n_tok + N_EXPERTS * MOE_TM


def _plan_kernel(sel_ref, pos_ref, tmap_ref):
    n_tok = sel_ref.shape[0]
    blk = 512
    lane = lax.broadcasted_iota(jnp.int32, (1, LANES), 1)
    earlier = (lax.broadcasted_iota(jnp.int32, (blk, blk), 1)
               < lax.broadcasted_iota(jnp.int32, (blk, blk), 0)).astype(BF16)
    before = (lax.broadcasted_iota(jnp.int32, (LANES, LANES), 0)
              < lax.broadcasted_iota(jnp.int32, (LANES, LANES), 1)).astype(BF16)

    def count(i, acc):
        s = pl.multiple_of(i * blk, blk)
        return acc + jnp.sum((sel_ref[pl.ds(s, blk), :] > 0.0).astype(F32), axis=0, keepdims=True)

    counts = lax.fori_loop(0, n_tok // blk, count, jnp.zeros((1, LANES), F32))
    tiles = jnp.floor((counts + (MOE_TM - 1)) * (1.0 / MOE_TM))
    tile_start = _dot(jnp.broadcast_to(tiles, (8, LANES)), before)[0:1, :]
    tile_end = tile_start + tiles
    base = tile_start * MOE_TM
    n_tiles = jnp.sum(tiles, axis=-1, keepdims=True)
    j = lax.broadcasted_iota(jnp.int32, (tmap_ref.shape[0], 1), 0).astype(F32)
    done = jnp.logical_and(tile_end <= j, lane < N_EXPERTS)
    expert = jnp.minimum(jnp.sum(done.astype(F32), axis=-1, keepdims=True), N_EXPERTS - 1.0)
    valid = (j < n_tiles).astype(F32)
    tmap_ref[...] = jnp.where(lane == 0, expert, jnp.where(lane == 1, valid, 0.0)).astype(jnp.int32)

    def place(i, seen):
        s = pl.multiple_of(i * blk, blk)
        sel = sel_ref[pl.ds(s, blk), :]
        one = (sel > 0.0).astype(F32)
        row = base + seen + _dot(earlier, one)
        p1 = jnp.sum(jnp.where(sel == 1.0, row, 0.0), axis=-1, keepdims=True)
        p2 = jnp.sum(jnp.where(sel == 2.0, row, 0.0), axis=-1, keepdims=True)
        pos_ref[pl.ds(s, blk), :] = jnp.where(lane == 0, p1, jnp.where(lane == 1, p2, 0.0)).astype(jnp.int32)
        return seen + jnp.sum(one, axis=0, keepdims=True)

    lax.fori_loop(0, n_tok // blk, place, jnp.zeros((1, LANES), F32))


def _moe_plan(sel):
    n_tok = sel.shape[0]
    n_tiles = _moe_rows(n_tok) // MOE_TM
    pos, tmap = pl.pallas_call(
        _plan_kernel,
        out_shape=[jax.ShapeDtypeStruct((n_tok, LANES), jnp.int32),
                   jax.ShapeDtypeStruct((-(-n_tiles // 8) * 8, LANES), jnp.int32)],
        compiler_params=pltpu.CompilerParams(vmem_limit_bytes=VMEM_LIMIT),
        name="moe_plan",
    )(sel)
    return jnp.transpose(pos[:, :2]), tmap[:n_tiles, 0], tmap[:n_tiles, 1]


def _dispatch_kernel(pos_ref, src_ref, init_ref, dst_ref, sem):
    del init_ref
    base = pl.program_id(0) * MOE_ROWS

    def copy(j, k):
        return pltpu.make_async_copy(src_ref.at[j], dst_ref.at[pos_ref[k, base + j]], sem)

    def start(j, c):
        copy(j, 0).start()
        copy(j, 1).start()
        return c

    def wait(j, c):
        copy(j, 0).wait()
        copy(j, 1).wait()
        return c

    lax.fori_loop(0, MOE_ROWS, start, 0)
    lax.fori_loop(0, MOE_ROWS, wait, 0)


def _dispatch(pos, h3, init):
    n_tok = h3.shape[0]
    any_spec = pl.BlockSpec(memory_space=pl.ANY)
    return pl.pallas_call(
        _dispatch_kernel,
        grid_spec=pltpu.PrefetchScalarGridSpec(
            num_scalar_prefetch=1, grid=(n_tok // MOE_ROWS,),
            in_specs=[pl.BlockSpec((MOE_ROWS, ROW_TILES, LANES), lambda i, p: (i, 0, 0)), any_spec],
            out_specs=any_spec, scratch_shapes=[pltpu.SemaphoreType.DMA(())]),
        out_shape=jax.ShapeDtypeStruct(init.shape, init.dtype),
        input_output_aliases={2: 0},
        compiler_params=pltpu.CompilerParams(dimension_semantics=("arbitrary",)),
        name="moe_dispatch",
    )(pos, h3, init)


def _experts_kernel(te_ref, tv_ref, xs_ref, wg_ref, wu_ref, wd_ref, ys_ref, wg_s, wu_s, wd_s):
    i = pl.program_id(0)
    fresh = jnp.logical_or(i == 0, te_ref[i] != te_ref[jnp.maximum(i - 1, 0)])

    @pl.when(jnp.logical_and(fresh, tv_ref[i] == 1))
    def _():
        wg_s[...] = wg_ref[0, 0].astype(BF16)
        wu_s[...] = wu_ref[0, 0].astype(BF16)
        wd_s[...] = wd_ref[0, 0].astype(BF16)

    @pl.when(tv_ref[i] == 1)
    def _():
        x = jnp.concatenate([xs_ref[pl.ds(c, MOE_TM, stride=ROW_TILES), :] for c in range(ROW_TILES)],
                            axis=1).astype(BF16)
        a = jnp.dot(x, wg_s[...], preferred_element_type=F32)
        b = jnp.dot(x, wu_s[...], preferred_element_type=F32)
        y = jnp.dot(((a * _sigmoid(a)) * b).astype(BF16), wd_s[...], preferred_element_type=F32)
        for c in range(ROW_TILES):
            ys_ref[pl.ds(c, MOE_TM, stride=ROW_TILES), :] = y[:, c * LANES:(c + 1) * LANES]

    @pl.when(tv_ref[i] == 0)
    def _():
        ys_ref[...] = jnp.zeros_like(ys_ref)


def _experts(tile_expert, tile_valid, xs, wg, wu, wd, layer):
    n_rows = xs.shape[0] // ROW_TILES
    rows = pl.BlockSpec((MOE_TM * ROW_TILES, LANES), lambda i, te, tv: (i, 0))
    return pl.pallas_call(
        _experts_kernel,
        grid_spec=pltpu.PrefetchScalarGridSpec(
            num_scalar_prefetch=2, grid=(n_rows // MOE_TM,),
            in_specs=[rows,
                      pl.BlockSpec((1, 1, D_MODEL, EXPERT_HIDDEN), lambda i, te, tv: (layer, te[i], 0, 0)),
                      pl.BlockSpec((1, 1, D_MODEL, EXPERT_HIDDEN), lambda i, te, tv: (layer, te[i], 0, 0)),
                      pl.BlockSpec((1, 1, EXPERT_HIDDEN, D_MODEL), lambda i, te, tv: (layer, te[i], 0, 0))],
            out_specs=rows,
            scratch_shapes=[pltpu.VMEM((D_MODEL, EXPERT_HIDDEN), BF16), pltpu.VMEM((D_MODEL, EXPERT_HIDDEN), BF16),
                            pltpu.VMEM((EXPERT_HIDDEN, D_MODEL), BF16)]),
        out_shape=jax.ShapeDtypeStruct(xs.shape, F32),
        compiler_params=_cparams(("arbitrary",)),
        name="moe_experts",
    )(tile_expert, tile_valid, xs, wg, wu, wd)


def _combine_kernel(pos_ref, ys_ref, comb_ref, sel_ref, xn_ref, mod_ref, o_ref, y_s, sems):
    i = pl.program_id(0)
    slot_rows = 2 * MOE_ROWS * ROW_TILES

    def copy(tile, slot, j, k):
        dst = pl.multiple_of(slot * slot_rows + (k * MOE_ROWS + j) * ROW_TILES, ROW_TILES)
        return pltpu.make_async_copy(ys_ref.at[pos_ref[k, tile * MOE_ROWS + j]], y_s.at[pl.ds(dst, ROW_TILES), :],
                                     sems.at[slot])

    def start_tile(tile, slot):
        def body(j, c):
            copy(tile, slot, j, 0).start()
            copy(tile, slot, j, 1).start()
            return c
        lax.fori_loop(0, MOE_ROWS, body, 0)

    def wait_tile(tile, slot):
        def body(j, c):
            copy(tile, slot, j, 0).wait()
            copy(tile, slot, j, 1).wait()
            return c
        lax.fori_loop(0, MOE_ROWS, body, 0)

    @pl.when(i == 0)
    def _():
        start_tile(0, 0)

    @pl.when(i + 1 < pl.num_programs(0))
    def _():
        start_tile(i + 1, (i + 1) % 2)

    comb, sel = comb_ref[...], sel_ref[...]
    w1 = jnp.sum(jnp.where(sel == 1.0, comb, 0.0), axis=-1, keepdims=True)
    w2 = jnp.sum(jnp.where(sel == 2.0, comb, 0.0), axis=-1, keepdims=True)
    slot = i % 2
    wait_tile(i, slot)
    for c in range(ROW_TILES):
        cs = slice(c * LANES, (c + 1) * LANES)
        y1 = y_s[pl.ds(slot * slot_rows + c, MOE_ROWS, stride=ROW_TILES), :]
        y2 = y_s[pl.ds(slot * slot_rows + MOE_ROWS * ROW_TILES + c, MOE_ROWS, stride=ROW_TILES), :]
        o_ref[:, cs] = xn_ref[:, cs] + mod_ref[0, 5:6, cs] * (w1 * y1 + w2 * y2)


def _combine(pos, ys, comb, sel, xn, mod, T):
    n_tok = xn.shape[0]
    tm = MOE_ROWS
    bm = mod.shape[0]
    mod_idx = (lambda i, p: (i * tm // T, 0, 0)) if bm > 1 else (lambda i, p: (0, 0, 0))
    row = lambda w: pl.BlockSpec((tm, w), lambda i, p: (i, 0))
    return pl.pallas_call(
        _combine_kernel,
        grid_spec=pltpu.PrefetchScalarGridSpec(
            num_scalar_prefetch=1, grid=(n_tok // tm,),
            in_specs=[pl.BlockSpec(memory_space=pl.ANY), row(LANES), row(LANES), row(D_MODEL),
                      pl.BlockSpec((1, 8, D_MODEL), mod_idx)],
            out_specs=row(D_MODEL),
            scratch_shapes=[pltpu.VMEM((2 * 2 * tm * ROW_TILES, LANES), F32), pltpu.SemaphoreType.DMA((2,))]),
        out_shape=jax.ShapeDtypeStruct((n_tok, D_MODEL), F32),
        compiler_params=_cparams(("arbitrary",)),
        name="moe_combine",
    )(pos, ys, comb, sel, xn, mod)


def _moe(groups, wg, wu, wd, layer):
    sizes = [g[3].shape[0] for g in groups]
    n_rows = _moe_rows(sum(sizes))
    pos, tile_expert, tile_valid = _moe_plan(jnp.concatenate([g[2] for g in groups], axis=0))
    xs = jnp.zeros((n_rows, ROW_TILES, LANES), F32)
    start = 0
    for (h3, _, _, _, _, _), n in zip(groups, sizes):
        xs = _dispatch(pos[:, start:start + n], h3.reshape(n, ROW_TILES, LANES), xs)
        start += n
    ys = _experts(tile_expert, tile_valid, xs.reshape(n_rows * ROW_TILES, LANES), wg, wu, wd, layer)
    ys = ys.reshape(n_rows, ROW_TILES, LANES)
    outs, start = [], 0
    for (_, comb, sel, xn, mod, T), n in zip(groups, sizes):
        outs.append(_combine(pos[:, start:start + n], ys, comb, sel, xn, mod, T))
        start += n
    return outs


def kernel(x_prompt, x_sample, cache_mla, state_mlstm_C, state_mlstm_n, state_mlstm_m, state_rwkv, state_gla, c, c_ctx, w_ada, b_ada, g_mix, g_ffn, w_in, w_out, b_ml_gates, g_ml_norm, g_mla_qlat, g_mla_kvlat, w_mla_uq, w_mla_ukv, g_mla_qn, g_mla_kn, rw_w0, rw_w_up, rw_a0, rw_a_up, rw_g_up, rw_k_k, rw_k_a, rw_r_k, rw_ln, gla_g_up, gla_g_b, gla_norm, moe_w_rg, moe_b_rg, moe_w_re, moe_b_re, moe_w_gate, moe_w_up, moe_w_down):
    cc = jnp.concatenate([c_ctx[None], c, jnp.zeros((3, D_MODEL), F32)], axis=0)
    mod = _modulation(cc, w_ada, b_ada).reshape(DEPTH, 8, 6, D_MODEL)
    mod = jnp.pad(mod, ((0, 0), (0, 0), (0, 2), (0, 0)))

    layers = []
    for l in range(DEPTH):
        p = {'b_ml_gates': b_ml_gates[l], 'g_ml_norm': g_ml_norm[l], 'g_mla_qlat': g_mla_qlat[l],
             'g_mla_kvlat': g_mla_kvlat[l], 'w_mla_uq': w_mla_uq[l], 'w_mla_ukv': w_mla_ukv[l],
             'g_mla_qn': g_mla_qn[l], 'g_mla_kn': g_mla_kn[l], 'rw_w0': rw_w0[l], 'rw_w_up': rw_w_up[l],
             'rw_a0': rw_a0[l], 'rw_a_up': rw_a_up[l], 'rw_g_up': rw_g_up[l], 'rw_k_k': rw_k_k[l],
             'rw_k_a': rw_k_a[l], 'rw_r_k': rw_r_k[l], 'rw_ln': rw_ln[l], 'gla_g_up': gla_g_up[l],
             'gla_g_b': gla_g_b[l], 'gla_norm': gla_norm[l]}
        router_w = jnp.zeros((D_MODEL, LANES), F32)
        router_w = router_w.at[:, :N_EXPERTS].set(moe_w_re[l]).at[:, N_EXPERTS:N_EXPERTS + N_GROUPS].set(moe_w_rg[l])
        router_b = jnp.zeros((1, LANES), F32)
        router_b = router_b.at[0, :N_EXPERTS].set(moe_b_re[l]).at[0, N_EXPERTS:N_EXPERTS + N_GROUPS].set(moe_b_rg[l])
        layers.append(dict(
            w_in=_pack_w_in(w_in[l]), g_mix=g_mix[l].reshape(1, -1), g_ffn=g_ffn[l].reshape(1, -1),
            w_out=w_out[l].astype(BF16), ml=_mlstm_params(p), mla=_mla_params(p), rw=_rwkv_params(p),
            gla=_gla_params(p), router_w=_split_weight(router_w), router_b=router_b,
            layer=l))

    def mix(x2, B, T, mod_g, lp, ctx):
        u3 = _in_proj(x2, mod_g, lp['g_mix'], lp['w_in'], T).reshape(B, T, PK_COLS)
        if ctx is None:
            ctx_kv = None
            ml_c0, ml_m0 = _mlstm_state_zero(B)
            rw_h0 = jnp.zeros((B, 2, RW_H, RW_N, RW_N), F32)
            gla_s0 = jnp.zeros((B, 2, GLA_H, GLA_DK, GLA_DV), F32)
        else:
            ctx_kv, ml_C0, ml_n0, ml_m0_, rw_S0, gla_s0 = ctx
            ml_c0, ml_m0 = _mlstm_state_in(ml_C0, ml_n0, ml_m0_)
            rw_h0 = jnp.swapaxes(rw_S0, -1, -2)
        ml_out, ml_c, ml_m = _mlstm(u3, *lp['ml'], ml_c0, ml_m0)
        mla_out, own_kv = _mla(u3, ctx_kv, lp['mla'])
        rw_out, rw_h = _rwkv(u3, lp['rw'], rw_h0)
        gla_out, gla_s = _gla(u3, lp['gla'], gla_s0)
        mixed = [t.reshape(B * T, GROUP_W) for t in (ml_out, mla_out, rw_out, gla_out)]
        xn, h3, comb, sel = _out_proj(x2, mixed, lp['w_out'], mod_g, lp['g_ffn'], lp['router_w'], lp['router_b'], T)
        ml_C, ml_n, ml_mm = _mlstm_state_out(ml_c, ml_m)
        return (h3, comb, sel, xn, mod_g, T), (own_kv, ml_C, ml_n, ml_mm, jnp.swapaxes(rw_h, -1, -2), gla_s)

    Bp, Tp = x_prompt.shape[:2]
    Bs, Ts = x_sample.shape[:2]
    xp = x_prompt.reshape(Bp * Tp, D_MODEL)
    xs = x_sample.reshape(Bs * Ts, D_MODEL)
    ctx_states = []
    for l in range(DEPTH):
        ctx = (cache_mla[:, l], state_mlstm_C[:, l], state_mlstm_n[:, l], state_mlstm_m[:, l],
               state_rwkv[:, l], state_gla[:, l])
        moe_p, st = mix(xp, Bp, Tp, mod[l, 0:1], layers[l], None)
        moe_s, _ = mix(xs, Bs, Ts, mod[l, 1:1 + Bs], layers[l], ctx)
        ctx_states.append(st)
        xp, xs = _moe([moe_p, moe_s], moe_w_gate, moe_w_up, moe_w_down, l)
    outs = [jnp.stack([s[i] for s in ctx_states], axis=1) for i in range(6)]
    return (xp.reshape(x_prompt.shape), xs.reshape(x_sample.shape), *outs)
```

```python
import functools
import math

import numpy as np
import jax
import jax.numpy as jnp
from jax import lax
from jax.experimental import pallas as pl
from jax.experimental.pallas import tpu as pltpu

F32 = jnp.float32
BF16 = jnp.bfloat16

D_MODEL = 2048
DEPTH = 2
GRID_W = 64
GROUP_W = 512
ML_H, ML_DK = 4, 128
MLA_H, MLA_NOPE, MLA_ROPE, MLA_V = 4, 128, 64, 128
MLA_QK = MLA_NOPE + MLA_ROPE
Q_LORA, KV_LORA = 384, 256
ROPE_THETA = 10000.0
RW_H, RW_N = 8, 64
RW_DECAY_SCALE = math.exp(-0.5)
RW_LN_EPS = 64e-5
GLA_H, GLA_DK, GLA_DV = 4, 64, 128
GLA_GATE_RANK = 16
GLA_NORMALIZER = 16.0
N_GROUPS, EXPERTS_PER_GROUP, N_EXPERTS = 4, 4, 16
EXPERT_HIDDEN = 512
NORM_EPS = 1e-6
LANES = 128
ROW_TILES = D_MODEL // LANES
VMEM_LIMIT = 56 * 1024 * 1024

_REF_SPLITS = (
    ('ml_q', 512), ('ml_k', 512), ('ml_v', 512), ('ml_o', 512), ('ml_g', 16),
    ('mla_ql', Q_LORA), ('mla_ckv', KV_LORA), ('mla_kr', MLA_ROPE),
    ('rw_r', 512), ('rw_k', 512), ('rw_v', 512), ('rw_wd', 64), ('rw_ad', 64), ('rw_gd', 128),
    ('gla_q', 256), ('gla_k', 256), ('gla_v', 512), ('gla_gd', GLA_GATE_RANK), ('gla_g', 512),
)
_REF_OFF = {}
_o = 0
for _n, _w in _REF_SPLITS:
    _REF_OFF[_n] = (_o, _w)
    _o += _w
IN_COLS = _o

_PACKED = (
    ('ml_q', 512), ('ml_k', 512), ('ml_v', 512), ('ml_o', 512),
    ('rw_r', 512), ('rw_k', 512), ('rw_v', 512), ('gla_v', 512), ('gla_g', 512),
    ('mla_ql', 384), ('mla_ckv', 256), ('mla_kr', 64), ('mla_kr_sw', 64),
    ('gla_q', 256), ('gla_k', 256),
    ('ml_g', 128), ('rw_wd', 64), ('rw_ad', 64), ('rw_gd', 128), ('gla_gd', 128),
)
PK_OFF = {}
_o = 0
for _n, _w in _PACKED:
    PK_OFF[_n] = _o
    _o += _w
PK_COLS = _o


def _rope_swap_perm():
    idx = np.arange(MLA_ROPE)
    axis, half, f = idx // 32, (idx % 32) // 16, idx % 16
    return axis * 32 + (1 - half) * 16 + f


def _packed_column_index():
    src = np.full((PK_COLS,), -1, np.int64)
    for name, width in _PACKED:
        off = PK_OFF[name]
        if name == 'mla_kr_sw':
            s, w = _REF_OFF['mla_kr']
            src[off:off + w] = s + _rope_swap_perm()
        else:
            s, w = _REF_OFF[name]
            src[off:off + w] = s + np.arange(w)
    return src


_PK_SRC = _packed_column_index()


def _column_runs(src):
    runs, i = [], 0
    while i < len(src):
        j = i + 1
        while j < len(src) and ((src[i] < 0 and src[j] < 0) or (src[i] >= 0 and src[j] == src[i] + (j - i))):
            j += 1
        runs.append((int(src[i]), j - i))
        i = j
    return runs


_PK_RUNS = _column_runs(_PK_SRC)


def _pack_w_in(w):
    parts = [w[:, s:s + n] if s >= 0 else jnp.zeros((w.shape[0], n), w.dtype) for s, n in _PK_RUNS]
    return jnp.concatenate(parts, axis=1).astype(BF16)


def _cparams(sem):
    return pltpu.CompilerParams(dimension_semantics=sem, vmem_limit_bytes=VMEM_LIMIT)


def _log_sigmoid(x):
    return jnp.minimum(x, 0.0) - jnp.log(1.0 + jnp.exp(-jnp.abs(x)))


def _sigmoid(x):
    return 1.0 / (1.0 + jnp.exp(-x))


def _dot(a, b):
    return jnp.dot(a.astype(BF16), b.astype(BF16), preferred_element_type=F32)


def _dot_nt(a, b):
    return lax.dot_general(a.astype(BF16), b.astype(BF16), (((1,), (1,)), ((), ())), preferred_element_type=F32)


def _dot_tn(a, b):
    return lax.dot_general(a.astype(BF16), b.astype(BF16), (((0,), (0,)), ((), ())), preferred_element_type=F32)


def _split2(x):
    hi = x.astype(BF16)
    return hi, (x - hi.astype(F32)).astype(BF16)


def _split_weight(w):
    hi, lo = _split2(w)
    return jnp.stack([hi, lo])


def _running_sum(tri, x):
    hi = x.astype(BF16)
    r = x - hi.astype(F32)
    mid = r.astype(BF16)
    lo = (r - mid.astype(F32)).astype(BF16)
    n = x.shape[1]
    s = jnp.dot(tri, jnp.concatenate([hi, mid, lo], axis=1), preferred_element_type=F32)
    return s[:, :n] + s[:, n:2 * n] + s[:, 2 * n:]


def _dot_split(a, w_ref):
    a_hi, a_lo = _split2(a)
    w_hi = w_ref[0]
    return (jnp.dot(a_hi, w_hi, preferred_element_type=F32) + jnp.dot(a_lo, w_hi, preferred_element_type=F32)
            + jnp.dot(a_hi, w_ref[1], preferred_element_type=F32))


def _mod_kernel(c_ref, w_ref, b_ref, o_ref):
    c = c_ref[...]
    s = c * _sigmoid(c)
    o_ref[0] = _dot(s, w_ref[0]) + b_ref[0]


def _modulation(cc, w_ada, b_ada):
    tn = 1536
    n = 6 * D_MODEL
    return pl.pallas_call(
        _mod_kernel,
        grid=(DEPTH, n // tn),
        in_specs=[pl.BlockSpec((8, D_MODEL), lambda l, j: (0, 0)),
                  pl.BlockSpec((1, D_MODEL, tn), lambda l, j: (l, 0, j)),
                  pl.BlockSpec((1, 1, tn), lambda l, j: (l, 0, j))],
        out_specs=pl.BlockSpec((1, 8, tn), lambda l, j: (l, 0, j)),
        out_shape=jax.ShapeDtypeStruct((DEPTH, 8, n), F32),
        compiler_params=_cparams(("arbitrary", "arbitrary")),
        name="adaln_mod",
    )(cc, w_ada, b_ada.reshape(DEPTH, 1, n))


def _inproj_kernel(x_ref, mod_ref, g_ref, w_ref, o_ref, h_scr):
    @pl.when(pl.program_id(1) == 0)
    def _():
        x = x_ref[...]
        xn = x * lax.rsqrt(jnp.mean(x * x, axis=-1, keepdims=True) + NORM_EPS) * g_ref[...]
        h_scr[...] = (xn * (1.0 + mod_ref[0, 1:2, :]) + mod_ref[0, 0:1, :]).astype(BF16)

    o_ref[...] = jnp.dot(h_scr[...], w_ref[...], preferred_element_type=F32)


def _in_proj(x2, mod, g, w_packed, T):
    n_tok = x2.shape[0]
    tm = 1024
    tn = 1280
    bm = mod.shape[0]
    mod_idx = (lambda i, j: (i * tm // T, 0, 0)) if bm > 1 else (lambda i, j: (0, 0, 0))
    return pl.pallas_call(
        _inproj_kernel,
        grid=(n_tok // tm, PK_COLS // tn),
        in_specs=[pl.BlockSpec((tm, D_MODEL), lambda i, j: (i, 0)),
                  pl.BlockSpec((1, 8, D_MODEL), mod_idx),
                  pl.BlockSpec((1, D_MODEL), lambda i, j: (0, 0)),
                  pl.BlockSpec((D_MODEL, tn), lambda i, j: (0, j))],
        out_specs=pl.BlockSpec((tm, tn), lambda i, j: (i, j)),
        out_shape=jax.ShapeDtypeStruct((n_tok, PK_COLS), F32),
        scratch_shapes=[pltpu.VMEM((tm, D_MODEL), BF16)],
        compiler_params=_cparams(("arbitrary", "arbitrary")),
        name="in_proj",
    )(x2, mod, g, w_packed)


ML_CHUNK = 256


def _mlstm_kernel(q_ref, k_ref, v_ref, o_ref, g_ref, bias_ref, gn_ref, c0_ref, m0_ref,
                  out_ref, c_ref, m_ref, hs_ref, *, T):
    L = ML_CHUNK
    nc = T // L
    c_ref[...] = c0_ref[...]
    m_ref[...] = m0_ref[...]
    ii = lax.broadcasted_iota(jnp.int32, (L, L), 0)
    jj = lax.broadcasted_iota(jnp.int32, (L, L), 1)
    lane = lax.broadcasted_iota(jnp.int32, (1, LANES), 1)
    is_f = jnp.logical_and(lane % 8 >= 4, lane < 16)
    ones_col = (lax.broadcasted_iota(jnp.int32, (L, LANES), 1) == 0).astype(BF16)
    scale = ML_DK ** -0.5

    def chunk(ci, carry):
        ch = []
        for d in range(2):
            mask = (jj <= ii) if d == 0 else (jj >= ii)
            c = ci if d == 0 else nc - 1 - ci
            s = pl.multiple_of(c * L, L)
            gates = g_ref[0, pl.ds(s, L), :] + bias_ref[...]
            gf = jnp.where(is_f, _log_sigmoid(gates), gates)
            cum = _running_sum(mask.astype(BF16), gf)
            gf_t = gf.T
            cum_t = cum.T
            for h in range(ML_H):
                ci_, cf_ = d * 8 + h, d * 8 + 4 + h
                hs = slice(h * ML_DK, (h + 1) * ML_DK)
                ig_col, ig_row = gf[:, ci_:ci_ + 1], gf_t[ci_:ci_ + 1, :]
                b_col, b_row = cum[:, cf_:cf_ + 1], cum_t[cf_:cf_ + 1, :]
                b_last = b_col[L - 1:L, :] if d == 0 else b_col[0:1, :]
                m_prev = m_ref[0, d, h][:, 0:1]
                dmat = jnp.where(mask, b_col + (ig_row - b_row), -jnp.inf)
                m_inter = b_col + m_prev
                m_row = jnp.maximum(m_inter, jnp.max(dmat, axis=-1, keepdims=True))
                dk_col = b_last - b_col + ig_col
                m_new = jnp.maximum(b_last + m_prev, jnp.max(dk_col, axis=0, keepdims=True))
                ch.append(dict(
                    d=d, h=h, s=s, hs=hs, dmat=dmat, m_row=m_row, w_inter=jnp.exp(m_inter - m_row), m_new=m_new,
                    w_key=jnp.exp(dk_col - m_new), c_scale=jnp.exp(b_last + m_prev - m_new),
                    q=(q_ref[0, pl.ds(s, L), hs] * scale).astype(BF16), k=k_ref[0, pl.ds(s, L), hs],
                    v_aug=jnp.concatenate([v_ref[0, pl.ds(s, L), hs].astype(BF16), ones_col], axis=1),
                    c_aug=c_ref[0, d, h]))
        qk = [_dot_nt(c['q'], c['k']) for c in ch]
        qc = [_dot(c['q'], c['c_aug']) for c in ch]
        sv = [_dot(s_ * jnp.exp(c['dmat'] - c['m_row']), c['v_aug']) for s_, c in zip(qk, ch)]
        kv = [_dot_tn(c['k'] * c['w_key'], c['v_aug']) for c in ch]
        for c, qc_, sv_, kv_ in zip(ch, qc, sv, kv):
            nd = c['w_inter'] * qc_ + sv_
            num, den = nd[:, :ML_DK], nd[:, ML_DK:ML_DK + 1]
            hh = num / jnp.maximum(jnp.abs(den), jnp.exp(-c['m_row']))
            hs_ref[pl.ds(c['s'], L), c['d'] * 512 + c['h'] * ML_DK:c['d'] * 512 + (c['h'] + 1) * ML_DK] = hh
            c_ref[0, c['d'], c['h']] = c['c_scale'] * c['c_aug'] + kv_
            m_ref[0, c['d'], c['h']] = jnp.broadcast_to(c['m_new'], (1, LANES))
        return carry

    lax.fori_loop(0, nc, chunk, 0)

    def finish(ci, carry):
        s = pl.multiple_of(ci * L, L)
        for h in range(ML_H):
            hs = slice(h * ML_DK, (h + 1) * ML_DK)
            x = hs_ref[pl.ds(s, L), hs] + hs_ref[pl.ds(s, L), 512 + h * ML_DK:512 + (h + 1) * ML_DK]
            xc = x - jnp.mean(x, axis=-1, keepdims=True)
            y = xc * lax.rsqrt(jnp.mean(xc * xc, axis=-1, keepdims=True) + NORM_EPS) * gn_ref[:, hs]
            out_ref[0, pl.ds(s, L), hs] = (y * _sigmoid(o_ref[0, pl.ds(s, L), hs])).astype(BF16)
        return carry

    lax.fori_loop(0, nc, finish, 0)


def _mlstm_params(p):
    bias = jnp.zeros((1, LANES), F32).at[0, :16].set(p['b_ml_gates'])
    return bias, p['g_ml_norm'].reshape(1, 512)


def _mlstm_state_in(C0, n0, m0):
    c0 = jnp.concatenate([C0, n0[..., None], jnp.zeros(C0.shape[:-1] + (ML_DK - 1,), F32)], axis=-1)
    return c0, jnp.broadcast_to(m0[..., None, None], m0.shape + (1, LANES))


def _mlstm_state_zero(B):
    return jnp.zeros((B, 2, ML_H, ML_DK, 2 * ML_DK), F32), jnp.zeros((B, 2, ML_H, 1, LANES), F32)


def _mlstm_state_out(c, m):
    return c[..., :ML_DK], c[..., ML_DK], m[..., 0, 0]


def _mlstm(u3, bias, gnorm, c0, m0):
    B, T, _ = u3.shape
    blk = lambda name: pl.BlockSpec((1, T, 512), lambda b, o=PK_OFF[name] // 512: (b, 0, o))
    st_c = pl.BlockSpec((1, 2, ML_H, ML_DK, 2 * ML_DK), lambda b: (b, 0, 0, 0, 0))
    st_m = pl.BlockSpec((1, 2, ML_H, 1, LANES), lambda b: (b, 0, 0, 0, 0))
    return pl.pallas_call(
        functools.partial(_mlstm_kernel, T=T),
        grid=(B,),
        in_specs=[blk('ml_q'), blk('ml_k'), blk('ml_v'), blk('ml_o'),
                  pl.BlockSpec((1, T, LANES), lambda b: (b, 0, PK_OFF['ml_g'] // LANES)),
                  pl.BlockSpec((1, LANES), lambda b: (0, 0)),
                  pl.BlockSpec((1, 512), lambda b: (0, 0)),
                  st_c, st_m],
        out_specs=[pl.BlockSpec((1, T, 512), lambda b: (b, 0, 0)), st_c, st_m],
        out_shape=[jax.ShapeDtypeStruct((B, T, 512), BF16),
                   jax.ShapeDtypeStruct(c0.shape, F32),
                   jax.ShapeDtypeStruct(m0.shape, F32)],
        scratch_shapes=[pltpu.VMEM((T, 1024), F32)],
        compiler_params=_cparams(("arbitrary",)),
        name="mlstm",
    )(u3, u3, u3, u3, u3, bias, gnorm, c0, m0)


MLA_BLK = 256


def _rope_tables(T):
    rows = T // GRID_W
    row = np.repeat(np.arange(rows, dtype=np.float64), GRID_W)
    col = np.tile(np.arange(GRID_W, dtype=np.float64), rows)
    inv = ROPE_THETA ** (-np.arange(MLA_ROPE // 4, dtype=np.float64) / (MLA_ROPE // 4))
    ang = np.stack([row[:, None] * inv, col[:, None] * inv], axis=1)
    cos = np.stack([np.cos(ang), np.cos(ang)], axis=2).reshape(T, MLA_ROPE)
    sin = np.stack([-np.sin(ang), np.sin(ang)], axis=2).reshape(T, MLA_ROPE)
    return jnp.asarray(np.concatenate([cos, sin], axis=1), F32)


def _mla_kernel(*refs, T, n_ctx, rope):
    if n_ctx:
        (u_ref, ctx_ref, cs_ref, gql_ref, gkv_ref, gains_ref, wqn_ref, wqr_ref, wqs_ref, wkn_ref, wv_ref,
         out_ref, kv_ref, qn_s, qr_s, kn_s, kr_s, v_s) = refs
    else:
        (u_ref, cs_ref, gql_ref, gkv_ref, gains_ref, wqn_ref, wqr_ref, wqs_ref, wkn_ref, wv_ref,
         out_ref, kv_ref, qn_s, qr_s, kn_s, kr_s, v_s) = refs
    Lb = MLA_BLK
    gq_n, gq_r, gq_s = gains_ref[0:1, :], gains_ref[1:2, 0:64], gains_ref[1:2, 64:128]
    gk_n, gk_r, gk_s = gains_ref[2:3, :], gains_ref[3:4, 0:64], gains_ref[3:4, 64:128]
    sm_scale = MLA_QK ** -0.5

    def store_keys(s, kn, kr, krs, cos, sin):
        kr_ss = jnp.sum(kr * kr, axis=-1, keepdims=True)
        for h in range(MLA_H):
            kn_h = kn[:, h * 128:(h + 1) * 128]
            rk = lax.rsqrt((jnp.sum(kn_h * kn_h, axis=-1, keepdims=True) + kr_ss) / MLA_QK + NORM_EPS)
            kn_s[pl.ds(s, Lb), h * 128:(h + 1) * 128] = (kn_h * rk * gk_n).astype(BF16)
            kr_h = kr * gk_r
            if cos is not None:
                kr_h = kr_h * cos + (krs * gk_s) * sin
            kr_s[pl.ds(s, Lb), h * 64:(h + 1) * 64] = (kr_h * rk).astype(BF16)

    def prep(ci, carry):
        s = pl.multiple_of(ci * Lb, Lb)
        u = u_ref[0, pl.ds(s, Lb), :]
        ql, ckv, kr, krs = u[:, :384], u[:, 384:640], u[:, 640:704], u[:, 704:768]
        qln = ql * lax.rsqrt(jnp.mean(ql * ql, axis=-1, keepdims=True) + NORM_EPS) * gql_ref[...]
        ckvn = ckv * lax.rsqrt(jnp.mean(ckv * ckv, axis=-1, keepdims=True) + NORM_EPS) * gkv_ref[...]
        kv_ref[0, pl.ds(s, Lb), :] = jnp.concatenate([ckvn, kr], axis=1)
        cos = cs_ref[pl.ds(s, Lb), 0:64] if rope else None
        sin = cs_ref[pl.ds(s, Lb), 64:128] if rope else None
        qn, qr, qs = _dot(qln, wqn_ref[...]), _dot(qln, wqr_ref[...]), _dot(qln, wqs_ref[...])
        for h in range(MLA_H):
            qn_h, qr_h = qn[:, h * 128:(h + 1) * 128], qr[:, h * 64:(h + 1) * 64]
            ss = jnp.sum(qn_h * qn_h, axis=-1, keepdims=True) + jnp.sum(qr_h * qr_h, axis=-1, keepdims=True)
            rq = lax.rsqrt(ss / MLA_QK + NORM_EPS) * sm_scale
            qn_s[pl.ds(s, Lb), h * 128:(h + 1) * 128] = (qn_h * rq * gq_n).astype(BF16)
            qr_h = qr_h * gq_r
            if rope:
                qr_h = qr_h * cos + (qs[:, h * 64:(h + 1) * 64] * gq_s) * sin
            qr_s[pl.ds(s, Lb), h * 64:(h + 1) * 64] = (qr_h * rq).astype(BF16)
        v_s[pl.ds(s, Lb), :] = _dot(ckvn, wv_ref[...]).astype(BF16)
        store_keys(s, _dot(ckvn, wkn_ref[...]), kr, krs, cos, sin)
        return carry

    lax.fori_loop(0, T // Lb, prep, 0)

    for ci in range(n_ctx // Lb):
        cx = ctx_ref[0, ci * Lb:(ci + 1) * Lb, :]
        ckv_c, kr_c = cx[:, :KV_LORA], cx[:, KV_LORA:KV_LORA + MLA_ROPE]
        v_s[T + ci * Lb:T + (ci + 1) * Lb, :] = _dot(ckv_c, wv_ref[...]).astype(BF16)
        store_keys(T + ci * Lb, _dot(ckv_c, wkn_ref[...]), kr_c, None, None, None)

    def attend(qi, carry):
        s = pl.multiple_of(qi * Lb, Lb)
        heads = range(MLA_H)
        sc = [_dot_nt(qn_s[pl.ds(s, Lb), h * 128:(h + 1) * 128], kn_s[:, h * 128:(h + 1) * 128])
              + _dot_nt(qr_s[pl.ds(s, Lb), h * 64:(h + 1) * 64], kr_s[:, h * 64:(h + 1) * 64]) for h in heads]
        p = [jnp.exp(x - jnp.max(x, axis=-1, keepdims=True)) for x in sc]
        o = [_dot(p[h], v_s[:, h * 128:(h + 1) * 128]) / jnp.sum(p[h], axis=-1, keepdims=True) for h in heads]
        for h in heads:
            out_ref[0, pl.ds(s, Lb), h * 128:(h + 1) * 128] = o[h].astype(BF16)
        return carry

    lax.fori_loop(0, T // Lb, attend, 0)


def _mla_params(p):
    wq = p['w_mla_uq'].reshape(Q_LORA, MLA_H, MLA_QK)
    sw = _rope_swap_perm()
    wq_n = wq[:, :, :MLA_NOPE].reshape(Q_LORA, 512).astype(BF16)
    wq_r = wq[:, :, MLA_NOPE:].reshape(Q_LORA, 256).astype(BF16)
    wq_s = wq[:, :, MLA_NOPE:][:, :, sw].reshape(Q_LORA, 256).astype(BF16)
    wkv = p['w_mla_ukv'].reshape(KV_LORA, MLA_H, MLA_NOPE + MLA_V)
    wk_n = wkv[:, :, :MLA_NOPE].reshape(KV_LORA, 512).astype(BF16)
    wv = wkv[:, :, MLA_NOPE:].reshape(KV_LORA, 512).astype(BF16)
    gq, gk = p['g_mla_qn'], p['g_mla_kn']
    gains = jnp.zeros((8, LANES), F32)
    gains = gains.at[0].set(gq[:128]).at[1, :64].set(gq[128:]).at[1, 64:].set(gq[128:][sw])
    gains = gains.at[2].set(gk[:128]).at[3, :64].set(gk[128:]).at[3, 64:].set(gk[128:][sw])
    return (p['g_mla_qlat'].reshape(1, -1), p['g_mla_kvlat'].reshape(1, -1), gains, wq_n, wq_r, wq_s, wk_n, wv)


def _mla(u3, ctx_kv, mp):
    B, T, _ = u3.shape
    n_ctx = 0 if ctx_kv is None else ctx_kv.shape[1]
    rope = ctx_kv is not None
    tk = T + n_ctx
    full = lambda a: pl.BlockSpec(a.shape, lambda b, n=a.ndim: (0,) * n)
    cs = _rope_tables(T) if rope else jnp.zeros((T, LANES), F32)
    ins = [u3] + ([ctx_kv] if rope else []) + [cs] + list(mp)
    specs = [pl.BlockSpec((1, T, 768), lambda b: (b, 0, PK_OFF['mla_ql'] // 768))]
    if rope:
        specs.append(pl.BlockSpec((1, n_ctx, KV_LORA + MLA_ROPE), lambda b: (b, 0, 0)))
    specs += [full(a) for a in ins[len(specs):]]
    return pl.pallas_call(
        functools.partial(_mla_kernel, T=T, n_ctx=n_ctx, rope=rope),
        grid=(B,),
        in_specs=specs,
        out_specs=[pl.BlockSpec((1, T, 512), lambda b: (b, 0, 0)),
                   pl.BlockSpec((1, T, KV_LORA + MLA_ROPE), lambda b: (b, 0, 0))],
        out_shape=[jax.ShapeDtypeStruct((B, T, 512), BF16),
                   jax.ShapeDtypeStruct((B, T, KV_LORA + MLA_ROPE), F32)],
        scratch_shapes=[pltpu.VMEM((T, 512), BF16), pltpu.VMEM((T, 256), BF16),
                        pltpu.VMEM((tk, 512), BF16), pltpu.VMEM((tk, 256), BF16), pltpu.VMEM((tk, 512), BF16)],
        compiler_params=_cparams(("arbitrary",)),
        name="mla",
    )(*ins)


GLA_CHUNK = 64
GLA_LEAF = 4
GLA_UNROLL = 2


def _gla_kernel(q_ref, k_ref, v_ref, gd_ref, g_ref, gup_ref, gb_ref, gn_ref, hsel_ref, s0_ref,
                out_ref, s_ref, os_ref, la_ref, *, T):
    L, C = GLA_CHUNK, GLA_LEAF
    nc = T // L
    Lf = 256
    s_ref[...] = s0_ref[...]

    def gates(ci, carry):
        s = pl.multiple_of(ci * Lf, Lf)
        x = _dot_split(gd_ref[0, pl.ds(s, Lf), :], gup_ref) + gb_ref[...]
        la_ref[pl.ds(s, Lf), :] = _log_sigmoid(x) / GLA_NORMALIZER
        return carry

    lax.fori_loop(0, T // Lf, gates, 0)
    ii = lax.broadcasted_iota(jnp.int32, (L, L), 0)
    jj = lax.broadcasted_iota(jnp.int32, (L, L), 1)
    eye = (lax.broadcasted_iota(jnp.int32, (GLA_DK, GLA_DK), 0)
           == lax.broadcasted_iota(jnp.int32, (GLA_DK, GLA_DK), 1)).astype(F32)
    row_id = lax.broadcasted_iota(jnp.int32, (L, 1), 0)
    scale = GLA_DK ** -0.5
    hsel = hsel_ref[...]
    levels = []
    span = C
    while span < L:
        levels.append(span)
        span *= 2

    U = GLA_UNROLL

    def chunk(ci, carry):
        parts = []
        for d in range(2):
            causal = (jj <= ii) if d == 0 else (jj >= ii)
            tri = causal.astype(BF16)
            for u in range(U):
                c = ci * U + u if d == 0 else nc - 1 - (ci * U + u)
                s = pl.multiple_of(c * L, L)
                la = la_ref[pl.ds(s, L), d * GLA_H * GLA_DK:(d + 1) * GLA_H * GLA_DK]
                b = _running_sum(tri, la)
                total = b[L - 1:L, :] if d == 0 else b[0:1, :]
                q = q_ref[0, pl.ds(s, L), :] * scale
                k = k_ref[0, pl.ds(s, L), :]
                lv = []
                for sp in levels:
                    b3 = b.reshape(L // (2 * sp), 2 * sp, GLA_H * GLA_DK)
                    edge = b3[:, sp - 1:sp, :] if d == 0 else b3[:, sp:sp + 1, :]
                    bref = jnp.broadcast_to(edge, b3.shape).reshape(L, GLA_H * GLA_DK)
                    later = (row_id % (2 * sp) >= sp) if d == 0 else (row_id % (2 * sp) < sp)
                    e = jnp.exp(jnp.where(later, b - bref, bref - b))
                    lv.append((jnp.where(later, q * e, 0.0), jnp.where(later, 0.0, k * e),
                               (ii // (2 * sp)) == (jj // (2 * sp))))
                terms = []
                for dl in range(C):
                    if dl == 0:
                        terms.append(q * k)
                        continue
                    sh = dl if d == 0 else L - dl
                    ok = (row_id % C >= dl) if d == 0 else (row_id % C < C - dl)
                    kd = pltpu.roll(k, sh, axis=0)
                    bd = pltpu.roll(b, sh, axis=0)
                    terms.append(q * kd * jnp.exp(jnp.where(ok, b - bd, 0.0)))
                tt = jnp.concatenate(terms, axis=0)
                t_hi = tt.astype(BF16)
                t_lo = (tt - t_hi.astype(F32)).astype(BF16)
                diag = (jnp.dot(t_hi, hsel, preferred_element_type=F32)
                        + jnp.dot(t_lo, hsel, preferred_element_type=F32))
                parts.append(dict(d=d, s=s, lv=lv, diag=diag, q_in=q * jnp.exp(b), k_out=k * jnp.exp(total - b),
                                  f_row=jnp.exp(total)))
        chains = [(p, h) for p in parts for h in range(GLA_H)]
        amats = []
        for p, h in chains:
            ks = slice(h * GLA_DK, (h + 1) * GLA_DK)
            a = jnp.zeros((L, L), F32)
            for qs, kt, same in p['lv']:
                a = a + jnp.where(same, _dot_nt(qs[:, ks], kt[:, ks]), 0.0)
            for dl in range(C):
                pair = jnp.logical_and(jj == (ii - dl if p['d'] == 0 else ii + dl), ii // C == jj // C)
                a = a + jnp.where(pair, p['diag'][dl * L:(dl + 1) * L, h:h + 1], 0.0)
            amats.append(a)
        vals = [v_ref[0, pl.ds(p['s'], L), h * GLA_DV:(h + 1) * GLA_DV] for p, h in chains]
        intra = [_dot(a, v) for a, v in zip(amats, vals)]
        upd = [_dot_tn(p['k_out'][:, h * GLA_DK:(h + 1) * GLA_DK], v) for (p, h), v in zip(chains, vals)]
        for d in range(2):
            for h in range(GLA_H):
                ks = slice(h * GLA_DK, (h + 1) * GLA_DK)
                st = s_ref[0, d, h]
                for u in range(U):
                    i = (d * U + u) * GLA_H + h
                    p = parts[d * U + u]
                    o = intra[i] + _dot(p['q_in'][:, ks], st)
                    os_ref[pl.ds(p['s'], L), d * 512 + h * GLA_DV:d * 512 + (h + 1) * GLA_DV] = o
                    f_col = jnp.sum(eye * p['f_row'][:, ks], axis=1, keepdims=True)
                    st = f_col * st + upd[i]
                s_ref[0, d, h] = st
        return carry

    lax.fori_loop(0, nc // U, chunk, 0)

    def finish(ci, carry):
        s = pl.multiple_of(ci * Lf, Lf)
        for h in range(GLA_H):
            vs = slice(h * GLA_DV, (h + 1) * GLA_DV)
            o = os_ref[pl.ds(s, Lf), vs] + os_ref[pl.ds(s, Lf), 512 + h * GLA_DV:512 + (h + 1) * GLA_DV]
            y = o * lax.rsqrt(jnp.mean(o * o, axis=-1, keepdims=True) + NORM_EPS) * gn_ref[:, vs]
            g = g_ref[0, pl.ds(s, Lf), vs]
            out_ref[0, pl.ds(s, Lf), vs] = (y * (g * _sigmoid(g))).astype(BF16)
        return carry

    lax.fori_loop(0, T // Lf, finish, 0)


def _gla_params(p):
    gup = jnp.zeros((LANES, 2 * GLA_H * GLA_DK), F32)
    gup = gup.at[:GLA_GATE_RANK, :].set(jnp.concatenate([p['gla_g_up'][0], p['gla_g_up'][1]], axis=1))
    hsel = jnp.asarray(np.arange(GLA_H * GLA_DK)[:, None] // GLA_DK == np.arange(LANES)[None, :], BF16)
    return _split_weight(gup), p['gla_g_b'].reshape(1, -1), p['gla_norm'].reshape(1, -1), hsel


def _gla(u3, gp, s0):
    B, T, _ = u3.shape
    blk = lambda name, w: pl.BlockSpec((1, T, w), lambda b, o=PK_OFF[name] // w: (b, 0, o))
    full = lambda a: pl.BlockSpec(a.shape, lambda b, n=a.ndim: (0,) * n)
    st = pl.BlockSpec((1, 2, GLA_H, GLA_DK, GLA_DV), lambda b: (b, 0, 0, 0, 0))
    return pl.pallas_call(
        functools.partial(_gla_kernel, T=T),
        grid=(B,),
        in_specs=[blk('gla_q', 256), blk('gla_k', 256), blk('gla_v', 512), blk('gla_gd', 128), blk('gla_g', 512),
                  full(gp[0]), full(gp[1]), full(gp[2]), full(gp[3]), st],
        out_specs=[pl.BlockSpec((1, T, 512), lambda b: (b, 0, 0)), st],
        out_shape=[jax.ShapeDtypeStruct((B, T, 512), BF16), jax.ShapeDtypeStruct(s0.shape, F32)],
        scratch_shapes=[pltpu.VMEM((T, 1024), F32), pltpu.VMEM((T, 2 * GLA_H * GLA_DK), F32)],
        compiler_params=_cparams(("arbitrary",)),
        name="gla",
    )(u3, u3, u3, u3, u3, *gp, s0)


RW_CHUNK = 64
RW_UNROLL = 2


def _seg_sum(x, bd):
    hi = x.astype(BF16)
    lo = (x - hi.astype(F32)).astype(BF16)
    return jnp.dot(hi, bd, preferred_element_type=F32) + jnp.dot(lo, bd, preferred_element_type=F32)


def _rwkv_kernel(r_ref, k_ref, v_ref, wa_ref, gd_ref, wwa_ref, w0a0_ref, gup_ref, kk_ref, ka_ref, rk_ref, ln_ref,
                 bd_ref, h0_ref, out_ref, h_ref, ys_ref, pre_ref, *, T):
    L, N = RW_CHUNK, RW_N
    nc = T // L
    h_ref[...] = h0_ref[...]
    ii = lax.broadcasted_iota(jnp.int32, (L, L), 0)
    jj = lax.broadcasted_iota(jnp.int32, (L, L), 1)
    eye = (ii == jj).astype(F32)
    lane = lax.broadcasted_iota(jnp.int32, (1, LANES), 1)
    bd = bd_ref[...]
    Lf = 256

    def gates(ci, carry):
        s = pl.multiple_of(ci * Lf, Lf)
        wa = wa_ref[0, pl.ds(s, Lf), :]
        pre_ref[pl.ds(s, Lf), :] = _dot_split(jnp.where(lane < 64, jnp.tanh(wa), wa), wwa_ref) + w0a0_ref[...]
        return carry

    lax.fori_loop(0, T // Lf, gates, 0)

    masks = [((jj < ii), (jj <= ii)), ((jj > ii), (jj >= ii))]
    U = RW_UNROLL

    def chunk(ci, carry):
        chains = []
        for d in range(2):
            strict, incl = masks[d]
            tri = incl.astype(BF16)
            for u in range(U):
                c = ci * U + u if d == 0 else nc - 1 - (ci * U + u)
                s = pl.multiple_of(c * L, L)
                r = r_ref[0, pl.ds(s, L), :]
                k = k_ref[0, pl.ds(s, L), :]
                v = v_ref[0, pl.ds(s, L), :]
                pre = pre_ref[pl.ds(s, L), d * 1024:(d + 1) * 1024]
                logw = -RW_DECAY_SCALE * _sigmoid(pre[:, :512])
                a = _sigmoid(pre[:, 512:])
                kkr = k * kk_ref[...]
                kk = kkr * lax.rsqrt(_seg_sum(kkr * kkr, bd) + 1e-12)
                kt = k * (1.0 + (a - 1.0) * ka_ref[...])
                bh = kk * a
                lg = _running_sum(tri, logw)
                lg_end = lg[L - 1:L, :] if d == 0 else lg[0:1, :]
                a_t = -kk * jnp.exp(lg - logw)
                r_t = r * jnp.exp(lg)
                e_inv = jnp.exp(-lg)
                k_t, b_t = kt * e_inv, bh * e_inv
                e_end = jnp.exp(lg_end - lg)
                k_e, b_e = kt * e_end, bh * e_end
                g_end = jnp.exp(lg_end)
                for h in range(RW_H):
                    sl = slice(h * N, (h + 1) * N)
                    chains.append(dict(d=d, u=u, h=h, s=s, strict=strict, incl=incl, a=a_t[:, sl], r=r_t[:, sl],
                                       b=b_t[:, sl], k=k_t[:, sl], ke=k_e[:, sl], be=b_e[:, sl], g=g_end[:, sl],
                                       v=v[:, sl]))
        ms = [_dot_nt(jnp.concatenate([c['a'], c['r']], axis=0), jnp.concatenate([c['b'], c['k']], axis=0))
              for c in chains]
        pws = [jnp.where(c['strict'], m[:L, :L], 0.0) for c, m in zip(chains, ms)]
        xs = [eye + n for n in pws]
        for _ in range(5):
            pws = [_dot(pw, pw) for pw in pws]
            xs = [x + _dot(x, pw) for x, pw in zip(xs, pws)]
        mvs = [_dot(jnp.where(c['strict'], m[:L, L:], 0.0), c['v']) for c, m in zip(chains, ms)]
        tws = [_dot(x, jnp.concatenate([c['a'], mv], axis=1)) for x, c, mv in zip(xs, chains, mvs)]
        qys = [_dot(jnp.where(c['incl'], m[L:, :L], 0.0), tw) for c, m, tw in zip(chains, ms, tws)]
        ylocs = [_dot(jnp.where(c['incl'], m[L:, L:], 0.0), c['v']) + qy[:, N:] for c, m, qy in zip(chains, ms, qys)]
        pgs = [_dot_tn(c['be'], tw) for c, tw in zip(chains, tws)]
        gmats = [_dot_tn(c['ke'], c['v']) + pg[:, N:] for c, pg in zip(chains, pgs)]
        for d in range(2):
            for h in range(RW_H):
                hst = h_ref[0, d, h]
                for u in range(U):
                    i = (d * U + u) * RW_H + h
                    c = chains[i]
                    y = _dot(c['r'] + qys[i][:, :N], hst) + ylocs[i]
                    ys_ref[pl.ds(c['s'], L), d * 512 + h * N:d * 512 + (h + 1) * N] = y
                    hst = _dot(eye * c['g'] + pgs[i][:, :N], hst) + gmats[i]
                h_ref[0, d, h] = hst
        return carry

    lax.fori_loop(0, nc // U, chunk, 0)

    def finish(ci, carry):
        s = pl.multiple_of(ci * Lf, Lf)
        r = r_ref[0, pl.ds(s, Lf), :]
        k = k_ref[0, pl.ds(s, Lf), :]
        rk = r * k * rk_ref[...]
        bonus = jnp.zeros((Lf, 512), F32)
        for d in range(2):
            a = _sigmoid(pre_ref[pl.ds(s, Lf), d * 1024 + 512:(d + 1) * 1024])
            bonus = bonus + _seg_sum(rk * (1.0 + (a - 1.0) * ka_ref[...]), bd)
        y = ys_ref[pl.ds(s, Lf), 0:512] + ys_ref[pl.ds(s, Lf), 512:1024]
        yc = y - _seg_sum(y, bd) / N
        yn = yc * lax.rsqrt(_seg_sum(yc * yc, bd) / N + RW_LN_EPS) * ln_ref[...]
        g = _dot(_sigmoid(gd_ref[0, pl.ds(s, Lf), :]), gup_ref[...])
        out_ref[0, pl.ds(s, Lf), :] = ((yn + bonus * v_ref[0, pl.ds(s, Lf), :]) * g).astype(BF16)
        return carry

    lax.fori_loop(0, T // Lf, finish, 0)


def _rwkv_params(p):
    wwa = jnp.zeros((LANES, 2048), F32)
    for d in range(2):
        wwa = wwa.at[:64, d * 1024:d * 1024 + 512].set(p['rw_w_up'][d])
        wwa = wwa.at[64:, d * 1024 + 512:(d + 1) * 1024].set(p['rw_a_up'][d])
    w0a0 = jnp.concatenate([p['rw_w0'][0], p['rw_a0'][0], p['rw_w0'][1], p['rw_a0'][1]]).reshape(1, 2048)
    seg = np.arange(512) // RW_N
    bd = jnp.asarray(seg[:, None] == seg[None, :], BF16)
    row = lambda n: p[n].reshape(1, -1)
    return (_split_weight(wwa), w0a0, p['rw_g_up'].astype(BF16), row('rw_k_k'), row('rw_k_a'), row('rw_r_k'),
            row('rw_ln'), bd)


def _rwkv(u3, rp, h0):
    B, T, _ = u3.shape
    blk = lambda name, w: pl.BlockSpec((1, T, w), lambda b, o=PK_OFF[name] // w: (b, 0, o))
    full = lambda a: pl.BlockSpec(a.shape, lambda b, n=a.ndim: (0,) * n)
    st = pl.BlockSpec((1, 2, RW_H, RW_N, RW_N), lambda b: (b, 0, 0, 0, 0))
    return pl.pallas_call(
        functools.partial(_rwkv_kernel, T=T),
        grid=(B,),
        in_specs=[blk('rw_r', 512), blk('rw_k', 512), blk('rw_v', 512), blk('rw_wd', 128), blk('rw_gd', 128)]
                 + [full(a) for a in rp] + [st],
        out_specs=[pl.BlockSpec((1, T, 512), lambda b: (b, 0, 0)), st],
        out_shape=[jax.ShapeDtypeStruct((B, T, 512), BF16), jax.ShapeDtypeStruct(h0.shape, F32)],
        scratch_shapes=[pltpu.VMEM((T, 1024), F32), pltpu.VMEM((T, 2048), F32)],
        compiler_params=_cparams(("arbitrary",)),
        name="rwkv7",
    )(u3, u3, u3, u3, u3, *rp, h0)


def _route(logits):
    lane = lax.broadcasted_iota(jnp.int32, (1, LANES), 1)
    far = jnp.int32(2 * LANES)
    neg = -jnp.inf
    gl = jnp.where(jnp.logical_and(lane >= N_EXPERTS, lane < N_EXPERTS + N_GROUPS), logits, neg)
    gmax = jnp.max(gl, axis=-1, keepdims=True)
    grp = jnp.min(jnp.where(gl == gmax, lane, far), axis=-1, keepdims=True) - N_EXPERTS
    p_grp = 1.0 / jnp.sum(jnp.exp(gl - gmax), axis=-1, keepdims=True)
    el = jnp.where(jnp.logical_and(lane < N_EXPERTS, lane // EXPERTS_PER_GROUP == grp), logits, neg)
    v1 = jnp.max(el, axis=-1, keepdims=True)
    i1 = jnp.min(jnp.where(el == v1, lane, far), axis=-1, keepdims=True)
    el2 = jnp.where(lane == i1, neg, el)
    v2 = jnp.max(el2, axis=-1, keepdims=True)
    i2 = jnp.min(jnp.where(el2 == v2, lane, far), axis=-1, keepdims=True)
    e = jnp.exp(v2 - v1)
    w1 = 1.0 / (1.0 + e)
    comb = jnp.where(lane == i1, p_grp * w1, jnp.where(lane == i2, p_grp * (e * w1), 0.0))
    sel = jnp.where(lane == i1, 1.0, jnp.where(lane == i2, 2.0, 0.0))
    return comb, sel


def _outproj_kernel(x_ref, m0_ref, m1_ref, m2_ref, m3_ref, w_ref, mod_ref, g_ref, wr_ref, br_ref,
                    xn_ref, h3_ref, comb_ref, sel_ref):
    y = jnp.dot(m0_ref[...], w_ref[0:512, :], preferred_element_type=F32)
    for i, m_ref in enumerate((m1_ref, m2_ref, m3_ref), start=1):
        y = y + jnp.dot(m_ref[...], w_ref[i * 512:(i + 1) * 512, :], preferred_element_type=F32)
    xn = x_ref[...] + mod_ref[0, 2:3, :] * y
    xn_ref[...] = xn
    h = xn * lax.rsqrt(jnp.mean(xn * xn, axis=-1, keepdims=True) + NORM_EPS) * g_ref[...]
    h = h * (1.0 + mod_ref[0, 4:5, :]) + mod_ref[0, 3:4, :]
    tm = h.shape[0]
    for c in range(ROW_TILES):
        h3_ref[pl.ds(c, tm, stride=ROW_TILES), :] = h[:, c * LANES:(c + 1) * LANES]
    comb_ref[...], sel_ref[...] = _route(_dot_split(h, wr_ref) + br_ref[...])


def _out_proj(x2, mixed, w_out, mod, g, router_w, router_b, T):
    n_tok = x2.shape[0]
    tm = 512
    bm = mod.shape[0]
    mod_idx = (lambda i: (i * tm // T, 0, 0)) if bm > 1 else (lambda i: (0, 0, 0))
    row = lambda w: pl.BlockSpec((tm, w), lambda i: (i, 0))
    full = lambda a: pl.BlockSpec(a.shape, lambda i, n=a.ndim: (0,) * n)
    return pl.pallas_call(
        _outproj_kernel,
        grid=(n_tok // tm,),
        in_specs=[row(D_MODEL)] + [row(GROUP_W)] * 4 + [full(w_out), pl.BlockSpec((1, 8, D_MODEL), mod_idx),
                                                       full(g), full(router_w), full(router_b)],
        out_specs=[row(D_MODEL), pl.BlockSpec((tm * ROW_TILES, LANES), lambda i: (i, 0)), row(LANES), row(LANES)],
        out_shape=[jax.ShapeDtypeStruct((n_tok, D_MODEL), F32),
                   jax.ShapeDtypeStruct((n_tok * ROW_TILES, LANES), F32),
                   jax.ShapeDtypeStruct((n_tok, LANES), F32), jax.ShapeDtypeStruct((n_tok, LANES), F32)],
        compiler_params=_cparams(("arbitrary",)),
        name="out_proj",
    )(x2, *mixed, w_out, mod, g, router_w, router_b)


MOE_TM = 256
MOE_ROWS = 256
MOE_DMA_UNROLL = 8


def _moe_rows(n_tok):
    return 2 * n_tok + N_EXPERTS * MOE_TM


def _plan_kernel(sel_ref, pos_ref, tmap_ref):
    n_tok = sel_ref.shape[0]
    blk = 512
    lane = lax.broadcasted_iota(jnp.int32, (1, LANES), 1)
    earlier = (lax.broadcasted_iota(jnp.int32, (blk, blk), 1)
               < lax.broadcasted_iota(jnp.int32, (blk, blk), 0)).astype(BF16)
    before = (lax.broadcasted_iota(jnp.int32, (LANES, LANES), 0)
              < lax.broadcasted_iota(jnp.int32, (LANES, LANES), 1)).astype(BF16)

    def count(i, acc):
        s = pl.multiple_of(i * blk, blk)
        return acc + jnp.sum((sel_ref[pl.ds(s, blk), :] > 0.0).astype(F32), axis=0, keepdims=True)

    counts = lax.fori_loop(0, n_tok // blk, count, jnp.zeros((1, LANES), F32))
    tiles = jnp.floor((counts + (MOE_TM - 1)) * (1.0 / MOE_TM))
    tile_start = _dot(jnp.broadcast_to(tiles, (8, LANES)), before)[0:1, :]
    tile_end = tile_start + tiles
    base = tile_start * MOE_TM
    n_tiles = jnp.sum(tiles, axis=-1, keepdims=True)
    j = lax.broadcasted_iota(jnp.int32, (tmap_ref.shape[0], 1), 0).astype(F32)
    done = jnp.logical_and(tile_end <= j, lane < N_EXPERTS)
    expert = jnp.minimum(jnp.sum(done.astype(F32), axis=-1, keepdims=True), N_EXPERTS - 1.0)
    valid = (j < n_tiles).astype(F32)
    tmap_ref[...] = jnp.where(lane == 0, expert, jnp.where(lane == 1, valid, 0.0)).astype(jnp.int32)

    def place(i, seen):
        s = pl.multiple_of(i * blk, blk)
        sel = sel_ref[pl.ds(s, blk), :]
        one = (sel > 0.0).astype(F32)
        row = base + seen + _dot(earlier, one)
        p1 = jnp.sum(jnp.where(sel == 1.0, row, 0.0), axis=-1, keepdims=True)
        p2 = jnp.sum(jnp.where(sel == 2.0, row, 0.0), axis=-1, keepdims=True)
        pos_ref[pl.ds(s, blk), :] = jnp.where(lane == 0, p1, jnp.where(lane == 1, p2, 0.0)).astype(jnp.int32)
        return seen + jnp.sum(one, axis=0, keepdims=True)

    lax.fori_loop(0, n_tok // blk, place, jnp.zeros((1, LANES), F32))


def _moe_plan(sel):
    n_tok = sel.shape[0]
    n_tiles = _moe_rows(n_tok) // MOE_TM
    pos, tmap = pl.pallas_call(
        _plan_kernel,
        out_shape=[jax.ShapeDtypeStruct((n_tok, LANES), jnp.int32),
                   jax.ShapeDtypeStruct((-(-n_tiles // 8) * 8, LANES), jnp.int32)],
        compiler_params=pltpu.CompilerParams(vmem_limit_bytes=VMEM_LIMIT),
        name="moe_plan",
    )(sel)
    return jnp.transpose(pos[:, :2]), tmap[:n_tiles, 0], tmap[:n_tiles, 1]


def _dispatch_kernel(pos_ref, src_ref, init_ref, dst_ref, sem):
    del init_ref
    base = pl.program_id(0) * MOE_ROWS

    def copy(j, k):
        return pltpu.make_async_copy(src_ref.at[j], dst_ref.at[pos_ref[k, base + j]], sem)

    def start(j, c):
        copy(j, 0).start()
        copy(j, 1).start()
        return c

    def wait(j, c):
        copy(j, 0).wait()
        copy(j, 1).wait()
        return c

    lax.fori_loop(0, MOE_ROWS, start, 0, unroll=MOE_DMA_UNROLL)
    lax.fori_loop(0, MOE_ROWS, wait, 0, unroll=MOE_DMA_UNROLL)


def _dispatch(pos, h3, init):
    n_tok = h3.shape[0]
    any_spec = pl.BlockSpec(memory_space=pl.ANY)
    return pl.pallas_call(
        _dispatch_kernel,
        grid_spec=pltpu.PrefetchScalarGridSpec(
            num_scalar_prefetch=1, grid=(n_tok // MOE_ROWS,),
            in_specs=[pl.BlockSpec((MOE_ROWS, ROW_TILES, LANES), lambda i, p: (i, 0, 0)), any_spec],
            out_specs=any_spec, scratch_shapes=[pltpu.SemaphoreType.DMA(())]),
        out_shape=jax.ShapeDtypeStruct(init.shape, init.dtype),
        input_output_aliases={2: 0},
        compiler_params=pltpu.CompilerParams(dimension_semantics=("arbitrary",)),
        name="moe_dispatch",
    )(pos, h3, init)


def _experts_kernel(te_ref, tv_ref, xs_ref, wg_ref, wu_ref, wd_ref, ys_ref, wg_s, wu_s, wd_s):
    i = pl.program_id(0)
    fresh = jnp.logical_or(i == 0, te_ref[i] != te_ref[jnp.maximum(i - 1, 0)])

    @pl.when(jnp.logical_and(fresh, tv_ref[i] == 1))
    def _():
        wg_s[...] = wg_ref[0, 0].astype(BF16)
        wu_s[...] = wu_ref[0, 0].astype(BF16)
        wd_s[...] = wd_ref[0, 0].astype(BF16)

    @pl.when(tv_ref[i] == 1)
    def _():
        x = jnp.concatenate([xs_ref[pl.ds(c, MOE_TM, stride=ROW_TILES), :] for c in range(ROW_TILES)],
                            axis=1).astype(BF16)
        a = jnp.dot(x, wg_s[...], preferred_element_type=F32)
        b = jnp.dot(x, wu_s[...], preferred_element_type=F32)
        y = jnp.dot(((a * _sigmoid(a)) * b).astype(BF16), wd_s[...], preferred_element_type=F32)
        for c in range(ROW_TILES):
            ys_ref[pl.ds(c, MOE_TM, stride=ROW_TILES), :] = y[:, c * LANES:(c + 1) * LANES]

    @pl.when(tv_ref[i] == 0)
    def _():
        ys_ref[...] = jnp.zeros_like(ys_ref)


def _experts(tile_expert, tile_valid, xs, wg, wu, wd, layer):
    n_rows = xs.shape[0] // ROW_TILES
    rows = pl.BlockSpec((MOE_TM * ROW_TILES, LANES), lambda i, te, tv: (i, 0))
    return pl.pallas_call(
        _experts_kernel,
        grid_spec=pltpu.PrefetchScalarGridSpec(
            num_scalar_prefetch=2, grid=(n_rows // MOE_TM,),
            in_specs=[rows,
                      pl.BlockSpec((1, 1, D_MODEL, EXPERT_HIDDEN), lambda i, te, tv: (layer, te[i], 0, 0)),
                      pl.BlockSpec((1, 1, D_MODEL, EXPERT_HIDDEN), lambda i, te, tv: (layer, te[i], 0, 0)),
                      pl.BlockSpec((1, 1, EXPERT_HIDDEN, D_MODEL), lambda i, te, tv: (layer, te[i], 0, 0))],
            out_specs=rows,
            scratch_shapes=[pltpu.VMEM((D_MODEL, EXPERT_HIDDEN), BF16), pltpu.VMEM((D_MODEL, EXPERT_HIDDEN), BF16),
                            pltpu.VMEM((EXPERT_HIDDEN, D_MODEL), BF16)]),
        out_shape=jax.ShapeDtypeStruct(xs.shape, F32),
        compiler_params=_cparams(("arbitrary",)),
        name="moe_experts",
    )(tile_expert, tile_valid, xs, wg, wu, wd)


def _combine_kernel(pos_ref, ys_ref, comb_ref, sel_ref, xn_ref, mod_ref, o_ref, y_s, sems):
    i = pl.program_id(0)
    slot_rows = 2 * MOE_ROWS * ROW_TILES

    def copy(tile, slot, j, k):
        dst = pl.multiple_of(slot * slot_rows + (k * MOE_ROWS + j) * ROW_TILES, ROW_TILES)
        return pltpu.make_async_copy(ys_ref.at[pos_ref[k, tile * MOE_ROWS + j]], y_s.at[pl.ds(dst, ROW_TILES), :],
                                     sems.at[slot])

    def start_tile(tile, slot):
        def body(j, c):
            copy(tile, slot, j, 0).start()
            copy(tile, slot, j, 1).start()
            return c
        lax.fori_loop(0, MOE_ROWS, body, 0, unroll=MOE_DMA_UNROLL)

    def wait_tile(tile, slot):
        def body(j, c):
            copy(tile, slot, j, 0).wait()
            copy(tile, slot, j, 1).wait()
            return c
        lax.fori_loop(0, MOE_ROWS, body, 0, unroll=MOE_DMA_UNROLL)

    @pl.when(i == 0)
    def _():
        start_tile(0, 0)

    @pl.when(i + 1 < pl.num_programs(0))
    def _():
        start_tile(i + 1, (i + 1) % 2)

    comb, sel = comb_ref[...], sel_ref[...]
    w1 = jnp.sum(jnp.where(sel == 1.0, comb, 0.0), axis=-1, keepdims=True)
    w2 = jnp.sum(jnp.where(sel == 2.0, comb, 0.0), axis=-1, keepdims=True)
    slot = i % 2
    wait_tile(i, slot)
    for c in range(ROW_TILES):
        cs = slice(c * LANES, (c + 1) * LANES)
        y1 = y_s[pl.ds(slot * slot_rows + c, MOE_ROWS, stride=ROW_TILES), :]
        y2 = y_s[pl.ds(slot * slot_rows + MOE_ROWS * ROW_TILES + c, MOE_ROWS, stride=ROW_TILES), :]
        o_ref[:, cs] = xn_ref[:, cs] + mod_ref[0, 5:6, cs] * (w1 * y1 + w2 * y2)


def _combine(pos, ys, comb, sel, xn, mod, T):
    n_tok = xn.shape[0]
    tm = MOE_ROWS
    bm = mod.shape[0]
    mod_idx = (lambda i, p: (i * tm // T, 0, 0)) if bm > 1 else (lambda i, p: (0, 0, 0))
    row = lambda w: pl.BlockSpec((tm, w), lambda i, p: (i, 0))
    return pl.pallas_call(
        _combine_kernel,
        grid_spec=pltpu.PrefetchScalarGridSpec(
            num_scalar_prefetch=1, grid=(n_tok // tm,),
            in_specs=[pl.BlockSpec(memory_space=pl.ANY), row(LANES), row(LANES), row(D_MODEL),
                      pl.BlockSpec((1, 8, D_MODEL), mod_idx)],
            out_specs=row(D_MODEL),
            scratch_shapes=[pltpu.VMEM((2 * 2 * tm * ROW_TILES, LANES), F32), pltpu.SemaphoreType.DMA((2,))]),
        out_shape=jax.ShapeDtypeStruct((n_tok, D_MODEL), F32),
        compiler_params=_cparams(("arbitrary",)),
        name="moe_combine",
    )(pos, ys, comb, sel, xn, mod)


def _moe(groups, wg, wu, wd, layer, sorted_buf=None):
    sizes = [g[3].shape[0] for g in groups]
    n_rows = _moe_rows(sum(sizes))
    pos, tile_expert, tile_valid = _moe_plan(jnp.concatenate([g[2] for g in groups], axis=0))
    xs = jnp.zeros((n_rows, ROW_TILES, LANES), F32) if sorted_buf is None else sorted_buf
    start = 0
    for (h3, _, _, _, _, _), n in zip(groups, sizes):
        xs = _dispatch(pos[:, start:start + n], h3.reshape(n, ROW_TILES, LANES), xs)
        start += n
    ys = _experts(tile_expert, tile_valid, xs.reshape(n_rows * ROW_TILES, LANES), wg, wu, wd, layer)
    ys = ys.reshape(n_rows, ROW_TILES, LANES)
    outs, start = [], 0
    for (_, comb, sel, xn, mod, T), n in zip(groups, sizes):
        outs.append(_combine(pos[:, start:start + n], ys, comb, sel, xn, mod, T))
        start += n
    return outs, xs


def kernel(x_prompt, x_sample, cache_mla, state_mlstm_C, state_mlstm_n, state_mlstm_m, state_rwkv, state_gla, c, c_ctx, w_ada, b_ada, g_mix, g_ffn, w_in, w_out, b_ml_gates, g_ml_norm, g_mla_qlat, g_mla_kvlat, w_mla_uq, w_mla_ukv, g_mla_qn, g_mla_kn, rw_w0, rw_w_up, rw_a0, rw_a_up, rw_g_up, rw_k_k, rw_k_a, rw_r_k, rw_ln, gla_g_up, gla_g_b, gla_norm, moe_w_rg, moe_b_rg, moe_w_re, moe_b_re, moe_w_gate, moe_w_up, moe_w_down):
    cc = jnp.concatenate([c_ctx[None], c, jnp.zeros((3, D_MODEL), F32)], axis=0)
    mod = _modulation(cc, w_ada, b_ada).reshape(DEPTH, 8, 6, D_MODEL)
    mod = jnp.pad(mod, ((0, 0), (0, 0), (0, 2), (0, 0)))

    layers = []
    for l in range(DEPTH):
        p = {'b_ml_gates': b_ml_gates[l], 'g_ml_norm': g_ml_norm[l], 'g_mla_qlat': g_mla_qlat[l],
             'g_mla_kvlat': g_mla_kvlat[l], 'w_mla_uq': w_mla_uq[l], 'w_mla_ukv': w_mla_ukv[l],
             'g_mla_qn': g_mla_qn[l], 'g_mla_kn': g_mla_kn[l], 'rw_w0': rw_w0[l], 'rw_w_up': rw_w_up[l],
             'rw_a0': rw_a0[l], 'rw_a_up': rw_a_up[l], 'rw_g_up': rw_g_up[l], 'rw_k_k': rw_k_k[l],
             'rw_k_a': rw_k_a[l], 'rw_r_k': rw_r_k[l], 'rw_ln': rw_ln[l], 'gla_g_up': gla_g_up[l],
             'gla_g_b': gla_g_b[l], 'gla_norm': gla_norm[l]}
        router_w = jnp.zeros((D_MODEL, LANES), F32)
        router_w = router_w.at[:, :N_EXPERTS].set(moe_w_re[l]).at[:, N_EXPERTS:N_EXPERTS + N_GROUPS].set(moe_w_rg[l])
        router_b = jnp.zeros((1, LANES), F32)
        router_b = router_b.at[0, :N_EXPERTS].set(moe_b_re[l]).at[0, N_EXPERTS:N_EXPERTS + N_GROUPS].set(moe_b_rg[l])
        layers.append(dict(
            w_in=_pack_w_in(w_in[l]), g_mix=g_mix[l].reshape(1, -1), g_ffn=g_ffn[l].reshape(1, -1),
            w_out=w_out[l].astype(BF16), ml=_mlstm_params(p), mla=_mla_params(p), rw=_rwkv_params(p),
            gla=_gla_params(p), router_w=_split_weight(router_w), router_b=router_b,
            layer=l))

    def mix(x2, B, T, mod_g, lp, ctx):
        u3 = _in_proj(x2, mod_g, lp['g_mix'], lp['w_in'], T).reshape(B, T, PK_COLS)
        if ctx is None:
            ctx_kv = None
            ml_c0, ml_m0 = _mlstm_state_zero(B)
            rw_h0 = jnp.zeros((B, 2, RW_H, RW_N, RW_N), F32)
            gla_s0 = jnp.zeros((B, 2, GLA_H, GLA_DK, GLA_DV), F32)
        else:
            ctx_kv, ml_C0, ml_n0, ml_m0_, rw_S0, gla_s0 = ctx
            ml_c0, ml_m0 = _mlstm_state_in(ml_C0, ml_n0, ml_m0_)
            rw_h0 = jnp.swapaxes(rw_S0, -1, -2)
        ml_out, ml_c, ml_m = _mlstm(u3, *lp['ml'], ml_c0, ml_m0)
        mla_out, own_kv = _mla(u3, ctx_kv, lp['mla'])
        rw_out, rw_h = _rwkv(u3, lp['rw'], rw_h0)
        gla_out, gla_s = _gla(u3, lp['gla'], gla_s0)
        mixed = [t.reshape(B * T, GROUP_W) for t in (ml_out, mla_out, rw_out, gla_out)]
        xn, h3, comb, sel = _out_proj(x2, mixed, lp['w_out'], mod_g, lp['g_ffn'], lp['router_w'], lp['router_b'], T)
        ml_C, ml_n, ml_mm = _mlstm_state_out(ml_c, ml_m)
        return (h3, comb, sel, xn, mod_g, T), (own_kv, ml_C, ml_n, ml_mm, jnp.swapaxes(rw_h, -1, -2), gla_s)

    Bp, Tp = x_prompt.shape[:2]
    Bs, Ts = x_sample.shape[:2]
    xp = x_prompt.reshape(Bp * Tp, D_MODEL)
    xs = x_sample.reshape(Bs * Ts, D_MODEL)
    ctx_states = []
    sorted_buf = None
    for l in range(DEPTH):
        ctx = (cache_mla[:, l], state_mlstm_C[:, l], state_mlstm_n[:, l], state_mlstm_m[:, l],
               state_rwkv[:, l], state_gla[:, l])
        moe_p, st = mix(xp, Bp, Tp, mod[l, 0:1], layers[l], None)
        moe_s, _ = mix(xs, Bs, Ts, mod[l, 1:1 + Bs], layers[l], ctx)
        ctx_states.append(st)
        (xp, xs), sorted_buf = _moe([moe_p, moe_s], moe_w_gate, moe_w_up, moe_w_down, l, sorted_buf)
    outs = [jnp.stack([s[i] for s in ctx_states], axis=1) for i in range(6)]
    return (xp.reshape(x_prompt.shape), xs.reshape(x_sample.shape), *outs)
```

```python
import functools
import math

import numpy as np
import jax
import jax.numpy as jnp
from jax import lax
from jax.experimental import pallas as pl
from jax.experimental.pallas import tpu as pltpu

F32 = jnp.float32
BF16 = jnp.bfloat16

D_MODEL = 2048
DEPTH = 2
GRID_W = 64
GROUP_W = 512
ML_H, ML_DK = 4, 128
MLA_H, MLA_NOPE, MLA_ROPE, MLA_V = 4, 128, 64, 128
MLA_QK = MLA_NOPE + MLA_ROPE
Q_LORA, KV_LORA = 384, 256
ROPE_THETA = 10000.0
RW_H, RW_N = 8, 64
RW_DECAY_SCALE = math.exp(-0.5)
RW_LN_EPS = 64e-5
GLA_H, GLA_DK, GLA_DV = 4, 64, 128
GLA_GATE_RANK = 16
GLA_NORMALIZER = 16.0
N_GROUPS, EXPERTS_PER_GROUP, N_EXPERTS = 4, 4, 16
EXPERT_HIDDEN = 512
NORM_EPS = 1e-6
LANES = 128
ROW_TILES = D_MODEL // LANES
VMEM_LIMIT = 56 * 1024 * 1024

_REF_SPLITS = (
    ('ml_q', 512), ('ml_k', 512), ('ml_v', 512), ('ml_o', 512), ('ml_g', 16),
    ('mla_ql', Q_LORA), ('mla_ckv', KV_LORA), ('mla_kr', MLA_ROPE),
    ('rw_r', 512), ('rw_k', 512), ('rw_v', 512), ('rw_wd', 64), ('rw_ad', 64), ('rw_gd', 128),
    ('gla_q', 256), ('gla_k', 256), ('gla_v', 512), ('gla_gd', GLA_GATE_RANK), ('gla_g', 512),
)
_REF_OFF = {}
_o = 0
for _n, _w in _REF_SPLITS:
    _REF_OFF[_n] = (_o, _w)
    _o += _w
IN_COLS = _o

_PACKED = (
    ('ml_q', 512), ('ml_k', 512), ('ml_v', 512), ('ml_o', 512),
    ('rw_r', 512), ('rw_k', 512), ('rw_v', 512), ('gla_v', 512), ('gla_g', 512),
    ('mla_ql', 384), ('mla_ckv', 256), ('mla_kr', 64), ('mla_kr_sw', 64),
    ('gla_q', 256), ('gla_k', 256),
    ('ml_g', 128), ('rw_wd', 64), ('rw_ad', 64), ('rw_gd', 128), ('gla_gd', 128),
)
PK_OFF = {}
_o = 0
for _n, _w in _PACKED:
    PK_OFF[_n] = _o
    _o += _w
PK_COLS = _o


def _rope_swap_perm():
    idx = np.arange(MLA_ROPE)
    axis, half, f = idx // 32, (idx % 32) // 16, idx % 16
    return axis * 32 + (1 - half) * 16 + f


def _packed_column_index():
    src = np.full((PK_COLS,), -1, np.int64)
    for name, width in _PACKED:
        off = PK_OFF[name]
        if name == 'mla_kr_sw':
            s, w = _REF_OFF['mla_kr']
            src[off:off + w] = s + _rope_swap_perm()
        else:
            s, w = _REF_OFF[name]
            src[off:off + w] = s + np.arange(w)
    return src


_PK_SRC = _packed_column_index()


def _column_runs(src):
    runs, i = [], 0
    while i < len(src):
        j = i + 1
        while j < len(src) and ((src[i] < 0 and src[j] < 0) or (src[i] >= 0 and src[j] == src[i] + (j - i))):
            j += 1
        runs.append((int(src[i]), j - i))
        i = j
    return runs


_PK_RUNS = _column_runs(_PK_SRC)


def _pack_w_in(w):
    parts = [w[:, s:s + n] if s >= 0 else jnp.zeros((w.shape[0], n), w.dtype) for s, n in _PK_RUNS]
    return jnp.concatenate(parts, axis=1).astype(BF16)


def _cparams(sem):
    return pltpu.CompilerParams(dimension_semantics=sem, vmem_limit_bytes=VMEM_LIMIT)


def _log_sigmoid(x):
    return jnp.minimum(x, 0.0) - jnp.log(1.0 + jnp.exp(-jnp.abs(x)))


def _sigmoid(x):
    return 1.0 / (1.0 + jnp.exp(-x))


def _dot(a, b):
    return jnp.dot(a.astype(BF16), b.astype(BF16), preferred_element_type=F32)


def _dot_nt(a, b):
    return lax.dot_general(a.astype(BF16), b.astype(BF16), (((1,), (1,)), ((), ())), preferred_element_type=F32)


def _dot_tn(a, b):
    return lax.dot_general(a.astype(BF16), b.astype(BF16), (((0,), (0,)), ((), ())), preferred_element_type=F32)


def _split2(x):
    hi = x.astype(BF16)
    return hi, (x - hi.astype(F32)).astype(BF16)


def _split_weight(w):
    hi, lo = _split2(w)
    return jnp.stack([hi, lo])


def _running_sum(tri, x):
    hi = x.astype(BF16)
    r = x - hi.astype(F32)
    mid = r.astype(BF16)
    lo = (r - mid.astype(F32)).astype(BF16)
    n = x.shape[1]
    s = jnp.dot(tri, jnp.concatenate([hi, mid, lo], axis=1), preferred_element_type=F32)
    return s[:, :n] + s[:, n:2 * n] + s[:, 2 * n:]


def _dot_split(a, w_ref):
    a_hi, a_lo = _split2(a)
    w_hi = w_ref[0]
    return (jnp.dot(a_hi, w_hi, preferred_element_type=F32) + jnp.dot(a_lo, w_hi, preferred_element_type=F32)
            + jnp.dot(a_hi, w_ref[1], preferred_element_type=F32))


def _mod_kernel(c_ref, w_ref, b_ref, o_ref):
    c = c_ref[...]
    s = c * _sigmoid(c)
    o_ref[0] = _dot(s, w_ref[0]) + b_ref[0]


def _modulation(cc, w_ada, b_ada):
    tn = 1536
    n = 6 * D_MODEL
    return pl.pallas_call(
        _mod_kernel,
        grid=(DEPTH, n // tn),
        in_specs=[pl.BlockSpec((8, D_MODEL), lambda l, j: (0, 0)),
                  pl.BlockSpec((1, D_MODEL, tn), lambda l, j: (l, 0, j)),
                  pl.BlockSpec((1, 1, tn), lambda l, j: (l, 0, j))],
        out_specs=pl.BlockSpec((1, 8, tn), lambda l, j: (l, 0, j)),
        out_shape=jax.ShapeDtypeStruct((DEPTH, 8, n), F32),
        compiler_params=_cparams(("arbitrary", "arbitrary")),
        name="adaln_mod",
    )(cc, w_ada, b_ada.reshape(DEPTH, 1, n))


def _inproj_kernel(x_ref, mod_ref, g_ref, w_ref, o_ref, h_scr):
    @pl.when(pl.program_id(1) == 0)
    def _():
        x = x_ref[...]
        xn = x * lax.rsqrt(jnp.mean(x * x, axis=-1, keepdims=True) + NORM_EPS) * g_ref[...]
        h_scr[...] = (xn * (1.0 + mod_ref[0, 1:2, :]) + mod_ref[0, 0:1, :]).astype(BF16)

    o_ref[...] = jnp.dot(h_scr[...], w_ref[...], preferred_element_type=F32)


def _in_proj(x2, mod, g, w_packed, T):
    n_tok = x2.shape[0]
    tm = 1024
    tn = 1280
    bm = mod.shape[0]
    mod_idx = (lambda i, j: (i * tm // T, 0, 0)) if bm > 1 else (lambda i, j: (0, 0, 0))
    return pl.pallas_call(
        _inproj_kernel,
        grid=(n_tok // tm, PK_COLS // tn),
        in_specs=[pl.BlockSpec((tm, D_MODEL), lambda i, j: (i, 0)),
                  pl.BlockSpec((1, 8, D_MODEL), mod_idx),
                  pl.BlockSpec((1, D_MODEL), lambda i, j: (0, 0)),
                  pl.BlockSpec((D_MODEL, tn), lambda i, j: (0, j))],
        out_specs=pl.BlockSpec((tm, tn), lambda i, j: (i, j)),
        out_shape=jax.ShapeDtypeStruct((n_tok, PK_COLS), F32),
        scratch_shapes=[pltpu.VMEM((tm, D_MODEL), BF16)],
        compiler_params=_cparams(("arbitrary", "arbitrary")),
        name="in_proj",
    )(x2, mod, g, w_packed)


ML_CHUNK = 256


def _mlstm_kernel(q_ref, k_ref, v_ref, o_ref, g_ref, bias_ref, gn_ref, c0_ref, m0_ref,
                  out_ref, c_ref, m_ref, hs_ref, *, T):
    L = ML_CHUNK
    nc = T // L
    c_ref[...] = c0_ref[...]
    m_ref[...] = m0_ref[...]
    ii = lax.broadcasted_iota(jnp.int32, (L, L), 0)
    jj = lax.broadcasted_iota(jnp.int32, (L, L), 1)
    lane = lax.broadcasted_iota(jnp.int32, (1, LANES), 1)
    is_f = jnp.logical_and(lane % 8 >= 4, lane < 16)
    ones_col = (lax.broadcasted_iota(jnp.int32, (L, LANES), 1) == 0).astype(BF16)
    scale = ML_DK ** -0.5

    def chunk(ci, carry):
        ch = []
        for d in range(2):
            mask = (jj <= ii) if d == 0 else (jj >= ii)
            c = ci if d == 0 else nc - 1 - ci
            s = pl.multiple_of(c * L, L)
            gates = g_ref[0, pl.ds(s, L), :] + bias_ref[...]
            gf = jnp.where(is_f, _log_sigmoid(gates), gates)
            cum = _running_sum(mask.astype(BF16), gf)
            gf_t = gf.T
            cum_t = cum.T
            for h in range(ML_H):
                ci_, cf_ = d * 8 + h, d * 8 + 4 + h
                hs = slice(h * ML_DK, (h + 1) * ML_DK)
                ig_col, ig_row = gf[:, ci_:ci_ + 1], gf_t[ci_:ci_ + 1, :]
                b_col, b_row = cum[:, cf_:cf_ + 1], cum_t[cf_:cf_ + 1, :]
                b_last = b_col[L - 1:L, :] if d == 0 else b_col[0:1, :]
                m_prev = m_ref[0, d, h][:, 0:1]
                dmat = jnp.where(mask, b_col + (ig_row - b_row), -jnp.inf)
                m_inter = b_col + m_prev
                m_row = jnp.maximum(m_inter, jnp.max(dmat, axis=-1, keepdims=True))
                dk_col = b_last - b_col + ig_col
                m_new = jnp.maximum(b_last + m_prev, jnp.max(dk_col, axis=0, keepdims=True))
                ch.append(dict(
                    d=d, h=h, s=s, hs=hs, dmat=dmat, m_row=m_row, w_inter=jnp.exp(m_inter - m_row), m_new=m_new,
                    w_key=jnp.exp(dk_col - m_new), c_scale=jnp.exp(b_last + m_prev - m_new),
                    q=(q_ref[0, pl.ds(s, L), hs] * scale).astype(BF16), k=k_ref[0, pl.ds(s, L), hs],
                    v_aug=jnp.concatenate([v_ref[0, pl.ds(s, L), hs].astype(BF16), ones_col], axis=1),
                    c_aug=c_ref[0, d, h]))
        qk = [_dot_nt(c['q'], c['k']) for c in ch]
        qc = [_dot(c['q'], c['c_aug']) for c in ch]
        sv = [_dot(s_ * jnp.exp(c['dmat'] - c['m_row']), c['v_aug']) for s_, c in zip(qk, ch)]
        kv = [_dot_tn(c['k'] * c['w_key'], c['v_aug']) for c in ch]
        for c, qc_, sv_, kv_ in zip(ch, qc, sv, kv):
            nd = c['w_inter'] * qc_ + sv_
            num, den = nd[:, :ML_DK], nd[:, ML_DK:ML_DK + 1]
            hh = num / jnp.maximum(jnp.abs(den), jnp.exp(-c['m_row']))
            hs_ref[pl.ds(c['s'], L), c['d'] * 512 + c['h'] * ML_DK:c['d'] * 512 + (c['h'] + 1) * ML_DK] = hh
            c_ref[0, c['d'], c['h']] = c['c_scale'] * c['c_aug'] + kv_
            m_ref[0, c['d'], c['h']] = jnp.broadcast_to(c['m_new'], (1, LANES))
        return carry

    lax.fori_loop(0, nc, chunk, 0)

    def finish(ci, carry):
        s = pl.multiple_of(ci * L, L)
        for h in range(ML_H):
            hs = slice(h * ML_DK, (h + 1) * ML_DK)
            x = hs_ref[pl.ds(s, L), hs] + hs_ref[pl.ds(s, L), 512 + h * ML_DK:512 + (h + 1) * ML_DK]
            xc = x - jnp.mean(x, axis=-1, keepdims=True)
            y = xc * lax.rsqrt(jnp.mean(xc * xc, axis=-1, keepdims=True) + NORM_EPS) * gn_ref[:, hs]
            out_ref[0, pl.ds(s, L), hs] = (y * _sigmoid(o_ref[0, pl.ds(s, L), hs])).astype(BF16)
        return carry

    lax.fori_loop(0, nc, finish, 0)


def _mlstm_params(p):
    bias = jnp.pad(p['b_ml_gates'].reshape(1, -1), ((0, 0), (0, LANES - 4 * ML_H)))
    return bias, p['g_ml_norm'].reshape(1, 512)


def _mlstm_state_in(C0, n0, m0):
    c0 = jnp.concatenate([C0, n0[..., None], jnp.zeros(C0.shape[:-1] + (ML_DK - 1,), F32)], axis=-1)
    return c0, jnp.broadcast_to(m0[..., None, None], m0.shape + (1, LANES))


def _mlstm_state_zero(B):
    return jnp.zeros((B, 2, ML_H, ML_DK, 2 * ML_DK), F32), jnp.zeros((B, 2, ML_H, 1, LANES), F32)


def _mlstm_state_out(c, m):
    return c[..., :ML_DK], c[..., ML_DK], m[..., 0, 0]


def _mlstm(u3, bias, gnorm, c0, m0):
    B, T, _ = u3.shape
    blk = lambda name: pl.BlockSpec((1, T, 512), lambda b, o=PK_OFF[name] // 512: (b, 0, o))
    st_c = pl.BlockSpec((1, 2, ML_H, ML_DK, 2 * ML_DK), lambda b: (b, 0, 0, 0, 0))
    st_m = pl.BlockSpec((1, 2, ML_H, 1, LANES), lambda b: (b, 0, 0, 0, 0))
    return pl.pallas_call(
        functools.partial(_mlstm_kernel, T=T),
        grid=(B,),
        in_specs=[blk('ml_q'), blk('ml_k'), blk('ml_v'), blk('ml_o'),
                  pl.BlockSpec((1, T, LANES), lambda b: (b, 0, PK_OFF['ml_g'] // LANES)),
                  pl.BlockSpec((1, LANES), lambda b: (0, 0)),
                  pl.BlockSpec((1, 512), lambda b: (0, 0)),
                  st_c, st_m],
        out_specs=[pl.BlockSpec((1, T, 512), lambda b: (b, 0, 0)), st_c, st_m],
        out_shape=[jax.ShapeDtypeStruct((B, T, 512), BF16),
                   jax.ShapeDtypeStruct(c0.shape, F32),
                   jax.ShapeDtypeStruct(m0.shape, F32)],
        scratch_shapes=[pltpu.VMEM((T, 1024), F32)],
        compiler_params=_cparams(("arbitrary",)),
        name="mlstm",
    )(u3, u3, u3, u3, u3, bias, gnorm, c0, m0)


MLA_BLK = 256


def _rope_tables(T):
    rows = T // GRID_W
    row = np.repeat(np.arange(rows, dtype=np.float64), GRID_W)
    col = np.tile(np.arange(GRID_W, dtype=np.float64), rows)
    inv = ROPE_THETA ** (-np.arange(MLA_ROPE // 4, dtype=np.float64) / (MLA_ROPE // 4))
    ang = np.stack([row[:, None] * inv, col[:, None] * inv], axis=1)
    cos = np.stack([np.cos(ang), np.cos(ang)], axis=2).reshape(T, MLA_ROPE)
    sin = np.stack([-np.sin(ang), np.sin(ang)], axis=2).reshape(T, MLA_ROPE)
    return jnp.asarray(np.concatenate([cos, sin], axis=1), F32)


def _mla_kernel(*refs, T, n_ctx, rope):
    if n_ctx:
        (u_ref, ctx_ref, cs_ref, gql_ref, gkv_ref, gains_ref, wqn_ref, wqr_ref, wqs_ref, wkn_ref, wv_ref,
         out_ref, kv_ref, qn_s, qr_s, kn_s, kr_s, v_s) = refs
    else:
        (u_ref, cs_ref, gql_ref, gkv_ref, gains_ref, wqn_ref, wqr_ref, wqs_ref, wkn_ref, wv_ref,
         out_ref, kv_ref, qn_s, qr_s, kn_s, kr_s, v_s) = refs
    Lb = MLA_BLK
    gq_n, gq_r, gq_s = gains_ref[0:1, :], gains_ref[1:2, 0:64], gains_ref[1:2, 64:128]
    gk_n, gk_r, gk_s = gains_ref[2:3, :], gains_ref[3:4, 0:64], gains_ref[3:4, 64:128]
    sm_scale = MLA_QK ** -0.5

    def store_keys(s, kn, kr, krs, cos, sin):
        kr_ss = jnp.sum(kr * kr, axis=-1, keepdims=True)
        for h in range(MLA_H):
            kn_h = kn[:, h * 128:(h + 1) * 128]
            rk = lax.rsqrt((jnp.sum(kn_h * kn_h, axis=-1, keepdims=True) + kr_ss) / MLA_QK + NORM_EPS)
            kn_s[pl.ds(s, Lb), h * 128:(h + 1) * 128] = (kn_h * rk * gk_n).astype(BF16)
            kr_h = kr * gk_r
            if cos is not None:
                kr_h = kr_h * cos + (krs * gk_s) * sin
            kr_s[pl.ds(s, Lb), h * 64:(h + 1) * 64] = (kr_h * rk).astype(BF16)

    def prep(ci, carry):
        s = pl.multiple_of(ci * Lb, Lb)
        u = u_ref[0, pl.ds(s, Lb), :]
        ql, ckv, kr, krs = u[:, :384], u[:, 384:640], u[:, 640:704], u[:, 704:768]
        qln = ql * lax.rsqrt(jnp.mean(ql * ql, axis=-1, keepdims=True) + NORM_EPS) * gql_ref[...]
        ckvn = ckv * lax.rsqrt(jnp.mean(ckv * ckv, axis=-1, keepdims=True) + NORM_EPS) * gkv_ref[...]
        kv_ref[0, pl.ds(s, Lb), :] = jnp.concatenate([ckvn, kr], axis=1)
        cos = cs_ref[pl.ds(s, Lb), 0:64] if rope else None
        sin = cs_ref[pl.ds(s, Lb), 64:128] if rope else None
        qn, qr, qs = _dot(qln, wqn_ref[...]), _dot(qln, wqr_ref[...]), _dot(qln, wqs_ref[...])
        for h in range(MLA_H):
            qn_h, qr_h = qn[:, h * 128:(h + 1) * 128], qr[:, h * 64:(h + 1) * 64]
            ss = jnp.sum(qn_h * qn_h, axis=-1, keepdims=True) + jnp.sum(qr_h * qr_h, axis=-1, keepdims=True)
            rq = lax.rsqrt(ss / MLA_QK + NORM_EPS) * sm_scale
            qn_s[pl.ds(s, Lb), h * 128:(h + 1) * 128] = (qn_h * rq * gq_n).astype(BF16)
            qr_h = qr_h * gq_r
            if rope:
                qr_h = qr_h * cos + (qs[:, h * 64:(h + 1) * 64] * gq_s) * sin
            qr_s[pl.ds(s, Lb), h * 64:(h + 1) * 64] = (qr_h * rq).astype(BF16)
        v_s[pl.ds(s, Lb), :] = _dot(ckvn, wv_ref[...]).astype(BF16)
        store_keys(s, _dot(ckvn, wkn_ref[...]), kr, krs, cos, sin)
        return carry

    lax.fori_loop(0, T // Lb, prep, 0)

    for ci in range(n_ctx // Lb):
        cx = ctx_ref[0, ci * Lb:(ci + 1) * Lb, :]
        ckv_c, kr_c = cx[:, :KV_LORA], cx[:, KV_LORA:KV_LORA + MLA_ROPE]
        v_s[T + ci * Lb:T + (ci + 1) * Lb, :] = _dot(ckv_c, wv_ref[...]).astype(BF16)
        store_keys(T + ci * Lb, _dot(ckv_c, wkn_ref[...]), kr_c, None, None, None)

    def attend(qi, carry):
        s = pl.multiple_of(qi * Lb, Lb)
        heads = range(MLA_H)
        sc = [_dot_nt(qn_s[pl.ds(s, Lb), h * 128:(h + 1) * 128], kn_s[:, h * 128:(h + 1) * 128])
              + _dot_nt(qr_s[pl.ds(s, Lb), h * 64:(h + 1) * 64], kr_s[:, h * 64:(h + 1) * 64]) for h in heads]
        p = [jnp.exp(x - jnp.max(x, axis=-1, keepdims=True)) for x in sc]
        o = [_dot(p[h], v_s[:, h * 128:(h + 1) * 128]) / jnp.sum(p[h], axis=-1, keepdims=True) for h in heads]
        for h in heads:
            out_ref[0, pl.ds(s, Lb), h * 128:(h + 1) * 128] = o[h].astype(BF16)
        return carry

    lax.fori_loop(0, T // Lb, attend, 0)


def _mla_params(p):
    def swap(t):
        return jnp.flip(t.reshape(t.shape[:-1] + (2, 2, MLA_ROPE // 4)), axis=-2).reshape(t.shape)

    wq = p['w_mla_uq'].reshape(Q_LORA, MLA_H, MLA_QK)
    wq_n = wq[:, :, :MLA_NOPE].reshape(Q_LORA, 512).astype(BF16)
    wq_r = wq[:, :, MLA_NOPE:].reshape(Q_LORA, 256).astype(BF16)
    wq_s = swap(wq[:, :, MLA_NOPE:]).reshape(Q_LORA, 256).astype(BF16)
    wkv = p['w_mla_ukv'].reshape(KV_LORA, MLA_H, MLA_NOPE + MLA_V)
    wk_n = wkv[:, :, :MLA_NOPE].reshape(KV_LORA, 512).astype(BF16)
    wv = wkv[:, :, MLA_NOPE:].reshape(KV_LORA, 512).astype(BF16)
    gq, gk = p['g_mla_qn'], p['g_mla_kn']
    gains = jnp.concatenate([gq[:128], gq[128:], swap(gq[128:]), gk[:128], gk[128:], swap(gk[128:]),
                             jnp.zeros((4 * LANES,), F32)]).reshape(8, LANES)
    return (p['g_mla_qlat'].reshape(1, -1), p['g_mla_kvlat'].reshape(1, -1), gains, wq_n, wq_r, wq_s, wk_n, wv)


def _mla(u3, ctx_kv, mp):
    B, T, _ = u3.shape
    n_ctx = 0 if ctx_kv is None else ctx_kv.shape[1]
    rope = ctx_kv is not None
    tk = T + n_ctx
    full = lambda a: pl.BlockSpec(a.shape, lambda b, n=a.ndim: (0,) * n)
    cs = _rope_tables(T) if rope else jnp.zeros((T, LANES), F32)
    ins = [u3] + ([ctx_kv] if rope else []) + [cs] + list(mp)
    specs = [pl.BlockSpec((1, T, 768), lambda b: (b, 0, PK_OFF['mla_ql'] // 768))]
    if rope:
        specs.append(pl.BlockSpec((1, n_ctx, KV_LORA + MLA_ROPE), lambda b: (b, 0, 0)))
    specs += [full(a) for a in ins[len(specs):]]
    return pl.pallas_call(
        functools.partial(_mla_kernel, T=T, n_ctx=n_ctx, rope=rope),
        grid=(B,),
        in_specs=specs,
        out_specs=[pl.BlockSpec((1, T, 512), lambda b: (b, 0, 0)),
                   pl.BlockSpec((1, T, KV_LORA + MLA_ROPE), lambda b: (b, 0, 0))],
        out_shape=[jax.ShapeDtypeStruct((B, T, 512), BF16),
                   jax.ShapeDtypeStruct((B, T, KV_LORA + MLA_ROPE), F32)],
        scratch_shapes=[pltpu.VMEM((T, 512), BF16), pltpu.VMEM((T, 256), BF16),
                        pltpu.VMEM((tk, 512), BF16), pltpu.VMEM((tk, 256), BF16), pltpu.VMEM((tk, 512), BF16)],
        compiler_params=_cparams(("arbitrary",)),
        name="mla",
    )(*ins)


GLA_CHUNK = 64
GLA_LEAF = 4
GLA_UNROLL = 2


def _gla_kernel(q_ref, k_ref, v_ref, gd_ref, g_ref, gup_ref, gb_ref, gn_ref, hsel_ref, s0_ref,
                out_ref, s_ref, os_ref, la_ref, *, T):
    L, C = GLA_CHUNK, GLA_LEAF
    nc = T // L
    Lf = 256
    s_ref[...] = s0_ref[...]

    def gates(ci, carry):
        s = pl.multiple_of(ci * Lf, Lf)
        x = _dot_split(gd_ref[0, pl.ds(s, Lf), :], gup_ref) + gb_ref[...]
        la_ref[pl.ds(s, Lf), :] = _log_sigmoid(x) / GLA_NORMALIZER
        return carry

    lax.fori_loop(0, T // Lf, gates, 0)
    ii = lax.broadcasted_iota(jnp.int32, (L, L), 0)
    jj = lax.broadcasted_iota(jnp.int32, (L, L), 1)
    eye = (lax.broadcasted_iota(jnp.int32, (GLA_DK, GLA_DK), 0)
           == lax.broadcasted_iota(jnp.int32, (GLA_DK, GLA_DK), 1)).astype(F32)
    row_id = lax.broadcasted_iota(jnp.int32, (L, 1), 0)
    scale = GLA_DK ** -0.5
    hsel = hsel_ref[...]
    levels = []
    span = C
    while span < L:
        levels.append(span)
        span *= 2

    U = GLA_UNROLL

    def chunk(ci, carry):
        parts = []
        for d in range(2):
            causal = (jj <= ii) if d == 0 else (jj >= ii)
            tri = causal.astype(BF16)
            for u in range(U):
                c = ci * U + u if d == 0 else nc - 1 - (ci * U + u)
                s = pl.multiple_of(c * L, L)
                la = la_ref[pl.ds(s, L), d * GLA_H * GLA_DK:(d + 1) * GLA_H * GLA_DK]
                b = _running_sum(tri, la)
                total = b[L - 1:L, :] if d == 0 else b[0:1, :]
                q = q_ref[0, pl.ds(s, L), :] * scale
                k = k_ref[0, pl.ds(s, L), :]
                lv = []
                for sp in levels:
                    b3 = b.reshape(L // (2 * sp), 2 * sp, GLA_H * GLA_DK)
                    edge = b3[:, sp - 1:sp, :] if d == 0 else b3[:, sp:sp + 1, :]
                    bref = jnp.broadcast_to(edge, b3.shape).reshape(L, GLA_H * GLA_DK)
                    later = (row_id % (2 * sp) >= sp) if d == 0 else (row_id % (2 * sp) < sp)
                    e = jnp.exp(jnp.where(later, b - bref, bref - b))
                    lv.append((jnp.where(later, q * e, 0.0), jnp.where(later, 0.0, k * e),
                               (ii // (2 * sp)) == (jj // (2 * sp))))
                terms = []
                for dl in range(C):
                    if dl == 0:
                        terms.append(q * k)
                        continue
                    sh = dl if d == 0 else L - dl
                    ok = (row_id % C >= dl) if d == 0 else (row_id % C < C - dl)
                    kd = pltpu.roll(k, sh, axis=0)
                    bd = pltpu.roll(b, sh, axis=0)
                    terms.append(q * kd * jnp.exp(jnp.where(ok, b - bd, 0.0)))
                tt = jnp.concatenate(terms, axis=0)
                t_hi = tt.astype(BF16)
                t_lo = (tt - t_hi.astype(F32)).astype(BF16)
                diag = (jnp.dot(t_hi, hsel, preferred_element_type=F32)
                        + jnp.dot(t_lo, hsel, preferred_element_type=F32))
                parts.append(dict(d=d, s=s, lv=lv, diag=diag, q_in=q * jnp.exp(b), k_out=k * jnp.exp(total - b),
                                  f_row=jnp.exp(total)))
        chains = [(p, h) for p in parts for h in range(GLA_H)]
        amats = []
        for p, h in chains:
            ks = slice(h * GLA_DK, (h + 1) * GLA_DK)
            a = jnp.zeros((L, L), F32)
            for qs, kt, same in p['lv']:
                a = a + jnp.where(same, _dot_nt(qs[:, ks], kt[:, ks]), 0.0)
            for dl in range(C):
                pair = jnp.logical_and(jj == (ii - dl if p['d'] == 0 else ii + dl), ii // C == jj // C)
                a = a + jnp.where(pair, p['diag'][dl * L:(dl + 1) * L, h:h + 1], 0.0)
            amats.append(a)
        vals = [v_ref[0, pl.ds(p['s'], L), h * GLA_DV:(h + 1) * GLA_DV] for p, h in chains]
        intra = [_dot(a, v) for a, v in zip(amats, vals)]
        upd = [_dot_tn(p['k_out'][:, h * GLA_DK:(h + 1) * GLA_DK], v) for (p, h), v in zip(chains, vals)]
        for d in range(2):
            for h in range(GLA_H):
                ks = slice(h * GLA_DK, (h + 1) * GLA_DK)
                st = s_ref[0, d, h]
                for u in range(U):
                    i = (d * U + u) * GLA_H + h
                    p = parts[d * U + u]
                    o = intra[i] + _dot(p['q_in'][:, ks], st)
                    os_ref[pl.ds(p['s'], L), d * 512 + h * GLA_DV:d * 512 + (h + 1) * GLA_DV] = o
                    f_col = jnp.sum(eye * p['f_row'][:, ks], axis=1, keepdims=True)
                    st = f_col * st + upd[i]
                s_ref[0, d, h] = st
        return carry

    lax.fori_loop(0, nc // U, chunk, 0)

    def finish(ci, carry):
        s = pl.multiple_of(ci * Lf, Lf)
        for h in range(GLA_H):
            vs = slice(h * GLA_DV, (h + 1) * GLA_DV)
            o = os_ref[pl.ds(s, Lf), vs] + os_ref[pl.ds(s, Lf), 512 + h * GLA_DV:512 + (h + 1) * GLA_DV]
            y = o * lax.rsqrt(jnp.mean(o * o, axis=-1, keepdims=True) + NORM_EPS) * gn_ref[:, vs]
            g = g_ref[0, pl.ds(s, Lf), vs]
            out_ref[0, pl.ds(s, Lf), vs] = (y * (g * _sigmoid(g))).astype(BF16)
        return carry

    lax.fori_loop(0, T // Lf, finish, 0)


def _gla_params(p):
    gup = jnp.pad(jnp.concatenate([p['gla_g_up'][0], p['gla_g_up'][1]], axis=1),
                  ((0, LANES - GLA_GATE_RANK), (0, 0)))
    hsel = jnp.asarray(np.arange(GLA_H * GLA_DK)[:, None] // GLA_DK == np.arange(LANES)[None, :], BF16)
    return _split_weight(gup), p['gla_g_b'].reshape(1, -1), p['gla_norm'].reshape(1, -1), hsel


def _gla(u3, gp, s0):
    B, T, _ = u3.shape
    blk = lambda name, w: pl.BlockSpec((1, T, w), lambda b, o=PK_OFF[name] // w: (b, 0, o))
    full = lambda a: pl.BlockSpec(a.shape, lambda b, n=a.ndim: (0,) * n)
    st = pl.BlockSpec((1, 2, GLA_H, GLA_DK, GLA_DV), lambda b: (b, 0, 0, 0, 0))
    return pl.pallas_call(
        functools.partial(_gla_kernel, T=T),
        grid=(B,),
        in_specs=[blk('gla_q', 256), blk('gla_k', 256), blk('gla_v', 512), blk('gla_gd', 128), blk('gla_g', 512),
                  full(gp[0]), full(gp[1]), full(gp[2]), full(gp[3]), st],
        out_specs=[pl.BlockSpec((1, T, 512), lambda b: (b, 0, 0)), st],
        out_shape=[jax.ShapeDtypeStruct((B, T, 512), BF16), jax.ShapeDtypeStruct(s0.shape, F32)],
        scratch_shapes=[pltpu.VMEM((T, 1024), F32), pltpu.VMEM((T, 2 * GLA_H * GLA_DK), F32)],
        compiler_params=_cparams(("arbitrary",)),
        name="gla",
    )(u3, u3, u3, u3, u3, *gp, s0)


RW_CHUNK = 64
RW_UNROLL = 2


def _seg_sum(x, bd):
    hi = x.astype(BF16)
    lo = (x - hi.astype(F32)).astype(BF16)
    return jnp.dot(hi, bd, preferred_element_type=F32) + jnp.dot(lo, bd, preferred_element_type=F32)


def _rwkv_kernel(r_ref, k_ref, v_ref, wa_ref, gd_ref, wwa_ref, w0a0_ref, gup_ref, kk_ref, ka_ref, rk_ref, ln_ref,
                 bd_ref, h0_ref, out_ref, h_ref, ys_ref, pre_ref, *, T):
    L, N = RW_CHUNK, RW_N
    nc = T // L
    h_ref[...] = h0_ref[...]
    ii = lax.broadcasted_iota(jnp.int32, (L, L), 0)
    jj = lax.broadcasted_iota(jnp.int32, (L, L), 1)
    eye = (ii == jj).astype(F32)
    lane = lax.broadcasted_iota(jnp.int32, (1, LANES), 1)
    bd = bd_ref[...]
    Lf = 256

    def gates(ci, carry):
        s = pl.multiple_of(ci * Lf, Lf)
        wa = wa_ref[0, pl.ds(s, Lf), :]
        pre_ref[pl.ds(s, Lf), :] = _dot_split(jnp.where(lane < 64, jnp.tanh(wa), wa), wwa_ref) + w0a0_ref[...]
        return carry

    lax.fori_loop(0, T // Lf, gates, 0)

    masks = [((jj < ii), (jj <= ii)), ((jj > ii), (jj >= ii))]
    U = RW_UNROLL

    def chunk(ci, carry):
        chains = []
        for d in range(2):
            strict, incl = masks[d]
            tri = incl.astype(BF16)
            for u in range(U):
                c = ci * U + u if d == 0 else nc - 1 - (ci * U + u)
                s = pl.multiple_of(c * L, L)
                r = r_ref[0, pl.ds(s, L), :]
                k = k_ref[0, pl.ds(s, L), :]
                v = v_ref[0, pl.ds(s, L), :]
                pre = pre_ref[pl.ds(s, L), d * 1024:(d + 1) * 1024]
                logw = -RW_DECAY_SCALE * _sigmoid(pre[:, :512])
                a = _sigmoid(pre[:, 512:])
                kkr = k * kk_ref[...]
                kk = kkr * lax.rsqrt(_seg_sum(kkr * kkr, bd) + 1e-12)
                kt = k * (1.0 + (a - 1.0) * ka_ref[...])
                bh = kk * a
                lg = _running_sum(tri, logw)
                lg_end = lg[L - 1:L, :] if d == 0 else lg[0:1, :]
                a_t = -kk * jnp.exp(lg - logw)
                r_t = r * jnp.exp(lg)
                e_inv = jnp.exp(-lg)
                k_t, b_t = kt * e_inv, bh * e_inv
                e_end = jnp.exp(lg_end - lg)
                k_e, b_e = kt * e_end, bh * e_end
                g_end = jnp.exp(lg_end)
                for h in range(RW_H):
                    sl = slice(h * N, (h + 1) * N)
                    chains.append(dict(d=d, u=u, h=h, s=s, strict=strict, incl=incl, a=a_t[:, sl], r=r_t[:, sl],
                                       b=b_t[:, sl], k=k_t[:, sl], ke=k_e[:, sl], be=b_e[:, sl], g=g_end[:, sl],
                                       v=v[:, sl]))
        ms = [_dot_nt(jnp.concatenate([c['a'], c['r']], axis=0), jnp.concatenate([c['b'], c['k']], axis=0))
              for c in chains]
        pws = [jnp.where(c['strict'], m[:L, :L], 0.0) for c, m in zip(chains, ms)]
        xs = [eye + n for n in pws]
        for _ in range(5):
            pws = [_dot(pw, pw) for pw in pws]
            xs = [x + _dot(x, pw) for x, pw in zip(xs, pws)]
        mvs = [_dot(jnp.where(c['strict'], m[:L, L:], 0.0), c['v']) for c, m in zip(chains, ms)]
        tws = [_dot(x, jnp.concatenate([c['a'], mv], axis=1)) for x, c, mv in zip(xs, chains, mvs)]
        qys = [_dot(jnp.where(c['incl'], m[L:, :L], 0.0), tw) for c, m, tw in zip(chains, ms, tws)]
        ylocs = [_dot(jnp.where(c['incl'], m[L:, L:], 0.0), c['v']) + qy[:, N:] for c, m, qy in zip(chains, ms, qys)]
        pgs = [_dot_tn(c['be'], tw) for c, tw in zip(chains, tws)]
        gmats = [_dot_tn(c['ke'], c['v']) + pg[:, N:] for c, pg in zip(chains, pgs)]
        for d in range(2):
            for h in range(RW_H):
                hst = h_ref[0, d, h]
                for u in range(U):
                    i = (d * U + u) * RW_H + h
                    c = chains[i]
                    y = _dot(c['r'] + qys[i][:, :N], hst) + ylocs[i]
                    ys_ref[pl.ds(c['s'], L), d * 512 + h * N:d * 512 + (h + 1) * N] = y
                    hst = _dot(eye * c['g'] + pgs[i][:, :N], hst) + gmats[i]
                h_ref[0, d, h] = hst
        return carry

    lax.fori_loop(0, nc // U, chunk, 0)

    def finish(ci, carry):
        s = pl.multiple_of(ci * Lf, Lf)
        r = r_ref[0, pl.ds(s, Lf), :]
        k = k_ref[0, pl.ds(s, Lf), :]
        rk = r * k * rk_ref[...]
        bonus = jnp.zeros((Lf, 512), F32)
        for d in range(2):
            a = _sigmoid(pre_ref[pl.ds(s, Lf), d * 1024 + 512:(d + 1) * 1024])
            bonus = bonus + _seg_sum(rk * (1.0 + (a - 1.0) * ka_ref[...]), bd)
        y = ys_ref[pl.ds(s, Lf), 0:512] + ys_ref[pl.ds(s, Lf), 512:1024]
        yc = y - _seg_sum(y, bd) / N
        yn = yc * lax.rsqrt(_seg_sum(yc * yc, bd) / N + RW_LN_EPS) * ln_ref[...]
        g = _dot(_sigmoid(gd_ref[0, pl.ds(s, Lf), :]), gup_ref[...])
        out_ref[0, pl.ds(s, Lf), :] = ((yn + bonus * v_ref[0, pl.ds(s, Lf), :]) * g).astype(BF16)
        return carry

    lax.fori_loop(0, T // Lf, finish, 0)


def _rwkv_params(p):
    z = jnp.zeros((64, 512), F32)
    wwa = jnp.concatenate([jnp.concatenate([p['rw_w_up'][0], z, p['rw_w_up'][1], z], axis=1),
                           jnp.concatenate([z, p['rw_a_up'][0], z, p['rw_a_up'][1]], axis=1)], axis=0)
    w0a0 = jnp.concatenate([p['rw_w0'][0], p['rw_a0'][0], p['rw_w0'][1], p['rw_a0'][1]]).reshape(1, 2048)
    seg = np.arange(512) // RW_N
    bd = jnp.asarray(seg[:, None] == seg[None, :], BF16)
    row = lambda n: p[n].reshape(1, -1)
    return (_split_weight(wwa), w0a0, p['rw_g_up'].astype(BF16), row('rw_k_k'), row('rw_k_a'), row('rw_r_k'),
            row('rw_ln'), bd)


def _rwkv(u3, rp, h0):
    B, T, _ = u3.shape
    blk = lambda name, w: pl.BlockSpec((1, T, w), lambda b, o=PK_OFF[name] // w: (b, 0, o))
    full = lambda a: pl.BlockSpec(a.shape, lambda b, n=a.ndim: (0,) * n)
    st = pl.BlockSpec((1, 2, RW_H, RW_N, RW_N), lambda b: (b, 0, 0, 0, 0))
    return pl.pallas_call(
        functools.partial(_rwkv_kernel, T=T),
        grid=(B,),
        in_specs=[blk('rw_r', 512), blk('rw_k', 512), blk('rw_v', 512), blk('rw_wd', 128), blk('rw_gd', 128)]
                 + [full(a) for a in rp] + [st],
        out_specs=[pl.BlockSpec((1, T, 512), lambda b: (b, 0, 0)), st],
        out_shape=[jax.ShapeDtypeStruct((B, T, 512), BF16), jax.ShapeDtypeStruct(h0.shape, F32)],
        scratch_shapes=[pltpu.VMEM((T, 1024), F32), pltpu.VMEM((T, 2048), F32)],
        compiler_params=_cparams(("arbitrary",)),
        name="rwkv7",
    )(u3, u3, u3, u3, u3, *rp, h0)


def _route(logits):
    lane = lax.broadcasted_iota(jnp.int32, (1, LANES), 1)
    far = jnp.int32(2 * LANES)
    neg = -jnp.inf
    gl = jnp.where(jnp.logical_and(lane >= N_EXPERTS, lane < N_EXPERTS + N_GROUPS), logits, neg)
    gmax = jnp.max(gl, axis=-1, keepdims=True)
    grp = jnp.min(jnp.where(gl == gmax, lane, far), axis=-1, keepdims=True) - N_EXPERTS
    p_grp = 1.0 / jnp.sum(jnp.exp(gl - gmax), axis=-1, keepdims=True)
    el = jnp.where(jnp.logical_and(lane < N_EXPERTS, lane // EXPERTS_PER_GROUP == grp), logits, neg)
    v1 = jnp.max(el, axis=-1, keepdims=True)
    i1 = jnp.min(jnp.where(el == v1, lane, far), axis=-1, keepdims=True)
    el2 = jnp.where(lane == i1, neg, el)
    v2 = jnp.max(el2, axis=-1, keepdims=True)
    i2 = jnp.min(jnp.where(el2 == v2, lane, far), axis=-1, keepdims=True)
    e = jnp.exp(v2 - v1)
    w1 = 1.0 / (1.0 + e)
    comb = jnp.where(lane == i1, p_grp * w1, jnp.where(lane == i2, p_grp * (e * w1), 0.0))
    sel = jnp.where(lane == i1, 1.0, jnp.where(lane == i2, 2.0, 0.0))
    return comb, sel


def _outproj_kernel(x_ref, m0_ref, m1_ref, m2_ref, m3_ref, w_ref, mod_ref, g_ref, wr_ref, br_ref,
                    xn_ref, h3_ref, comb_ref, sel_ref):
    y = jnp.dot(m0_ref[...], w_ref[0:512, :], preferred_element_type=F32)
    for i, m_ref in enumerate((m1_ref, m2_ref, m3_ref), start=1):
        y = y + jnp.dot(m_ref[...], w_ref[i * 512:(i + 1) * 512, :], preferred_element_type=F32)
    xn = x_ref[...] + mod_ref[0, 2:3, :] * y
    xn_ref[...] = xn
    h = xn * lax.rsqrt(jnp.mean(xn * xn, axis=-1, keepdims=True) + NORM_EPS) * g_ref[...]
    h = h * (1.0 + mod_ref[0, 4:5, :]) + mod_ref[0, 3:4, :]
    tm = h.shape[0]
    for c in range(ROW_TILES):
        h3_ref[pl.ds(c, tm, stride=ROW_TILES), :] = h[:, c * LANES:(c + 1) * LANES]
    comb_ref[...], sel_ref[...] = _route(_dot_split(h, wr_ref) + br_ref[...])


def _out_proj(x2, mixed, w_out, mod, g, router_w, router_b, T):
    n_tok = x2.shape[0]
    tm = 512
    bm = mod.shape[0]
    mod_idx = (lambda i: (i * tm // T, 0, 0)) if bm > 1 else (lambda i: (0, 0, 0))
    row = lambda w: pl.BlockSpec((tm, w), lambda i: (i, 0))
    full = lambda a: pl.BlockSpec(a.shape, lambda i, n=a.ndim: (0,) * n)
    return pl.pallas_call(
        _outproj_kernel,
        grid=(n_tok // tm,),
        in_specs=[row(D_MODEL)] + [row(GROUP_W)] * 4 + [full(w_out), pl.BlockSpec((1, 8, D_MODEL), mod_idx),
                                                       full(g), full(router_w), full(router_b)],
        out_specs=[row(D_MODEL), pl.BlockSpec((tm * ROW_TILES, LANES), lambda i: (i, 0)), row(LANES), row(LANES)],
        out_shape=[jax.ShapeDtypeStruct((n_tok, D_MODEL), F32),
                   jax.ShapeDtypeStruct((n_tok * ROW_TILES, LANES), F32),
                   jax.ShapeDtypeStruct((n_tok, LANES), F32), jax.ShapeDtypeStruct((n_tok, LANES), F32)],
        compiler_params=_cparams(("arbitrary",)),
        name="out_proj",
    )(x2, *mixed, w_out, mod, g, router_w, router_b)


MOE_TM = 512
MOE_ROWS = 256
MOE_DMA_UNROLL = 8


def _moe_rows(n_tok):
    return 2 * n_tok + N_EXPERTS * MOE_TM


def _plan_kernel(sel_ref, pos_ref, tmap_ref):
    n_tok = sel_ref.shape[0]
    blk = 512
    lane = lax.broadcasted_iota(jnp.int32, (1, LANES), 1)
    earlier = (lax.broadcasted_iota(jnp.int32, (blk, blk), 1)
               < lax.broadcasted_iota(jnp.int32, (blk, blk), 0)).astype(BF16)
    before = (lax.broadcasted_iota(jnp.int32, (LANES, LANES), 0)
              < lax.broadcasted_iota(jnp.int32, (LANES, LANES), 1)).astype(BF16)

    def count(i, acc):
        s = pl.multiple_of(i * blk, blk)
        return acc + jnp.sum((sel_ref[pl.ds(s, blk), :] > 0.0).astype(F32), axis=0, keepdims=True)

    counts = lax.fori_loop(0, n_tok // blk, count, jnp.zeros((1, LANES), F32))
    tiles = jnp.floor((counts + (MOE_TM - 1)) * (1.0 / MOE_TM))
    tile_start = _dot(jnp.broadcast_to(tiles, (8, LANES)), before)[0:1, :]
    tile_end = tile_start + tiles
    base = tile_start * MOE_TM
    n_tiles = jnp.sum(tiles, axis=-1, keepdims=True)
    j = lax.broadcasted_iota(jnp.int32, (tmap_ref.shape[0], 1), 0).astype(F32)
    done = jnp.logical_and(tile_end <= j, lane < N_EXPERTS)
    expert = jnp.minimum(jnp.sum(done.astype(F32), axis=-1, keepdims=True), N_EXPERTS - 1.0)
    valid = (j < n_tiles).astype(F32)
    tmap_ref[...] = jnp.where(lane == 0, expert, jnp.where(lane == 1, valid, 0.0)).astype(jnp.int32)

    def place(i, seen):
        s = pl.multiple_of(i * blk, blk)
        sel = sel_ref[pl.ds(s, blk), :]
        one = (sel > 0.0).astype(F32)
        row = base + seen + _dot(earlier, one)
        p1 = jnp.sum(jnp.where(sel == 1.0, row, 0.0), axis=-1, keepdims=True)
        p2 = jnp.sum(jnp.where(sel == 2.0, row, 0.0), axis=-1, keepdims=True)
        pos_ref[pl.ds(s, blk), :] = jnp.where(lane == 0, p1, jnp.where(lane == 1, p2, 0.0)).astype(jnp.int32)
        return seen + jnp.sum(one, axis=0, keepdims=True)

    lax.fori_loop(0, n_tok // blk, place, jnp.zeros((1, LANES), F32))


def _moe_plan(sel):
    n_tok = sel.shape[0]
    n_tiles = _moe_rows(n_tok) // MOE_TM
    pos, tmap = pl.pallas_call(
        _plan_kernel,
        out_shape=[jax.ShapeDtypeStruct((n_tok, LANES), jnp.int32),
                   jax.ShapeDtypeStruct((-(-n_tiles // 8) * 8, LANES), jnp.int32)],
        compiler_params=pltpu.CompilerParams(vmem_limit_bytes=VMEM_LIMIT),
        name="moe_plan",
    )(sel)
    return jnp.transpose(pos[:, :2]), tmap[:n_tiles, 0], tmap[:n_tiles, 1]


def _dispatch_kernel(pos_ref, src_ref, init_ref, dst_ref, sem):
    del init_ref
    base = pl.program_id(0) * MOE_ROWS

    def copy(j, k):
        return pltpu.make_async_copy(src_ref.at[j], dst_ref.at[pos_ref[k, base + j]], sem)

    def start(j, c):
        copy(j, 0).start()
        copy(j, 1).start()
        return c

    def wait(j, c):
        copy(j, 0).wait()
        copy(j, 1).wait()
        return c

    lax.fori_loop(0, MOE_ROWS, start, 0, unroll=MOE_DMA_UNROLL)
    lax.fori_loop(0, MOE_ROWS, wait, 0, unroll=MOE_DMA_UNROLL)


def _dispatch(pos, h3, init):
    n_tok = h3.shape[0]
    any_spec = pl.BlockSpec(memory_space=pl.ANY)
    return pl.pallas_call(
        _dispatch_kernel,
        grid_spec=pltpu.PrefetchScalarGridSpec(
            num_scalar_prefetch=1, grid=(n_tok // MOE_ROWS,),
            in_specs=[pl.BlockSpec((MOE_ROWS, ROW_TILES, LANES), lambda i, p: (i, 0, 0)), any_spec],
            out_specs=any_spec, scratch_shapes=[pltpu.SemaphoreType.DMA(())]),
        out_shape=jax.ShapeDtypeStruct(init.shape, init.dtype),
        input_output_aliases={2: 0},
        compiler_params=pltpu.CompilerParams(dimension_semantics=("arbitrary",)),
        name="moe_dispatch",
    )(pos, h3, init)


def _experts_kernel(te_ref, tv_ref, xs_ref, wg_ref, wu_ref, wd_ref, ys_ref, wg_s, wu_s, wd_s):
    i = pl.program_id(0)
    fresh = jnp.logical_or(i == 0, te_ref[i] != te_ref[jnp.maximum(i - 1, 0)])

    @pl.when(jnp.logical_and(fresh, tv_ref[i] == 1))
    def _():
        wg_s[...] = wg_ref[0, 0].astype(BF16)
        wu_s[...] = wu_ref[0, 0].astype(BF16)
        wd_s[...] = wd_ref[0, 0].astype(BF16)

    @pl.when(tv_ref[i] == 1)
    def _():
        x = jnp.concatenate([xs_ref[pl.ds(c, MOE_TM, stride=ROW_TILES), :] for c in range(ROW_TILES)],
                            axis=1).astype(BF16)
        a = jnp.dot(x, wg_s[...], preferred_element_type=F32)
        b = jnp.dot(x, wu_s[...], preferred_element_type=F32)
        y = jnp.dot(((a * _sigmoid(a)) * b).astype(BF16), wd_s[...], preferred_element_type=F32)
        for c in range(ROW_TILES):
            ys_ref[pl.ds(c, MOE_TM, stride=ROW_TILES), :] = y[:, c * LANES:(c + 1) * LANES]

    @pl.when(tv_ref[i] == 0)
    def _():
        ys_ref[...] = jnp.zeros_like(ys_ref)


def _experts(tile_expert, tile_valid, xs, wg, wu, wd, layer):
    n_rows = xs.shape[0] // ROW_TILES
    rows = pl.BlockSpec((MOE_TM * ROW_TILES, LANES), lambda i, te, tv: (i, 0))
    return pl.pallas_call(
        _experts_kernel,
        grid_spec=pltpu.PrefetchScalarGridSpec(
            num_scalar_prefetch=2, grid=(n_rows // MOE_TM,),
            in_specs=[rows,
                      pl.BlockSpec((1, 1, D_MODEL, EXPERT_HIDDEN), lambda i, te, tv: (layer, te[i], 0, 0)),
                      pl.BlockSpec((1, 1, D_MODEL, EXPERT_HIDDEN), lambda i, te, tv: (layer, te[i], 0, 0)),
                      pl.BlockSpec((1, 1, EXPERT_HIDDEN, D_MODEL), lambda i, te, tv: (layer, te[i], 0, 0))],
            out_specs=rows,
            scratch_shapes=[pltpu.VMEM((D_MODEL, EXPERT_HIDDEN), BF16), pltpu.VMEM((D_MODEL, EXPERT_HIDDEN), BF16),
                            pltpu.VMEM((EXPERT_HIDDEN, D_MODEL), BF16)]),
        out_shape=jax.ShapeDtypeStruct(xs.shape, F32),
        compiler_params=_cparams(("arbitrary",)),
        name="moe_experts",
    )(tile_expert, tile_valid, xs, wg, wu, wd)


def _combine_kernel(pos_ref, ys_ref, comb_ref, sel_ref, xn_ref, mod_ref, o_ref, y_s, sems):
    i = pl.program_id(0)
    slot_rows = 2 * MOE_ROWS * ROW_TILES

    def copy(tile, slot, j, k):
        dst = pl.multiple_of(slot * slot_rows + (k * MOE_ROWS + j) * ROW_TILES, ROW_TILES)
        return pltpu.make_async_copy(ys_ref.at[pos_ref[k, tile * MOE_ROWS + j]], y_s.at[pl.ds(dst, ROW_TILES), :],
                                     sems.at[slot])

    def start_tile(tile, slot):
        def body(j, c):
            copy(tile, slot, j, 0).start()
            copy(tile, slot, j, 1).start()
            return c
        lax.fori_loop(0, MOE_ROWS, body, 0, unroll=MOE_DMA_UNROLL)

    def wait_tile(tile, slot):
        def body(j, c):
            copy(tile, slot, j, 0).wait()
            copy(tile, slot, j, 1).wait()
            return c
        lax.fori_loop(0, MOE_ROWS, body, 0, unroll=MOE_DMA_UNROLL)

    @pl.when(i == 0)
    def _():
        start_tile(0, 0)

    @pl.when(i + 1 < pl.num_programs(0))
    def _():
        start_tile(i + 1, (i + 1) % 2)

    comb, sel = comb_ref[...], sel_ref[...]
    w1 = jnp.sum(jnp.where(sel == 1.0, comb, 0.0), axis=-1, keepdims=True)
    w2 = jnp.sum(jnp.where(sel == 2.0, comb, 0.0), axis=-1, keepdims=True)
    slot = i % 2
    wait_tile(i, slot)
    for c in range(ROW_TILES):
        cs = slice(c * LANES, (c + 1) * LANES)
        y1 = y_s[pl.ds(slot * slot_rows + c, MOE_ROWS, stride=ROW_TILES), :]
        y2 = y_s[pl.ds(slot * slot_rows + MOE_ROWS * ROW_TILES + c, MOE_ROWS, stride=ROW_TILES), :]
        o_ref[:, cs] = xn_ref[:, cs] + mod_ref[0, 5:6, cs] * (w1 * y1 + w2 * y2)


def _combine(pos, ys, comb, sel, xn, mod, T):
    n_tok = xn.shape[0]
    tm = MOE_ROWS
    bm = mod.shape[0]
    mod_idx = (lambda i, p: (i * tm // T, 0, 0)) if bm > 1 else (lambda i, p: (0, 0, 0))
    row = lambda w: pl.BlockSpec((tm, w), lambda i, p: (i, 0))
    return pl.pallas_call(
        _combine_kernel,
        grid_spec=pltpu.PrefetchScalarGridSpec(
            num_scalar_prefetch=1, grid=(n_tok // tm,),
            in_specs=[pl.BlockSpec(memory_space=pl.ANY), row(LANES), row(LANES), row(D_MODEL),
                      pl.BlockSpec((1, 8, D_MODEL), mod_idx)],
            out_specs=row(D_MODEL),
            scratch_shapes=[pltpu.VMEM((2 * 2 * tm * ROW_TILES, LANES), F32), pltpu.SemaphoreType.DMA((2,))]),
        out_shape=jax.ShapeDtypeStruct((n_tok, D_MODEL), F32),
        compiler_params=_cparams(("arbitrary",)),
        name="moe_combine",
    )(pos, ys, comb, sel, xn, mod)


def _moe(groups, wg, wu, wd, layer, sorted_buf=None):
    sizes = [g[3].shape[0] for g in groups]
    n_rows = _moe_rows(sum(sizes))
    pos, tile_expert, tile_valid = _moe_plan(jnp.concatenate([g[2] for g in groups], axis=0))
    xs = jnp.zeros((n_rows, ROW_TILES, LANES), F32) if sorted_buf is None else sorted_buf
    start = 0
    for (h3, _, _, _, _, _), n in zip(groups, sizes):
        xs = _dispatch(pos[:, start:start + n], h3.reshape(n, ROW_TILES, LANES), xs)
        start += n
    ys = _experts(tile_expert, tile_valid, xs.reshape(n_rows * ROW_TILES, LANES), wg, wu, wd, layer)
    ys = ys.reshape(n_rows, ROW_TILES, LANES)
    outs, start = [], 0
    for (_, comb, sel, xn, mod, T), n in zip(groups, sizes):
        outs.append(_combine(pos[:, start:start + n], ys, comb, sel, xn, mod, T))
        start += n
    return outs, xs


def kernel(x_prompt, x_sample, cache_mla, state_mlstm_C, state_mlstm_n, state_mlstm_m, state_rwkv, state_gla, c, c_ctx, w_ada, b_ada, g_mix, g_ffn, w_in, w_out, b_ml_gates, g_ml_norm, g_mla_qlat, g_mla_kvlat, w_mla_uq, w_mla_ukv, g_mla_qn, g_mla_kn, rw_w0, rw_w_up, rw_a0, rw_a_up, rw_g_up, rw_k_k, rw_k_a, rw_r_k, rw_ln, gla_g_up, gla_g_b, gla_norm, moe_w_rg, moe_b_rg, moe_w_re, moe_b_re, moe_w_gate, moe_w_up, moe_w_down):
    cc = jnp.concatenate([c_ctx[None], c, jnp.zeros((3, D_MODEL), F32)], axis=0)
    mod = _modulation(cc, w_ada, b_ada).reshape(DEPTH, 8, 6, D_MODEL)
    mod = jnp.pad(mod, ((0, 0), (0, 0), (0, 2), (0, 0)))

    layers = []
    for l in range(DEPTH):
        p = {'b_ml_gates': b_ml_gates[l], 'g_ml_norm': g_ml_norm[l], 'g_mla_qlat': g_mla_qlat[l],
             'g_mla_kvlat': g_mla_kvlat[l], 'w_mla_uq': w_mla_uq[l], 'w_mla_ukv': w_mla_ukv[l],
             'g_mla_qn': g_mla_qn[l], 'g_mla_kn': g_mla_kn[l], 'rw_w0': rw_w0[l], 'rw_w_up': rw_w_up[l],
             'rw_a0': rw_a0[l], 'rw_a_up': rw_a_up[l], 'rw_g_up': rw_g_up[l], 'rw_k_k': rw_k_k[l],
             'rw_k_a': rw_k_a[l], 'rw_r_k': rw_r_k[l], 'rw_ln': rw_ln[l], 'gla_g_up': gla_g_up[l],
             'gla_g_b': gla_g_b[l], 'gla_norm': gla_norm[l]}
        lane_pad = LANES - N_EXPERTS - N_GROUPS
        router_w = jnp.concatenate([moe_w_re[l], moe_w_rg[l], jnp.zeros((D_MODEL, lane_pad), F32)], axis=1)
        router_b = jnp.concatenate([moe_b_re[l], moe_b_rg[l], jnp.zeros((lane_pad,), F32)]).reshape(1, LANES)
        layers.append(dict(
            w_in=_pack_w_in(w_in[l]), g_mix=g_mix[l].reshape(1, -1), g_ffn=g_ffn[l].reshape(1, -1),
            w_out=w_out[l].astype(BF16), ml=_mlstm_params(p), mla=_mla_params(p), rw=_rwkv_params(p),
            gla=_gla_params(p), router_w=_split_weight(router_w), router_b=router_b,
            layer=l))

    def mix(x2, B, T, mod_g, lp, ctx):
        u3 = _in_proj(x2, mod_g, lp['g_mix'], lp['w_in'], T).reshape(B, T, PK_COLS)
        if ctx is None:
            ctx_kv = None
            ml_c0, ml_m0 = _mlstm_state_zero(B)
            rw_h0 = jnp.zeros((B, 2, RW_H, RW_N, RW_N), F32)
            gla_s0 = jnp.zeros((B, 2, GLA_H, GLA_DK, GLA_DV), F32)
        else:
            ctx_kv, ml_C0, ml_n0, ml_m0_, rw_S0, gla_s0 = ctx
            ml_c0, ml_m0 = _mlstm_state_in(ml_C0, ml_n0, ml_m0_)
            rw_h0 = jnp.swapaxes(rw_S0, -1, -2)
        ml_out, ml_c, ml_m = _mlstm(u3, *lp['ml'], ml_c0, ml_m0)
        mla_out, own_kv = _mla(u3, ctx_kv, lp['mla'])
        rw_out, rw_h = _rwkv(u3, lp['rw'], rw_h0)
        gla_out, gla_s = _gla(u3, lp['gla'], gla_s0)
        mixed = [t.reshape(B * T, GROUP_W) for t in (ml_out, mla_out, rw_out, gla_out)]
        xn, h3, comb, sel = _out_proj(x2, mixed, lp['w_out'], mod_g, lp['g_ffn'], lp['router_w'], lp['router_b'], T)
        ml_C, ml_n, ml_mm = _mlstm_state_out(ml_c, ml_m)
        return (h3, comb, sel, xn, mod_g, T), (own_kv, ml_C, ml_n, ml_mm, jnp.swapaxes(rw_h, -1, -2), gla_s)

    Bp, Tp = x_prompt.shape[:2]
    Bs, Ts = x_sample.shape[:2]
    xp = x_prompt.reshape(Bp * Tp, D_MODEL)
    xs = x_sample.reshape(Bs * Ts, D_MODEL)
    ctx_states = []
    sorted_buf = None
    for l in range(DEPTH):
        ctx = (cache_mla[:, l], state_mlstm_C[:, l], state_mlstm_n[:, l], state_mlstm_m[:, l],
               state_rwkv[:, l], state_gla[:, l])
        moe_p, st = mix(xp, Bp, Tp, mod[l, 0:1], layers[l], None)
        moe_s, _ = mix(xs, Bs, Ts, mod[l, 1:1 + Bs], layers[l], ctx)
        ctx_states.append(st)
        (xp, xs), sorted_buf = _moe([moe_p, moe_s], moe_w_gate, moe_w_up, moe_w_down, l, sorted_buf)
    outs = [jnp.stack([s[i] for s in ctx_states], axis=1) for i in range(6)]
    return (xp.reshape(x_prompt.shape), xs.reshape(x_sample.shape), *outs)
```

```python
import functools
import math

import numpy as np
import jax
import jax.numpy as jnp
from jax import lax
from jax.experimental import pallas as pl
from jax.experimental.pallas import tpu as pltpu

F32 = jnp.float32
BF16 = jnp.bfloat16

D_MODEL = 2048
DEPTH = 2
GRID_W = 64
GROUP_W = 512
ML_H, ML_DK = 4, 128
MLA_H, MLA_NOPE, MLA_ROPE, MLA_V = 4, 128, 64, 128
MLA_QK = MLA_NOPE + MLA_ROPE
Q_LORA, KV_LORA = 384, 256
ROPE_THETA = 10000.0
RW_H, RW_N = 8, 64
RW_DECAY_SCALE = math.exp(-0.5)
RW_LN_EPS = 64e-5
GLA_H, GLA_DK, GLA_DV = 4, 64, 128
GLA_GATE_RANK = 16
GLA_NORMALIZER = 16.0
N_GROUPS, EXPERTS_PER_GROUP, N_EXPERTS = 4, 4, 16
EXPERT_HIDDEN = 512
NORM_EPS = 1e-6
LANES = 128
ROW_TILES = D_MODEL // LANES
VMEM_LIMIT = 56 * 1024 * 1024

_REF_SPLITS = (
    ('ml_q', 512), ('ml_k', 512), ('ml_v', 512), ('ml_o', 512), ('ml_g', 16),
    ('mla_ql', Q_LORA), ('mla_ckv', KV_LORA), ('mla_kr', MLA_ROPE),
    ('rw_r', 512), ('rw_k', 512), ('rw_v', 512), ('rw_wd', 64), ('rw_ad', 64), ('rw_gd', 128),
    ('gla_q', 256), ('gla_k', 256), ('gla_v', 512), ('gla_gd', GLA_GATE_RANK), ('gla_g', 512),
)
_REF_OFF = {}
_o = 0
for _n, _w in _REF_SPLITS:
    _REF_OFF[_n] = (_o, _w)
    _o += _w
IN_COLS = _o

_PACKED = (
    ('ml_q', 512), ('ml_k', 512), ('ml_v', 512), ('ml_o', 512),
    ('rw_r', 512), ('rw_k', 512), ('rw_v', 512), ('gla_v', 512), ('gla_g', 512),
    ('mla_ql', 384), ('mla_ckv', 256), ('mla_kr', 64), ('mla_kr_sw', 64),
    ('gla_q', 256), ('gla_k', 256),
    ('ml_g', 128), ('rw_wd', 64), ('rw_ad', 64), ('rw_gd', 128), ('gla_gd', 128),
)
PK_OFF = {}
_o = 0
for _n, _w in _PACKED:
    PK_OFF[_n] = _o
    _o += _w
PK_COLS = _o


def _rope_swap_perm():
    idx = np.arange(MLA_ROPE)
    axis, half, f = idx // 32, (idx % 32) // 16, idx % 16
    return axis * 32 + (1 - half) * 16 + f


def _packed_column_index():
    src = np.full((PK_COLS,), -1, np.int64)
    for name, width in _PACKED:
        off = PK_OFF[name]
        if name == 'mla_kr_sw':
            s, w = _REF_OFF['mla_kr']
            src[off:off + w] = s + _rope_swap_perm()
        else:
            s, w = _REF_OFF[name]
            src[off:off + w] = s + np.arange(w)
    return src


_PK_SRC = _packed_column_index()


def _column_runs(src):
    runs, i = [], 0
    while i < len(src):
        j = i + 1
        while j < len(src) and ((src[i] < 0 and src[j] < 0) or (src[i] >= 0 and src[j] == src[i] + (j - i))):
            j += 1
        runs.append((int(src[i]), j - i))
        i = j
    return runs


_PK_RUNS = _column_runs(_PK_SRC)


def _pack_w_in(w):
    parts = [w[:, s:s + n] if s >= 0 else jnp.zeros((w.shape[0], n), w.dtype) for s, n in _PK_RUNS]
    return jnp.concatenate(parts, axis=1).astype(BF16)


def _cparams(sem):
    return pltpu.CompilerParams(dimension_semantics=sem, vmem_limit_bytes=VMEM_LIMIT)


def _log_sigmoid(x):
    return jnp.minimum(x, 0.0) - jnp.log(1.0 + jnp.exp(-jnp.abs(x)))


def _sigmoid(x):
    return 1.0 / (1.0 + jnp.exp(-x))


def _dot(a, b):
    return jnp.dot(a.astype(BF16), b.astype(BF16), preferred_element_type=F32)


def _dot_nt(a, b):
    return lax.dot_general(a.astype(BF16), b.astype(BF16), (((1,), (1,)), ((), ())), preferred_element_type=F32)


def _dot_tn(a, b):
    return lax.dot_general(a.astype(BF16), b.astype(BF16), (((0,), (0,)), ((), ())), preferred_element_type=F32)


def _split2(x):
    hi = x.astype(BF16)
    return hi, (x - hi.astype(F32)).astype(BF16)


def _split_weight(w):
    hi, lo = _split2(w)
    return jnp.stack([hi, lo])


def _running_sum(tri, x):
    hi = x.astype(BF16)
    r = x - hi.astype(F32)
    mid = r.astype(BF16)
    lo = (r - mid.astype(F32)).astype(BF16)
    n = x.shape[1]
    s = jnp.dot(tri, jnp.concatenate([hi, mid, lo], axis=1), preferred_element_type=F32)
    return s[:, :n] + s[:, n:2 * n] + s[:, 2 * n:]


def _dot_split(a, w_ref):
    a_hi, a_lo = _split2(a)
    w_hi = w_ref[0]
    return (jnp.dot(a_hi, w_hi, preferred_element_type=F32) + jnp.dot(a_lo, w_hi, preferred_element_type=F32)
            + jnp.dot(a_hi, w_ref[1], preferred_element_type=F32))


def _mod_kernel(c_ref, w_ref, b_ref, o_ref):
    c = c_ref[...]
    s = c * _sigmoid(c)
    o_ref[0] = _dot(s, w_ref[0]) + b_ref[0]


def _modulation(cc, w_ada, b_ada):
    tn = 1536
    n = 6 * D_MODEL
    return pl.pallas_call(
        _mod_kernel,
        grid=(DEPTH, n // tn),
        in_specs=[pl.BlockSpec((8, D_MODEL), lambda l, j: (0, 0)),
                  pl.BlockSpec((1, D_MODEL, tn), lambda l, j: (l, 0, j)),
                  pl.BlockSpec((1, 1, tn), lambda l, j: (l, 0, j))],
        out_specs=pl.BlockSpec((1, 8, tn), lambda l, j: (l, 0, j)),
        out_shape=jax.ShapeDtypeStruct((DEPTH, 8, n), F32),
        compiler_params=_cparams(("arbitrary", "arbitrary")),
        name="adaln_mod",
    )(cc, w_ada, b_ada.reshape(DEPTH, 1, n))


def _inproj_kernel(x_ref, mod_ref, g_ref, w_ref, o_ref, h_scr):
    @pl.when(pl.program_id(1) == 0)
    def _():
        x = x_ref[...]
        xn = x * lax.rsqrt(jnp.mean(x * x, axis=-1, keepdims=True) + NORM_EPS) * g_ref[...]
        h_scr[...] = (xn * (1.0 + mod_ref[0, 1:2, :]) + mod_ref[0, 0:1, :]).astype(BF16)

    o_ref[...] = jnp.dot(h_scr[...], w_ref[...], preferred_element_type=F32)


def _in_proj(x2, mod, g, w_packed, T):
    n_tok = x2.shape[0]
    tm = 1024
    tn = 1280
    bm = mod.shape[0]
    mod_idx = (lambda i, j: (i * tm // T, 0, 0)) if bm > 1 else (lambda i, j: (0, 0, 0))
    return pl.pallas_call(
        _inproj_kernel,
        grid=(n_tok // tm, PK_COLS // tn),
        in_specs=[pl.BlockSpec((tm, D_MODEL), lambda i, j: (i, 0)),
                  pl.BlockSpec((1, 8, D_MODEL), mod_idx),
                  pl.BlockSpec((1, D_MODEL), lambda i, j: (0, 0)),
                  pl.BlockSpec((D_MODEL, tn), lambda i, j: (0, j))],
        out_specs=pl.BlockSpec((tm, tn), lambda i, j: (i, j)),
        out_shape=jax.ShapeDtypeStruct((n_tok, PK_COLS), F32),
        scratch_shapes=[pltpu.VMEM((tm, D_MODEL), BF16)],
        compiler_params=_cparams(("arbitrary", "arbitrary")),
        name="in_proj",
    )(x2, mod, g, w_packed)


ML_CHUNK = 256


def _mlstm_kernel(q_ref, k_ref, v_ref, o_ref, g_ref, bias_ref, gn_ref, c0_ref, m0_ref,
                  out_ref, c_ref, m_ref, hs_ref, *, T):
    L = ML_CHUNK
    nc = T // L
    c_ref[...] = c0_ref[...]
    m_ref[...] = m0_ref[...]
    ii = lax.broadcasted_iota(jnp.int32, (L, L), 0)
    jj = lax.broadcasted_iota(jnp.int32, (L, L), 1)
    lane = lax.broadcasted_iota(jnp.int32, (1, LANES), 1)
    is_f = jnp.logical_and(lane % 8 >= 4, lane < 16)
    ones_col = (lax.broadcasted_iota(jnp.int32, (L, LANES), 1) == 0).astype(BF16)
    scale = ML_DK ** -0.5

    def chunk(ci, carry):
        ch = []
        for d in range(2):
            mask = (jj <= ii) if d == 0 else (jj >= ii)
            c = ci if d == 0 else nc - 1 - ci
            s = pl.multiple_of(c * L, L)
            gates = g_ref[0, pl.ds(s, L), :] + bias_ref[...]
            gf = jnp.where(is_f, _log_sigmoid(gates), gates)
            cum = _running_sum(mask.astype(BF16), gf)
            gf_t = gf.T
            cum_t = cum.T
            for h in range(ML_H):
                ci_, cf_ = d * 8 + h, d * 8 + 4 + h
                hs = slice(h * ML_DK, (h + 1) * ML_DK)
                ig_col, ig_row = gf[:, ci_:ci_ + 1], gf_t[ci_:ci_ + 1, :]
                b_col, b_row = cum[:, cf_:cf_ + 1], cum_t[cf_:cf_ + 1, :]
                b_last = b_col[L - 1:L, :] if d == 0 else b_col[0:1, :]
                m_prev = m_ref[0, d, h][:, 0:1]
                dmat = jnp.where(mask, b_col + (ig_row - b_row), -jnp.inf)
                m_inter = b_col + m_prev
                m_row = jnp.maximum(m_inter, jnp.max(dmat, axis=-1, keepdims=True))
                dk_col = b_last - b_col + ig_col
                m_new = jnp.maximum(b_last + m_prev, jnp.max(dk_col, axis=0, keepdims=True))
                ch.append(dict(
                    d=d, h=h, s=s, hs=hs, dmat=dmat, m_row=m_row, w_inter=jnp.exp(m_inter - m_row), m_new=m_new,
                    w_key=jnp.exp(dk_col - m_new), c_scale=jnp.exp(b_last + m_prev - m_new),
                    q=(q_ref[0, pl.ds(s, L), hs] * scale).astype(BF16), k=k_ref[0, pl.ds(s, L), hs],
                    v_aug=jnp.concatenate([v_ref[0, pl.ds(s, L), hs].astype(BF16), ones_col], axis=1),
                    c_aug=c_ref[0, d, h]))
        qk = [_dot_nt(c['q'], c['k']) for c in ch]
        qc = [_dot(c['q'], c['c_aug']) for c in ch]
        sv = [_dot(s_ * jnp.exp(c['dmat'] - c['m_row']), c['v_aug']) for s_, c in zip(qk, ch)]
        kv = [_dot_tn(c['k'] * c['w_key'], c['v_aug']) for c in ch]
        for c, qc_, sv_, kv_ in zip(ch, qc, sv, kv):
            nd = c['w_inter'] * qc_ + sv_
            num, den = nd[:, :ML_DK], nd[:, ML_DK:ML_DK + 1]
            hh = num / jnp.maximum(jnp.abs(den), jnp.exp(-c['m_row']))
            hs_ref[pl.ds(c['s'], L), c['d'] * 512 + c['h'] * ML_DK:c['d'] * 512 + (c['h'] + 1) * ML_DK] = hh
            c_ref[0, c['d'], c['h']] = c['c_scale'] * c['c_aug'] + kv_
            m_ref[0, c['d'], c['h']] = jnp.broadcast_to(c['m_new'], (1, LANES))
        return carry

    lax.fori_loop(0, nc, chunk, 0)

    def finish(ci, carry):
        s = pl.multiple_of(ci * L, L)
        for h in range(ML_H):
            hs = slice(h * ML_DK, (h + 1) * ML_DK)
            x = hs_ref[pl.ds(s, L), hs] + hs_ref[pl.ds(s, L), 512 + h * ML_DK:512 + (h + 1) * ML_DK]
            xc = x - jnp.mean(x, axis=-1, keepdims=True)
            y = xc * lax.rsqrt(jnp.mean(xc * xc, axis=-1, keepdims=True) + NORM_EPS) * gn_ref[:, hs]
            out_ref[0, pl.ds(s, L), hs] = (y * _sigmoid(o_ref[0, pl.ds(s, L), hs])).astype(BF16)
        return carry

    lax.fori_loop(0, nc, finish, 0)


def _mlstm_params(p):
    bias = jnp.pad(p['b_ml_gates'].reshape(1, -1), ((0, 0), (0, LANES - 4 * ML_H)))
    return bias, p['g_ml_norm'].reshape(1, 512)


def _mlstm_state_in(C0, n0, m0):
    c0 = jnp.concatenate([C0, n0[..., None], jnp.zeros(C0.shape[:-1] + (ML_DK - 1,), F32)], axis=-1)
    return c0, jnp.broadcast_to(m0[..., None, None], m0.shape + (1, LANES))


def _mlstm_state_zero(B):
    return jnp.zeros((B, 2, ML_H, ML_DK, 2 * ML_DK), F32), jnp.zeros((B, 2, ML_H, 1, LANES), F32)


def _mlstm_state_out(c, m):
    return c[..., :ML_DK], c[..., ML_DK], m[..., 0, 0]


def _mlstm(u3, bias, gnorm, c0, m0):
    B, T, _ = u3.shape
    blk = lambda name: pl.BlockSpec((1, T, 512), lambda b, o=PK_OFF[name] // 512: (b, 0, o))
    st_c = pl.BlockSpec((1, 2, ML_H, ML_DK, 2 * ML_DK), lambda b: (b, 0, 0, 0, 0))
    st_m = pl.BlockSpec((1, 2, ML_H, 1, LANES), lambda b: (b, 0, 0, 0, 0))
    return pl.pallas_call(
        functools.partial(_mlstm_kernel, T=T),
        grid=(B,),
        in_specs=[blk('ml_q'), blk('ml_k'), blk('ml_v'), blk('ml_o'),
                  pl.BlockSpec((1, T, LANES), lambda b: (b, 0, PK_OFF['ml_g'] // LANES)),
                  pl.BlockSpec((1, LANES), lambda b: (0, 0)),
                  pl.BlockSpec((1, 512), lambda b: (0, 0)),
                  st_c, st_m],
        out_specs=[pl.BlockSpec((1, T, 512), lambda b: (b, 0, 0)), st_c, st_m],
        out_shape=[jax.ShapeDtypeStruct((B, T, 512), BF16),
                   jax.ShapeDtypeStruct(c0.shape, F32),
                   jax.ShapeDtypeStruct(m0.shape, F32)],
        scratch_shapes=[pltpu.VMEM((T, 1024), F32)],
        compiler_params=_cparams(("arbitrary",)),
        name="mlstm",
    )(u3, u3, u3, u3, u3, bias, gnorm, c0, m0)


MLA_BLK = 256


def _rope_tables(T):
    rows = T // GRID_W
    row = np.repeat(np.arange(rows, dtype=np.float64), GRID_W)
    col = np.tile(np.arange(GRID_W, dtype=np.float64), rows)
    inv = ROPE_THETA ** (-np.arange(MLA_ROPE // 4, dtype=np.float64) / (MLA_ROPE // 4))
    ang = np.stack([row[:, None] * inv, col[:, None] * inv], axis=1)
    cos = np.stack([np.cos(ang), np.cos(ang)], axis=2).reshape(T, MLA_ROPE)
    sin = np.stack([-np.sin(ang), np.sin(ang)], axis=2).reshape(T, MLA_ROPE)
    return jnp.asarray(np.concatenate([cos, sin], axis=1), F32)


def _mla_kernel(*refs, T, n_ctx, rope):
    if n_ctx:
        (u_ref, ctx_ref, cs_ref, gql_ref, gkv_ref, gains_ref, wqn_ref, wqr_ref, wqs_ref, wkn_ref, wv_ref,
         out_ref, kv_ref, qn_s, qr_s, kn_s, kr_s, v_s) = refs
    else:
        (u_ref, cs_ref, gql_ref, gkv_ref, gains_ref, wqn_ref, wqr_ref, wqs_ref, wkn_ref, wv_ref,
         out_ref, kv_ref, qn_s, qr_s, kn_s, kr_s, v_s) = refs
    Lb = MLA_BLK
    gq_n, gq_r, gq_s = gains_ref[0:1, :], gains_ref[1:2, 0:64], gains_ref[1:2, 64:128]
    gk_n, gk_r, gk_s = gains_ref[2:3, :], gains_ref[3:4, 0:64], gains_ref[3:4, 64:128]
    sm_scale = MLA_QK ** -0.5

    def store_keys(s, kn, kr, krs, cos, sin):
        kr_ss = jnp.sum(kr * kr, axis=-1, keepdims=True)
        for h in range(MLA_H):
            kn_h = kn[:, h * 128:(h + 1) * 128]
            rk = lax.rsqrt((jnp.sum(kn_h * kn_h, axis=-1, keepdims=True) + kr_ss) / MLA_QK + NORM_EPS)
            kn_s[pl.ds(s, Lb), h * 128:(h + 1) * 128] = (kn_h * rk * gk_n).astype(BF16)
            kr_h = kr * gk_r
            if cos is not None:
                kr_h = kr_h * cos + (krs * gk_s) * sin
            kr_s[pl.ds(s, Lb), h * 64:(h + 1) * 64] = (kr_h * rk).astype(BF16)

    def prep(ci, carry):
        s = pl.multiple_of(ci * Lb, Lb)
        u = u_ref[0, pl.ds(s, Lb), :]
        ql, ckv, kr, krs = u[:, :384], u[:, 384:640], u[:, 640:704], u[:, 704:768]
        qln = ql * lax.rsqrt(jnp.mean(ql * ql, axis=-1, keepdims=True) + NORM_EPS) * gql_ref[...]
        ckvn = ckv * lax.rsqrt(jnp.mean(ckv * ckv, axis=-1, keepdims=True) + NORM_EPS) * gkv_ref[...]
        kv_ref[0, pl.ds(s, Lb), :] = jnp.concatenate([ckvn, kr], axis=1)
        cos = cs_ref[pl.ds(s, Lb), 0:64] if rope else None
        sin = cs_ref[pl.ds(s, Lb), 64:128] if rope else None
        qn, qr, qs = _dot(qln, wqn_ref[...]), _dot(qln, wqr_ref[...]), _dot(qln, wqs_ref[...])
        for h in range(MLA_H):
            qn_h, qr_h = qn[:, h * 128:(h + 1) * 128], qr[:, h * 64:(h + 1) * 64]
            ss = jnp.sum(qn_h * qn_h, axis=-1, keepdims=True) + jnp.sum(qr_h * qr_h, axis=-1, keepdims=True)
            rq = lax.rsqrt(ss / MLA_QK + NORM_EPS) * sm_scale
            qn_s[pl.ds(s, Lb), h * 128:(h + 1) * 128] = (qn_h * rq * gq_n).astype(BF16)
            qr_h = qr_h * gq_r
            if rope:
                qr_h = qr_h * cos + (qs[:, h * 64:(h + 1) * 64] * gq_s) * sin
            qr_s[pl.ds(s, Lb), h * 64:(h + 1) * 64] = (qr_h * rq).astype(BF16)
        v_s[pl.ds(s, Lb), :] = _dot(ckvn, wv_ref[...]).astype(BF16)
        store_keys(s, _dot(ckvn, wkn_ref[...]), kr, krs, cos, sin)
        return carry

    lax.fori_loop(0, T // Lb, prep, 0)

    for ci in range(n_ctx // Lb):
        cx = ctx_ref[0, ci * Lb:(ci + 1) * Lb, :]
        ckv_c, kr_c = cx[:, :KV_LORA], cx[:, KV_LORA:KV_LORA + MLA_ROPE]
        v_s[T + ci * Lb:T + (ci + 1) * Lb, :] = _dot(ckv_c, wv_ref[...]).astype(BF16)
        store_keys(T + ci * Lb, _dot(ckv_c, wkn_ref[...]), kr_c, None, None, None)

    def attend(qi, carry):
        s = pl.multiple_of(qi * Lb, Lb)
        heads = range(MLA_H)
        sc = [_dot_nt(qn_s[pl.ds(s, Lb), h * 128:(h + 1) * 128], kn_s[:, h * 128:(h + 1) * 128])
              + _dot_nt(qr_s[pl.ds(s, Lb), h * 64:(h + 1) * 64], kr_s[:, h * 64:(h + 1) * 64]) for h in heads]
        p = [jnp.exp(x - jnp.max(x, axis=-1, keepdims=True)) for x in sc]
        o = [_dot(p[h], v_s[:, h * 128:(h + 1) * 128]) / jnp.sum(p[h], axis=-1, keepdims=True) for h in heads]
        for h in heads:
            out_ref[0, pl.ds(s, Lb), h * 128:(h + 1) * 128] = o[h].astype(BF16)
        return carry

    lax.fori_loop(0, T // Lb, attend, 0)


def _mla_params(p):
    def swap(t):
        return jnp.flip(t.reshape(t.shape[:-1] + (2, 2, MLA_ROPE // 4)), axis=-2).reshape(t.shape)

    wq = p['w_mla_uq'].reshape(Q_LORA, MLA_H, MLA_QK)
    wq_n = wq[:, :, :MLA_NOPE].reshape(Q_LORA, 512).astype(BF16)
    wq_r = wq[:, :, MLA_NOPE:].reshape(Q_LORA, 256).astype(BF16)
    wq_s = swap(wq[:, :, MLA_NOPE:]).reshape(Q_LORA, 256).astype(BF16)
    wkv = p['w_mla_ukv'].reshape(KV_LORA, MLA_H, MLA_NOPE + MLA_V)
    wk_n = wkv[:, :, :MLA_NOPE].reshape(KV_LORA, 512).astype(BF16)
    wv = wkv[:, :, MLA_NOPE:].reshape(KV_LORA, 512).astype(BF16)
    gq, gk = p['g_mla_qn'], p['g_mla_kn']
    gains = jnp.concatenate([gq[:128], gq[128:], swap(gq[128:]), gk[:128], gk[128:], swap(gk[128:]),
                             jnp.zeros((4 * LANES,), F32)]).reshape(8, LANES)
    return (p['g_mla_qlat'].reshape(1, -1), p['g_mla_kvlat'].reshape(1, -1), gains, wq_n, wq_r, wq_s, wk_n, wv)


def _mla(u3, ctx_kv, mp):
    B, T, _ = u3.shape
    n_ctx = 0 if ctx_kv is None else ctx_kv.shape[1]
    rope = ctx_kv is not None
    tk = T + n_ctx
    full = lambda a: pl.BlockSpec(a.shape, lambda b, n=a.ndim: (0,) * n)
    cs = _rope_tables(T) if rope else jnp.zeros((T, LANES), F32)
    ins = [u3] + ([ctx_kv] if rope else []) + [cs] + list(mp)
    specs = [pl.BlockSpec((1, T, 768), lambda b: (b, 0, PK_OFF['mla_ql'] // 768))]
    if rope:
        specs.append(pl.BlockSpec((1, n_ctx, KV_LORA + MLA_ROPE), lambda b: (b, 0, 0)))
    specs += [full(a) for a in ins[len(specs):]]
    return pl.pallas_call(
        functools.partial(_mla_kernel, T=T, n_ctx=n_ctx, rope=rope),
        grid=(B,),
        in_specs=specs,
        out_specs=[pl.BlockSpec((1, T, 512), lambda b: (b, 0, 0)),
                   pl.BlockSpec((1, T, KV_LORA + MLA_ROPE), lambda b: (b, 0, 0))],
        out_shape=[jax.ShapeDtypeStruct((B, T, 512), BF16),
                   jax.ShapeDtypeStruct((B, T, KV_LORA + MLA_ROPE), F32)],
        scratch_shapes=[pltpu.VMEM((T, 512), BF16), pltpu.VMEM((T, 256), BF16),
                        pltpu.VMEM((tk, 512), BF16), pltpu.VMEM((tk, 256), BF16), pltpu.VMEM((tk, 512), BF16)],
        compiler_params=_cparams(("arbitrary",)),
        name="mla",
    )(*ins)


GLA_CHUNK = 64
GLA_LEAF = 4
GLA_UNROLL = 2


def _gla_kernel(q_ref, k_ref, v_ref, gd_ref, g_ref, gup_ref, gb_ref, gn_ref, hsel_ref, s0_ref,
                out_ref, s_ref, os_ref, la_ref, *, T):
    L, C = GLA_CHUNK, GLA_LEAF
    nc = T // L
    Lf = 256
    s_ref[...] = s0_ref[...]

    def gates(ci, carry):
        s = pl.multiple_of(ci * Lf, Lf)
        x = _dot_split(gd_ref[0, pl.ds(s, Lf), :], gup_ref) + gb_ref[...]
        la_ref[pl.ds(s, Lf), :] = _log_sigmoid(x) / GLA_NORMALIZER
        return carry

    lax.fori_loop(0, T // Lf, gates, 0)
    ii = lax.broadcasted_iota(jnp.int32, (L, L), 0)
    jj = lax.broadcasted_iota(jnp.int32, (L, L), 1)
    eye = (lax.broadcasted_iota(jnp.int32, (GLA_DK, GLA_DK), 0)
           == lax.broadcasted_iota(jnp.int32, (GLA_DK, GLA_DK), 1)).astype(F32)
    row_id = lax.broadcasted_iota(jnp.int32, (L, 1), 0)
    scale = GLA_DK ** -0.5
    hsel = hsel_ref[...]
    levels = []
    span = C
    while span < L:
        levels.append(span)
        span *= 2

    U = GLA_UNROLL

    def chunk(ci, carry):
        parts = []
        for d in range(2):
            causal = (jj <= ii) if d == 0 else (jj >= ii)
            tri = causal.astype(BF16)
            for u in range(U):
                c = ci * U + u if d == 0 else nc - 1 - (ci * U + u)
                s = pl.multiple_of(c * L, L)
                la = la_ref[pl.ds(s, L), d * GLA_H * GLA_DK:(d + 1) * GLA_H * GLA_DK]
                b = _running_sum(tri, la)
                total = b[L - 1:L, :] if d == 0 else b[0:1, :]
                q = q_ref[0, pl.ds(s, L), :] * scale
                k = k_ref[0, pl.ds(s, L), :]
                lv = []
                for sp in levels:
                    b3 = b.reshape(L // (2 * sp), 2 * sp, GLA_H * GLA_DK)
                    edge = b3[:, sp - 1:sp, :] if d == 0 else b3[:, sp:sp + 1, :]
                    bref = jnp.broadcast_to(edge, b3.shape).reshape(L, GLA_H * GLA_DK)
                    later = (row_id % (2 * sp) >= sp) if d == 0 else (row_id % (2 * sp) < sp)
                    e = jnp.exp(jnp.where(later, b - bref, bref - b))
                    lv.append((jnp.where(later, q * e, 0.0), jnp.where(later, 0.0, k * e),
                               (ii // (2 * sp)) == (jj // (2 * sp))))
                terms = []
                for dl in range(C):
                    if dl == 0:
                        terms.append(q * k)
                        continue
                    sh = dl if d == 0 else L - dl
                    ok = (row_id % C >= dl) if d == 0 else (row_id % C < C - dl)
                    kd = pltpu.roll(k, sh, axis=0)
                    bd = pltpu.roll(b, sh, axis=0)
                    terms.append(q * kd * jnp.exp(jnp.where(ok, b - bd, 0.0)))
                tt = jnp.concatenate(terms, axis=0)
                t_hi = tt.astype(BF16)
                t_lo = (tt - t_hi.astype(F32)).astype(BF16)
                diag = (jnp.dot(t_hi, hsel, preferred_element_type=F32)
                        + jnp.dot(t_lo, hsel, preferred_element_type=F32))
                parts.append(dict(d=d, s=s, lv=lv, diag=diag, q_in=q * jnp.exp(b), k_out=k * jnp.exp(total - b),
                                  f_row=jnp.exp(total)))
        chains = [(p, h) for p in parts for h in range(GLA_H)]
        amats = []
        for p, h in chains:
            ks = slice(h * GLA_DK, (h + 1) * GLA_DK)
            a = jnp.zeros((L, L), F32)
            for qs, kt, same in p['lv']:
                a = a + jnp.where(same, _dot_nt(qs[:, ks], kt[:, ks]), 0.0)
            for dl in range(C):
                pair = jnp.logical_and(jj == (ii - dl if p['d'] == 0 else ii + dl), ii // C == jj // C)
                a = a + jnp.where(pair, p['diag'][dl * L:(dl + 1) * L, h:h + 1], 0.0)
            amats.append(a)
        vals = [v_ref[0, pl.ds(p['s'], L), h * GLA_DV:(h + 1) * GLA_DV] for p, h in chains]
        intra = [_dot(a, v) for a, v in zip(amats, vals)]
        upd = [_dot_tn(p['k_out'][:, h * GLA_DK:(h + 1) * GLA_DK], v) for (p, h), v in zip(chains, vals)]
        for d in range(2):
            for h in range(GLA_H):
                ks = slice(h * GLA_DK, (h + 1) * GLA_DK)
                st = s_ref[0, d, h]
                for u in range(U):
                    i = (d * U + u) * GLA_H + h
                    p = parts[d * U + u]
                    o = intra[i] + _dot(p['q_in'][:, ks], st)
                    os_ref[pl.ds(p['s'], L), d * 512 + h * GLA_DV:d * 512 + (h + 1) * GLA_DV] = o
                    f_col = jnp.sum(eye * p['f_row'][:, ks], axis=1, keepdims=True)
                    st = f_col * st + upd[i]
                s_ref[0, d, h] = st
        return carry

    lax.fori_loop(0, nc // U, chunk, 0)

    def finish(ci, carry):
        s = pl.multiple_of(ci * Lf, Lf)
        for h in range(GLA_H):
            vs = slice(h * GLA_DV, (h + 1) * GLA_DV)
            o = os_ref[pl.ds(s, Lf), vs] + os_ref[pl.ds(s, Lf), 512 + h * GLA_DV:512 + (h + 1) * GLA_DV]
            y = o * lax.rsqrt(jnp.mean(o * o, axis=-1, keepdims=True) + NORM_EPS) * gn_ref[:, vs]
            g = g_ref[0, pl.ds(s, Lf), vs]
            out_ref[0, pl.ds(s, Lf), vs] = (y * (g * _sigmoid(g))).astype(BF16)
        return carry

    lax.fori_loop(0, T // Lf, finish, 0)


def _gla_params(p):
    gup = jnp.pad(jnp.concatenate([p['gla_g_up'][0], p['gla_g_up'][1]], axis=1),
                  ((0, LANES - GLA_GATE_RANK), (0, 0)))
    hsel = jnp.asarray(np.arange(GLA_H * GLA_DK)[:, None] // GLA_DK == np.arange(LANES)[None, :], BF16)
    return _split_weight(gup), p['gla_g_b'].reshape(1, -1), p['gla_norm'].reshape(1, -1), hsel


def _gla(u3, gp, s0):
    B, T, _ = u3.shape
    blk = lambda name, w: pl.BlockSpec((1, T, w), lambda b, o=PK_OFF[name] // w: (b, 0, o))
    full = lambda a: pl.BlockSpec(a.shape, lambda b, n=a.ndim: (0,) * n)
    st = pl.BlockSpec((1, 2, GLA_H, GLA_DK, GLA_DV), lambda b: (b, 0, 0, 0, 0))
    return pl.pallas_call(
        functools.partial(_gla_kernel, T=T),
        grid=(B,),
        in_specs=[blk('gla_q', 256), blk('gla_k', 256), blk('gla_v', 512), blk('gla_gd', 128), blk('gla_g', 512),
                  full(gp[0]), full(gp[1]), full(gp[2]), full(gp[3]), st],
        out_specs=[pl.BlockSpec((1, T, 512), lambda b: (b, 0, 0)), st],
        out_shape=[jax.ShapeDtypeStruct((B, T, 512), BF16), jax.ShapeDtypeStruct(s0.shape, F32)],
        scratch_shapes=[pltpu.VMEM((T, 1024), F32), pltpu.VMEM((T, 2 * GLA_H * GLA_DK), F32)],
        compiler_params=_cparams(("arbitrary",)),
        name="gla",
    )(u3, u3, u3, u3, u3, *gp, s0)


RW_CHUNK = 64
RW_UNROLL = 2


def _seg_sum(x, bd):
    hi = x.astype(BF16)
    lo = (x - hi.astype(F32)).astype(BF16)
    return jnp.dot(hi, bd, preferred_element_type=F32) + jnp.dot(lo, bd, preferred_element_type=F32)


def _rwkv_kernel(r_ref, k_ref, v_ref, wa_ref, gd_ref, wwa_ref, w0a0_ref, gup_ref, kk_ref, ka_ref, rk_ref, ln_ref,
                 bd_ref, h0_ref, out_ref, h_ref, ys_ref, pre_ref, *, T):
    L, N = RW_CHUNK, RW_N
    nc = T // L
    h_ref[...] = h0_ref[...]
    ii = lax.broadcasted_iota(jnp.int32, (L, L), 0)
    jj = lax.broadcasted_iota(jnp.int32, (L, L), 1)
    eye = (ii == jj).astype(F32)
    lane = lax.broadcasted_iota(jnp.int32, (1, LANES), 1)
    bd = bd_ref[...]
    Lf = 256

    def gates(ci, carry):
        s = pl.multiple_of(ci * Lf, Lf)
        wa = wa_ref[0, pl.ds(s, Lf), :]
        pre_ref[pl.ds(s, Lf), :] = _dot_split(jnp.where(lane < 64, jnp.tanh(wa), wa), wwa_ref) + w0a0_ref[...]
        return carry

    lax.fori_loop(0, T // Lf, gates, 0)

    masks = [((jj < ii), (jj <= ii)), ((jj > ii), (jj >= ii))]
    U = RW_UNROLL

    def chunk(ci, carry):
        chains = []
        for d in range(2):
            strict, incl = masks[d]
            tri = incl.astype(BF16)
            for u in range(U):
                c = ci * U + u if d == 0 else nc - 1 - (ci * U + u)
                s = pl.multiple_of(c * L, L)
                r = r_ref[0, pl.ds(s, L), :]
                k = k_ref[0, pl.ds(s, L), :]
                v = v_ref[0, pl.ds(s, L), :]
                pre = pre_ref[pl.ds(s, L), d * 1024:(d + 1) * 1024]
                logw = -RW_DECAY_SCALE * _sigmoid(pre[:, :512])
                a = _sigmoid(pre[:, 512:])
                kkr = k * kk_ref[...]
                kk = kkr * lax.rsqrt(_seg_sum(kkr * kkr, bd) + 1e-12)
                kt = k * (1.0 + (a - 1.0) * ka_ref[...])
                bh = kk * a
                lg = _running_sum(tri, logw)
                lg_end = lg[L - 1:L, :] if d == 0 else lg[0:1, :]
                a_t = -kk * jnp.exp(lg - logw)
                r_t = r * jnp.exp(lg)
                e_inv = jnp.exp(-lg)
                k_t, b_t = kt * e_inv, bh * e_inv
                e_end = jnp.exp(lg_end - lg)
                k_e, b_e = kt * e_end, bh * e_end
                g_end = jnp.exp(lg_end)
                for h in range(RW_H):
                    sl = slice(h * N, (h + 1) * N)
                    chains.append(dict(d=d, u=u, h=h, s=s, strict=strict, incl=incl, a=a_t[:, sl], r=r_t[:, sl],
                                       b=b_t[:, sl], k=k_t[:, sl], ke=k_e[:, sl], be=b_e[:, sl], g=g_end[:, sl],
                                       v=v[:, sl]))
        ms = [_dot_nt(jnp.concatenate([c['a'], c['r']], axis=0), jnp.concatenate([c['b'], c['k']], axis=0))
              for c in chains]
        pws = [jnp.where(c['strict'], m[:L, :L], 0.0) for c, m in zip(chains, ms)]
        xs = [eye + n for n in pws]
        for _ in range(5):
            pws = [_dot(pw, pw) for pw in pws]
            xs = [x + _dot(x, pw) for x, pw in zip(xs, pws)]
        mvs = [_dot(jnp.where(c['strict'], m[:L, L:], 0.0), c['v']) for c, m in zip(chains, ms)]
        tws = [_dot(x, jnp.concatenate([c['a'], mv], axis=1)) for x, c, mv in zip(xs, chains, mvs)]
        qys = [_dot(jnp.where(c['incl'], m[L:, :L], 0.0), tw) for c, m, tw in zip(chains, ms, tws)]
        ylocs = [_dot(jnp.where(c['incl'], m[L:, L:], 0.0), c['v']) + qy[:, N:] for c, m, qy in zip(chains, ms, qys)]
        pgs = [_dot_tn(c['be'], tw) for c, tw in zip(chains, tws)]
        gmats = [_dot_tn(c['ke'], c['v']) + pg[:, N:] for c, pg in zip(chains, pgs)]
        for d in range(2):
            for h in range(RW_H):
                hst = h_ref[0, d, h]
                for u in range(U):
                    i = (d * U + u) * RW_H + h
                    c = chains[i]
                    y = _dot(c['r'] + qys[i][:, :N], hst) + ylocs[i]
                    ys_ref[pl.ds(c['s'], L), d * 512 + h * N:d * 512 + (h + 1) * N] = y
                    hst = _dot(eye * c['g'] + pgs[i][:, :N], hst) + gmats[i]
                h_ref[0, d, h] = hst
        return carry

    lax.fori_loop(0, nc // U, chunk, 0)

    def finish(ci, carry):
        s = pl.multiple_of(ci * Lf, Lf)
        r = r_ref[0, pl.ds(s, Lf), :]
        k = k_ref[0, pl.ds(s, Lf), :]
        rk = r * k * rk_ref[...]
        bonus = jnp.zeros((Lf, 512), F32)
        for d in range(2):
            a = _sigmoid(pre_ref[pl.ds(s, Lf), d * 1024 + 512:(d + 1) * 1024])
            bonus = bonus + _seg_sum(rk * (1.0 + (a - 1.0) * ka_ref[...]), bd)
        y = ys_ref[pl.ds(s, Lf), 0:512] + ys_ref[pl.ds(s, Lf), 512:1024]
        yc = y - _seg_sum(y, bd) / N
        yn = yc * lax.rsqrt(_seg_sum(yc * yc, bd) / N + RW_LN_EPS) * ln_ref[...]
        g = _dot(_sigmoid(gd_ref[0, pl.ds(s, Lf), :]), gup_ref[...])
        out_ref[0, pl.ds(s, Lf), :] = ((yn + bonus * v_ref[0, pl.ds(s, Lf), :]) * g).astype(BF16)
        return carry

    lax.fori_loop(0, T // Lf, finish, 0)


def _rwkv_params(p):
    z = jnp.zeros((64, 512), F32)
    wwa = jnp.concatenate([jnp.concatenate([p['rw_w_up'][0], z, p['rw_w_up'][1], z], axis=1),
                           jnp.concatenate([z, p['rw_a_up'][0], z, p['rw_a_up'][1]], axis=1)], axis=0)
    w0a0 = jnp.concatenate([p['rw_w0'][0], p['rw_a0'][0], p['rw_w0'][1], p['rw_a0'][1]]).reshape(1, 2048)
    seg = np.arange(512) // RW_N
    bd = jnp.asarray(seg[:, None] == seg[None, :], BF16)
    row = lambda n: p[n].reshape(1, -1)
    return (_split_weight(wwa), w0a0, p['rw_g_up'].astype(BF16), row('rw_k_k'), row('rw_k_a'), row('rw_r_k'),
            row('rw_ln'), bd)


def _rwkv(u3, rp, h0):
    B, T, _ = u3.shape
    blk = lambda name, w: pl.BlockSpec((1, T, w), lambda b, o=PK_OFF[name] // w: (b, 0, o))
    full = lambda a: pl.BlockSpec(a.shape, lambda b, n=a.ndim: (0,) * n)
    st = pl.BlockSpec((1, 2, RW_H, RW_N, RW_N), lambda b: (b, 0, 0, 0, 0))
    return pl.pallas_call(
        functools.partial(_rwkv_kernel, T=T),
        grid=(B,),
        in_specs=[blk('rw_r', 512), blk('rw_k', 512), blk('rw_v', 512), blk('rw_wd', 128), blk('rw_gd', 128)]
                 + [full(a) for a in rp] + [st],
        out_specs=[pl.BlockSpec((1, T, 512), lambda b: (b, 0, 0)), st],
        out_shape=[jax.ShapeDtypeStruct((B, T, 512), BF16), jax.ShapeDtypeStruct(h0.shape, F32)],
        scratch_shapes=[pltpu.VMEM((T, 1024), F32), pltpu.VMEM((T, 2048), F32)],
        compiler_params=_cparams(("arbitrary",)),
        name="rwkv7",
    )(u3, u3, u3, u3, u3, *rp, h0)


def _route(logits):
    lane = lax.broadcasted_iota(jnp.int32, (1, LANES), 1)
    far = jnp.int32(2 * LANES)
    neg = -jnp.inf
    gl = jnp.where(jnp.logical_and(lane >= N_EXPERTS, lane < N_EXPERTS + N_GROUPS), logits, neg)
    gmax = jnp.max(gl, axis=-1, keepdims=True)
    grp = jnp.min(jnp.where(gl == gmax, lane, far), axis=-1, keepdims=True) - N_EXPERTS
    p_grp = 1.0 / jnp.sum(jnp.exp(gl - gmax), axis=-1, keepdims=True)
    el = jnp.where(jnp.logical_and(lane < N_EXPERTS, lane // EXPERTS_PER_GROUP == grp), logits, neg)
    v1 = jnp.max(el, axis=-1, keepdims=True)
    i1 = jnp.min(jnp.where(el == v1, lane, far), axis=-1, keepdims=True)
    el2 = jnp.where(lane == i1, neg, el)
    v2 = jnp.max(el2, axis=-1, keepdims=True)
    i2 = jnp.min(jnp.where(el2 == v2, lane, far), axis=-1, keepdims=True)
    e = jnp.exp(v2 - v1)
    w1 = 1.0 / (1.0 + e)
    comb = jnp.where(lane == i1, p_grp * w1, jnp.where(lane == i2, p_grp * (e * w1), 0.0))
    sel = jnp.where(lane == i1, 1.0, jnp.where(lane == i2, 2.0, 0.0))
    return comb, sel


def _outproj_kernel(x_ref, m0_ref, m1_ref, m2_ref, m3_ref, w_ref, mod_ref, g_ref, wr_ref, br_ref,
                    xn_ref, h3_ref, comb_ref, sel_ref):
    y = jnp.dot(m0_ref[...], w_ref[0:512, :], preferred_element_type=F32)
    for i, m_ref in enumerate((m1_ref, m2_ref, m3_ref), start=1):
        y = y + jnp.dot(m_ref[...], w_ref[i * 512:(i + 1) * 512, :], preferred_element_type=F32)
    xn = x_ref[...] + mod_ref[0, 2:3, :] * y
    xn_ref[...] = xn
    h = xn * lax.rsqrt(jnp.mean(xn * xn, axis=-1, keepdims=True) + NORM_EPS) * g_ref[...]
    h = h * (1.0 + mod_ref[0, 4:5, :]) + mod_ref[0, 3:4, :]
    tm = h.shape[0]
    for c in range(ROW_TILES):
        h3_ref[pl.ds(c, tm, stride=ROW_TILES), :] = h[:, c * LANES:(c + 1) * LANES]
    comb_ref[...], sel_ref[...] = _route(_dot_split(h, wr_ref) + br_ref[...])


def _out_proj(x2, mixed, w_out, mod, g, router_w, router_b, T):
    n_tok = x2.shape[0]
    tm = 512
    bm = mod.shape[0]
    mod_idx = (lambda i: (i * tm // T, 0, 0)) if bm > 1 else (lambda i: (0, 0, 0))
    row = lambda w: pl.BlockSpec((tm, w), lambda i: (i, 0))
    full = lambda a: pl.BlockSpec(a.shape, lambda i, n=a.ndim: (0,) * n)
    return pl.pallas_call(
        _outproj_kernel,
        grid=(n_tok // tm,),
        in_specs=[row(D_MODEL)] + [row(GROUP_W)] * 4 + [full(w_out), pl.BlockSpec((1, 8, D_MODEL), mod_idx),
                                                       full(g), full(router_w), full(router_b)],
        out_specs=[row(D_MODEL), pl.BlockSpec((tm * ROW_TILES, LANES), lambda i: (i, 0)), row(LANES), row(LANES)],
        out_shape=[jax.ShapeDtypeStruct((n_tok, D_MODEL), F32),
                   jax.ShapeDtypeStruct((n_tok * ROW_TILES, LANES), F32),
                   jax.ShapeDtypeStruct((n_tok, LANES), F32), jax.ShapeDtypeStruct((n_tok, LANES), F32)],
        compiler_params=_cparams(("arbitrary",)),
        name="out_proj",
    )(x2, *mixed, w_out, mod, g, router_w, router_b)


MOE_TM = 256
MOE_ROWS = 256
MOE_DMA_UNROLL = 8


def _moe_rows(n_tok):
    return 2 * n_tok + N_EXPERTS * MOE_TM


def _plan_kernel(sel_ref, pos_ref, tmap_ref):
    n_tok = sel_ref.shape[0]
    blk = 512
    lane = lax.broadcasted_iota(jnp.int32, (1, LANES), 1)
    earlier = (lax.broadcasted_iota(jnp.int32, (blk, blk), 1)
               < lax.broadcasted_iota(jnp.int32, (blk, blk), 0)).astype(BF16)
    before = (lax.broadcasted_iota(jnp.int32, (LANES, LANES), 0)
              < lax.broadcasted_iota(jnp.int32, (LANES, LANES), 1)).astype(BF16)

    def count(i, acc):
        s = pl.multiple_of(i * blk, blk)
        return acc + jnp.sum((sel_ref[pl.ds(s, blk), :] > 0.0).astype(F32), axis=0, keepdims=True)

    counts = lax.fori_loop(0, n_tok // blk, count, jnp.zeros((1, LANES), F32))
    tiles = jnp.floor((counts + (MOE_TM - 1)) * (1.0 / MOE_TM))
    tile_start = _dot(jnp.broadcast_to(tiles, (8, LANES)), before)[0:1, :]
    tile_end = tile_start + tiles
    base = tile_start * MOE_TM
    n_tiles = jnp.sum(tiles, axis=-1, keepdims=True)
    j = lax.broadcasted_iota(jnp.int32, (tmap_ref.shape[0], 1), 0).astype(F32)
    done = jnp.logical_and(tile_end <= j, lane < N_EXPERTS)
    expert = jnp.minimum(jnp.sum(done.astype(F32), axis=-1, keepdims=True), N_EXPERTS - 1.0)
    valid = (j < n_tiles).astype(F32)
    tmap_ref[...] = jnp.where(lane == 0, expert, jnp.where(lane == 1, valid, 0.0)).astype(jnp.int32)

    def place(i, seen):
        s = pl.multiple_of(i * blk, blk)
        sel = sel_ref[pl.ds(s, blk), :]
        one = (sel > 0.0).astype(F32)
        row = base + seen + _dot(earlier, one)
        p1 = jnp.sum(jnp.where(sel == 1.0, row, 0.0), axis=-1, keepdims=True)
        p2 = jnp.sum(jnp.where(sel == 2.0, row, 0.0), axis=-1, keepdims=True)
        pos_ref[pl.ds(s, blk), :] = jnp.where(lane == 0, p1, jnp.where(lane == 1, p2, 0.0)).astype(jnp.int32)
        return seen + jnp.sum(one, axis=0, keepdims=True)

    lax.fori_loop(0, n_tok // blk, place, jnp.zeros((1, LANES), F32))


def _moe_plan(sel):
    n_tok = sel.shape[0]
    n_tiles = _moe_rows(n_tok) // MOE_TM
    pos, tmap = pl.pallas_call(
        _plan_kernel,
        out_shape=[jax.ShapeDtypeStruct((n_tok, LANES), jnp.int32),
                   jax.ShapeDtypeStruct((-(-n_tiles // 8) * 8, LANES), jnp.int32)],
        compiler_params=pltpu.CompilerParams(vmem_limit_bytes=VMEM_LIMIT),
        name="moe_plan",
    )(sel)
    return jnp.transpose(pos[:, :2]), tmap[:n_tiles, 0], tmap[:n_tiles, 1]


def _dispatch_kernel(pos_ref, src_ref, init_ref, dst_ref, sem):
    del init_ref
    base = pl.program_id(0) * MOE_ROWS

    def copy(j, k):
        return pltpu.make_async_copy(src_ref.at[j], dst_ref.at[pos_ref[k, base + j]], sem)

    def start(j, c):
        copy(j, 0).start()
        copy(j, 1).start()
        return c

    def wait(j, c):
        copy(j, 0).wait()
        copy(j, 1).wait()
        return c

    lax.fori_loop(0, MOE_ROWS, start, 0, unroll=MOE_DMA_UNROLL)
    lax.fori_loop(0, MOE_ROWS, wait, 0, unroll=MOE_DMA_UNROLL)


def _dispatch(pos, h3, init):
    n_tok = h3.shape[0]
    any_spec = pl.BlockSpec(memory_space=pl.ANY)
    return pl.pallas_call(
        _dispatch_kernel,
        grid_spec=pltpu.PrefetchScalarGridSpec(
            num_scalar_prefetch=1, grid=(n_tok // MOE_ROWS,),
            in_specs=[pl.BlockSpec((MOE_ROWS, ROW_TILES, LANES), lambda i, p: (i, 0, 0)), any_spec],
            out_specs=any_spec, scratch_shapes=[pltpu.SemaphoreType.DMA(())]),
        out_shape=jax.ShapeDtypeStruct(init.shape, init.dtype),
        input_output_aliases={2: 0},
        compiler_params=pltpu.CompilerParams(dimension_semantics=("arbitrary",)),
        name="moe_dispatch",
    )(pos, h3, init)


def _experts_kernel(te_ref, tv_ref, xs_ref, wg_ref, wu_ref, wd_ref, ys_ref, wg_s, wu_s, wd_s):
    i = pl.program_id(0)
    fresh = jnp.logical_or(i == 0, te_ref[i] != te_ref[jnp.maximum(i - 1, 0)])

    @pl.when(jnp.logical_and(fresh, tv_ref[i] == 1))
    def _():
        wg_s[...] = wg_ref[0, 0].astype(BF16)
        wu_s[...] = wu_ref[0, 0].astype(BF16)
        wd_s[...] = wd_ref[0, 0].astype(BF16)

    @pl.when(tv_ref[i] == 1)
    def _():
        x = jnp.concatenate([xs_ref[pl.ds(c, MOE_TM, stride=ROW_TILES), :] for c in range(ROW_TILES)],
                            axis=1).astype(BF16)
        a = jnp.dot(x, wg_s[...], preferred_element_type=F32)
        b = jnp.dot(x, wu_s[...], preferred_element_type=F32)
        y = jnp.dot(((a * _sigmoid(a)) * b).astype(BF16), wd_s[...], preferred_element_type=F32)
        for c in range(ROW_TILES):
            ys_ref[pl.ds(c, MOE_TM, stride=ROW_TILES), :] = y[:, c * LANES:(c + 1) * LANES]

    @pl.when(tv_ref[i] == 0)
    def _():
        ys_ref[...] = jnp.zeros_like(ys_ref)


def _experts(tile_expert, tile_valid, xs, wg, wu, wd, layer):
    n_rows = xs.shape[0] // ROW_TILES
    rows = pl.BlockSpec((MOE_TM * ROW_TILES, LANES), lambda i, te, tv: (i, 0))
    return pl.pallas_call(
        _experts_kernel,
        grid_spec=pltpu.PrefetchScalarGridSpec(
            num_scalar_prefetch=2, grid=(n_rows // MOE_TM,),
            in_specs=[rows,
                      pl.BlockSpec((1, 1, D_MODEL, EXPERT_HIDDEN), lambda i, te, tv: (layer, te[i], 0, 0)),
                      pl.BlockSpec((1, 1, D_MODEL, EXPERT_HIDDEN), lambda i, te, tv: (layer, te[i], 0, 0)),
                      pl.BlockSpec((1, 1, EXPERT_HIDDEN, D_MODEL), lambda i, te, tv: (layer, te[i], 0, 0))],
            out_specs=rows,
            scratch_shapes=[pltpu.VMEM((D_MODEL, EXPERT_HIDDEN), BF16), pltpu.VMEM((D_MODEL, EXPERT_HIDDEN), BF16),
                            pltpu.VMEM((EXPERT_HIDDEN, D_MODEL), BF16)]),
        out_shape=jax.ShapeDtypeStruct(xs.shape, F32),
        compiler_params=_cparams(("arbitrary",)),
        name="moe_experts",
    )(tile_expert, tile_valid, xs, wg, wu, wd)


def _combine_kernel(pos_ref, ys_ref, comb_ref, sel_ref, xn_ref, mod_ref, o_ref, y_s, sems):
    i = pl.program_id(0)
    slot_rows = 2 * MOE_ROWS * ROW_TILES

    def copy(tile, slot, j, k):
        dst = pl.multiple_of(slot * slot_rows + (k * MOE_ROWS + j) * ROW_TILES, ROW_TILES)
        return pltpu.make_async_copy(ys_ref.at[pos_ref[k, tile * MOE_ROWS + j]], y_s.at[pl.ds(dst, ROW_TILES), :],
                                     sems.at[slot])

    def start_tile(tile, slot):
        def body(j, c):
            copy(tile, slot, j, 0).start()
            copy(tile, slot, j, 1).start()
            return c
        lax.fori_loop(0, MOE_ROWS, body, 0, unroll=MOE_DMA_UNROLL)

    def wait_tile(tile, slot):
        def body(j, c):
            copy(tile, slot, j, 0).wait()
            copy(tile, slot, j, 1).wait()
            return c
        lax.fori_loop(0, MOE_ROWS, body, 0, unroll=MOE_DMA_UNROLL)

    @pl.when(i == 0)
    def _():
        start_tile(0, 0)

    @pl.when(i + 1 < pl.num_programs(0))
    def _():
        start_tile(i + 1, (i + 1) % 2)

    comb, sel = comb_ref[...], sel_ref[...]
    w1 = jnp.sum(jnp.where(sel == 1.0, comb, 0.0), axis=-1, keepdims=True)
    w2 = jnp.sum(jnp.where(sel == 2.0, comb, 0.0), axis=-1, keepdims=True)
    slot = i % 2
    wait_tile(i, slot)
    for c in range(ROW_TILES):
        cs = slice(c * LANES, (c + 1) * LANES)
        y1 = y_s[pl.ds(slot * slot_rows + c, MOE_ROWS, stride=ROW_TILES), :]
        y2 = y_s[pl.ds(slot * slot_rows + MOE_ROWS * ROW_TILES + c, MOE_ROWS, stride=ROW_TILES), :]
        o_ref[:, cs] = xn_ref[:, cs] + mod_ref[0, 5:6, cs] * (w1 * y1 + w2 * y2)


def _combine(pos, ys, comb, sel, xn, mod, T):
    n_tok = xn.shape[0]
    tm = MOE_ROWS
    bm = mod.shape[0]
    mod_idx = (lambda i, p: (i * tm // T, 0, 0)) if bm > 1 else (lambda i, p: (0, 0, 0))
    row = lambda w: pl.BlockSpec((tm, w), lambda i, p: (i, 0))
    return pl.pallas_call(
        _combine_kernel,
        grid_spec=pltpu.PrefetchScalarGridSpec(
            num_scalar_prefetch=1, grid=(n_tok // tm,),
            in_specs=[pl.BlockSpec(memory_space=pl.ANY), row(LANES), row(LANES), row(D_MODEL),
                      pl.BlockSpec((1, 8, D_MODEL), mod_idx)],
            out_specs=row(D_MODEL),
            scratch_shapes=[pltpu.VMEM((2 * 2 * tm * ROW_TILES, LANES), F32), pltpu.SemaphoreType.DMA((2,))]),
        out_shape=jax.ShapeDtypeStruct((n_tok, D_MODEL), F32),
        compiler_params=_cparams(("arbitrary",)),
        name="moe_combine",
    )(pos, ys, comb, sel, xn, mod)


def _moe(groups, wg, wu, wd, layer, sorted_buf=None):
    sizes = [g[3].shape[0] for g in groups]
    n_rows = _moe_rows(sum(sizes))
    pos, tile_expert, tile_valid = _moe_plan(jnp.concatenate([g[2] for g in groups], axis=0))
    xs = jnp.zeros((n_rows, ROW_TILES, LANES), F32) if sorted_buf is None else sorted_buf
    start = 0
    for (h3, _, _, _, _, _), n in zip(groups, sizes):
        xs = _dispatch(pos[:, start:start + n], h3.reshape(n, ROW_TILES, LANES), xs)
        start += n
    ys = _experts(tile_expert, tile_valid, xs.reshape(n_rows * ROW_TILES, LANES), wg, wu, wd, layer)
    ys = ys.reshape(n_rows, ROW_TILES, LANES)
    outs, start = [], 0
    for (_, comb, sel, xn, mod, T), n in zip(groups, sizes):
        outs.append(_combine(pos[:, start:start + n], ys, comb, sel, xn, mod, T))
        start += n
    return outs, xs


def kernel(x_prompt, x_sample, cache_mla, state_mlstm_C, state_mlstm_n, state_mlstm_m, state_rwkv, state_gla, c, c_ctx, w_ada, b_ada, g_mix, g_ffn, w_in, w_out, b_ml_gates, g_ml_norm, g_mla_qlat, g_mla_kvlat, w_mla_uq, w_mla_ukv, g_mla_qn, g_mla_kn, rw_w0, rw_w_up, rw_a0, rw_a_up, rw_g_up, rw_k_k, rw_k_a, rw_r_k, rw_ln, gla_g_up, gla_g_b, gla_norm, moe_w_rg, moe_b_rg, moe_w_re, moe_b_re, moe_w_gate, moe_w_up, moe_w_down):
    cc = jnp.concatenate([c_ctx[None], c, jnp.zeros((3, D_MODEL), F32)], axis=0)
    mod = _modulation(cc, w_ada, b_ada).reshape(DEPTH, 8, 6, D_MODEL)
    mod = jnp.pad(mod, ((0, 0), (0, 0), (0, 2), (0, 0)))

    layers = []
    for l in range(DEPTH):
        p = {'b_ml_gates': b_ml_gates[l], 'g_ml_norm': g_ml_norm[l], 'g_mla_qlat': g_mla_qlat[l],
             'g_mla_kvlat': g_mla_kvlat[l], 'w_mla_uq': w_mla_uq[l], 'w_mla_ukv': w_mla_ukv[l],
             'g_mla_qn': g_mla_qn[l], 'g_mla_kn': g_mla_kn[l], 'rw_w0': rw_w0[l], 'rw_w_up': rw_w_up[l],
             'rw_a0': rw_a0[l], 'rw_a_up': rw_a_up[l], 'rw_g_up': rw_g_up[l], 'rw_k_k': rw_k_k[l],
             'rw_k_a': rw_k_a[l], 'rw_r_k': rw_r_k[l], 'rw_ln': rw_ln[l], 'gla_g_up': gla_g_up[l],
             'gla_g_b': gla_g_b[l], 'gla_norm': gla_norm[l]}
        lane_pad = LANES - N_EXPERTS - N_GROUPS
        router_w = jnp.concatenate([moe_w_re[l], moe_w_rg[l], jnp.zeros((D_MODEL, lane_pad), F32)], axis=1)
        router_b = jnp.concatenate([moe_b_re[l], moe_b_rg[l], jnp.zeros((lane_pad,), F32)]).reshape(1, LANES)
        layers.append(dict(
            w_in=_pack_w_in(w_in[l]), g_mix=g_mix[l].reshape(1, -1), g_ffn=g_ffn[l].reshape(1, -1),
            w_out=w_out[l].astype(BF16), ml=_mlstm_params(p), mla=_mla_params(p), rw=_rwkv_params(p),
            gla=_gla_params(p), router_w=_split_weight(router_w), router_b=router_b,
            layer=l))

    def mix(x2, B, T, mod_g, lp, ctx):
        u3 = _in_proj(x2, mod_g, lp['g_mix'], lp['w_in'], T).reshape(B, T, PK_COLS)
        if ctx is None:
            ctx_kv = None
            ml_c0, ml_m0 = _mlstm_state_zero(B)
            rw_h0 = jnp.zeros((B, 2, RW_H, RW_N, RW_N), F32)
            gla_s0 = jnp.zeros((B, 2, GLA_H, GLA_DK, GLA_DV), F32)
        else:
            ctx_kv, ml_C0, ml_n0, ml_m0_, rw_S0, gla_s0 = ctx
            ml_c0, ml_m0 = _mlstm_state_in(ml_C0, ml_n0, ml_m0_)
            rw_h0 = jnp.swapaxes(rw_S0, -1, -2)
        ml_out, ml_c, ml_m = _mlstm(u3, *lp['ml'], ml_c0, ml_m0)
        mla_out, own_kv = _mla(u3, ctx_kv, lp['mla'])
        rw_out, rw_h = _rwkv(u3, lp['rw'], rw_h0)
        gla_out, gla_s = _gla(u3, lp['gla'], gla_s0)
        mixed = [t.reshape(B * T, GROUP_W) for t in (ml_out, mla_out, rw_out, gla_out)]
        xn, h3, comb, sel = _out_proj(x2, mixed, lp['w_out'], mod_g, lp['g_ffn'], lp['router_w'], lp['router_b'], T)
        ml_C, ml_n, ml_mm = _mlstm_state_out(ml_c, ml_m)
        return (h3, comb, sel, xn, mod_g, T), (own_kv, ml_C, ml_n, ml_mm, jnp.swapaxes(rw_h, -1, -2), gla_s)

    Bp, Tp = x_prompt.shape[:2]
    Bs, Ts = x_sample.shape[:2]
    xp = x_prompt.reshape(Bp * Tp, D_MODEL)
    xs = x_sample.reshape(Bs * Ts, D_MODEL)
    ctx_states = []
    sorted_buf = None
    for l in range(DEPTH):
        ctx = (cache_mla[:, l], state_mlstm_C[:, l], state_mlstm_n[:, l], state_mlstm_m[:, l],
               state_rwkv[:, l], state_gla[:, l])
        moe_p, st = mix(xp, Bp, Tp, mod[l, 0:1], layers[l], None)
        moe_s, _ = mix(xs, Bs, Ts, mod[l, 1:1 + Bs], layers[l], ctx)
        ctx_states.append(st)
        (xp, xs), sorted_buf = _moe([moe_p, moe_s], moe_w_gate, moe_w_up, moe_w_down, l, sorted_buf)
    outs = [jnp.stack([s[i] for s in ctx_states], axis=1) for i in range(6)]
    return (xp.reshape(x_prompt.shape), xs.reshape(x_sample.shape), *outs)
```

```python
import functools
import math

import numpy as np
import jax
import jax.numpy as jnp
from jax import lax
from jax.experimental import pallas as pl
from jax.experimental.pallas import tpu as pltpu

F32 = jnp.float32
BF16 = jnp.bfloat16

D_MODEL = 2048
DEPTH = 2
GRID_W = 64
GROUP_W = 512
ML_H, ML_DK = 4, 128
MLA_H, MLA_NOPE, MLA_ROPE, MLA_V = 4, 128, 64, 128
MLA_QK = MLA_NOPE + MLA_ROPE
Q_LORA, KV_LORA = 384, 256
ROPE_THETA = 10000.0
RW_H, RW_N = 8, 64
RW_DECAY_SCALE = math.exp(-0.5)
RW_LN_EPS = 64e-5
GLA_H, GLA_DK, GLA_DV = 4, 64, 128
GLA_GATE_RANK = 16
GLA_NORMALIZER = 16.0
N_GROUPS, EXPERTS_PER_GROUP, N_EXPERTS = 4, 4, 16
EXPERT_HIDDEN = 512
NORM_EPS = 1e-6
LANES = 128
ROW_TILES = D_MODEL // LANES
VMEM_LIMIT = 56 * 1024 * 1024

_REF_SPLITS = (
    ('ml_q', 512), ('ml_k', 512), ('ml_v', 512), ('ml_o', 512), ('ml_g', 16),
    ('mla_ql', Q_LORA), ('mla_ckv', KV_LORA), ('mla_kr', MLA_ROPE),
    ('rw_r', 512), ('rw_k', 512), ('rw_v', 512), ('rw_wd', 64), ('rw_ad', 64), ('rw_gd', 128),
    ('gla_q', 256), ('gla_k', 256), ('gla_v', 512), ('gla_gd', GLA_GATE_RANK), ('gla_g', 512),
)
_REF_OFF = {}
_o = 0
for _n, _w in _REF_SPLITS:
    _REF_OFF[_n] = (_o, _w)
    _o += _w
IN_COLS = _o

_PACKED = (
    ('ml_q', 512), ('ml_k', 512), ('ml_v', 512), ('ml_o', 512),
    ('rw_r', 512), ('rw_k', 512), ('rw_v', 512), ('gla_v', 512), ('gla_g', 512),
    ('mla_ql', 384), ('mla_ckv', 256), ('mla_kr', 64), ('mla_kr_sw', 64),
    ('gla_q', 256), ('gla_k', 256),
    ('ml_g', 128), ('rw_wd', 64), ('rw_ad', 64), ('rw_gd', 128), ('gla_gd', 128),
)
PK_OFF = {}
_o = 0
for _n, _w in _PACKED:
    PK_OFF[_n] = _o
    _o += _w
PK_COLS = _o


def _rope_swap_perm():
    idx = np.arange(MLA_ROPE)
    axis, half, f = idx // 32, (idx % 32) // 16, idx % 16
    return axis * 32 + (1 - half) * 16 + f


def _packed_column_index():
    src = np.full((PK_COLS,), -1, np.int64)
    for name, width in _PACKED:
        off = PK_OFF[name]
        if name == 'mla_kr_sw':
            s, w = _REF_OFF['mla_kr']
            src[off:off + w] = s + _rope_swap_perm()
        else:
            s, w = _REF_OFF[name]
            src[off:off + w] = s + np.arange(w)
    return src


_PK_SRC = _packed_column_index()


def _column_runs(src):
    runs, i = [], 0
    while i < len(src):
        j = i + 1
        while j < len(src) and ((src[i] < 0 and src[j] < 0) or (src[i] >= 0 and src[j] == src[i] + (j - i))):
            j += 1
        runs.append((int(src[i]), j - i))
        i = j
    return runs


_PK_RUNS = _column_runs(_PK_SRC)


def _pack_w_in(w):
    parts = [w[:, s:s + n] if s >= 0 else jnp.zeros((w.shape[0], n), w.dtype) for s, n in _PK_RUNS]
    return jnp.concatenate(parts, axis=1).astype(BF16)


def _cparams(sem):
    return pltpu.CompilerParams(dimension_semantics=sem, vmem_limit_bytes=VMEM_LIMIT)


def _log_sigmoid(x):
    return jnp.minimum(x, 0.0) - jnp.log(1.0 + jnp.exp(-jnp.abs(x)))


def _sigmoid(x):
    return 1.0 / (1.0 + jnp.exp(-x))


def _dot(a, b):
    return jnp.dot(a.astype(BF16), b.astype(BF16), preferred_element_type=F32)


def _dot_nt(a, b):
    return lax.dot_general(a.astype(BF16), b.astype(BF16), (((1,), (1,)), ((), ())), preferred_element_type=F32)


def _dot_tn(a, b):
    return lax.dot_general(a.astype(BF16), b.astype(BF16), (((0,), (0,)), ((), ())), preferred_element_type=F32)


def _split2(x):
    hi = x.astype(BF16)
    return hi, (x - hi.astype(F32)).astype(BF16)


def _split_weight(w):
    hi, lo = _split2(w)
    return jnp.stack([hi, lo])


def _running_sum(tri, x):
    hi = x.astype(BF16)
    r = x - hi.astype(F32)
    mid = r.astype(BF16)
    lo = (r - mid.astype(F32)).astype(BF16)
    n = x.shape[1]
    s = jnp.dot(tri, jnp.concatenate([hi, mid, lo], axis=1), preferred_element_type=F32)
    return s[:, :n] + s[:, n:2 * n] + s[:, 2 * n:]


def _dot_split(a, w_ref):
    a_hi, a_lo = _split2(a)
    w_hi = w_ref[0]
    return (jnp.dot(a_hi, w_hi, preferred_element_type=F32) + jnp.dot(a_lo, w_hi, preferred_element_type=F32)
            + jnp.dot(a_hi, w_ref[1], preferred_element_type=F32))


def _mod_kernel(c_ref, w_ref, b_ref, o_ref):
    c = c_ref[...]
    s = c * _sigmoid(c)
    o_ref[0] = _dot(s, w_ref[0]) + b_ref[0]


def _modulation(cc, w_ada, b_ada):
    tn = 1536
    n = 6 * D_MODEL
    return pl.pallas_call(
        _mod_kernel,
        grid=(DEPTH, n // tn),
        in_specs=[pl.BlockSpec((8, D_MODEL), lambda l, j: (0, 0)),
                  pl.BlockSpec((1, D_MODEL, tn), lambda l, j: (l, 0, j)),
                  pl.BlockSpec((1, 1, tn), lambda l, j: (l, 0, j))],
        out_specs=pl.BlockSpec((1, 8, tn), lambda l, j: (l, 0, j)),
        out_shape=jax.ShapeDtypeStruct((DEPTH, 8, n), F32),
        compiler_params=_cparams(("arbitrary", "arbitrary")),
        name="adaln_mod",
    )(cc, w_ada, b_ada.reshape(DEPTH, 1, n))


def _inproj_kernel(x_ref, mod_ref, g_ref, w_ref, o_ref, h_scr):
    @pl.when(pl.program_id(1) == 0)
    def _():
        x = x_ref[...]
        xn = x * lax.rsqrt(jnp.mean(x * x, axis=-1, keepdims=True) + NORM_EPS) * g_ref[...]
        h_scr[...] = (xn * (1.0 + mod_ref[0, 1:2, :]) + mod_ref[0, 0:1, :]).astype(BF16)

    o_ref[...] = jnp.dot(h_scr[...], w_ref[...], preferred_element_type=F32)


def _in_proj(x2, mod, g, w_packed, T):
    n_tok = x2.shape[0]
    tm = 1024
    tn = 1280
    bm = mod.shape[0]
    mod_idx = (lambda i, j: (i * tm // T, 0, 0)) if bm > 1 else (lambda i, j: (0, 0, 0))
    return pl.pallas_call(
        _inproj_kernel,
        grid=(n_tok // tm, PK_COLS // tn),
        in_specs=[pl.BlockSpec((tm, D_MODEL), lambda i, j: (i, 0)),
                  pl.BlockSpec((1, 8, D_MODEL), mod_idx),
                  pl.BlockSpec((1, D_MODEL), lambda i, j: (0, 0)),
                  pl.BlockSpec((D_MODEL, tn), lambda i, j: (0, j))],
        out_specs=pl.BlockSpec((tm, tn), lambda i, j: (i, j)),
        out_shape=jax.ShapeDtypeStruct((n_tok, PK_COLS), F32),
        scratch_shapes=[pltpu.VMEM((tm, D_MODEL), BF16)],
        compiler_params=_cparams(("arbitrary", "arbitrary")),
        name="in_proj",
    )(x2, mod, g, w_packed)


ML_CHUNK = 256


def _mlstm_kernel(q_ref, k_ref, v_ref, o_ref, g_ref, bias_ref, gn_ref, c0_ref, m0_ref,
                  out_ref, c_ref, m_ref, hs_ref, *, T):
    L = ML_CHUNK
    nc = T // L
    c_ref[...] = c0_ref[...]
    m_ref[...] = m0_ref[...]
    ii = lax.broadcasted_iota(jnp.int32, (L, L), 0)
    jj = lax.broadcasted_iota(jnp.int32, (L, L), 1)
    lane = lax.broadcasted_iota(jnp.int32, (1, LANES), 1)
    is_f = jnp.logical_and(lane % 8 >= 4, lane < 16)
    ones_col = (lax.broadcasted_iota(jnp.int32, (L, LANES), 1) == 0).astype(BF16)
    scale = ML_DK ** -0.5

    def chunk(ci, carry):
        ch = []
        for d in range(2):
            mask = (jj <= ii) if d == 0 else (jj >= ii)
            c = ci if d == 0 else nc - 1 - ci
            s = pl.multiple_of(c * L, L)
            gates = g_ref[0, pl.ds(s, L), :] + bias_ref[...]
            gf = jnp.where(is_f, _log_sigmoid(gates), gates)
            cum = _running_sum(mask.astype(BF16), gf)
            gf_t = gf.T
            cum_t = cum.T
            for h in range(ML_H):
                ci_, cf_ = d * 8 + h, d * 8 + 4 + h
                hs = slice(h * ML_DK, (h + 1) * ML_DK)
                ig_col, ig_row = gf[:, ci_:ci_ + 1], gf_t[ci_:ci_ + 1, :]
                b_col, b_row = cum[:, cf_:cf_ + 1], cum_t[cf_:cf_ + 1, :]
                b_last = b_col[L - 1:L, :] if d == 0 else b_col[0:1, :]
                m_prev = m_ref[0, d, h][:, 0:1]
                dmat = jnp.where(mask, b_col + (ig_row - b_row), -jnp.inf)
                m_inter = b_col + m_prev
                m_row = jnp.maximum(m_inter, jnp.max(dmat, axis=-1, keepdims=True))
                dk_col = b_last - b_col + ig_col
                m_new = jnp.maximum(b_last + m_prev, jnp.max(dk_col, axis=0, keepdims=True))
                ch.append(dict(
                    d=d, h=h, s=s, hs=hs, dmat=dmat, m_row=m_row, w_inter=jnp.exp(m_inter - m_row), m_new=m_new,
                    w_key=jnp.exp(dk_col - m_new), c_scale=jnp.exp(b_last + m_prev - m_new),
                    q=(q_ref[0, pl.ds(s, L), hs] * scale).astype(BF16), k=k_ref[0, pl.ds(s, L), hs],
                    v_aug=jnp.concatenate([v_ref[0, pl.ds(s, L), hs].astype(BF16), ones_col], axis=1),
                    c_aug=c_ref[0, d, h]))
        qk = [_dot_nt(c['q'], c['k']) for c in ch]
        qc = [_dot(c['q'], c['c_aug']) for c in ch]
        sv = [_dot(s_ * jnp.exp(c['dmat'] - c['m_row']), c['v_aug']) for s_, c in zip(qk, ch)]
        kv = [_dot_tn(c['k'] * c['w_key'], c['v_aug']) for c in ch]
        for c, qc_, sv_, kv_ in zip(ch, qc, sv, kv):
            nd = c['w_inter'] * qc_ + sv_
            num, den = nd[:, :ML_DK], nd[:, ML_DK:ML_DK + 1]
            hh = num / jnp.maximum(jnp.abs(den), jnp.exp(-c['m_row']))
            hs_ref[pl.ds(c['s'], L), c['d'] * 512 + c['h'] * ML_DK:c['d'] * 512 + (c['h'] + 1) * ML_DK] = hh
            c_ref[0, c['d'], c['h']] = c['c_scale'] * c['c_aug'] + kv_
            m_ref[0, c['d'], c['h']] = jnp.broadcast_to(c['m_new'], (1, LANES))
        return carry

    lax.fori_loop(0, nc, chunk, 0)

    def finish(ci, carry):
        s = pl.multiple_of(ci * L, L)
        for h in range(ML_H):
            hs = slice(h * ML_DK, (h + 1) * ML_DK)
            x = hs_ref[pl.ds(s, L), hs] + hs_ref[pl.ds(s, L), 512 + h * ML_DK:512 + (h + 1) * ML_DK]
            xc = x - jnp.mean(x, axis=-1, keepdims=True)
            y = xc * lax.rsqrt(jnp.mean(xc * xc, axis=-1, keepdims=True) + NORM_EPS) * gn_ref[:, hs]
            out_ref[0, pl.ds(s, L), hs] = (y * _sigmoid(o_ref[0, pl.ds(s, L), hs])).astype(BF16)
        return carry

    lax.fori_loop(0, nc, finish, 0)


def _mlstm_params(p):
    bias = jnp.pad(p['b_ml_gates'].reshape(1, -1), ((0, 0), (0, LANES - 4 * ML_H)))
    return bias, p['g_ml_norm'].reshape(1, 512)


def _mlstm_state_in(C0, n0, m0):
    c0 = jnp.concatenate([C0, n0[..., None], jnp.zeros(C0.shape[:-1] + (ML_DK - 1,), F32)], axis=-1)
    return c0, jnp.broadcast_to(m0[..., None, None], m0.shape + (1, LANES))


def _mlstm_state_zero(B):
    return jnp.zeros((B, 2, ML_H, ML_DK, 2 * ML_DK), F32), jnp.zeros((B, 2, ML_H, 1, LANES), F32)


def _mlstm_state_out(c, m):
    return c[..., :ML_DK], c[..., ML_DK], m[..., 0, 0]


def _mlstm(u3, bias, gnorm, c0, m0):
    B, T, _ = u3.shape
    blk = lambda name: pl.BlockSpec((1, T, 512), lambda b, o=PK_OFF[name] // 512: (b, 0, o))
    st_c = pl.BlockSpec((1, 2, ML_H, ML_DK, 2 * ML_DK), lambda b: (b, 0, 0, 0, 0))
    st_m = pl.BlockSpec((1, 2, ML_H, 1, LANES), lambda b: (b, 0, 0, 0, 0))
    return pl.pallas_call(
        functools.partial(_mlstm_kernel, T=T),
        grid=(B,),
        in_specs=[blk('ml_q'), blk('ml_k'), blk('ml_v'), blk('ml_o'),
                  pl.BlockSpec((1, T, LANES), lambda b: (b, 0, PK_OFF['ml_g'] // LANES)),
                  pl.BlockSpec((1, LANES), lambda b: (0, 0)),
                  pl.BlockSpec((1, 512), lambda b: (0, 0)),
                  st_c, st_m],
        out_specs=[pl.BlockSpec((1, T, 512), lambda b: (b, 0, 0)), st_c, st_m],
        out_shape=[jax.ShapeDtypeStruct((B, T, 512), BF16),
                   jax.ShapeDtypeStruct(c0.shape, F32),
                   jax.ShapeDtypeStruct(m0.shape, F32)],
        scratch_shapes=[pltpu.VMEM((T, 1024), F32)],
        compiler_params=_cparams(("arbitrary",)),
        name="mlstm",
    )(u3, u3, u3, u3, u3, bias, gnorm, c0, m0)


MLA_BLK = 256


def _rope_tables(T):
    rows = T // GRID_W
    row = np.repeat(np.arange(rows, dtype=np.float64), GRID_W)
    col = np.tile(np.arange(GRID_W, dtype=np.float64), rows)
    inv = ROPE_THETA ** (-np.arange(MLA_ROPE // 4, dtype=np.float64) / (MLA_ROPE // 4))
    ang = np.stack([row[:, None] * inv, col[:, None] * inv], axis=1)
    cos = np.stack([np.cos(ang), np.cos(ang)], axis=2).reshape(T, MLA_ROPE)
    sin = np.stack([-np.sin(ang), np.sin(ang)], axis=2).reshape(T, MLA_ROPE)
    return jnp.asarray(np.concatenate([cos, sin], axis=1), F32)


def _mla_kernel(*refs, T, n_ctx, rope):
    if n_ctx:
        (u_ref, ctx_ref, cs_ref, gql_ref, gkv_ref, gains_ref, wqn_ref, wqr_ref, wqs_ref, wkn_ref, wv_ref,
         out_ref, kv_ref, qn_s, qr_s, kn_s, kr_s, v_s) = refs
    else:
        (u_ref, cs_ref, gql_ref, gkv_ref, gains_ref, wqn_ref, wqr_ref, wqs_ref, wkn_ref, wv_ref,
         out_ref, kv_ref, qn_s, qr_s, kn_s, kr_s, v_s) = refs
    Lb = MLA_BLK
    gq_n, gq_r, gq_s = gains_ref[0:1, :], gains_ref[1:2, 0:64], gains_ref[1:2, 64:128]
    gk_n, gk_r, gk_s = gains_ref[2:3, :], gains_ref[3:4, 0:64], gains_ref[3:4, 64:128]
    sm_scale = MLA_QK ** -0.5

    def store_keys(s, kn, kr, krs, cos, sin):
        kr_ss = jnp.sum(kr * kr, axis=-1, keepdims=True)
        for h in range(MLA_H):
            kn_h = kn[:, h * 128:(h + 1) * 128]
            rk = lax.rsqrt((jnp.sum(kn_h * kn_h, axis=-1, keepdims=True) + kr_ss) / MLA_QK + NORM_EPS)
            kn_s[pl.ds(s, Lb), h * 128:(h + 1) * 128] = (kn_h * rk * gk_n).astype(BF16)
            kr_h = kr * gk_r
            if cos is not None:
                kr_h = kr_h * cos + (krs * gk_s) * sin
            kr_s[pl.ds(s, Lb), h * 64:(h + 1) * 64] = (kr_h * rk).astype(BF16)

    def prep(ci, carry):
        s = pl.multiple_of(ci * Lb, Lb)
        u = u_ref[0, pl.ds(s, Lb), :]
        ql, ckv, kr, krs = u[:, :384], u[:, 384:640], u[:, 640:704], u[:, 704:768]
        qln = ql * lax.rsqrt(jnp.mean(ql * ql, axis=-1, keepdims=True) + NORM_EPS) * gql_ref[...]
        ckvn = ckv * lax.rsqrt(jnp.mean(ckv * ckv, axis=-1, keepdims=True) + NORM_EPS) * gkv_ref[...]
        kv_ref[0, pl.ds(s, Lb), :] = jnp.concatenate([ckvn, kr], axis=1)
        cos = cs_ref[pl.ds(s, Lb), 0:64] if rope else None
        sin = cs_ref[pl.ds(s, Lb), 64:128] if rope else None
        qn, qr, qs = _dot(qln, wqn_ref[...]), _dot(qln, wqr_ref[...]), _dot(qln, wqs_ref[...])
        for h in range(MLA_H):
            qn_h, qr_h = qn[:, h * 128:(h + 1) * 128], qr[:, h * 64:(h + 1) * 64]
            ss = jnp.sum(qn_h * qn_h, axis=-1, keepdims=True) + jnp.sum(qr_h * qr_h, axis=-1, keepdims=True)
            rq = lax.rsqrt(ss / MLA_QK + NORM_EPS) * sm_scale
            qn_s[pl.ds(s, Lb), h * 128:(h + 1) * 128] = (qn_h * rq * gq_n).astype(BF16)
            qr_h = qr_h * gq_r
            if rope:
                qr_h = qr_h * cos + (qs[:, h * 64:(h + 1) * 64] * gq_s) * sin
            qr_s[pl.ds(s, Lb), h * 64:(h + 1) * 64] = (qr_h * rq).astype(BF16)
        v_s[pl.ds(s, Lb), :] = _dot(ckvn, wv_ref[...]).astype(BF16)
        store_keys(s, _dot(ckvn, wkn_ref[...]), kr, krs, cos, sin)
        return carry

    lax.fori_loop(0, T // Lb, prep, 0)

    for ci in range(n_ctx // Lb):
        cx = ctx_ref[0, ci * Lb:(ci + 1) * Lb, :]
        ckv_c, kr_c = cx[:, :KV_LORA], cx[:, KV_LORA:KV_LORA + MLA_ROPE]
        v_s[T + ci * Lb:T + (ci + 1) * Lb, :] = _dot(ckv_c, wv_ref[...]).astype(BF16)
        store_keys(T + ci * Lb, _dot(ckv_c, wkn_ref[...]), kr_c, None, None, None)

    def attend(qi, carry):
        s = pl.multiple_of(qi * Lb, Lb)
        heads = range(MLA_H)
        sc = [_dot_nt(qn_s[pl.ds(s, Lb), h * 128:(h + 1) * 128], kn_s[:, h * 128:(h + 1) * 128])
              + _dot_nt(qr_s[pl.ds(s, Lb), h * 64:(h + 1) * 64], kr_s[:, h * 64:(h + 1) * 64]) for h in heads]
        p = [jnp.exp(x - jnp.max(x, axis=-1, keepdims=True)) for x in sc]
        o = [_dot(p[h], v_s[:, h * 128:(h + 1) * 128]) / jnp.sum(p[h], axis=-1, keepdims=True) for h in heads]
        for h in heads:
            out_ref[0, pl.ds(s, Lb), h * 128:(h + 1) * 128] = o[h].astype(BF16)
        return carry

    lax.fori_loop(0, T // Lb, attend, 0)


def _mla_params(p):
    def swap(t):
        return jnp.flip(t.reshape(t.shape[:-1] + (2, 2, MLA_ROPE // 4)), axis=-2).reshape(t.shape)

    wq = p['w_mla_uq'].reshape(Q_LORA, MLA_H, MLA_QK)
    wq_n = wq[:, :, :MLA_NOPE].reshape(Q_LORA, 512).astype(BF16)
    wq_r = wq[:, :, MLA_NOPE:].reshape(Q_LORA, 256).astype(BF16)
    wq_s = swap(wq[:, :, MLA_NOPE:]).reshape(Q_LORA, 256).astype(BF16)
    wkv = p['w_mla_ukv'].reshape(KV_LORA, MLA_H, MLA_NOPE + MLA_V)
    wk_n = wkv[:, :, :MLA_NOPE].reshape(KV_LORA, 512).astype(BF16)
    wv = wkv[:, :, MLA_NOPE:].reshape(KV_LORA, 512).astype(BF16)
    gq, gk = p['g_mla_qn'], p['g_mla_kn']
    gains = jnp.concatenate([gq[:128], gq[128:], swap(gq[128:]), gk[:128], gk[128:], swap(gk[128:]),
                             jnp.zeros((4 * LANES,), F32)]).reshape(8, LANES)
    return (p['g_mla_qlat'].reshape(1, -1), p['g_mla_kvlat'].reshape(1, -1), gains, wq_n, wq_r, wq_s, wk_n, wv)


def _mla(u3, ctx_kv, mp):
    B, T, _ = u3.shape
    n_ctx = 0 if ctx_kv is None else ctx_kv.shape[1]
    rope = ctx_kv is not None
    tk = T + n_ctx
    full = lambda a: pl.BlockSpec(a.shape, lambda b, n=a.ndim: (0,) * n)
    cs = _rope_tables(T) if rope else jnp.zeros((T, LANES), F32)
    ins = [u3] + ([ctx_kv] if rope else []) + [cs] + list(mp)
    specs = [pl.BlockSpec((1, T, 768), lambda b: (b, 0, PK_OFF['mla_ql'] // 768))]
    if rope:
        specs.append(pl.BlockSpec((1, n_ctx, KV_LORA + MLA_ROPE), lambda b: (b, 0, 0)))
    specs += [full(a) for a in ins[len(specs):]]
    return pl.pallas_call(
        functools.partial(_mla_kernel, T=T, n_ctx=n_ctx, rope=rope),
        grid=(B,),
        in_specs=specs,
        out_specs=[pl.BlockSpec((1, T, 512), lambda b: (b, 0, 0)),
                   pl.BlockSpec((1, T, KV_LORA + MLA_ROPE), lambda b: (b, 0, 0))],
        out_shape=[jax.ShapeDtypeStruct((B, T, 512), BF16),
                   jax.ShapeDtypeStruct((B, T, KV_LORA + MLA_ROPE), F32)],
        scratch_shapes=[pltpu.VMEM((T, 512), BF16), pltpu.VMEM((T, 256), BF16),
                        pltpu.VMEM((tk, 512), BF16), pltpu.VMEM((tk, 256), BF16), pltpu.VMEM((tk, 512), BF16)],
        compiler_params=_cparams(("arbitrary",)),
        name="mla",
    )(*ins)


GLA_CHUNK = 64
GLA_LEAF = 4
GLA_UNROLL = 2


def _gla_kernel(q_ref, k_ref, v_ref, gd_ref, g_ref, gup_ref, gb_ref, gn_ref, hsel_ref, s0_ref,
                out_ref, s_ref, os_ref, la_ref, *, T):
    L, C = GLA_CHUNK, GLA_LEAF
    nc = T // L
    Lf = 256
    s_ref[...] = s0_ref[...]

    def gates(ci, carry):
        s = pl.multiple_of(ci * Lf, Lf)
        x = _dot_split(gd_ref[0, pl.ds(s, Lf), :], gup_ref) + gb_ref[...]
        la_ref[pl.ds(s, Lf), :] = _log_sigmoid(x) / GLA_NORMALIZER
        return carry

    lax.fori_loop(0, T // Lf, gates, 0)
    ii = lax.broadcasted_iota(jnp.int32, (L, L), 0)
    jj = lax.broadcasted_iota(jnp.int32, (L, L), 1)
    eye = (lax.broadcasted_iota(jnp.int32, (GLA_DK, GLA_DK), 0)
           == lax.broadcasted_iota(jnp.int32, (GLA_DK, GLA_DK), 1)).astype(F32)
    row_id = lax.broadcasted_iota(jnp.int32, (L, 1), 0)
    scale = GLA_DK ** -0.5
    hsel = hsel_ref[...]
    levels = []
    span = C
    while span < L:
        levels.append(span)
        span *= 2

    U = GLA_UNROLL

    def chunk(ci, carry):
        parts = []
        for d in range(2):
            causal = (jj <= ii) if d == 0 else (jj >= ii)
            tri = causal.astype(BF16)
            for u in range(U):
                c = ci * U + u if d == 0 else nc - 1 - (ci * U + u)
                s = pl.multiple_of(c * L, L)
                la = la_ref[pl.ds(s, L), d * GLA_H * GLA_DK:(d + 1) * GLA_H * GLA_DK]
                b = _running_sum(tri, la)
                total = b[L - 1:L, :] if d == 0 else b[0:1, :]
                q = q_ref[0, pl.ds(s, L), :] * scale
                k = k_ref[0, pl.ds(s, L), :]
                lv = []
                for sp in levels:
                    b3 = b.reshape(L // (2 * sp), 2 * sp, GLA_H * GLA_DK)
                    edge = b3[:, sp - 1:sp, :] if d == 0 else b3[:, sp:sp + 1, :]
                    bref = jnp.broadcast_to(edge, b3.shape).reshape(L, GLA_H * GLA_DK)
                    later = (row_id % (2 * sp) >= sp) if d == 0 else (row_id % (2 * sp) < sp)
                    e = jnp.exp(jnp.where(later, b - bref, bref - b))
                    lv.append((jnp.where(later, q * e, 0.0), jnp.where(later, 0.0, k * e),
                               (ii // (2 * sp)) == (jj // (2 * sp))))
                terms = []
                for dl in range(C):
                    if dl == 0:
                        terms.append(q * k)
                        continue
                    sh = dl if d == 0 else L - dl
                    ok = (row_id % C >= dl) if d == 0 else (row_id % C < C - dl)
                    kd = pltpu.roll(k, sh, axis=0)
                    bd = pltpu.roll(b, sh, axis=0)
                    terms.append(q * kd * jnp.exp(jnp.where(ok, b - bd, 0.0)))
                tt = jnp.concatenate(terms, axis=0)
                t_hi = tt.astype(BF16)
                t_lo = (tt - t_hi.astype(F32)).astype(BF16)
                diag = (jnp.dot(t_hi, hsel, preferred_element_type=F32)
                        + jnp.dot(t_lo, hsel, preferred_element_type=F32))
                parts.append(dict(d=d, s=s, lv=lv, diag=diag, q_in=q * jnp.exp(b), k_out=k * jnp.exp(total - b),
                                  f_row=jnp.exp(total)))
        chains = [(p, h) for p in parts for h in range(GLA_H)]
        amats = []
        for p, h in chains:
            ks = slice(h * GLA_DK, (h + 1) * GLA_DK)
            a = jnp.zeros((L, L), F32)
            for qs, kt, same in p['lv']:
                a = a + jnp.where(same, _dot_nt(qs[:, ks], kt[:, ks]), 0.0)
            for dl in range(C):
                pair = jnp.logical_and(jj == (ii - dl if p['d'] == 0 else ii + dl), ii // C == jj // C)
                a = a + jnp.where(pair, p['diag'][dl * L:(dl + 1) * L, h:h + 1], 0.0)
            amats.append(a)
        vals = [v_ref[0, pl.ds(p['s'], L), h * GLA_DV:(h + 1) * GLA_DV] for p, h in chains]
        intra = [_dot(a, v) for a, v in zip(amats, vals)]
        upd = [_dot_tn(p['k_out'][:, h * GLA_DK:(h + 1) * GLA_DK], v) for (p, h), v in zip(chains, vals)]
        for d in range(2):
            for h in range(GLA_H):
                ks = slice(h * GLA_DK, (h + 1) * GLA_DK)
                st = s_ref[0, d, h]
                for u in range(U):
                    i = (d * U + u) * GLA_H + h
                    p = parts[d * U + u]
                    o = intra[i] + _dot(p['q_in'][:, ks], st)
                    os_ref[pl.ds(p['s'], L), d * 512 + h * GLA_DV:d * 512 + (h + 1) * GLA_DV] = o
                    f_col = jnp.sum(eye * p['f_row'][:, ks], axis=1, keepdims=True)
                    st = f_col * st + upd[i]
                s_ref[0, d, h] = st
        return carry

    lax.fori_loop(0, nc // U, chunk, 0)

    def finish(ci, carry):
        s = pl.multiple_of(ci * Lf, Lf)
        for h in range(GLA_H):
            vs = slice(h * GLA_DV, (h + 1) * GLA_DV)
            o = os_ref[pl.ds(s, Lf), vs] + os_ref[pl.ds(s, Lf), 512 + h * GLA_DV:512 + (h + 1) * GLA_DV]
            y = o * lax.rsqrt(jnp.mean(o * o, axis=-1, keepdims=True) + NORM_EPS) * gn_ref[:, vs]
            g = g_ref[0, pl.ds(s, Lf), vs]
            out_ref[0, pl.ds(s, Lf), vs] = (y * (g * _sigmoid(g))).astype(BF16)
        return carry

    lax.fori_loop(0, T // Lf, finish, 0)


def _gla_params(p):
    gup = jnp.pad(jnp.concatenate([p['gla_g_up'][0], p['gla_g_up'][1]], axis=1),
                  ((0, LANES - GLA_GATE_RANK), (0, 0)))
    hsel = jnp.asarray(np.arange(GLA_H * GLA_DK)[:, None] // GLA_DK == np.arange(LANES)[None, :], BF16)
    return _split_weight(gup), p['gla_g_b'].reshape(1, -1), p['gla_norm'].reshape(1, -1), hsel


def _gla(u3, gp, s0):
    B, T, _ = u3.shape
    blk = lambda name, w: pl.BlockSpec((1, T, w), lambda b, o=PK_OFF[name] // w: (b, 0, o))
    full = lambda a: pl.BlockSpec(a.shape, lambda b, n=a.ndim: (0,) * n)
    st = pl.BlockSpec((1, 2, GLA_H, GLA_DK, GLA_DV), lambda b: (b, 0, 0, 0, 0))
    return pl.pallas_call(
        functools.partial(_gla_kernel, T=T),
        grid=(B,),
        in_specs=[blk('gla_q', 256), blk('gla_k', 256), blk('gla_v', 512), blk('gla_gd', 128), blk('gla_g', 512),
                  full(gp[0]), full(gp[1]), full(gp[2]), full(gp[3]), st],
        out_specs=[pl.BlockSpec((1, T, 512), lambda b: (b, 0, 0)), st],
        out_shape=[jax.ShapeDtypeStruct((B, T, 512), BF16), jax.ShapeDtypeStruct(s0.shape, F32)],
        scratch_shapes=[pltpu.VMEM((T, 1024), F32), pltpu.VMEM((T, 2 * GLA_H * GLA_DK), F32)],
        compiler_params=_cparams(("arbitrary",)),
        name="gla",
    )(u3, u3, u3, u3, u3, *gp, s0)


RW_CHUNK = 64
RW_UNROLL = 2


def _seg_sum(x, bd):
    hi = x.astype(BF16)
    lo = (x - hi.astype(F32)).astype(BF16)
    return jnp.dot(hi, bd, preferred_element_type=F32) + jnp.dot(lo, bd, preferred_element_type=F32)


def _rwkv_kernel(r_ref, k_ref, v_ref, wa_ref, gd_ref, wwa_ref, w0a0_ref, gup_ref, kk_ref, ka_ref, rk_ref, ln_ref,
                 bd_ref, h0_ref, out_ref, h_ref, ys_ref, pre_ref, *, T):
    L, N = RW_CHUNK, RW_N
    nc = T // L
    h_ref[...] = h0_ref[...]
    ii = lax.broadcasted_iota(jnp.int32, (L, L), 0)
    jj = lax.broadcasted_iota(jnp.int32, (L, L), 1)
    eye = (ii == jj).astype(F32)
    lane = lax.broadcasted_iota(jnp.int32, (1, LANES), 1)
    bd = bd_ref[...]
    Lf = 256

    def gates(ci, carry):
        s = pl.multiple_of(ci * Lf, Lf)
        wa = wa_ref[0, pl.ds(s, Lf), :]
        pre_ref[pl.ds(s, Lf), :] = _dot_split(jnp.where(lane < 64, jnp.tanh(wa), wa), wwa_ref) + w0a0_ref[...]
        return carry

    lax.fori_loop(0, T // Lf, gates, 0)

    masks = [((jj < ii), (jj <= ii)), ((jj > ii), (jj >= ii))]
    U = RW_UNROLL

    def chunk(ci, carry):
        chains = []
        for d in range(2):
            strict, incl = masks[d]
            tri = incl.astype(BF16)
            for u in range(U):
                c = ci * U + u if d == 0 else nc - 1 - (ci * U + u)
                s = pl.multiple_of(c * L, L)
                r = r_ref[0, pl.ds(s, L), :]
                k = k_ref[0, pl.ds(s, L), :]
                v = v_ref[0, pl.ds(s, L), :]
                pre = pre_ref[pl.ds(s, L), d * 1024:(d + 1) * 1024]
                logw = -RW_DECAY_SCALE * _sigmoid(pre[:, :512])
                a = _sigmoid(pre[:, 512:])
                kkr = k * kk_ref[...]
                kk = kkr * lax.rsqrt(_seg_sum(kkr * kkr, bd) + 1e-12)
                kt = k * (1.0 + (a - 1.0) * ka_ref[...])
                bh = kk * a
                lg = _running_sum(tri, logw)
                lg_end = lg[L - 1:L, :] if d == 0 else lg[0:1, :]
                a_t = -kk * jnp.exp(lg - logw)
                r_t = r * jnp.exp(lg)
                e_inv = jnp.exp(-lg)
                k_t, b_t = kt * e_inv, bh * e_inv
                e_end = jnp.exp(lg_end - lg)
                k_e, b_e = kt * e_end, bh * e_end
                g_end = jnp.exp(lg_end)
                for h in range(RW_H):
                    sl = slice(h * N, (h + 1) * N)
                    chains.append(dict(d=d, u=u, h=h, s=s, strict=strict, incl=incl, a=a_t[:, sl], r=r_t[:, sl],
                                       b=b_t[:, sl], k=k_t[:, sl], ke=k_e[:, sl], be=b_e[:, sl], g=g_end[:, sl],
                                       v=v[:, sl]))
        ms = [_dot_nt(jnp.concatenate([c['a'], c['r']], axis=0), jnp.concatenate([c['b'], c['k']], axis=0))
              for c in chains]
        pws = [jnp.where(c['strict'], m[:L, :L], 0.0) for c, m in zip(chains, ms)]
        xs = [eye + n for n in pws]
        for _ in range(5):
            pws = [_dot(pw, pw) for pw in pws]
            xs = [x + _dot(x, pw) for x, pw in zip(xs, pws)]
        mvs = [_dot(jnp.where(c['strict'], m[:L, L:], 0.0), c['v']) for c, m in zip(chains, ms)]
        tws = [_dot(x, jnp.concatenate([c['a'], mv], axis=1)) for x, c, mv in zip(xs, chains, mvs)]
        qys = [_dot(jnp.where(c['incl'], m[L:, :L], 0.0), tw) for c, m, tw in zip(chains, ms, tws)]
        ylocs = [_dot(jnp.where(c['incl'], m[L:, L:], 0.0), c['v']) + qy[:, N:] for c, m, qy in zip(chains, ms, qys)]
        pgs = [_dot_tn(c['be'], tw) for c, tw in zip(chains, tws)]
        gmats = [_dot_tn(c['ke'], c['v']) + pg[:, N:] for c, pg in zip(chains, pgs)]
        for d in range(2):
            for h in range(RW_H):
                hst = h_ref[0, d, h]
                for u in range(U):
                    i = (d * U + u) * RW_H + h
                    c = chains[i]
                    y = _dot(c['r'] + qys[i][:, :N], hst) + ylocs[i]
                    ys_ref[pl.ds(c['s'], L), d * 512 + h * N:d * 512 + (h + 1) * N] = y
                    hst = _dot(eye * c['g'] + pgs[i][:, :N], hst) + gmats[i]
                h_ref[0, d, h] = hst
        return carry

    lax.fori_loop(0, nc // U, chunk, 0)

    def finish(ci, carry):
        s = pl.multiple_of(ci * Lf, Lf)
        r = r_ref[0, pl.ds(s, Lf), :]
        k = k_ref[0, pl.ds(s, Lf), :]
        rk = r * k * rk_ref[...]
        bonus = jnp.zeros((Lf, 512), F32)
        for d in range(2):
            a = _sigmoid(pre_ref[pl.ds(s, Lf), d * 1024 + 512:(d + 1) * 1024])
            bonus = bonus + _seg_sum(rk * (1.0 + (a - 1.0) * ka_ref[...]), bd)
        y = ys_ref[pl.ds(s, Lf), 0:512] + ys_ref[pl.ds(s, Lf), 512:1024]
        yc = y - _seg_sum(y, bd) / N
        yn = yc * lax.rsqrt(_seg_sum(yc * yc, bd) / N + RW_LN_EPS) * ln_ref[...]
        g = _dot(_sigmoid(gd_ref[0, pl.ds(s, Lf), :]), gup_ref[...])
        out_ref[0, pl.ds(s, Lf), :] = ((yn + bonus * v_ref[0, pl.ds(s, Lf), :]) * g).astype(BF16)
        return carry

    lax.fori_loop(0, T // Lf, finish, 0)


def _rwkv_params(p):
    z = jnp.zeros((64, 512), F32)
    wwa = jnp.concatenate([jnp.concatenate([p['rw_w_up'][0], z, p['rw_w_up'][1], z], axis=1),
                           jnp.concatenate([z, p['rw_a_up'][0], z, p['rw_a_up'][1]], axis=1)], axis=0)
    w0a0 = jnp.concatenate([p['rw_w0'][0], p['rw_a0'][0], p['rw_w0'][1], p['rw_a0'][1]]).reshape(1, 2048)
    seg = np.arange(512) // RW_N
    bd = jnp.asarray(seg[:, None] == seg[None, :], BF16)
    row = lambda n: p[n].reshape(1, -1)
    return (_split_weight(wwa), w0a0, p['rw_g_up'].astype(BF16), row('rw_k_k'), row('rw_k_a'), row('rw_r_k'),
            row('rw_ln'), bd)


def _rwkv(u3, rp, h0):
    B, T, _ = u3.shape
    blk = lambda name, w: pl.BlockSpec((1, T, w), lambda b, o=PK_OFF[name] // w: (b, 0, o))
    full = lambda a: pl.BlockSpec(a.shape, lambda b, n=a.ndim: (0,) * n)
    st = pl.BlockSpec((1, 2, RW_H, RW_N, RW_N), lambda b: (b, 0, 0, 0, 0))
    return pl.pallas_call(
        functools.partial(_rwkv_kernel, T=T),
        grid=(B,),
        in_specs=[blk('rw_r', 512), blk('rw_k', 512), blk('rw_v', 512), blk('rw_wd', 128), blk('rw_gd', 128)]
                 + [full(a) for a in rp] + [st],
        out_specs=[pl.BlockSpec((1, T, 512), lambda b: (b, 0, 0)), st],
        out_shape=[jax.ShapeDtypeStruct((B, T, 512), BF16), jax.ShapeDtypeStruct(h0.shape, F32)],
        scratch_shapes=[pltpu.VMEM((T, 1024), F32), pltpu.VMEM((T, 2048), F32)],
        compiler_params=_cparams(("arbitrary",)),
        name="rwkv7",
    )(u3, u3, u3, u3, u3, *rp, h0)


def _route(logits):
    lane = lax.broadcasted_iota(jnp.int32, (1, LANES), 1)
    far = jnp.int32(2 * LANES)
    neg = -jnp.inf
    gl = jnp.where(jnp.logical_and(lane >= N_EXPERTS, lane < N_EXPERTS + N_GROUPS), logits, neg)
    gmax = jnp.max(gl, axis=-1, keepdims=True)
    grp = jnp.min(jnp.where(gl == gmax, lane, far), axis=-1, keepdims=True) - N_EXPERTS
    p_grp = 1.0 / jnp.sum(jnp.exp(gl - gmax), axis=-1, keepdims=True)
    el = jnp.where(jnp.logical_and(lane < N_EXPERTS, lane // EXPERTS_PER_GROUP == grp), logits, neg)
    v1 = jnp.max(el, axis=-1, keepdims=True)
    i1 = jnp.min(jnp.where(el == v1, lane, far), axis=-1, keepdims=True)
    el2 = jnp.where(lane == i1, neg, el)
    v2 = jnp.max(el2, axis=-1, keepdims=True)
    i2 = jnp.min(jnp.where(el2 == v2, lane, far), axis=-1, keepdims=True)
    e = jnp.exp(v2 - v1)
    w1 = 1.0 / (1.0 + e)
    comb = jnp.where(lane == i1, p_grp * w1, jnp.where(lane == i2, p_grp * (e * w1), 0.0))
    sel = jnp.where(lane == i1, 1.0, jnp.where(lane == i2, 2.0, 0.0))
    return comb, sel


def _outproj_kernel(x_ref, m0_ref, m1_ref, m2_ref, m3_ref, w_ref, mod_ref, g_ref, wr_ref, br_ref,
                    xn_ref, h3_ref, comb_ref, sel_ref):
    y = jnp.dot(m0_ref[...], w_ref[0:512, :], preferred_element_type=F32)
    for i, m_ref in enumerate((m1_ref, m2_ref, m3_ref), start=1):
        y = y + jnp.dot(m_ref[...], w_ref[i * 512:(i + 1) * 512, :], preferred_element_type=F32)
    xn = x_ref[...] + mod_ref[0, 2:3, :] * y
    xn_ref[...] = xn
    h = xn * lax.rsqrt(jnp.mean(xn * xn, axis=-1, keepdims=True) + NORM_EPS) * g_ref[...]
    h = h * (1.0 + mod_ref[0, 4:5, :]) + mod_ref[0, 3:4, :]
    tm = h.shape[0]
    for c in range(ROW_TILES):
        h3_ref[pl.ds(c, tm, stride=ROW_TILES), :] = h[:, c * LANES:(c + 1) * LANES]
    comb_ref[...], sel_ref[...] = _route(_dot_split(h, wr_ref) + br_ref[...])


def _out_proj(x2, mixed, w_out, mod, g, router_w, router_b, T):
    n_tok = x2.shape[0]
    tm = 512
    bm = mod.shape[0]
    mod_idx = (lambda i: (i * tm // T, 0, 0)) if bm > 1 else (lambda i: (0, 0, 0))
    row = lambda w: pl.BlockSpec((tm, w), lambda i: (i, 0))
    full = lambda a: pl.BlockSpec(a.shape, lambda i, n=a.ndim: (0,) * n)
    return pl.pallas_call(
        _outproj_kernel,
        grid=(n_tok // tm,),
        in_specs=[row(D_MODEL)] + [row(GROUP_W)] * 4 + [full(w_out), pl.BlockSpec((1, 8, D_MODEL), mod_idx),
                                                       full(g), full(router_w), full(router_b)],
        out_specs=[row(D_MODEL), pl.BlockSpec((tm * ROW_TILES, LANES), lambda i: (i, 0)), row(LANES), row(LANES)],
        out_shape=[jax.ShapeDtypeStruct((n_tok, D_MODEL), F32),
                   jax.ShapeDtypeStruct((n_tok * ROW_TILES, LANES), F32),
                   jax.ShapeDtypeStruct((n_tok, LANES), F32), jax.ShapeDtypeStruct((n_tok, LANES), F32)],
        compiler_params=_cparams(("arbitrary",)),
        name="out_proj",
    )(x2, *mixed, w_out, mod, g, router_w, router_b)


MOE_TM = 256
MOE_ROWS = 256
MOE_DMA_UNROLL = 8


def _moe_rows(n_tok):
    return 2 * n_tok + N_EXPERTS * MOE_TM


def _plan_kernel(sel_ref, pos_ref, tmap_ref):
    n_tok = sel_ref.shape[0]
    blk = 512
    lane = lax.broadcasted_iota(jnp.int32, (1, LANES), 1)
    earlier = (lax.broadcasted_iota(jnp.int32, (blk, blk), 1)
               < lax.broadcasted_iota(jnp.int32, (blk, blk), 0)).astype(BF16)
    before = (lax.broadcasted_iota(jnp.int32, (LANES, LANES), 0)
              < lax.broadcasted_iota(jnp.int32, (LANES, LANES), 1)).astype(BF16)

    def count(i, acc):
        s = pl.multiple_of(i * blk, blk)
        return acc + jnp.sum((sel_ref[pl.ds(s, blk), :] > 0.0).astype(F32), axis=0, keepdims=True)

    counts = lax.fori_loop(0, n_tok // blk, count, jnp.zeros((1, LANES), F32))
    tiles = jnp.floor((counts + (MOE_TM - 1)) * (1.0 / MOE_TM))
    tile_start = _dot(jnp.broadcast_to(tiles, (8, LANES)), before)[0:1, :]
    tile_end = tile_start + tiles
    base = tile_start * MOE_TM
    n_tiles = jnp.sum(tiles, axis=-1, keepdims=True)
    j = lax.broadcasted_iota(jnp.int32, (tmap_ref.shape[0], 1), 0).astype(F32)
    done = jnp.logical_and(tile_end <= j, lane < N_EXPERTS)
    expert = jnp.minimum(jnp.sum(done.astype(F32), axis=-1, keepdims=True), N_EXPERTS - 1.0)
    valid = (j < n_tiles).astype(F32)
    tmap_ref[...] = jnp.where(lane == 0, expert, jnp.where(lane == 1, valid, 0.0)).astype(jnp.int32)

    def place(i, seen):
        s = pl.multiple_of(i * blk, blk)
        sel = sel_ref[pl.ds(s, blk), :]
        one = (sel > 0.0).astype(F32)
        row = base + seen + _dot(earlier, one)
        p1 = jnp.sum(jnp.where(sel == 1.0, row, 0.0), axis=-1, keepdims=True)
        p2 = jnp.sum(jnp.where(sel == 2.0, row, 0.0), axis=-1, keepdims=True)
        pos_ref[pl.ds(s, blk), :] = jnp.where(lane == 0, p1, jnp.where(lane == 1, p2, 0.0)).astype(jnp.int32)
        return seen + jnp.sum(one, axis=0, keepdims=True)

    lax.fori_loop(0, n_tok // blk, place, jnp.zeros((1, LANES), F32))


def _moe_plan(sel):
    n_tok = sel.shape[0]
    n_tiles = _moe_rows(n_tok) // MOE_TM
    pos, tmap = pl.pallas_call(
        _plan_kernel,
        out_shape=[jax.ShapeDtypeStruct((n_tok, LANES), jnp.int32),
                   jax.ShapeDtypeStruct((-(-n_tiles // 8) * 8, LANES), jnp.int32)],
        compiler_params=pltpu.CompilerParams(vmem_limit_bytes=VMEM_LIMIT),
        name="moe_plan",
    )(sel)
    return jnp.transpose(pos[:, :2]), tmap[:n_tiles, 0], tmap[:n_tiles, 1]


def _dispatch_kernel(pos_ref, src_ref, init_ref, dst_ref, sem):
    del init_ref
    base = pl.program_id(0) * MOE_ROWS

    def copy(j, k):
        return pltpu.make_async_copy(src_ref.at[j], dst_ref.at[pos_ref[k, base + j]], sem)

    def start(j, c):
        copy(j, 0).start()
        copy(j, 1).start()
        return c

    def wait(j, c):
        copy(j, 0).wait()
        copy(j, 1).wait()
        return c

    lax.fori_loop(0, MOE_ROWS, start, 0, unroll=MOE_DMA_UNROLL)
    lax.fori_loop(0, MOE_ROWS, wait, 0, unroll=MOE_DMA_UNROLL)


def _dispatch(pos, h3, init):
    n_tok = h3.shape[0]
    any_spec = pl.BlockSpec(memory_space=pl.ANY)
    return pl.pallas_call(
        _dispatch_kernel,
        grid_spec=pltpu.PrefetchScalarGridSpec(
            num_scalar_prefetch=1, grid=(n_tok // MOE_ROWS,),
            in_specs=[pl.BlockSpec((MOE_ROWS, ROW_TILES, LANES), lambda i, p: (i, 0, 0)), any_spec],
            out_specs=any_spec, scratch_shapes=[pltpu.SemaphoreType.DMA(())]),
        out_shape=jax.ShapeDtypeStruct(init.shape, init.dtype),
        input_output_aliases={2: 0},
        compiler_params=pltpu.CompilerParams(dimension_semantics=("arbitrary",)),
        name="moe_dispatch",
    )(pos, h3, init)


def _experts_kernel(te_ref, tv_ref, xs_ref, wg_ref, wu_ref, wd_ref, ys_ref, wg_s, wu_s, wd_s):
    i = pl.program_id(0)
    fresh = jnp.logical_or(i == 0, te_ref[i] != te_ref[jnp.maximum(i - 1, 0)])

    @pl.when(jnp.logical_and(fresh, tv_ref[i] == 1))
    def _():
        wg_s[...] = wg_ref[0, 0].astype(BF16)
        wu_s[...] = wu_ref[0, 0].astype(BF16)
        wd_s[...] = wd_ref[0, 0].astype(BF16)

    @pl.when(tv_ref[i] == 1)
    def _():
        x = jnp.concatenate([xs_ref[pl.ds(c, MOE_TM, stride=ROW_TILES), :] for c in range(ROW_TILES)],
                            axis=1).astype(BF16)
        a = jnp.dot(x, wg_s[...], preferred_element_type=F32)
        b = jnp.dot(x, wu_s[...], preferred_element_type=F32)
        y = jnp.dot(((a * _sigmoid(a)) * b).astype(BF16), wd_s[...], preferred_element_type=F32)
        for c in range(ROW_TILES):
            ys_ref[pl.ds(c, MOE_TM, stride=ROW_TILES), :] = y[:, c * LANES:(c + 1) * LANES]

    @pl.when(tv_ref[i] == 0)
    def _():
        ys_ref[...] = jnp.zeros_like(ys_ref)


def _experts(tile_expert, tile_valid, xs, wg, wu, wd, layer):
    n_rows = xs.shape[0] // ROW_TILES
    rows = pl.BlockSpec((MOE_TM * ROW_TILES, LANES), lambda i, te, tv: (i, 0))
    return pl.pallas_call(
        _experts_kernel,
        grid_spec=pltpu.PrefetchScalarGridSpec(
            num_scalar_prefetch=2, grid=(n_rows // MOE_TM,),
            in_specs=[rows,
                      pl.BlockSpec((1, 1, D_MODEL, EXPERT_HIDDEN), lambda i, te, tv: (layer, te[i], 0, 0)),
                      pl.BlockSpec((1, 1, D_MODEL, EXPERT_HIDDEN), lambda i, te, tv: (layer, te[i], 0, 0)),
                      pl.BlockSpec((1, 1, EXPERT_HIDDEN, D_MODEL), lambda i, te, tv: (layer, te[i], 0, 0))],
            out_specs=rows,
            scratch_shapes=[pltpu.VMEM((D_MODEL, EXPERT_HIDDEN), BF16), pltpu.VMEM((D_MODEL, EXPERT_HIDDEN), BF16),
                            pltpu.VMEM((EXPERT_HIDDEN, D_MODEL), BF16)]),
        out_shape=jax.ShapeDtypeStruct(xs.shape, F32),
        compiler_params=_cparams(("arbitrary",)),
        name="moe_experts",
    )(tile_expert, tile_valid, xs, wg, wu, wd)


def _combine_kernel(pos_ref, ys_ref, comb_ref, sel_ref, xn_ref, mod_ref, o_ref, y_s, sems):
    i = pl.program_id(0)
    slot_rows = 2 * MOE_ROWS * ROW_TILES

    def copy(tile, slot, j, k):
        dst = pl.multiple_of(slot * slot_rows + (k * MOE_ROWS + j) * ROW_TILES, ROW_TILES)
        return pltpu.make_async_copy(ys_ref.at[pos_ref[k, tile * MOE_ROWS + j]], y_s.at[pl.ds(dst, ROW_TILES), :],
                                     sems.at[slot])

    def start_tile(tile, slot):
        def body(j, c):
            copy(tile, slot, j, 0).start()
            copy(tile, slot, j, 1).start()
            return c
        lax.fori_loop(0, MOE_ROWS, body, 0, unroll=MOE_DMA_UNROLL)

    def wait_tile(tile, slot):
        def body(j, c):
            copy(tile, slot, j, 0).wait()
            copy(tile, slot, j, 1).wait()
            return c
        lax.fori_loop(0, MOE_ROWS, body, 0, unroll=MOE_DMA_UNROLL)

    @pl.when(i == 0)
    def _():
        start_tile(0, 0)

    @pl.when(i + 1 < pl.num_programs(0))
    def _():
        start_tile(i + 1, (i + 1) % 2)

    comb, sel = comb_ref[...], sel_ref[...]
    w1 = jnp.sum(jnp.where(sel == 1.0, comb, 0.0), axis=-1, keepdims=True)
    w2 = jnp.sum(jnp.where(sel == 2.0, comb, 0.0), axis=-1, keepdims=True)
    slot = i % 2
    wait_tile(i, slot)
    for c in range(ROW_TILES):
        cs = slice(c * LANES, (c + 1) * LANES)
        y1 = y_s[pl.ds(slot * slot_rows + c, MOE_ROWS, stride=ROW_TILES), :]
        y2 = y_s[pl.ds(slot * slot_rows + MOE_ROWS * ROW_TILES + c, MOE_ROWS, stride=ROW_TILES), :]
        o_ref[:, cs] = xn_ref[:, cs] + mod_ref[0, 5:6, cs] * (w1 * y1 + w2 * y2)


def _combine(pos, ys, comb, sel, xn, mod, T):
    n_tok = xn.shape[0]
    tm = MOE_ROWS
    bm = mod.shape[0]
    mod_idx = (lambda i, p: (i * tm // T, 0, 0)) if bm > 1 else (lambda i, p: (0, 0, 0))
    row = lambda w: pl.BlockSpec((tm, w), lambda i, p: (i, 0))
    return pl.pallas_call(
        _combine_kernel,
        grid_spec=pltpu.PrefetchScalarGridSpec(
            num_scalar_prefetch=1, grid=(n_tok // tm,),
            in_specs=[pl.BlockSpec(memory_space=pl.ANY), row(LANES), row(LANES), row(D_MODEL),
                      pl.BlockSpec((1, 8, D_MODEL), mod_idx)],
            out_specs=row(D_MODEL),
            scratch_shapes=[pltpu.VMEM((2 * 2 * tm * ROW_TILES, LANES), F32), pltpu.SemaphoreType.DMA((2,))]),
        out_shape=jax.ShapeDtypeStruct((n_tok, D_MODEL), F32),
        compiler_params=_cparams(("arbitrary",)),
        name="moe_combine",
    )(pos, ys, comb, sel, xn, mod)


def _moe(groups, wg, wu, wd, layer, sorted_buf=None):
    sizes = [g[3].shape[0] for g in groups]
    n_rows = _moe_rows(sum(sizes))
    pos, tile_expert, tile_valid = _moe_plan(jnp.concatenate([g[2] for g in groups], axis=0))
    xs = jnp.zeros((n_rows, ROW_TILES, LANES), F32) if sorted_buf is None else sorted_buf
    start = 0
    for (h3, _, _, _, _, _), n in zip(groups, sizes):
        xs = _dispatch(pos[:, start:start + n], h3.reshape(n, ROW_TILES, LANES), xs)
        start += n
    ys = _experts(tile_expert, tile_valid, xs.reshape(n_rows * ROW_TILES, LANES), wg, wu, wd, layer)
    ys = ys.reshape(n_rows, ROW_TILES, LANES)
    outs, start = [], 0
    for (_, comb, sel, xn, mod, T), n in zip(groups, sizes):
        outs.append(_combine(pos[:, start:start + n], ys, comb, sel, xn, mod, T))
        start += n
    return outs, xs


def kernel(x_prompt, x_sample, cache_mla, state_mlstm_C, state_mlstm_n, state_mlstm_m, state_rwkv, state_gla, c, c_ctx, w_ada, b_ada, g_mix, g_ffn, w_in, w_out, b_ml_gates, g_ml_norm, g_mla_qlat, g_mla_kvlat, w_mla_uq, w_mla_ukv, g_mla_qn, g_mla_kn, rw_w0, rw_w_up, rw_a0, rw_a_up, rw_g_up, rw_k_k, rw_k_a, rw_r_k, rw_ln, gla_g_up, gla_g_b, gla_norm, moe_w_rg, moe_b_rg, moe_w_re, moe_b_re, moe_w_gate, moe_w_up, moe_w_down):
    cc = jnp.concatenate([c_ctx[None], c, jnp.zeros((3, D_MODEL), F32)], axis=0)
    mod = _modulation(cc, w_ada, b_ada).reshape(DEPTH, 8, 6, D_MODEL)
    mod = jnp.pad(mod, ((0, 0), (0, 0), (0, 2), (0, 0)))

    layers = []
    for l in range(DEPTH):
        p = {'b_ml_gates': b_ml_gates[l], 'g_ml_norm': g_ml_norm[l], 'g_mla_qlat': g_mla_qlat[l],
             'g_mla_kvlat': g_mla_kvlat[l], 'w_mla_uq': w_mla_uq[l], 'w_mla_ukv': w_mla_ukv[l],
             'g_mla_qn': g_mla_qn[l], 'g_mla_kn': g_mla_kn[l], 'rw_w0': rw_w0[l], 'rw_w_up': rw_w_up[l],
             'rw_a0': rw_a0[l], 'rw_a_up': rw_a_up[l], 'rw_g_up': rw_g_up[l], 'rw_k_k': rw_k_k[l],
             'rw_k_a': rw_k_a[l], 'rw_r_k': rw_r_k[l], 'rw_ln': rw_ln[l], 'gla_g_up': gla_g_up[l],
             'gla_g_b': gla_g_b[l], 'gla_norm': gla_norm[l]}
        lane_pad = LANES - N_EXPERTS - N_GROUPS
        router_w = jnp.concatenate([moe_w_re[l], moe_w_rg[l], jnp.zeros((D_MODEL, lane_pad), F32)], axis=1)
        router_b = jnp.concatenate([moe_b_re[l], moe_b_rg[l], jnp.zeros((lane_pad,), F32)]).reshape(1, LANES)
        layers.append(dict(
            w_in=_pack_w_in(w_in[l]), g_mix=g_mix[l].reshape(1, -1), g_ffn=g_ffn[l].reshape(1, -1),
            w_out=w_out[l].astype(BF16), ml=_mlstm_params(p), mla=_mla_params(p), rw=_rwkv_params(p),
            gla=_gla_params(p), router_w=_split_weight(router_w), router_b=router_b))

    def mix(x2, B, T, mod_g, lp, ctx):
        u3 = _in_proj(x2, mod_g, lp['g_mix'], lp['w_in'], T).reshape(B, T, PK_COLS)
        if ctx is None:
            ctx_kv = None
            ml_c0, ml_m0 = _mlstm_state_zero(B)
            rw_h0 = jnp.zeros((B, 2, RW_H, RW_N, RW_N), F32)
            gla_s0 = jnp.zeros((B, 2, GLA_H, GLA_DK, GLA_DV), F32)
        else:
            ctx_kv, ml_C0, ml_n0, ml_m0_, rw_S0, gla_s0 = ctx
            ml_c0, ml_m0 = _mlstm_state_in(ml_C0, ml_n0, ml_m0_)
            rw_h0 = jnp.swapaxes(rw_S0, -1, -2)
        ml_out, ml_c, ml_m = _mlstm(u3, *lp['ml'], ml_c0, ml_m0)
        mla_out, own_kv = _mla(u3, ctx_kv, lp['mla'])
        rw_out, rw_h = _rwkv(u3, lp['rw'], rw_h0)
        gla_out, gla_s = _gla(u3, lp['gla'], gla_s0)
        mixed = [t.reshape(B * T, GROUP_W) for t in (ml_out, mla_out, rw_out, gla_out)]
        xn, h3, comb, sel = _out_proj(x2, mixed, lp['w_out'], mod_g, lp['g_ffn'], lp['router_w'], lp['router_b'], T)
        ml_C, ml_n, ml_mm = _mlstm_state_out(ml_c, ml_m)
        return (h3, comb, sel, xn, mod_g, T), (own_kv, ml_C, ml_n, ml_mm, jnp.swapaxes(rw_h, -1, -2), gla_s)

    Bp, Tp = x_prompt.shape[:2]
    Bs, Ts = x_sample.shape[:2]
    xp = x_prompt.reshape(Bp * Tp, D_MODEL)
    xs = x_sample.reshape(Bs * Ts, D_MODEL)
    ctx_states = []
    sorted_buf = None
    for l in range(DEPTH):
        ctx = (cache_mla[:, l], state_mlstm_C[:, l], state_mlstm_n[:, l], state_mlstm_m[:, l],
               state_rwkv[:, l], state_gla[:, l])
        moe_p, st = mix(xp, Bp, Tp, mod[l, 0:1], layers[l], None)
        moe_s, _ = mix(xs, Bs, Ts, mod[l, 1:1 + Bs], layers[l], ctx)
        ctx_states.append(st)
        (xp, xs), sorted_buf = _moe([moe_p, moe_s], moe_w_gate, moe_w_up, moe_w_down, l, sorted_buf)
    outs = [jnp.stack([s[i] for s in ctx_states], axis=1) for i in range(6)]
    return (xp.reshape(x_prompt.shape), xs.reshape(x_sample.shape), *outs)
```

```python
import functools
import math

import numpy as np
import jax
import jax.numpy as jnp
from jax import lax
from jax.experimental import pallas as pl
from jax.experimental.pallas import tpu as pltpu

F32 = jnp.float32
BF16 = jnp.bfloat16

D_MODEL = 2048
DEPTH = 2
GRID_W = 64
GROUP_W = 512
ML_H, ML_DK = 4, 128
MLA_H, MLA_NOPE, MLA_ROPE, MLA_V = 4, 128, 64, 128
MLA_QK = MLA_NOPE + MLA_ROPE
Q_LORA, KV_LORA = 384, 256
ROPE_THETA = 10000.0
RW_H, RW_N = 8, 64
RW_DECAY_SCALE = math.exp(-0.5)
RW_LN_EPS = 64e-5
GLA_H, GLA_DK, GLA_DV = 4, 64, 128
GLA_GATE_RANK = 16
GLA_NORMALIZER = 16.0
N_GROUPS, EXPERTS_PER_GROUP, N_EXPERTS = 4, 4, 16
EXPERT_HIDDEN = 512
NORM_EPS = 1e-6
LANES = 128
ROW_TILES = D_MODEL // LANES
VMEM_LIMIT = 56 * 1024 * 1024

_REF_SPLITS = (
    ('ml_q', 512), ('ml_k', 512), ('ml_v', 512), ('ml_o', 512), ('ml_g', 16),
    ('mla_ql', Q_LORA), ('mla_ckv', KV_LORA), ('mla_kr', MLA_ROPE),
    ('rw_r', 512), ('rw_k', 512), ('rw_v', 512), ('rw_wd', 64), ('rw_ad', 64), ('rw_gd', 128),
    ('gla_q', 256), ('gla_k', 256), ('gla_v', 512), ('gla_gd', GLA_GATE_RANK), ('gla_g', 512),
)
_REF_OFF = {}
_o = 0
for _n, _w in _REF_SPLITS:
    _REF_OFF[_n] = (_o, _w)
    _o += _w
IN_COLS = _o

_PACKED = (
    ('ml_q', 512), ('ml_k', 512), ('ml_v', 512), ('ml_o', 512),
    ('rw_r', 512), ('rw_k', 512), ('rw_v', 512), ('gla_v', 512), ('gla_g', 512),
    ('mla_ql', 384), ('mla_ckv', 256), ('mla_kr', 64), ('mla_kr_sw', 64),
    ('gla_q', 256), ('gla_k', 256),
    ('ml_g', 128), ('rw_wd', 64), ('rw_ad', 64), ('rw_gd', 128), ('gla_gd', 128),
)
PK_OFF = {}
_o = 0
for _n, _w in _PACKED:
    PK_OFF[_n] = _o
    _o += _w
PK_COLS = _o


def _rope_swap_perm():
    idx = np.arange(MLA_ROPE)
    axis, half, f = idx // 32, (idx % 32) // 16, idx % 16
    return axis * 32 + (1 - half) * 16 + f


def _packed_column_index():
    src = np.full((PK_COLS,), -1, np.int64)
    for name, width in _PACKED:
        off = PK_OFF[name]
        if name == 'mla_kr_sw':
            s, w = _REF_OFF['mla_kr']
            src[off:off + w] = s + _rope_swap_perm()
        else:
            s, w = _REF_OFF[name]
            src[off:off + w] = s + np.arange(w)
    return src


_PK_SRC = _packed_column_index()


def _column_runs(src):
    runs, i = [], 0
    while i < len(src):
        j = i + 1
        while j < len(src) and ((src[i] < 0 and src[j] < 0) or (src[i] >= 0 and src[j] == src[i] + (j - i))):
            j += 1
        runs.append((int(src[i]), j - i))
        i = j
    return runs


_PK_RUNS = _column_runs(_PK_SRC)


def _pack_w_in(w):
    parts = [w[:, s:s + n] if s >= 0 else jnp.zeros((w.shape[0], n), w.dtype) for s, n in _PK_RUNS]
    return jnp.concatenate(parts, axis=1).astype(BF16)


def _cparams(sem):
    return pltpu.CompilerParams(dimension_semantics=sem, vmem_limit_bytes=VMEM_LIMIT)


def _log_sigmoid(x):
    return jnp.minimum(x, 0.0) - jnp.log(1.0 + jnp.exp(-jnp.abs(x)))


def _sigmoid(x):
    return 1.0 / (1.0 + jnp.exp(-x))


def _dot(a, b):
    return jnp.dot(a.astype(BF16), b.astype(BF16), preferred_element_type=F32)


def _dot_nt(a, b):
    return lax.dot_general(a.astype(BF16), b.astype(BF16), (((1,), (1,)), ((), ())), preferred_element_type=F32)


def _dot_tn(a, b):
    return lax.dot_general(a.astype(BF16), b.astype(BF16), (((0,), (0,)), ((), ())), preferred_element_type=F32)


def _split2(x):
    hi = x.astype(BF16)
    return hi, (x - hi.astype(F32)).astype(BF16)


def _split_weight(w):
    hi, lo = _split2(w)
    return jnp.stack([hi, lo])


def _running_sum(tri, x):
    hi = x.astype(BF16)
    r = x - hi.astype(F32)
    mid = r.astype(BF16)
    lo = (r - mid.astype(F32)).astype(BF16)
    n = x.shape[1]
    s = jnp.dot(tri, jnp.concatenate([hi, mid, lo], axis=1), preferred_element_type=F32)
    return s[:, :n] + s[:, n:2 * n] + s[:, 2 * n:]


def _dot_split(a, w_ref):
    a_hi, a_lo = _split2(a)
    w_hi = w_ref[0]
    return (jnp.dot(a_hi, w_hi, preferred_element_type=F32) + jnp.dot(a_lo, w_hi, preferred_element_type=F32)
            + jnp.dot(a_hi, w_ref[1], preferred_element_type=F32))


def _mod_kernel(c_ref, w_ref, b_ref, o_ref):
    c = c_ref[...]
    s = c * _sigmoid(c)
    o_ref[0] = _dot(s, w_ref[0]) + b_ref[0]


def _modulation(cc, w_ada, b_ada):
    tn = 1536
    n = 6 * D_MODEL
    return pl.pallas_call(
        _mod_kernel,
        grid=(DEPTH, n // tn),
        in_specs=[pl.BlockSpec((8, D_MODEL), lambda l, j: (0, 0)),
                  pl.BlockSpec((1, D_MODEL, tn), lambda l, j: (l, 0, j)),
                  pl.BlockSpec((1, 1, tn), lambda l, j: (l, 0, j))],
        out_specs=pl.BlockSpec((1, 8, tn), lambda l, j: (l, 0, j)),
        out_shape=jax.ShapeDtypeStruct((DEPTH, 8, n), F32),
        compiler_params=_cparams(("arbitrary", "arbitrary")),
        name="adaln_mod",
    )(cc, w_ada, b_ada.reshape(DEPTH, 1, n))


def _inproj_kernel(x_ref, mod_ref, g_ref, w_ref, o_ref, h_scr):
    @pl.when(pl.program_id(1) == 0)
    def _():
        x = x_ref[...]
        xn = x * lax.rsqrt(jnp.mean(x * x, axis=-1, keepdims=True) + NORM_EPS) * g_ref[...]
        h_scr[...] = (xn * (1.0 + mod_ref[0, 1:2, :]) + mod_ref[0, 0:1, :]).astype(BF16)

    o_ref[...] = jnp.dot(h_scr[...], w_ref[...], preferred_element_type=F32)


def _in_proj(x2, mod, g, w_packed, T):
    n_tok = x2.shape[0]
    tm = 1024
    tn = 1280
    bm = mod.shape[0]
    mod_idx = (lambda i, j: (i * tm // T, 0, 0)) if bm > 1 else (lambda i, j: (0, 0, 0))
    return pl.pallas_call(
        _inproj_kernel,
        grid=(n_tok // tm, PK_COLS // tn),
        in_specs=[pl.BlockSpec((tm, D_MODEL), lambda i, j: (i, 0)),
                  pl.BlockSpec((1, 8, D_MODEL), mod_idx),
                  pl.BlockSpec((1, D_MODEL), lambda i, j: (0, 0)),
                  pl.BlockSpec((D_MODEL, tn), lambda i, j: (0, j))],
        out_specs=pl.BlockSpec((tm, tn), lambda i, j: (i, j)),
        out_shape=jax.ShapeDtypeStruct((n_tok, PK_COLS), F32),
        scratch_shapes=[pltpu.VMEM((tm, D_MODEL), BF16)],
        compiler_params=_cparams(("arbitrary", "arbitrary")),
        name="in_proj",
    )(x2, mod, g, w_packed)


ML_CHUNK = 256


def _mlstm_kernel(q_ref, k_ref, v_ref, o_ref, g_ref, bias_ref, gn_ref, c0_ref, m0_ref,
                  out_ref, c_ref, m_ref, hs_ref, *, T):
    L = ML_CHUNK
    nc = T // L
    c_ref[...] = c0_ref[...]
    m_ref[...] = m0_ref[...]
    ii = lax.broadcasted_iota(jnp.int32, (L, L), 0)
    jj = lax.broadcasted_iota(jnp.int32, (L, L), 1)
    lane = lax.broadcasted_iota(jnp.int32, (1, LANES), 1)
    is_f = jnp.logical_and(lane % 8 >= 4, lane < 16)
    ones_col = (lax.broadcasted_iota(jnp.int32, (L, LANES), 1) == 0).astype(BF16)
    scale = ML_DK ** -0.5

    def chunk(ci, carry):
        ch = []
        for d in range(2):
            mask = (jj <= ii) if d == 0 else (jj >= ii)
            c = ci if d == 0 else nc - 1 - ci
            s = pl.multiple_of(c * L, L)
            gates = g_ref[0, pl.ds(s, L), :] + bias_ref[...]
            gf = jnp.where(is_f, _log_sigmoid(gates), gates)
            cum = _running_sum(mask.astype(BF16), gf)
            gf_t = gf.T
            cum_t = cum.T
            for h in range(ML_H):
                ci_, cf_ = d * 8 + h, d * 8 + 4 + h
                hs = slice(h * ML_DK, (h + 1) * ML_DK)
                ig_col, ig_row = gf[:, ci_:ci_ + 1], gf_t[ci_:ci_ + 1, :]
                b_col, b_row = cum[:, cf_:cf_ + 1], cum_t[cf_:cf_ + 1, :]
                b_last = b_col[L - 1:L, :] if d == 0 else b_col[0:1, :]
                m_prev = m_ref[0, d, h][:, 0:1]
                dmat = jnp.where(mask, b_col + (ig_row - b_row), -jnp.inf)
                m_inter = b_col + m_prev
                m_row = jnp.maximum(m_inter, jnp.max(dmat, axis=-1, keepdims=True))
                dk_col = b_last - b_col + ig_col
                m_new = jnp.maximum(b_last + m_prev, jnp.max(dk_col, axis=0, keepdims=True))
                ch.append(dict(
                    d=d, h=h, s=s, hs=hs, dmat=dmat, m_row=m_row, w_inter=jnp.exp(m_inter - m_row), m_new=m_new,
                    w_key=jnp.exp(dk_col - m_new), c_scale=jnp.exp(b_last + m_prev - m_new),
                    q=(q_ref[0, pl.ds(s, L), hs] * scale).astype(BF16), k=k_ref[0, pl.ds(s, L), hs],
                    v_aug=jnp.concatenate([v_ref[0, pl.ds(s, L), hs].astype(BF16), ones_col], axis=1),
                    c_aug=c_ref[0, d, h]))
        qk = [_dot_nt(c['q'], c['k']) for c in ch]
        qc = [_dot(c['q'], c['c_aug']) for c in ch]
        sv = [_dot(s_ * jnp.exp(c['dmat'] - c['m_row']), c['v_aug']) for s_, c in zip(qk, ch)]
        kv = [_dot_tn(c['k'] * c['w_key'], c['v_aug']) for c in ch]
        for c, qc_, sv_, kv_ in zip(ch, qc, sv, kv):
            nd = c['w_inter'] * qc_ + sv_
            num, den = nd[:, :ML_DK], nd[:, ML_DK:ML_DK + 1]
            hh = num / jnp.maximum(jnp.abs(den), jnp.exp(-c['m_row']))
            hs_ref[pl.ds(c['s'], L), c['d'] * 512 + c['h'] * ML_DK:c['d'] * 512 + (c['h'] + 1) * ML_DK] = hh
            c_ref[0, c['d'], c['h']] = c['c_scale'] * c['c_aug'] + kv_
            m_ref[0, c['d'], c['h']] = jnp.broadcast_to(c['m_new'], (1, LANES))
        return carry

    lax.fori_loop(0, nc, chunk, 0)

    def finish(ci, carry):
        s = pl.multiple_of(ci * L, L)
        for h in range(ML_H):
            hs = slice(h * ML_DK, (h + 1) * ML_DK)
            x = hs_ref[pl.ds(s, L), hs] + hs_ref[pl.ds(s, L), 512 + h * ML_DK:512 + (h + 1) * ML_DK]
            xc = x - jnp.mean(x, axis=-1, keepdims=True)
            y = xc * lax.rsqrt(jnp.mean(xc * xc, axis=-1, keepdims=True) + NORM_EPS) * gn_ref[:, hs]
            out_ref[0, pl.ds(s, L), hs] = (y * _sigmoid(o_ref[0, pl.ds(s, L), hs])).astype(BF16)
        return carry

    lax.fori_loop(0, nc, finish, 0)


def _mlstm_params(p):
    bias = jnp.pad(p['b_ml_gates'].reshape(1, -1), ((0, 0), (0, LANES - 4 * ML_H)))
    return bias, p['g_ml_norm'].reshape(1, 512)


def _mlstm_state_in(C0, n0, m0):
    c0 = jnp.concatenate([C0, n0[..., None], jnp.zeros(C0.shape[:-1] + (ML_DK - 1,), F32)], axis=-1)
    return c0, jnp.broadcast_to(m0[..., None, None], m0.shape + (1, LANES))


def _mlstm_state_zero(B):
    return jnp.zeros((B, 2, ML_H, ML_DK, 2 * ML_DK), F32), jnp.zeros((B, 2, ML_H, 1, LANES), F32)


def _mlstm_state_out(c, m):
    return c[..., :ML_DK], c[..., ML_DK], m[..., 0, 0]


def _mlstm(u3, bias, gnorm, c0, m0):
    B, T, _ = u3.shape
    blk = lambda name: pl.BlockSpec((1, T, 512), lambda b, o=PK_OFF[name] // 512: (b, 0, o))
    st_c = pl.BlockSpec((1, 2, ML_H, ML_DK, 2 * ML_DK), lambda b: (b, 0, 0, 0, 0))
    st_m = pl.BlockSpec((1, 2, ML_H, 1, LANES), lambda b: (b, 0, 0, 0, 0))
    return pl.pallas_call(
        functools.partial(_mlstm_kernel, T=T),
        grid=(B,),
        in_specs=[blk('ml_q'), blk('ml_k'), blk('ml_v'), blk('ml_o'),
                  pl.BlockSpec((1, T, LANES), lambda b: (b, 0, PK_OFF['ml_g'] // LANES)),
                  pl.BlockSpec((1, LANES), lambda b: (0, 0)),
                  pl.BlockSpec((1, 512), lambda b: (0, 0)),
                  st_c, st_m],
        out_specs=[pl.BlockSpec((1, T, 512), lambda b: (b, 0, 0)), st_c, st_m],
        out_shape=[jax.ShapeDtypeStruct((B, T, 512), BF16),
                   jax.ShapeDtypeStruct(c0.shape, F32),
                   jax.ShapeDtypeStruct(m0.shape, F32)],
        scratch_shapes=[pltpu.VMEM((T, 1024), F32)],
        compiler_params=_cparams(("arbitrary",)),
        name="mlstm",
    )(u3, u3, u3, u3, u3, bias, gnorm, c0, m0)


MLA_BLK = 256


def _rope_tables(T):
    rows = T // GRID_W
    row = np.repeat(np.arange(rows, dtype=np.float64), GRID_W)
    col = np.tile(np.arange(GRID_W, dtype=np.float64), rows)
    inv = ROPE_THETA ** (-np.arange(MLA_ROPE // 4, dtype=np.float64) / (MLA_ROPE // 4))
    ang = np.stack([row[:, None] * inv, col[:, None] * inv], axis=1)
    cos = np.stack([np.cos(ang), np.cos(ang)], axis=2).reshape(T, MLA_ROPE)
    sin = np.stack([-np.sin(ang), np.sin(ang)], axis=2).reshape(T, MLA_ROPE)
    return jnp.asarray(np.concatenate([cos, sin], axis=1), F32)


def _mla_kernel(*refs, T, n_ctx, rope):
    if n_ctx:
        (u_ref, ctx_ref, cs_ref, gql_ref, gkv_ref, gains_ref, wqn_ref, wqr_ref, wqs_ref, wkn_ref, wv_ref,
         out_ref, kv_ref, qn_s, qr_s, kn_s, kr_s, v_s) = refs
    else:
        (u_ref, cs_ref, gql_ref, gkv_ref, gains_ref, wqn_ref, wqr_ref, wqs_ref, wkn_ref, wv_ref,
         out_ref, kv_ref, qn_s, qr_s, kn_s, kr_s, v_s) = refs
    Lb = MLA_BLK
    gq_n, gq_r, gq_s = gains_ref[0:1, :], gains_ref[1:2, 0:64], gains_ref[1:2, 64:128]
    gk_n, gk_r, gk_s = gains_ref[2:3, :], gains_ref[3:4, 0:64], gains_ref[3:4, 64:128]
    sm_scale = MLA_QK ** -0.5

    def store_keys(s, kn, kr, krs, cos, sin):
        kr_ss = jnp.sum(kr * kr, axis=-1, keepdims=True)
        for h in range(MLA_H):
            kn_h = kn[:, h * 128:(h + 1) * 128]
            rk = lax.rsqrt((jnp.sum(kn_h * kn_h, axis=-1, keepdims=True) + kr_ss) / MLA_QK + NORM_EPS)
            kn_s[pl.ds(s, Lb), h * 128:(h + 1) * 128] = (kn_h * rk * gk_n).astype(BF16)
            kr_h = kr * gk_r
            if cos is not None:
                kr_h = kr_h * cos + (krs * gk_s) * sin
            kr_s[pl.ds(s, Lb), h * 64:(h + 1) * 64] = (kr_h * rk).astype(BF16)

    def prep(ci, carry):
        s = pl.multiple_of(ci * Lb, Lb)
        u = u_ref[0, pl.ds(s, Lb), :]
        ql, ckv, kr, krs = u[:, :384], u[:, 384:640], u[:, 640:704], u[:, 704:768]
        qln = ql * lax.rsqrt(jnp.mean(ql * ql, axis=-1, keepdims=True) + NORM_EPS) * gql_ref[...]
        ckvn = ckv * lax.rsqrt(jnp.mean(ckv * ckv, axis=-1, keepdims=True) + NORM_EPS) * gkv_ref[...]
        kv_ref[0, pl.ds(s, Lb), :] = jnp.concatenate([ckvn, kr], axis=1)
        cos = cs_ref[pl.ds(s, Lb), 0:64] if rope else None
        sin = cs_ref[pl.ds(s, Lb), 64:128] if rope else None
        qn, qr, qs = _dot(qln, wqn_ref[...]), _dot(qln, wqr_ref[...]), _dot(qln, wqs_ref[...])
        for h in range(MLA_H):
            qn_h, qr_h = qn[:, h * 128:(h + 1) * 128], qr[:, h * 64:(h + 1) * 64]
            ss = jnp.sum(qn_h * qn_h, axis=-1, keepdims=True) + jnp.sum(qr_h * qr_h, axis=-1, keepdims=True)
            rq = lax.rsqrt(ss / MLA_QK + NORM_EPS) * sm_scale
            qn_s[pl.ds(s, Lb), h * 128:(h + 1) * 128] = (qn_h * rq * gq_n).astype(BF16)
            qr_h = qr_h * gq_r
            if rope:
                qr_h = qr_h * cos + (qs[:, h * 64:(h + 1) * 64] * gq_s) * sin
            qr_s[pl.ds(s, Lb), h * 64:(h + 1) * 64] = (qr_h * rq).astype(BF16)
        v_s[pl.ds(s, Lb), :] = _dot(ckvn, wv_ref[...]).astype(BF16)
        store_keys(s, _dot(ckvn, wkn_ref[...]), kr, krs, cos, sin)
        return carry

    lax.fori_loop(0, T // Lb, prep, 0)

    for ci in range(n_ctx // Lb):
        cx = ctx_ref[0, ci * Lb:(ci + 1) * Lb, :]
        ckv_c, kr_c = cx[:, :KV_LORA], cx[:, KV_LORA:KV_LORA + MLA_ROPE]
        v_s[T + ci * Lb:T + (ci + 1) * Lb, :] = _dot(ckv_c, wv_ref[...]).astype(BF16)
        store_keys(T + ci * Lb, _dot(ckv_c, wkn_ref[...]), kr_c, None, None, None)

    def attend(qi, carry):
        s = pl.multiple_of(qi * Lb, Lb)
        heads = range(MLA_H)
        sc = [_dot_nt(qn_s[pl.ds(s, Lb), h * 128:(h + 1) * 128], kn_s[:, h * 128:(h + 1) * 128])
              + _dot_nt(qr_s[pl.ds(s, Lb), h * 64:(h + 1) * 64], kr_s[:, h * 64:(h + 1) * 64]) for h in heads]
        p = [jnp.exp(x - jnp.max(x, axis=-1, keepdims=True)) for x in sc]
        o = [_dot(p[h], v_s[:, h * 128:(h + 1) * 128]) / jnp.sum(p[h], axis=-1, keepdims=True) for h in heads]
        for h in heads:
            out_ref[0, pl.ds(s, Lb), h * 128:(h + 1) * 128] = o[h].astype(BF16)
        return carry

    lax.fori_loop(0, T // Lb, attend, 0)


def _mla_params(p):
    def swap(t):
        return jnp.flip(t.reshape(t.shape[:-1] + (2, 2, MLA_ROPE // 4)), axis=-2).reshape(t.shape)

    wq = p['w_mla_uq'].reshape(Q_LORA, MLA_H, MLA_QK)
    wq_n = wq[:, :, :MLA_NOPE].reshape(Q_LORA, 512).astype(BF16)
    wq_r = wq[:, :, MLA_NOPE:].reshape(Q_LORA, 256).astype(BF16)
    wq_s = swap(wq[:, :, MLA_NOPE:]).reshape(Q_LORA, 256).astype(BF16)
    wkv = p['w_mla_ukv'].reshape(KV_LORA, MLA_H, MLA_NOPE + MLA_V)
    wk_n = wkv[:, :, :MLA_NOPE].reshape(KV_LORA, 512).astype(BF16)
    wv = wkv[:, :, MLA_NOPE:].reshape(KV_LORA, 512).astype(BF16)
    gq, gk = p['g_mla_qn'], p['g_mla_kn']
    gains = jnp.concatenate([gq[:128], gq[128:], swap(gq[128:]), gk[:128], gk[128:], swap(gk[128:]),
                             jnp.zeros((4 * LANES,), F32)]).reshape(8, LANES)
    return (p['g_mla_qlat'].reshape(1, -1), p['g_mla_kvlat'].reshape(1, -1), gains, wq_n, wq_r, wq_s, wk_n, wv)


def _mla(u3, ctx_kv, mp):
    B, T, _ = u3.shape
    n_ctx = 0 if ctx_kv is None else ctx_kv.shape[1]
    rope = ctx_kv is not None
    tk = T + n_ctx
    full = lambda a: pl.BlockSpec(a.shape, lambda b, n=a.ndim: (0,) * n)
    cs = _rope_tables(T) if rope else jnp.zeros((T, LANES), F32)
    ins = [u3] + ([ctx_kv] if rope else []) + [cs] + list(mp)
    specs = [pl.BlockSpec((1, T, 768), lambda b: (b, 0, PK_OFF['mla_ql'] // 768))]
    if rope:
        specs.append(pl.BlockSpec((1, n_ctx, KV_LORA + MLA_ROPE), lambda b: (b, 0, 0)))
    specs += [full(a) for a in ins[len(specs):]]
    return pl.pallas_call(
        functools.partial(_mla_kernel, T=T, n_ctx=n_ctx, rope=rope),
        grid=(B,),
        in_specs=specs,
        out_specs=[pl.BlockSpec((1, T, 512), lambda b: (b, 0, 0)),
                   pl.BlockSpec((1, T, KV_LORA + MLA_ROPE), lambda b: (b, 0, 0))],
        out_shape=[jax.ShapeDtypeStruct((B, T, 512), BF16),
                   jax.ShapeDtypeStruct((B, T, KV_LORA + MLA_ROPE), F32)],
        scratch_shapes=[pltpu.VMEM((T, 512), BF16), pltpu.VMEM((T, 256), BF16),
                        pltpu.VMEM((tk, 512), BF16), pltpu.VMEM((tk, 256), BF16), pltpu.VMEM((tk, 512), BF16)],
        compiler_params=_cparams(("arbitrary",)),
        name="mla",
    )(*ins)


GLA_CHUNK = 64
GLA_LEAF = 4
GLA_UNROLL = 2


def _gla_kernel(q_ref, k_ref, v_ref, gd_ref, g_ref, gup_ref, gb_ref, gn_ref, hsel_ref, s0_ref,
                out_ref, s_ref, os_ref, la_ref, *, T):
    L, C = GLA_CHUNK, GLA_LEAF
    nc = T // L
    Lf = 256
    s_ref[...] = s0_ref[...]

    def gates(ci, carry):
        s = pl.multiple_of(ci * Lf, Lf)
        x = _dot_split(gd_ref[0, pl.ds(s, Lf), :], gup_ref) + gb_ref[...]
        la_ref[pl.ds(s, Lf), :] = _log_sigmoid(x) / GLA_NORMALIZER
        return carry

    lax.fori_loop(0, T // Lf, gates, 0)
    ii = lax.broadcasted_iota(jnp.int32, (L, L), 0)
    jj = lax.broadcasted_iota(jnp.int32, (L, L), 1)
    eye = (lax.broadcasted_iota(jnp.int32, (GLA_DK, GLA_DK), 0)
           == lax.broadcasted_iota(jnp.int32, (GLA_DK, GLA_DK), 1)).astype(F32)
    row_id = lax.broadcasted_iota(jnp.int32, (L, 1), 0)
    scale = GLA_DK ** -0.5
    hsel = hsel_ref[...]
    levels = []
    span = C
    while span < L:
        levels.append(span)
        span *= 2

    U = GLA_UNROLL

    def chunk(ci, carry):
        parts = []
        for d in range(2):
            causal = (jj <= ii) if d == 0 else (jj >= ii)
            tri = causal.astype(BF16)
            for u in range(U):
                c = ci * U + u if d == 0 else nc - 1 - (ci * U + u)
                s = pl.multiple_of(c * L, L)
                la = la_ref[pl.ds(s, L), d * GLA_H * GLA_DK:(d + 1) * GLA_H * GLA_DK]
                b = _running_sum(tri, la)
                total = b[L - 1:L, :] if d == 0 else b[0:1, :]
                q = q_ref[0, pl.ds(s, L), :] * scale
                k = k_ref[0, pl.ds(s, L), :]
                lv = []
                for sp in levels:
                    b3 = b.reshape(L // (2 * sp), 2 * sp, GLA_H * GLA_DK)
                    edge = b3[:, sp - 1:sp, :] if d == 0 else b3[:, sp:sp + 1, :]
                    bref = jnp.broadcast_to(edge, b3.shape).reshape(L, GLA_H * GLA_DK)
                    later = (row_id % (2 * sp) >= sp) if d == 0 else (row_id % (2 * sp) < sp)
                    e = jnp.exp(jnp.where(later, b - bref, bref - b))
                    lv.append((jnp.where(later, q * e, 0.0), jnp.where(later, 0.0, k * e),
                               (ii // (2 * sp)) == (jj // (2 * sp))))
                terms = []
                for dl in range(C):
                    if dl == 0:
                        terms.append(q * k)
                        continue
                    sh = dl if d == 0 else L - dl
                    ok = (row_id % C >= dl) if d == 0 else (row_id % C < C - dl)
                    kd = pltpu.roll(k, sh, axis=0)
                    bd = pltpu.roll(b, sh, axis=0)
                    terms.append(q * kd * jnp.exp(jnp.where(ok, b - bd, 0.0)))
                tt = jnp.concatenate(terms, axis=0)
                t_hi = tt.astype(BF16)
                t_lo = (tt - t_hi.astype(F32)).astype(BF16)
                diag = (jnp.dot(t_hi, hsel, preferred_element_type=F32)
                        + jnp.dot(t_lo, hsel, preferred_element_type=F32))
                parts.append(dict(d=d, s=s, lv=lv, diag=diag, q_in=q * jnp.exp(b), k_out=k * jnp.exp(total - b),
                                  f_row=jnp.exp(total)))
        chains = [(p, h) for p in parts for h in range(GLA_H)]
        amats = []
        for p, h in chains:
            ks = slice(h * GLA_DK, (h + 1) * GLA_DK)
            a = jnp.zeros((L, L), F32)
            for qs, kt, same in p['lv']:
                a = a + jnp.where(same, _dot_nt(qs[:, ks], kt[:, ks]), 0.0)
            for dl in range(C):
                pair = jnp.logical_and(jj == (ii - dl if p['d'] == 0 else ii + dl), ii // C == jj // C)
                a = a + jnp.where(pair, p['diag'][dl * L:(dl + 1) * L, h:h + 1], 0.0)
            amats.append(a)
        vals = [v_ref[0, pl.ds(p['s'], L), h * GLA_DV:(h + 1) * GLA_DV] for p, h in chains]
        intra = [_dot(a, v) for a, v in zip(amats, vals)]
        upd = [_dot_tn(p['k_out'][:, h * GLA_DK:(h + 1) * GLA_DK], v) for (p, h), v in zip(chains, vals)]
        for d in range(2):
            for h in range(GLA_H):
                ks = slice(h * GLA_DK, (h + 1) * GLA_DK)
                st = s_ref[0, d, h]
                for u in range(U):
                    i = (d * U + u) * GLA_H + h
                    p = parts[d * U + u]
                    o = intra[i] + _dot(p['q_in'][:, ks], st)
                    os_ref[pl.ds(p['s'], L), d * 512 + h * GLA_DV:d * 512 + (h + 1) * GLA_DV] = o
                    f_col = jnp.sum(eye * p['f_row'][:, ks], axis=1, keepdims=True)
                    st = f_col * st + upd[i]
                s_ref[0, d, h] = st
        return carry

    lax.fori_loop(0, nc // U, chunk, 0)

    def finish(ci, carry):
        s = pl.multiple_of(ci * Lf, Lf)
        for h in range(GLA_H):
            vs = slice(h * GLA_DV, (h + 1) * GLA_DV)
            o = os_ref[pl.ds(s, Lf), vs] + os_ref[pl.ds(s, Lf), 512 + h * GLA_DV:512 + (h + 1) * GLA_DV]
            y = o * lax.rsqrt(jnp.mean(o * o, axis=-1, keepdims=True) + NORM_EPS) * gn_ref[:, vs]
            g = g_ref[0, pl.ds(s, Lf), vs]
            out_ref[0, pl.ds(s, Lf), vs] = (y * (g * _sigmoid(g))).astype(BF16)
        return carry

    lax.fori_loop(0, T // Lf, finish, 0)


def _gla_params(p):
    gup = jnp.pad(jnp.concatenate([p['gla_g_up'][0], p['gla_g_up'][1]], axis=1),
                  ((0, LANES - GLA_GATE_RANK), (0, 0)))
    hsel = jnp.asarray(np.arange(GLA_H * GLA_DK)[:, None] // GLA_DK == np.arange(LANES)[None, :], BF16)
    return _split_weight(gup), p['gla_g_b'].reshape(1, -1), p['gla_norm'].reshape(1, -1), hsel


def _gla(u3, gp, s0):
    B, T, _ = u3.shape
    blk = lambda name, w: pl.BlockSpec((1, T, w), lambda b, o=PK_OFF[name] // w: (b, 0, o))
    full = lambda a: pl.BlockSpec(a.shape, lambda b, n=a.ndim: (0,) * n)
    st = pl.BlockSpec((1, 2, GLA_H, GLA_DK, GLA_DV), lambda b: (b, 0, 0, 0, 0))
    return pl.pallas_call(
        functools.partial(_gla_kernel, T=T),
        grid=(B,),
        in_specs=[blk('gla_q', 256), blk('gla_k', 256), blk('gla_v', 512), blk('gla_gd', 128), blk('gla_g', 512),
                  full(gp[0]), full(gp[1]), full(gp[2]), full(gp[3]), st],
        out_specs=[pl.BlockSpec((1, T, 512), lambda b: (b, 0, 0)), st],
        out_shape=[jax.ShapeDtypeStruct((B, T, 512), BF16), jax.ShapeDtypeStruct(s0.shape, F32)],
        scratch_shapes=[pltpu.VMEM((T, 1024), F32), pltpu.VMEM((T, 2 * GLA_H * GLA_DK), F32)],
        compiler_params=_cparams(("arbitrary",)),
        name="gla",
    )(u3, u3, u3, u3, u3, *gp, s0)


RW_CHUNK = 64
RW_UNROLL = 2


def _seg_sum(x, bd):
    hi = x.astype(BF16)
    lo = (x - hi.astype(F32)).astype(BF16)
    return jnp.dot(hi, bd, preferred_element_type=F32) + jnp.dot(lo, bd, preferred_element_type=F32)


def _rwkv_kernel(r_ref, k_ref, v_ref, wa_ref, gd_ref, wwa_ref, w0a0_ref, gup_ref, kk_ref, ka_ref, rk_ref, ln_ref,
                 bd_ref, h0_ref, out_ref, h_ref, ys_ref, pre_ref, *, T):
    L, N = RW_CHUNK, RW_N
    nc = T // L
    h_ref[...] = h0_ref[...]
    ii = lax.broadcasted_iota(jnp.int32, (L, L), 0)
    jj = lax.broadcasted_iota(jnp.int32, (L, L), 1)
    eye = (ii == jj).astype(F32)
    lane = lax.broadcasted_iota(jnp.int32, (1, LANES), 1)
    bd = bd_ref[...]
    Lf = 256

    def gates(ci, carry):
        s = pl.multiple_of(ci * Lf, Lf)
        wa = wa_ref[0, pl.ds(s, Lf), :]
        pre_ref[pl.ds(s, Lf), :] = _dot_split(jnp.where(lane < 64, jnp.tanh(wa), wa), wwa_ref) + w0a0_ref[...]
        return carry

    lax.fori_loop(0, T // Lf, gates, 0)

    masks = [((jj < ii), (jj <= ii)), ((jj > ii), (jj >= ii))]
    U = RW_UNROLL

    def chunk(ci, carry):
        chains = []
        for d in range(2):
            strict, incl = masks[d]
            tri = incl.astype(BF16)
            for u in range(U):
                c = ci * U + u if d == 0 else nc - 1 - (ci * U + u)
                s = pl.multiple_of(c * L, L)
                r = r_ref[0, pl.ds(s, L), :]
                k = k_ref[0, pl.ds(s, L), :]
                v = v_ref[0, pl.ds(s, L), :]
                pre = pre_ref[pl.ds(s, L), d * 1024:(d + 1) * 1024]
                logw = -RW_DECAY_SCALE * _sigmoid(pre[:, :512])
                a = _sigmoid(pre[:, 512:])
                kkr = k * kk_ref[...]
                kk = kkr * lax.rsqrt(_seg_sum(kkr * kkr, bd) + 1e-12)
                kt = k * (1.0 + (a - 1.0) * ka_ref[...])
                bh = kk * a
                lg = _running_sum(tri, logw)
                lg_end = lg[L - 1:L, :] if d == 0 else lg[0:1, :]
                a_t = -kk * jnp.exp(lg - logw)
                r_t = r * jnp.exp(lg)
                e_inv = jnp.exp(-lg)
                k_t, b_t = kt * e_inv, bh * e_inv
                e_end = jnp.exp(lg_end - lg)
                k_e, b_e = kt * e_end, bh * e_end
                g_end = jnp.exp(lg_end)
                for h in range(RW_H):
                    sl = slice(h * N, (h + 1) * N)
                    chains.append(dict(d=d, u=u, h=h, s=s, strict=strict, incl=incl, a=a_t[:, sl], r=r_t[:, sl],
                                       b=b_t[:, sl], k=k_t[:, sl], ke=k_e[:, sl], be=b_e[:, sl], g=g_end[:, sl],
                                       v=v[:, sl]))
        ms = [_dot_nt(jnp.concatenate([c['a'], c['r']], axis=0), jnp.concatenate([c['b'], c['k']], axis=0))
              for c in chains]
        pws = [jnp.where(c['strict'], m[:L, :L], 0.0) for c, m in zip(chains, ms)]
        xs = [eye + n for n in pws]
        for _ in range(5):
            pws = [_dot(pw, pw) for pw in pws]
            xs = [x + _dot(x, pw) for x, pw in zip(xs, pws)]
        mvs = [_dot(jnp.where(c['strict'], m[:L, L:], 0.0), c['v']) for c, m in zip(chains, ms)]
        tws = [_dot(x, jnp.concatenate([c['a'], mv], axis=1)) for x, c, mv in zip(xs, chains, mvs)]
        qys = [_dot(jnp.where(c['incl'], m[L:, :L], 0.0), tw) for c, m, tw in zip(chains, ms, tws)]
        ylocs = [_dot(jnp.where(c['incl'], m[L:, L:], 0.0), c['v']) + qy[:, N:] for c, m, qy in zip(chains, ms, qys)]
        pgs = [_dot_tn(c['be'], tw) for c, tw in zip(chains, tws)]
        gmats = [_dot_tn(c['ke'], c['v']) + pg[:, N:] for c, pg in zip(chains, pgs)]
        for d in range(2):
            for h in range(RW_H):
                hst = h_ref[0, d, h]
                for u in range(U):
                    i = (d * U + u) * RW_H + h
                    c = chains[i]
                    y = _dot(c['r'] + qys[i][:, :N], hst) + ylocs[i]
                    ys_ref[pl.ds(c['s'], L), d * 512 + h * N:d * 512 + (h + 1) * N] = y
                    hst = _dot(eye * c['g'] + pgs[i][:, :N], hst) + gmats[i]
                h_ref[0, d, h] = hst
        return carry

    lax.fori_loop(0, nc // U, chunk, 0)

    def finish(ci, carry):
        s = pl.multiple_of(ci * Lf, Lf)
        r = r_ref[0, pl.ds(s, Lf), :]
        k = k_ref[0, pl.ds(s, Lf), :]
        rk = r * k * rk_ref[...]
        bonus = jnp.zeros((Lf, 512), F32)
        for d in range(2):
            a = _sigmoid(pre_ref[pl.ds(s, Lf), d * 1024 + 512:(d + 1) * 1024])
            bonus = bonus + _seg_sum(rk * (1.0 + (a - 1.0) * ka_ref[...]), bd)
        y = ys_ref[pl.ds(s, Lf), 0:512] + ys_ref[pl.ds(s, Lf), 512:1024]
        yc = y - _seg_sum(y, bd) / N
        yn = yc * lax.rsqrt(_seg_sum(yc * yc, bd) / N + RW_LN_EPS) * ln_ref[...]
        g = _dot(_sigmoid(gd_ref[0, pl.ds(s, Lf), :]), gup_ref[...])
        out_ref[0, pl.ds(s, Lf), :] = ((yn + bonus * v_ref[0, pl.ds(s, Lf), :]) * g).astype(BF16)
        return carry

    lax.fori_loop(0, T // Lf, finish, 0)


def _rwkv_params(p):
    z = jnp.zeros((64, 512), F32)
    wwa = jnp.concatenate([jnp.concatenate([p['rw_w_up'][0], z, p['rw_w_up'][1], z], axis=1),
                           jnp.concatenate([z, p['rw_a_up'][0], z, p['rw_a_up'][1]], axis=1)], axis=0)
    w0a0 = jnp.concatenate([p['rw_w0'][0], p['rw_a0'][0], p['rw_w0'][1], p['rw_a0'][1]]).reshape(1, 2048)
    seg = np.arange(512) // RW_N
    bd = jnp.asarray(seg[:, None] == seg[None, :], BF16)
    row = lambda n: p[n].reshape(1, -1)
    return (_split_weight(wwa), w0a0, p['rw_g_up'].astype(BF16), row('rw_k_k'), row('rw_k_a'), row('rw_r_k'),
            row('rw_ln'), bd)


def _rwkv(u3, rp, h0):
    B, T, _ = u3.shape
    blk = lambda name, w: pl.BlockSpec((1, T, w), lambda b, o=PK_OFF[name] // w: (b, 0, o))
    full = lambda a: pl.BlockSpec(a.shape, lambda b, n=a.ndim: (0,) * n)
    st = pl.BlockSpec((1, 2, RW_H, RW_N, RW_N), lambda b: (b, 0, 0, 0, 0))
    return pl.pallas_call(
        functools.partial(_rwkv_kernel, T=T),
        grid=(B,),
        in_specs=[blk('rw_r', 512), blk('rw_k', 512), blk('rw_v', 512), blk('rw_wd', 128), blk('rw_gd', 128)]
                 + [full(a) for a in rp] + [st],
        out_specs=[pl.BlockSpec((1, T, 512), lambda b: (b, 0, 0)), st],
        out_shape=[jax.ShapeDtypeStruct((B, T, 512), BF16), jax.ShapeDtypeStruct(h0.shape, F32)],
        scratch_shapes=[pltpu.VMEM((T, 1024), F32), pltpu.VMEM((T, 2048), F32)],
        compiler_params=_cparams(("arbitrary",)),
        name="rwkv7",
    )(u3, u3, u3, u3, u3, *rp, h0)


def _route(logits):
    lane = lax.broadcasted_iota(jnp.int32, (1, LANES), 1)
    far = jnp.int32(2 * LANES)
    neg = -jnp.inf
    gl = jnp.where(jnp.logical_and(lane >= N_EXPERTS, lane < N_EXPERTS + N_GROUPS), logits, neg)
    gmax = jnp.max(gl, axis=-1, keepdims=True)
    grp = jnp.min(jnp.where(gl == gmax, lane, far), axis=-1, keepdims=True) - N_EXPERTS
    p_grp = 1.0 / jnp.sum(jnp.exp(gl - gmax), axis=-1, keepdims=True)
    el = jnp.where(jnp.logical_and(lane < N_EXPERTS, lane // EXPERTS_PER_GROUP == grp), logits, neg)
    v1 = jnp.max(el, axis=-1, keepdims=True)
    i1 = jnp.min(jnp.where(el == v1, lane, far), axis=-1, keepdims=True)
    el2 = jnp.where(lane == i1, neg, el)
    v2 = jnp.max(el2, axis=-1, keepdims=True)
    i2 = jnp.min(jnp.where(el2 == v2, lane, far), axis=-1, keepdims=True)
    e = jnp.exp(v2 - v1)
    w1 = 1.0 / (1.0 + e)
    comb = jnp.where(lane == i1, p_grp * w1, jnp.where(lane == i2, p_grp * (e * w1), 0.0))
    sel = jnp.where(lane == i1, 1.0, jnp.where(lane == i2, 2.0, 0.0))
    return comb, sel


def _outproj_kernel(x_ref, m0_ref, m1_ref, m2_ref, m3_ref, w_ref, mod_ref, g_ref, wr_ref, br_ref,
                    xn_ref, h3_ref, comb_ref, sel_ref):
    y = jnp.dot(m0_ref[...], w_ref[0:512, :], preferred_element_type=F32)
    for i, m_ref in enumerate((m1_ref, m2_ref, m3_ref), start=1):
        y = y + jnp.dot(m_ref[...], w_ref[i * 512:(i + 1) * 512, :], preferred_element_type=F32)
    xn = x_ref[...] + mod_ref[0, 2:3, :] * y
    xn_ref[...] = xn
    h = xn * lax.rsqrt(jnp.mean(xn * xn, axis=-1, keepdims=True) + NORM_EPS) * g_ref[...]
    h = h * (1.0 + mod_ref[0, 4:5, :]) + mod_ref[0, 3:4, :]
    tm = h.shape[0]
    for c in range(ROW_TILES):
        h3_ref[pl.ds(c, tm, stride=ROW_TILES), :] = h[:, c * LANES:(c + 1) * LANES]
    comb_ref[...], sel_ref[...] = _route(_dot_split(h, wr_ref) + br_ref[...])


def _out_proj(x2, mixed, w_out, mod, g, router_w, router_b, T):
    n_tok = x2.shape[0]
    tm = 512
    bm = mod.shape[0]
    mod_idx = (lambda i: (i * tm // T, 0, 0)) if bm > 1 else (lambda i: (0, 0, 0))
    row = lambda w: pl.BlockSpec((tm, w), lambda i: (i, 0))
    full = lambda a: pl.BlockSpec(a.shape, lambda i, n=a.ndim: (0,) * n)
    return pl.pallas_call(
        _outproj_kernel,
        grid=(n_tok // tm,),
        in_specs=[row(D_MODEL)] + [row(GROUP_W)] * 4 + [full(w_out), pl.BlockSpec((1, 8, D_MODEL), mod_idx),
                                                       full(g), full(router_w), full(router_b)],
        out_specs=[row(D_MODEL), pl.BlockSpec((tm * ROW_TILES, LANES), lambda i: (i, 0)), row(LANES), row(LANES)],
        out_shape=[jax.ShapeDtypeStruct((n_tok, D_MODEL), F32),
                   jax.ShapeDtypeStruct((n_tok * ROW_TILES, LANES), F32),
                   jax.ShapeDtypeStruct((n_tok, LANES), F32), jax.ShapeDtypeStruct((n_tok, LANES), F32)],
        compiler_params=_cparams(("arbitrary",)),
        name="out_proj",
    )(x2, *mixed, w_out, mod, g, router_w, router_b)


MOE_TM = 256
MOE_ROWS = 256
MOE_DMA_UNROLL = 8


def _moe_rows(n_tok):
    return 2 * n_tok + N_EXPERTS * MOE_TM


def _plan_kernel(sel_ref, pos_ref, tmap_ref):
    n_tok = sel_ref.shape[0]
    blk = 512
    lane = lax.broadcasted_iota(jnp.int32, (1, LANES), 1)
    earlier = (lax.broadcasted_iota(jnp.int32, (blk, blk), 1)
               < lax.broadcasted_iota(jnp.int32, (blk, blk), 0)).astype(BF16)
    before = (lax.broadcasted_iota(jnp.int32, (LANES, LANES), 0)
              < lax.broadcasted_iota(jnp.int32, (LANES, LANES), 1)).astype(BF16)

    def count(i, acc):
        s = pl.multiple_of(i * blk, blk)
        return acc + jnp.sum((sel_ref[pl.ds(s, blk), :] > 0.0).astype(F32), axis=0, keepdims=True)

    counts = lax.fori_loop(0, n_tok // blk, count, jnp.zeros((1, LANES), F32))
    tiles = jnp.floor((counts + (MOE_TM - 1)) * (1.0 / MOE_TM))
    tile_start = _dot(jnp.broadcast_to(tiles, (8, LANES)), before)[0:1, :]
    tile_end = tile_start + tiles
    base = tile_start * MOE_TM
    n_tiles = jnp.sum(tiles, axis=-1, keepdims=True)
    j = lax.broadcasted_iota(jnp.int32, (tmap_ref.shape[0], 1), 0).astype(F32)
    done = jnp.logical_and(tile_end <= j, lane < N_EXPERTS)
    expert = jnp.minimum(jnp.sum(done.astype(F32), axis=-1, keepdims=True), N_EXPERTS - 1.0)
    valid = (j < n_tiles).astype(F32)
    tmap_ref[...] = jnp.where(lane == 0, expert, jnp.where(lane == 1, valid, 0.0)).astype(jnp.int32)

    def place(i, seen):
        s = pl.multiple_of(i * blk, blk)
        sel = sel_ref[pl.ds(s, blk), :]
        one = (sel > 0.0).astype(F32)
        row = base + seen + _dot(earlier, one)
        p1 = jnp.sum(jnp.where(sel == 1.0, row, 0.0), axis=-1, keepdims=True)
        p2 = jnp.sum(jnp.where(sel == 2.0, row, 0.0), axis=-1, keepdims=True)
        pos_ref[pl.ds(s, blk), :] = jnp.where(lane == 0, p1, jnp.where(lane == 1, p2, 0.0)).astype(jnp.int32)
        return seen + jnp.sum(one, axis=0, keepdims=True)

    lax.fori_loop(0, n_tok // blk, place, jnp.zeros((1, LANES), F32))


def _moe_plan(sel):
    n_tok = sel.shape[0]
    n_tiles = _moe_rows(n_tok) // MOE_TM
    pos, tmap = pl.pallas_call(
        _plan_kernel,
        out_shape=[jax.ShapeDtypeStruct((n_tok, LANES), jnp.int32),
                   jax.ShapeDtypeStruct((-(-n_tiles // 8) * 8, LANES), jnp.int32)],
        compiler_params=pltpu.CompilerParams(vmem_limit_bytes=VMEM_LIMIT),
        name="moe_plan",
    )(sel)
    return jnp.transpose(pos[:, :2]), tmap[:n_tiles, 0], tmap[:n_tiles, 1]


def _dispatch_kernel(pos_ref, src_ref, init_ref, dst_ref, sem):
    del init_ref
    base = pl.program_id(0) * MOE_ROWS

    def copy(j, k):
        return pltpu.make_async_copy(src_ref.at[j], dst_ref.at[pos_ref[k, base + j]], sem)

    def start(j, c):
        copy(j, 0).start(priority=0)
        copy(j, 1).start(priority=1)
        return c

    def wait(j, c):
        copy(j, 0).wait()
        copy(j, 1).wait()
        return c

    lax.fori_loop(0, MOE_ROWS, start, 0, unroll=MOE_DMA_UNROLL)
    lax.fori_loop(0, MOE_ROWS, wait, 0, unroll=MOE_DMA_UNROLL)


def _dispatch(pos, h3, init):
    n_tok = h3.shape[0]
    any_spec = pl.BlockSpec(memory_space=pl.ANY)
    return pl.pallas_call(
        _dispatch_kernel,
        grid_spec=pltpu.PrefetchScalarGridSpec(
            num_scalar_prefetch=1, grid=(n_tok // MOE_ROWS,),
            in_specs=[pl.BlockSpec((MOE_ROWS, ROW_TILES, LANES), lambda i, p: (i, 0, 0)), any_spec],
            out_specs=any_spec, scratch_shapes=[pltpu.SemaphoreType.DMA(())]),
        out_shape=jax.ShapeDtypeStruct(init.shape, init.dtype),
        input_output_aliases={2: 0},
        compiler_params=pltpu.CompilerParams(dimension_semantics=("arbitrary",)),
        name="moe_dispatch",
    )(pos, h3, init)


def _experts_kernel(te_ref, tv_ref, xs_ref, wg_ref, wu_ref, wd_ref, ys_ref, wg_s, wu_s, wd_s):
    i = pl.program_id(0)
    fresh = jnp.logical_or(i == 0, te_ref[i] != te_ref[jnp.maximum(i - 1, 0)])

    @pl.when(jnp.logical_and(fresh, tv_ref[i] == 1))
    def _():
        wg_s[...] = wg_ref[0, 0].astype(BF16)
        wu_s[...] = wu_ref[0, 0].astype(BF16)
        wd_s[...] = wd_ref[0, 0].astype(BF16)

    @pl.when(tv_ref[i] == 1)
    def _():
        x = jnp.concatenate([xs_ref[pl.ds(c, MOE_TM, stride=ROW_TILES), :] for c in range(ROW_TILES)],
                            axis=1).astype(BF16)
        a = jnp.dot(x, wg_s[...], preferred_element_type=F32)
        b = jnp.dot(x, wu_s[...], preferred_element_type=F32)
        y = jnp.dot(((a * _sigmoid(a)) * b).astype(BF16), wd_s[...], preferred_element_type=F32)
        for c in range(ROW_TILES):
            ys_ref[pl.ds(c, MOE_TM, stride=ROW_TILES), :] = y[:, c * LANES:(c + 1) * LANES]

    @pl.when(tv_ref[i] == 0)
    def _():
        ys_ref[...] = jnp.zeros_like(ys_ref)


def _experts(tile_expert, tile_valid, xs, wg, wu, wd, layer):
    n_rows = xs.shape[0] // ROW_TILES
    rows = pl.BlockSpec((MOE_TM * ROW_TILES, LANES), lambda i, te, tv: (i, 0))
    return pl.pallas_call(
        _experts_kernel,
        grid_spec=pltpu.PrefetchScalarGridSpec(
            num_scalar_prefetch=2, grid=(n_rows // MOE_TM,),
            in_specs=[rows,
                      pl.BlockSpec((1, 1, D_MODEL, EXPERT_HIDDEN), lambda i, te, tv: (layer, te[i], 0, 0)),
                      pl.BlockSpec((1, 1, D_MODEL, EXPERT_HIDDEN), lambda i, te, tv: (layer, te[i], 0, 0)),
                      pl.BlockSpec((1, 1, EXPERT_HIDDEN, D_MODEL), lambda i, te, tv: (layer, te[i], 0, 0))],
            out_specs=rows,
            scratch_shapes=[pltpu.VMEM((D_MODEL, EXPERT_HIDDEN), BF16), pltpu.VMEM((D_MODEL, EXPERT_HIDDEN), BF16),
                            pltpu.VMEM((EXPERT_HIDDEN, D_MODEL), BF16)]),
        out_shape=jax.ShapeDtypeStruct(xs.shape, F32),
        compiler_params=_cparams(("arbitrary",)),
        name="moe_experts",
    )(tile_expert, tile_valid, xs, wg, wu, wd)


def _combine_kernel(pos_ref, ys_ref, comb_ref, sel_ref, xn_ref, mod_ref, o_ref, y_s, sems):
    i = pl.program_id(0)
    slot_rows = 2 * MOE_ROWS * ROW_TILES

    def copy(tile, slot, j, k):
        dst = pl.multiple_of(slot * slot_rows + (k * MOE_ROWS + j) * ROW_TILES, ROW_TILES)
        return pltpu.make_async_copy(ys_ref.at[pos_ref[k, tile * MOE_ROWS + j]], y_s.at[pl.ds(dst, ROW_TILES), :],
                                     sems.at[slot])

    def start_tile(tile, slot):
        def body(j, c):
            copy(tile, slot, j, 0).start(priority=0)
            copy(tile, slot, j, 1).start(priority=1)
            return c
        lax.fori_loop(0, MOE_ROWS, body, 0, unroll=MOE_DMA_UNROLL)

    def wait_tile(tile, slot):
        def body(j, c):
            copy(tile, slot, j, 0).wait()
            copy(tile, slot, j, 1).wait()
            return c
        lax.fori_loop(0, MOE_ROWS, body, 0, unroll=MOE_DMA_UNROLL)

    @pl.when(i == 0)
    def _():
        start_tile(0, 0)

    @pl.when(i + 1 < pl.num_programs(0))
    def _():
        start_tile(i + 1, (i + 1) % 2)

    comb, sel = comb_ref[...], sel_ref[...]
    w1 = jnp.sum(jnp.where(sel == 1.0, comb, 0.0), axis=-1, keepdims=True)
    w2 = jnp.sum(jnp.where(sel == 2.0, comb, 0.0), axis=-1, keepdims=True)
    slot = i % 2
    wait_tile(i, slot)
    for c in range(ROW_TILES):
        cs = slice(c * LANES, (c + 1) * LANES)
        y1 = y_s[pl.ds(slot * slot_rows + c, MOE_ROWS, stride=ROW_TILES), :]
        y2 = y_s[pl.ds(slot * slot_rows + MOE_ROWS * ROW_TILES + c, MOE_ROWS, stride=ROW_TILES), :]
        o_ref[:, cs] = xn_ref[:, cs] + mod_ref[0, 5:6, cs] * (w1 * y1 + w2 * y2)


def _combine(pos, ys, comb, sel, xn, mod, T):
    n_tok = xn.shape[0]
    tm = MOE_ROWS
    bm = mod.shape[0]
    mod_idx = (lambda i, p: (i * tm // T, 0, 0)) if bm > 1 else (lambda i, p: (0, 0, 0))
    row = lambda w: pl.BlockSpec((tm, w), lambda i, p: (i, 0))
    return pl.pallas_call(
        _combine_kernel,
        grid_spec=pltpu.PrefetchScalarGridSpec(
            num_scalar_prefetch=1, grid=(n_tok // tm,),
            in_specs=[pl.BlockSpec(memory_space=pl.ANY), row(LANES), row(LANES), row(D_MODEL),
                      pl.BlockSpec((1, 8, D_MODEL), mod_idx)],
            out_specs=row(D_MODEL),
            scratch_shapes=[pltpu.VMEM((2 * 2 * tm * ROW_TILES, LANES), F32), pltpu.SemaphoreType.DMA((2,))]),
        out_shape=jax.ShapeDtypeStruct((n_tok, D_MODEL), F32),
        compiler_params=_cparams(("arbitrary",)),
        name="moe_combine",
    )(pos, ys, comb, sel, xn, mod)


def _moe(groups, wg, wu, wd, layer, sorted_buf=None):
    sizes = [g[3].shape[0] for g in groups]
    n_rows = _moe_rows(sum(sizes))
    pos, tile_expert, tile_valid = _moe_plan(jnp.concatenate([g[2] for g in groups], axis=0))
    xs = jnp.zeros((n_rows, ROW_TILES, LANES), F32) if sorted_buf is None else sorted_buf
    start = 0
    for (h3, _, _, _, _, _), n in zip(groups, sizes):
        xs = _dispatch(pos[:, start:start + n], h3.reshape(n, ROW_TILES, LANES), xs)
        start += n
    ys = _experts(tile_expert, tile_valid, xs.reshape(n_rows * ROW_TILES, LANES), wg, wu, wd, layer)
    ys = ys.reshape(n_rows, ROW_TILES, LANES)
    outs, start = [], 0
    for (_, comb, sel, xn, mod, T), n in zip(groups, sizes):
        outs.append(_combine(pos[:, start:start + n], ys, comb, sel, xn, mod, T))
        start += n
    return outs, xs


def kernel(x_prompt, x_sample, cache_mla, state_mlstm_C, state_mlstm_n, state_mlstm_m, state_rwkv, state_gla, c, c_ctx, w_ada, b_ada, g_mix, g_ffn, w_in, w_out, b_ml_gates, g_ml_norm, g_mla_qlat, g_mla_kvlat, w_mla_uq, w_mla_ukv, g_mla_qn, g_mla_kn, rw_w0, rw_w_up, rw_a0, rw_a_up, rw_g_up, rw_k_k, rw_k_a, rw_r_k, rw_ln, gla_g_up, gla_g_b, gla_norm, moe_w_rg, moe_b_rg, moe_w_re, moe_b_re, moe_w_gate, moe_w_up, moe_w_down):
    cc = jnp.concatenate([c_ctx[None], c, jnp.zeros((3, D_MODEL), F32)], axis=0)
    mod = _modulation(cc, w_ada, b_ada).reshape(DEPTH, 8, 6, D_MODEL)
    mod = jnp.pad(mod, ((0, 0), (0, 0), (0, 2), (0, 0)))

    layers = []
    for l in range(DEPTH):
        p = {'b_ml_gates': b_ml_gates[l], 'g_ml_norm': g_ml_norm[l], 'g_mla_qlat': g_mla_qlat[l],
             'g_mla_kvlat': g_mla_kvlat[l], 'w_mla_uq': w_mla_uq[l], 'w_mla_ukv': w_mla_ukv[l],
             'g_mla_qn': g_mla_qn[l], 'g_mla_kn': g_mla_kn[l], 'rw_w0': rw_w0[l], 'rw_w_up': rw_w_up[l],
             'rw_a0': rw_a0[l], 'rw_a_up': rw_a_up[l], 'rw_g_up': rw_g_up[l], 'rw_k_k': rw_k_k[l],
             'rw_k_a': rw_k_a[l], 'rw_r_k': rw_r_k[l], 'rw_ln': rw_ln[l], 'gla_g_up': gla_g_up[l],
             'gla_g_b': gla_g_b[l], 'gla_norm': gla_norm[l]}
        lane_pad = LANES - N_EXPERTS - N_GROUPS
        router_w = jnp.concatenate([moe_w_re[l], moe_w_rg[l], jnp.zeros((D_MODEL, lane_pad), F32)], axis=1)
        router_b = jnp.concatenate([moe_b_re[l], moe_b_rg[l], jnp.zeros((lane_pad,), F32)]).reshape(1, LANES)
        layers.append(dict(
            w_in=_pack_w_in(w_in[l]), g_mix=g_mix[l].reshape(1, -1), g_ffn=g_ffn[l].reshape(1, -1),
            w_out=w_out[l].astype(BF16), ml=_mlstm_params(p), mla=_mla_params(p), rw=_rwkv_params(p),
            gla=_gla_params(p), router_w=_split_weight(router_w), router_b=router_b))

    def mix(x2, B, T, mod_g, lp, ctx):
        u3 = _in_proj(x2, mod_g, lp['g_mix'], lp['w_in'], T).reshape(B, T, PK_COLS)
        if ctx is None:
            ctx_kv = None
            ml_c0, ml_m0 = _mlstm_state_zero(B)
            rw_h0 = jnp.zeros((B, 2, RW_H, RW_N, RW_N), F32)
            gla_s0 = jnp.zeros((B, 2, GLA_H, GLA_DK, GLA_DV), F32)
        else:
            ctx_kv, ml_C0, ml_n0, ml_m0_, rw_S0, gla_s0 = ctx
            ml_c0, ml_m0 = _mlstm_state_in(ml_C0, ml_n0, ml_m0_)
            rw_h0 = jnp.swapaxes(rw_S0, -1, -2)
        ml_out, ml_c, ml_m = _mlstm(u3, *lp['ml'], ml_c0, ml_m0)
        mla_out, own_kv = _mla(u3, ctx_kv, lp['mla'])
        rw_out, rw_h = _rwkv(u3, lp['rw'], rw_h0)
        gla_out, gla_s = _gla(u3, lp['gla'], gla_s0)
        mixed = [t.reshape(B * T, GROUP_W) for t in (ml_out, mla_out, rw_out, gla_out)]
        xn, h3, comb, sel = _out_proj(x2, mixed, lp['w_out'], mod_g, lp['g_ffn'], lp['router_w'], lp['router_b'], T)
        ml_C, ml_n, ml_mm = _mlstm_state_out(ml_c, ml_m)
        return (h3, comb, sel, xn, mod_g, T), (own_kv, ml_C, ml_n, ml_mm, jnp.swapaxes(rw_h, -1, -2), gla_s)

    Bp, Tp = x_prompt.shape[:2]
    Bs, Ts = x_sample.shape[:2]
    xp = x_prompt.reshape(Bp * Tp, D_MODEL)
    xs = x_sample.reshape(Bs * Ts, D_MODEL)
    ctx_states = []
    sorted_buf = None
    for l in range(DEPTH):
        ctx = (cache_mla[:, l], state_mlstm_C[:, l], state_mlstm_n[:, l], state_mlstm_m[:, l],
               state_rwkv[:, l], state_gla[:, l])
        moe_p, st = mix(xp, Bp, Tp, mod[l, 0:1], layers[l], None)
        moe_s, _ = mix(xs, Bs, Ts, mod[l, 1:1 + Bs], layers[l], ctx)
        ctx_states.append(st)
        (xp, xs), sorted_buf = _moe([moe_p, moe_s], moe_w_gate, moe_w_up, moe_w_down, l, sorted_buf)
    outs = [jnp.stack([s[i] for s in ctx_states], axis=1) for i in range(6)]
    return (xp.reshape(x_prompt.shape), xs.reshape(x_sample.shape), *outs)
```
